```python
import math
import jax
import jax.numpy as jnp
from jax import lax
import numpy as np


D_MODEL = 1024
BATCH = 2
SEQ = 8192
DEPTH = 2

HEAD_DIM = 64
CHUNK = 128
EPS = 1e-6

SSD_HEADS = 8
SSD_GROUPS = 2
SSD_STATE = 64
SSD_CONV = 4
SSD_WIDTH = SSD_HEADS * HEAD_DIM
SSD_XBC = SSD_WIDTH + 2 * SSD_GROUPS * SSD_STATE
SSD_COLS = SSD_WIDTH + SSD_XBC + SSD_HEADS
RET_HEADS = 8
RET_WIDTH = RET_HEADS * HEAD_DIM
RET_COLS = 4 * RET_WIDTH
EVEN_COLS = SSD_COLS + RET_COLS

RWKV_HEADS = 8
RWKV_WIDTH = RWKV_HEADS * HEAD_DIM
RWKV_DECAY_RANK = 64
RWKV_ICLR_RANK = 64
RWKV_GATE_RANK = 128
RWKV_COLS = 3 * RWKV_WIDTH + RWKV_DECAY_RANK + RWKV_ICLR_RANK + RWKV_GATE_RANK
RWKV_SPLITS = [RWKV_WIDTH, 2 * RWKV_WIDTH, 3 * RWKV_WIDTH,
               3 * RWKV_WIDTH + RWKV_DECAY_RANK,
               3 * RWKV_WIDTH + RWKV_DECAY_RANK + RWKV_ICLR_RANK]
RWKV_LN_EPS = 64e-5
RWKV_DECAY_SCALE = 0.606531
FOX_HEADS = 8
FOX_WIDTH = FOX_HEADS * HEAD_DIM
FOX_COLS = 3 * FOX_WIDTH + FOX_HEADS
ODD_COLS = RWKV_COLS + FOX_COLS

MIX_WIDTH = 1024
D_FF = 2816
FFN_CONV = 3
N_EVEN = (DEPTH + 1) // 2
N_ODD = DEPTH // 2

kernel_name = 'hybrid_ssd_retnet_rwkv7_fox_convffn'


def rms_norm(x, w, eps=EPS):
    xf = x.astype(jnp.float32)
    y = xf * lax.rsqrt(jnp.mean(xf * xf, axis=-1, keepdims=True) + eps)
    return (y * w.astype(jnp.float32)).astype(x.dtype)


def causal_dwconv(x, w, b):
    K = w.shape[0]
    S = x.shape[1]
    xp = jnp.pad(x, ((0, 0), (K - 1, 0), (0, 0)))
    out = b
    for k in range(K):
        out = out + w[k] * xp[:, k:k + S]
    return out


def rotary(x, pos):
    half = x.shape[-1] // 2
    inv = 1.0 / (10000.0 ** (jnp.arange(half, dtype=jnp.float32) / half))
    ang = pos.astype(jnp.float32)[:, None] * inv[None, :]
    cos = jnp.cos(ang)[None, :, None, :]
    sin = jnp.sin(ang)[None, :, None, :]
    x1, x2 = x[..., :half], x[..., half:]
    return jnp.concatenate([x1 * cos - x2 * sin, x1 * sin + x2 * cos], axis=-1)


def ssd_mixer(proj, conv_w, conv_b, dt_bias, a_log, d_skip, norm_w):
    b_, s_, _ = proj.shape
    nc = s_ // CHUNK
    r_ = SSD_HEADS // SSD_GROUPS
    z = proj[..., :SSD_WIDTH]
    xbc = jax.nn.silu(causal_dwconv(proj[..., SSD_WIDTH:SSD_WIDTH + SSD_XBC], conv_w, conv_b))
    dt = jax.nn.softplus(proj[..., SSD_WIDTH + SSD_XBC:] + dt_bias)
    x = xbc[..., :SSD_WIDTH].reshape(b_, nc, CHUNK, SSD_GROUPS, r_, HEAD_DIM)
    n_bc = SSD_GROUPS * SSD_STATE
    bm = xbc[..., SSD_WIDTH:SSD_WIDTH + n_bc].reshape(b_, nc, CHUNK, SSD_GROUPS, SSD_STATE)
    cm = xbc[..., SSD_WIDTH + n_bc:].reshape(b_, nc, CHUNK, SSD_GROUPS, SSD_STATE)
    dt = dt.reshape(b_, nc, CHUNK, SSD_GROUPS, r_)
    a = dt * (-jnp.exp(a_log)).reshape(SSD_GROUPS, r_)
    a_cum = jnp.cumsum(a, axis=2)
    xdt = x * dt[..., None]
    causal = jnp.tril(jnp.ones((CHUNK, CHUNK), dtype=bool))
    seg = a_cum[:, :, :, None] - a_cum[:, :, None, :]
    decay = jnp.exp(jnp.where(causal[:, :, None, None], seg, -jnp.inf))
    cb = jnp.einsum('bclgn,bcsgn->bclsg', cm, bm)
    y_diag = jnp.einsum('bclsg,bclsgr,bcsgrp->bclgrp', cb, decay, xdt)
    to_end = jnp.exp(a_cum[:, :, -1:] - a_cum)
    chunk_states = jnp.einsum('bclgn,bclgr,bclgrp->bcgrpn', bm, to_end, xdt)
    chunk_decay = jnp.exp(a_cum[:, :, -1])

    def step(h, inp):
        st, dc = inp
        return h * dc[..., None, None] + st, h

    _, states_in = lax.scan(step, jnp.zeros_like(chunk_states[:, 0]),
                            (jnp.moveaxis(chunk_states, 1, 0), jnp.moveaxis(chunk_decay, 1, 0)))
    states_in = jnp.moveaxis(states_in, 0, 1)
    y_off = jnp.einsum('bclgn,bcgrpn,bclgr->bclgrp', cm, states_in, jnp.exp(a_cum))
    y = y_diag + y_off + d_skip.reshape(SSD_GROUPS, r_)[:, :, None] * x
    y = y.reshape(b_, s_, SSD_WIDTH)
    return rms_norm(y * jax.nn.silu(z), norm_w)


def retention_mixer(proj, pos, norm_w):
    b_, s_, _ = proj.shape
    nc = s_ // CHUNK
    q, k, v, g = jnp.split(proj, 4, axis=-1)
    hs = (b_, s_, RET_HEADS, HEAD_DIM)
    cs = (b_, nc, CHUNK, RET_HEADS, HEAD_DIM)
    q = rotary(q.reshape(hs), pos).reshape(cs)
    k = (rotary(k.reshape(hs), pos) * HEAD_DIM ** -0.5).reshape(cs)
    v = v.reshape(cs)
    log_gamma = jnp.log1p(-(2.0 ** (-5.0 - jnp.arange(RET_HEADS, dtype=jnp.float32))))
    idx = jnp.arange(CHUNK, dtype=jnp.float32)
    rel = idx[:, None] - idx[None, :]
    intra = jnp.where(rel[None] >= 0,
                      jnp.exp(jnp.maximum(rel, 0.0)[None] * log_gamma[:, None, None]), 0.0)
    scores = jnp.einsum('bclhd,bcshd->bchls', q, k) * intra
    y_in = jnp.einsum('bchls,bcshe->bclhe', scores, v)
    k_to_end = jnp.exp((CHUNK - 1 - idx)[:, None] * log_gamma[None, :])
    chunk_kv = jnp.einsum('bcshd,sh,bcshe->bchde', k, k_to_end, v)
    chunk_gamma = jnp.exp(CHUNK * log_gamma)[:, None, None]

    def step(r, kv):
        return r * chunk_gamma + kv, r

    _, r_in = lax.scan(step, jnp.zeros_like(chunk_kv[:, 0]), jnp.moveaxis(chunk_kv, 1, 0))
    r_in = jnp.moveaxis(r_in, 0, 1)
    q_decay = jnp.exp((idx + 1.0)[:, None] * log_gamma[None, :])
    y_x = jnp.einsum('bclhd,bchde,lh->bclhe', q, r_in, q_decay)
    y = (y_in + y_x).reshape(hs)
    y = rms_norm(y, norm_w.reshape(RET_HEADS, HEAD_DIM)).reshape(b_, s_, RET_WIDTH)
    return jax.nn.silu(g) * y


def rwkv7_mixer(proj, mu, w0, w_up, a0, a_up, g_up, k_k, k_a, r_k, ln_w, ln_b):
    b_, s_, _ = proj.shape
    prev = jnp.pad(proj, ((0, 0), (1, 0), (0, 0)))[:, :-1]
    proj = proj + (prev - proj) * mu
    r, k, v, w_lo, a_lo, g_lo = jnp.split(proj, RWKV_SPLITS, axis=-1)
    w = jnp.exp(-RWKV_DECAY_SCALE * jax.nn.sigmoid(w0 + jnp.tanh(w_lo) @ w_up))
    a = jax.nn.sigmoid(a0 + a_lo @ a_up)
    g = jax.nn.sigmoid(g_lo) @ g_up
    hs = (b_, s_, RWKV_HEADS, HEAD_DIM)
    r, k, v, w, a = (t.reshape(hs) for t in (r, k, v, w, a))
    kk = k * k_k.reshape(RWKV_HEADS, HEAD_DIM)
    kk = kk / jnp.maximum(jnp.sqrt(jnp.sum(kk * kk, axis=-1, keepdims=True)), 1e-12)
    k = k * (1.0 + (a - 1.0) * k_a.reshape(RWKV_HEADS, HEAD_DIM))

    def step(state, inp):
        r_t, w_t, k_t, v_t, kk_t, a_t = inp
        sa = jnp.einsum('bhvk,bhk->bhv', state, -kk_t)
        state = (state * w_t[:, :, None, :] + sa[..., None] * (kk_t * a_t)[:, :, None, :]
                 + v_t[..., None] * k_t[:, :, None, :])
        return state, jnp.einsum('bhvk,bhk->bhv', state, r_t)

    state0 = jnp.zeros((b_, RWKV_HEADS, HEAD_DIM, HEAD_DIM), dtype=r.dtype)
    seq_first = lambda t: jnp.moveaxis(t, 1, 0)
    _, y = lax.scan(step, state0, (seq_first(r), seq_first(w), seq_first(k),
                                   seq_first(v), seq_first(kk), seq_first(a)))
    y = jnp.moveaxis(y, 0, 1)
    mean = jnp.mean(y, axis=-1, keepdims=True)
    var = jnp.mean(jnp.square(y - mean), axis=-1, keepdims=True)
    y = ((y - mean) * lax.rsqrt(var + RWKV_LN_EPS) * ln_w.reshape(RWKV_HEADS, HEAD_DIM)
         + ln_b.reshape(RWKV_HEADS, HEAD_DIM))
    bonus = jnp.sum(r * k * r_k, axis=-1, keepdims=True) * v
    return (y + bonus).reshape(b_, s_, RWKV_WIDTH) * g


def fox_mixer(proj, q_norm_w, k_norm_w, f_bias):
    b_, s_, _ = proj.shape
    nb = s_ // CHUNK
    q, k, v, f = jnp.split(proj, [FOX_WIDTH, 2 * FOX_WIDTH, 3 * FOX_WIDTH], axis=-1)
    hs = (b_, s_, FOX_HEADS, HEAD_DIM)
    q = rms_norm(q.reshape(hs), q_norm_w) * HEAD_DIM ** -0.5
    k = rms_norm(k.reshape(hs), k_norm_w)
    v = v.reshape(hs)
    log_f = jax.nn.log_sigmoid((f + f_bias).astype(jnp.float32))
    c = jnp.cumsum(log_f, axis=1)
    c_hs = jnp.moveaxis(c, -1, 1)
    qb = jnp.moveaxis(q.reshape(b_, nb, CHUNK, FOX_HEADS, HEAD_DIM), 1, 0)
    cb = jnp.moveaxis(c.reshape(b_, nb, CHUNK, FOX_HEADS), 1, 0)
    k_pos = jnp.arange(s_)

    def block(args):
        i, q_i, c_i = args
        logits = jnp.einsum('bqhd,bkhd->bhqk', q_i, k).astype(jnp.float32)
        logits = logits + jnp.moveaxis(c_i, -1, 1)[..., None] - c_hs[:, :, None, :]
        q_pos = i * CHUNK + jnp.arange(CHUNK)
        logits = jnp.where(q_pos[:, None] >= k_pos[None, :], logits, -jnp.inf)
        p = jax.nn.softmax(logits, axis=-1)
        return jnp.einsum('bhqk,bkhd->bqhd', p.astype(v.dtype), v)

    out = lax.map(block, (jnp.arange(nb), qb, cb))
    return jnp.moveaxis(out, 0, 1).reshape(b_, s_, FOX_WIDTH)


def even_layer(x, pos, norm_w, w_in, conv_w, conv_b, dt_bias, a_log, d_skip, ssd_norm_w, ret_norm_w, w_out):
    h = (rms_norm(x, norm_w) @ w_in).astype(jnp.float32)
    y_ssd = ssd_mixer(h[..., :SSD_COLS], conv_w, conv_b, dt_bias, a_log, d_skip, ssd_norm_w)
    y_ret = retention_mixer(h[..., SSD_COLS:], pos, ret_norm_w)
    y = jnp.concatenate([y_ssd, y_ret], axis=-1).astype(x.dtype)
    return x + y @ w_out


def odd_layer(x, norm_w, w_in, mu, w0, w_up, a0, a_up, g_up, k_k, k_a, r_k, ln_w, ln_b,
              q_norm_w, k_norm_w, f_bias, w_out):
    h = (rms_norm(x, norm_w) @ w_in).astype(jnp.float32)
    y_rwkv = rwkv7_mixer(h[..., :RWKV_COLS], mu, w0, w_up, a0, a_up, g_up, k_k, k_a, r_k, ln_w, ln_b)
    y_fox = fox_mixer(h[..., RWKV_COLS:], q_norm_w, k_norm_w, f_bias)
    y = jnp.concatenate([y_rwkv, y_fox], axis=-1).astype(x.dtype)
    return x + y @ w_out


def conv_ffn(x, w_up, conv_w, conv_b, w_down):
    gate, up = jnp.split(x @ w_up, 2, axis=-1)
    gate = causal_dwconv(gate, conv_w, conv_b)
    return (jax.nn.silu(gate) * up) @ w_down


def setup_inputs(seed: int = 0) -> dict:
    key = jax.random.key(seed)
    ks = iter(jax.random.split(key, 48))
    f32 = jnp.float32

    def nrm(shape, scale):
        return jax.random.normal(next(ks), shape, f32) * scale

    def uni(shape, lo, hi):
        return jax.random.uniform(next(ks), shape, f32, minval=lo, maxval=hi)

    dt0 = jnp.exp(uni((N_EVEN, SSD_HEADS), math.log(1e-3), math.log(1e-1)))
    return {
        'x': nrm((BATCH, SEQ, D_MODEL), 1.0),
        'ev_norm_w': 1.0 + nrm((N_EVEN, D_MODEL), 0.02),
        'ev_w_in': nrm((N_EVEN, D_MODEL, EVEN_COLS), D_MODEL ** -0.5),
        'ev_ssd_conv_w': nrm((N_EVEN, SSD_CONV, SSD_XBC), SSD_CONV ** -0.5),
        'ev_ssd_conv_b': nrm((N_EVEN, SSD_XBC), 0.02),
        'ev_ssd_dt_bias': dt0 + jnp.log(-jnp.expm1(-dt0)),
        'ev_ssd_a_log': jnp.log(uni((N_EVEN, SSD_HEADS), 1.0, 16.0)),
        'ev_ssd_d': 1.0 + nrm((N_EVEN, SSD_HEADS), 0.1),
        'ev_ssd_norm_w': 1.0 + nrm((N_EVEN, SSD_WIDTH), 0.02),
        'ev_ret_norm_w': 1.0 + nrm((N_EVEN, RET_WIDTH), 0.02),
        'ev_w_out': nrm((N_EVEN, MIX_WIDTH, D_MODEL), MIX_WIDTH ** -0.5),
        'od_norm_w': 1.0 + nrm((N_ODD, D_MODEL), 0.02),
        'od_w_in': nrm((N_ODD, D_MODEL, ODD_COLS), D_MODEL ** -0.5),
        'od_rwkv_mu': uni((N_ODD, RWKV_COLS), 0.0, 1.0),
        'od_rwkv_w0': uni((N_ODD, RWKV_WIDTH), -4.0, 2.0),
        'od_rwkv_w_up': nrm((N_ODD, RWKV_DECAY_RANK, RWKV_WIDTH), 0.1),
        'od_rwkv_a0': nrm((N_ODD, RWKV_WIDTH), 0.5),
        'od_rwkv_a_up': nrm((N_ODD, RWKV_ICLR_RANK, RWKV_WIDTH), 0.1),
        'od_rwkv_g_up': nrm((N_ODD, RWKV_GATE_RANK, RWKV_WIDTH), RWKV_GATE_RANK ** -0.5),
        'od_rwkv_k_k': 0.85 + nrm((N_ODD, RWKV_WIDTH), 0.05),
        'od_rwkv_k_a': 1.0 + nrm((N_ODD, RWKV_WIDTH), 0.05),
        'od_rwkv_r_k': nrm((N_ODD, RWKV_HEADS, HEAD_DIM), 0.1),
        'od_rwkv_ln_w': 1.0 + nrm((N_ODD, RWKV_WIDTH), 0.02),
        'od_rwkv_ln_b': nrm((N_ODD, RWKV_WIDTH), 0.02),
        'od_fox_q_norm_w': 1.0 + nrm((N_ODD, HEAD_DIM), 0.02),
        'od_fox_k_norm_w': 1.0 + nrm((N_ODD, HEAD_DIM), 0.02),
        'od_fox_f_bias': uni((N_ODD, FOX_HEADS), 1.0, 5.0),
        'od_w_out': nrm((N_ODD, MIX_WIDTH, D_MODEL), MIX_WIDTH ** -0.5),
        'ffn_norm_w': 1.0 + nrm((DEPTH, D_MODEL), 0.02),
        'ffn_w_up': nrm((DEPTH, D_MODEL, 2 * D_FF), D_MODEL ** -0.5),
        'ffn_conv_w': nrm((DEPTH, FFN_CONV, D_FF), FFN_CONV ** -0.5),
        'ffn_conv_b': nrm((DEPTH, D_FF), 0.02),
        'ffn_w_down': nrm((DEPTH, D_FF, D_MODEL), D_FF ** -0.5),
    }


def reference(x, ev_norm_w, ev_w_in, ev_ssd_conv_w, ev_ssd_conv_b, ev_ssd_dt_bias, ev_ssd_a_log,
              ev_ssd_d, ev_ssd_norm_w, ev_ret_norm_w, ev_w_out,
              od_norm_w, od_w_in, od_rwkv_mu, od_rwkv_w0, od_rwkv_w_up, od_rwkv_a0, od_rwkv_a_up,
              od_rwkv_g_up, od_rwkv_k_k, od_rwkv_k_a, od_rwkv_r_k, od_rwkv_ln_w, od_rwkv_ln_b,
              od_fox_q_norm_w, od_fox_k_norm_w, od_fox_f_bias, od_w_out,
              ffn_norm_w, ffn_w_up, ffn_conv_w, ffn_conv_b, ffn_w_down):
    pos = jnp.arange(x.shape[1])
    for layer in range(DEPTH):
        i = layer // 2
        if layer % 2 == 0:
            x = even_layer(x, pos, ev_norm_w[i], ev_w_in[i], ev_ssd_conv_w[i], ev_ssd_conv_b[i],
                           ev_ssd_dt_bias[i], ev_ssd_a_log[i], ev_ssd_d[i], ev_ssd_norm_w[i],
                           ev_ret_norm_w[i], ev_w_out[i])
        else:
            x = odd_layer(x, od_norm_w[i], od_w_in[i], od_rwkv_mu[i], od_rwkv_w0[i], od_rwkv_w_up[i],
                          od_rwkv_a0[i], od_rwkv_a_up[i], od_rwkv_g_up[i], od_rwkv_k_k[i],
                          od_rwkv_k_a[i], od_rwkv_r_k[i], od_rwkv_ln_w[i], od_rwkv_ln_b[i],
                          od_fox_q_norm_w[i], od_fox_k_norm_w[i], od_fox_f_bias[i], od_w_out[i])
        x = x + conv_ffn(rms_norm(x, ffn_norm_w[layer]), ffn_w_up[layer], ffn_conv_w[layer],
                         ffn_conv_b[layer], ffn_w_down[layer])
    return x
```

```python
import functools
import math

import jax
import jax.numpy as jnp
from jax import lax
from jax.experimental import pallas as pl
from jax.experimental.pallas import tpu as pltpu

F32 = jnp.float32
BF16 = jnp.bfloat16
HIGHEST = lax.Precision.HIGHEST

D_MODEL = 1024
HEAD_DIM = 64
N_HEADS = 8
WIDTH = N_HEADS * HEAD_DIM
LANES = 128
CHUNK = 128
HALO = 8
EPS = 1e-6
SSD_CONV = 4
SSD_STATE = 64
RWKV_LN_EPS = 64e-5
RWKV_DECAY_SCALE = 0.606531
D_FF = 2816
FF_TILE = 256
FFN_HALO = 16
NEG = -1e30
VMEM_LIMIT = 56 * 1024 * 1024


def _mm(a, b):
    return jnp.dot(a.astype(BF16), b.astype(BF16), preferred_element_type=F32)


def _mm_nt(a, b):
    return lax.dot_general(a.astype(BF16), b.astype(BF16), (((1,), (1,)), ((), ())),
                           preferred_element_type=F32)


def _mm_tn(a, b):
    return lax.dot_general(a.astype(BF16), b.astype(BF16), (((0,), (0,)), ((), ())),
                           preferred_element_type=F32)


def _mm_hi(a, b):
    return jnp.dot(a, b, preferred_element_type=F32, precision=HIGHEST)


def _sigmoid(x):
    return 1.0 / (1.0 + jnp.exp(-x))


def _silu(x):
    return x * _sigmoid(x)


def _softplus(x):
    return jnp.maximum(x, 0.0) + jnp.log1p(jnp.exp(-jnp.abs(x)))


def _iota2(shape, dim):
    return lax.broadcasted_iota(jnp.int32, shape, dim)


def _head_mask(j, width=LANES):
    lane = _iota2((1, width), 1)
    return ((lane % LANES) // HEAD_DIM == j).astype(F32)


def _cparams(sem):
    return pltpu.CompilerParams(dimension_semantics=sem, vmem_limit_bytes=VMEM_LIMIT)


def _const_spec(shape):
    nd = len(shape)
    return pl.BlockSpec(shape, lambda *_: (0,) * nd)


def _inproj_kernel(x_ref, nw_ref, w_ref, *o_refs, segments):
    x = x_ref[...]
    ms = jnp.mean(x * x, axis=-1, keepdims=True)
    xn = (x * lax.rsqrt(ms + EPS) * nw_ref[...]).astype(BF16)
    for o_ref, (off, width) in zip(o_refs, segments):
        o_ref[...] = jnp.dot(xn, w_ref[:, off:off + width], preferred_element_type=F32)


def _inproj(x, norm_w, w, segments, tm=512):
    t = x.shape[0]
    n = w.shape[1]
    return pl.pallas_call(
        functools.partial(_inproj_kernel, segments=segments),
        grid=(t // tm,),
        in_specs=[pl.BlockSpec((tm, D_MODEL), lambda i: (i, 0)),
                  _const_spec((1, D_MODEL)),
                  _const_spec((D_MODEL, n))],
        out_specs=[pl.BlockSpec((tm, wd), lambda i: (i, 0)) for _, wd in segments],
        out_shape=[jax.ShapeDtypeStruct((t, wd), F32) for _, wd in segments],
        compiler_params=_cparams(("parallel",)),
        name="inproj",
    )(x, norm_w.reshape(1, D_MODEL), w)


def _outproj_kernel(ya_ref, yb_ref, w_ref, res_ref, o_ref):
    acc = jnp.dot(ya_ref[...], w_ref[:WIDTH, :], preferred_element_type=F32)
    acc = acc + jnp.dot(yb_ref[...], w_ref[WIDTH:, :], preferred_element_type=F32)
    o_ref[...] = res_ref[...] + acc


def _outproj(ya, yb, w, res, tm=512):
    t = res.shape[0]
    return pl.pallas_call(
        _outproj_kernel,
        grid=(t // tm,),
        in_specs=[pl.BlockSpec((tm, WIDTH), lambda i: (i, 0)),
                  pl.BlockSpec((tm, WIDTH), lambda i: (i, 0)),
                  _const_spec((2 * WIDTH, D_MODEL)),
                  pl.BlockSpec((tm, D_MODEL), lambda i: (i, 0))],
        out_specs=pl.BlockSpec((tm, D_MODEL), lambda i: (i, 0)),
        out_shape=jax.ShapeDtypeStruct((t, D_MODEL), F32),
        compiler_params=_cparams(("parallel",)),
        name="outproj",
    )(ya, yb, w, res)


def _ffn_kernel(x_ref, xh_ref, nw_ref, wg_ref, wu_ref, cw_ref, cb_ref, wd_ref, o_ref,
                xn_ref, g_ref, acc_ref, *, tm, tiles_per_seq):
    nw = nw_ref[...]

    def norm(v):
        ms = jnp.mean(v * v, axis=-1, keepdims=True)
        return v * lax.rsqrt(ms + EPS) * nw

    x = x_ref[...]
    keep = (pl.program_id(0) % tiles_per_seq != 0).astype(F32)
    xn_ref[:FFN_HALO, :] = (norm(xh_ref[...]) * keep).astype(BF16)
    xn_ref[FFN_HALO:, :] = norm(x).astype(BF16)
    acc_ref[...] = jnp.zeros_like(acc_ref)

    def body(f, carry):
        g_ref[...] = jnp.dot(xn_ref[...], wg_ref[f], preferred_element_type=F32)
        up = jnp.dot(xn_ref[FFN_HALO:, :], wu_ref[f], preferred_element_type=F32)
        cw = cw_ref[f]
        gate = (cb_ref[f]
                + cw[0:1, :] * g_ref[FFN_HALO - 2:FFN_HALO - 2 + tm, :]
                + cw[1:2, :] * g_ref[FFN_HALO - 1:FFN_HALO - 1 + tm, :]
                + cw[2:3, :] * g_ref[FFN_HALO:, :])
        act = (_silu(gate) * up).astype(BF16)
        acc_ref[...] += jnp.dot(act, wd_ref[f], preferred_element_type=F32)
        return carry

    lax.fori_loop(0, D_FF // FF_TILE, body, 0)
    o_ref[...] = x + acc_ref[...]


def _ffn(x, seq, norm_w, w_up, conv_w, conv_b, w_down, tm=512):
    t = x.shape[0]
    nf = D_FF // FF_TILE
    wg = w_up[:, :D_FF].reshape(D_MODEL, nf, FF_TILE).transpose(1, 0, 2).astype(BF16)
    wu = w_up[:, D_FF:].reshape(D_MODEL, nf, FF_TILE).transpose(1, 0, 2).astype(BF16)
    wd = w_down.reshape(nf, FF_TILE, D_MODEL).astype(BF16)
    cw = conv_w.reshape(3, nf, FF_TILE).transpose(1, 0, 2)
    cb = conv_b.reshape(nf, 1, FF_TILE)
    halo_per_tile = tm // FFN_HALO
    return pl.pallas_call(
        functools.partial(_ffn_kernel, tm=tm, tiles_per_seq=seq // tm),
        grid=(t // tm,),
        in_specs=[pl.BlockSpec((tm, D_MODEL), lambda i: (i, 0)),
                  pl.BlockSpec((FFN_HALO, D_MODEL),
                               lambda i: (jnp.maximum(i * halo_per_tile - 1, 0), 0)),
                  _const_spec((1, D_MODEL)),
                  _const_spec((nf, D_MODEL, FF_TILE)),
                  _const_spec((nf, D_MODEL, FF_TILE)),
                  _const_spec((nf, 3, FF_TILE)),
                  _const_spec((nf, 1, FF_TILE)),
                  _const_spec((nf, FF_TILE, D_MODEL))],
        out_specs=pl.BlockSpec((tm, D_MODEL), lambda i: (i, 0)),
        out_shape=jax.ShapeDtypeStruct((t, D_MODEL), F32),
        scratch_shapes=[pltpu.VMEM((tm + FFN_HALO, D_MODEL), BF16),
                        pltpu.VMEM((tm + FFN_HALO, FF_TILE), F32),
                        pltpu.VMEM((tm, D_MODEL), F32)],
        compiler_params=_cparams(("parallel",)),
        name="convffn",
    )(x, x, norm_w.reshape(1, D_MODEL), wg, wu, cw, cb, wd)


def _row_spec(width, nc):
    return pl.BlockSpec((CHUNK, width), lambda b, c: (b * nc + c, 0))


def _halo_spec(width, nc):
    per = CHUNK // HALO
    return pl.BlockSpec((HALO, width), lambda b, c: (jnp.maximum((b * nc + c) * per - 1, 0), 0))


def _param_spec2(shape):
    return pl.BlockSpec(shape, lambda b, c: (0,) * len(shape))


def _with_prev_rows(scr_ref, main_ref, halo_ref, keep):
    scr_ref[:HALO, :] = halo_ref[...] * keep
    scr_ref[HALO:, :] = main_ref[...]


def _ssd_kernel(z_ref, x_ref, xh_ref, bc_ref, bch_ref, dt_ref,
                cwx_ref, cbx_ref, cwb_ref, cbb_ref, dtb_ref, alog_ref, dexp_ref, nw_ref, e_ref,
                o_ref, xe_ref, bce_ref, st_ref, y_ref):
    c = pl.program_id(1)

    @pl.when(c == 0)
    def _():
        st_ref[...] = jnp.zeros_like(st_ref)

    keep = (c > 0).astype(F32)
    _with_prev_rows(xe_ref, x_ref, xh_ref, keep)
    _with_prev_rows(bce_ref, bc_ref, bch_ref, keep)

    def conv(ref, cw_ref, cb_ref):
        out = cb_ref[...]
        for k in range(SSD_CONV):
            lo = HALO - (SSD_CONV - 1) + k
            out = out + cw_ref[k:k + 1, :] * ref[lo:lo + CHUNK, :]
        return _silu(out)

    x = conv(xe_ref, cwx_ref, cbx_ref)
    bc = conv(bce_ref, cwb_ref, cbb_ref)
    bm = bc[:, :LANES]
    cm = bc[:, LANES:]

    row = _iota2((CHUNK, CHUNK), 0)
    col = _iota2((CHUNK, CHUNK), 1)
    causal = row >= col

    dt = _softplus(dt_ref[...] + dtb_ref[...])
    a = dt * (-jnp.exp(alog_ref[...]))
    a_cum = _mm_hi(causal.astype(F32), a)
    a_cum_t = a_cum.T
    a_last = a_cum[CHUNK - 1:CHUNK, :]
    per_head = jnp.concatenate(
        [dt, jnp.exp(a_cum), jnp.exp(a_last - a_cum),
         jnp.broadcast_to(jnp.exp(a_last), (HALO, LANES))], axis=0)
    per_lane = _mm_hi(per_head, e_ref[...])
    dt_e = per_lane[:CHUNK]
    ea_e = per_lane[CHUNK:2 * CHUNK]
    te_e = per_lane[2 * CHUNK:3 * CHUNK]
    cd_e = per_lane[3 * CHUNK:3 * CHUNK + 1]

    xdt = x * dt_e
    xs = xdt * te_e
    bm_t = bm.T
    group_rows = _iota2((LANES, 1), 0) // SSD_STATE

    for g in range(2):
        cg = cm * _head_mask(g)
        cb = _mm_nt(cg, bm)
        for pp in range(2):
            p = 2 * g + pp
            xp = xdt[:, p * LANES:(p + 1) * LANES]
            yp = jnp.zeros((CHUNK, LANES), F32)
            for j in range(2):
                h = 2 * p + j
                seg = a_cum[:, h:h + 1] - a_cum_t[h:h + 1, :]
                decay = jnp.where(causal, jnp.exp(jnp.minimum(seg, 0.0)), 0.0)
                yp = yp + _mm(cb * decay, xp * _head_mask(j))
            y_ref[:, p * LANES:(p + 1) * LANES] = yp
        gs = slice(g * 2 * LANES, (g + 1) * 2 * LANES)
        st = st_ref[g]
        y_ref[:, gs] += _mm(cg, st) * ea_e[:, gs]
        st_new = _mm(bm_t, xs[:, gs])
        st_ref[g] = st * cd_e[:, gs] + jnp.where(group_rows == g, st_new, 0.0)

    y = y_ref[...] + dexp_ref[...] * x
    gated = y * _silu(z_ref[...])
    ms = jnp.mean(gated * gated, axis=-1, keepdims=True)
    o_ref[...] = (gated * lax.rsqrt(ms + EPS) * nw_ref[...]).astype(BF16)


def _ssd(z, x, bc, dt, conv_w, conv_b, dt_bias, a_log, d_skip, norm_w, batch, seq):
    t = z.shape[0]
    nc = seq // CHUNK
    pad = LANES - N_HEADS
    e = jnp.repeat(jnp.eye(LANES, N_HEADS, dtype=F32), HEAD_DIM, axis=1)
    args = (z, x, x, bc, bc, dt,
            conv_w[:, :WIDTH], conv_b[:WIDTH].reshape(1, WIDTH),
            conv_w[:, WIDTH:], conv_b[WIDTH:].reshape(1, 2 * LANES),
            jnp.pad(dt_bias, (0, pad)).reshape(1, LANES),
            jnp.pad(a_log, (0, pad)).reshape(1, LANES),
            jnp.repeat(d_skip, HEAD_DIM).reshape(1, WIDTH),
            norm_w.reshape(1, WIDTH), e)
    return pl.pallas_call(
        _ssd_kernel,
        grid=(batch, nc),
        in_specs=[_row_spec(WIDTH, nc), _row_spec(WIDTH, nc), _halo_spec(WIDTH, nc),
                  _row_spec(2 * LANES, nc), _halo_spec(2 * LANES, nc), _row_spec(LANES, nc),
                  _param_spec2((SSD_CONV, WIDTH)), _param_spec2((1, WIDTH)),
                  _param_spec2((SSD_CONV, 2 * LANES)), _param_spec2((1, 2 * LANES)),
                  _param_spec2((1, LANES)), _param_spec2((1, LANES)),
                  _param_spec2((1, WIDTH)), _param_spec2((1, WIDTH)),
                  _param_spec2((LANES, WIDTH))],
        out_specs=_row_spec(WIDTH, nc),
        out_shape=jax.ShapeDtypeStruct((t, WIDTH), BF16),
        scratch_shapes=[pltpu.VMEM((CHUNK + HALO, WIDTH), F32),
                        pltpu.VMEM((CHUNK + HALO, 2 * LANES), F32),
                        pltpu.VMEM((2, LANES, 2 * LANES), F32),
                        pltpu.VMEM((CHUNK, WIDTH), F32)],
        compiler_params=_cparams(("parallel", "arbitrary")),
        name="ssd",
    )(*args)


def _ret_kernel(q_ref, k_ref, v_ref, g_ref, cos_ref, sin_ref, intra_ref, qdec_ref, kte_ref,
                cg_ref, bd_ref, nw_ref, o_ref, r_ref, y_ref):
    c = pl.program_id(1)

    @pl.when(c == 0)
    def _():
        r_ref[...] = jnp.zeros_like(r_ref)

    lane = _iota2((1, WIDTH), 1)
    first_half = (lane % HEAD_DIM) < (HEAD_DIM // 2)
    cos = cos_ref[...]
    sin = sin_ref[...]

    def rotary(v):
        other = jnp.where(first_half,
                          pltpu.roll(v, WIDTH - HEAD_DIM // 2, 1),
                          pltpu.roll(v, HEAD_DIM // 2, 1))
        return v * cos + other * sin

    q = rotary(q_ref[...])
    k = rotary(k_ref[...]) * (HEAD_DIM ** -0.5)
    v = v_ref[...]

    for p in range(N_HEADS // 2):
        ps = slice(p * LANES, (p + 1) * LANES)
        qp, kp, vp = q[:, ps], k[:, ps], v[:, ps]
        yp = jnp.zeros((CHUNK, LANES), F32)
        for j in range(2):
            mj = _head_mask(j)
            scores = _mm_nt(qp * mj, kp) * intra_ref[2 * p + j]
            yp = yp + _mm(scores, vp * mj)
        y_ref[:, ps] = yp

    bd = bd_ref[...]
    y = y_ref[...] + _mm(q * qdec_ref[...], r_ref[...])
    r_ref[...] = r_ref[...] * cg_ref[...] + bd * _mm_tn(k * kte_ref[...], v)
    ms = _mm(y * y, bd) * (1.0 / HEAD_DIM)
    o_ref[...] = (_silu(g_ref[...]) * (y * lax.rsqrt(ms + EPS) * nw_ref[...])).astype(BF16)


def _retention_tables(seq):
    half = HEAD_DIM // 2
    inv = 1.0 / (10000.0 ** (jnp.arange(half, dtype=F32) / half))
    ang = jnp.arange(seq, dtype=F32)[:, None] * inv[None, :]
    cos = jnp.tile(jnp.concatenate([jnp.cos(ang), jnp.cos(ang)], axis=-1), (1, N_HEADS))
    sin = jnp.tile(jnp.concatenate([-jnp.sin(ang), jnp.sin(ang)], axis=-1), (1, N_HEADS))
    log_gamma = jnp.log1p(-(2.0 ** (-5.0 - jnp.arange(N_HEADS, dtype=F32))))
    idx = jnp.arange(CHUNK, dtype=F32)
    rel = idx[:, None] - idx[None, :]
    intra = jnp.where(rel[None] >= 0,
                      jnp.exp(jnp.maximum(rel, 0.0)[None] * log_gamma[:, None, None]), 0.0)
    expand = lambda m: jnp.repeat(m, HEAD_DIM, axis=1)
    qdec = expand(jnp.exp((idx + 1.0)[:, None] * log_gamma[None, :]))
    kte = expand(jnp.exp((CHUNK - 1 - idx)[:, None] * log_gamma[None, :]))
    cgam = expand(jnp.exp(CHUNK * log_gamma)[None, :])
    head = jnp.arange(WIDTH) // HEAD_DIM
    bd = (head[:, None] == head[None, :]).astype(F32)
    return cos, sin, intra, qdec, kte, cgam, bd


def _retention(q, k, v, g, norm_w, batch, seq):
    t = q.shape[0]
    nc = seq // CHUNK
    cos, sin, intra, qdec, kte, cgam, bd = _retention_tables(seq)
    table_spec = pl.BlockSpec((CHUNK, WIDTH), lambda b, c: (c, 0))
    return pl.pallas_call(
        _ret_kernel,
        grid=(batch, nc),
        in_specs=[_row_spec(WIDTH, nc)] * 4 + [table_spec, table_spec,
                  _param_spec2((N_HEADS, CHUNK, CHUNK)),
                  _param_spec2((CHUNK, WIDTH)), _param_spec2((CHUNK, WIDTH)),
                  _param_spec2((1, WIDTH)), _param_spec2((WIDTH, WIDTH)),
                  _param_spec2((1, WIDTH))],
        out_specs=_row_spec(WIDTH, nc),
        out_shape=jax.ShapeDtypeStruct((t, WIDTH), BF16),
        scratch_shapes=[pltpu.VMEM((WIDTH, WIDTH), F32), pltpu.VMEM((CHUNK, WIDTH), F32)],
        compiler_params=_cparams(("parallel", "arbitrary")),
        name="retention",
    )(q, k, v, g, cos, sin, intra, qdec, kte, cgam, bd, norm_w.reshape(1, WIDTH))


def _rwkv_kernel(r_ref, rh_ref, k_ref, kh_ref, v_ref, vh_ref, lo_ref, loh_ref,
                 mur_ref, muk_ref, muv_ref, mulo_ref, w0_ref, wup_ref, a0_ref, aup_ref, gup_ref,
                 kkw_ref, kaw_ref, rkw_ref, lnw_ref, lnb_ref, bd_ref,
                 o_ref, s_ref, sr_ref, sk_ref, sv_ref, slo_ref):
    c = pl.program_id(2)

    @pl.when(c == 0)
    def _():
        s_ref[...] = jnp.zeros_like(s_ref)

    keep = (c > 0).astype(F32)

    def token_shift(main_ref, halo_ref, mu_ref, scr_ref):
        _with_prev_rows(scr_ref, main_ref, halo_ref, keep)
        cur = main_ref[...]
        prev = scr_ref[HALO - 1:HALO - 1 + CHUNK, :]
        return cur + (prev - cur) * mu_ref[...]

    r = token_shift(r_ref, rh_ref, mur_ref, sr_ref)
    k = token_shift(k_ref, kh_ref, muk_ref, sk_ref)
    v = token_shift(v_ref, vh_ref, muv_ref, sv_ref)
    lo = token_shift(lo_ref, loh_ref, mulo_ref, slo_ref)
    lo_wa = lo[:, :LANES]
    bd = bd_ref[...]

    log_w = -RWKV_DECAY_SCALE * _sigmoid(w0_ref[...] + _mm_hi(jnp.tanh(lo_wa), wup_ref[...]))
    a = _sigmoid(a0_ref[...] + _mm_hi(lo_wa, aup_ref[...]))
    gate = _mm(_sigmoid(lo[:, LANES:]), gup_ref[...])
    kk = k * kkw_ref[...]
    kk = kk / jnp.maximum(jnp.sqrt(_mm_hi(kk * kk, bd)), 1e-12)
    k = k * (1.0 + (a - 1.0) * kaw_ref[...])

    row = _iota2((CHUNK, CHUNK), 0)
    col = _iota2((CHUNK, CHUNK), 1)
    strict = row > col
    causal = row >= col
    eye = (row == col).astype(F32)

    cum = _mm_hi(causal.astype(F32), log_w)
    mid = cum[CHUNK // 2 - 1:CHUNK // 2, :]
    last = cum[CHUNK - 1:CHUNK, :]
    a_t = -kk * jnp.exp(cum - log_w - mid)
    r_t = r * jnp.exp(cum - mid)
    inv_p = jnp.exp(mid - cum)
    b_t = kk * a * inv_p
    k_t = k * inv_p
    e_mid = jnp.exp(mid)
    to_end = jnp.exp(last - mid)

    rhs = jnp.concatenate([b_t, k_t], axis=0)
    ap = jnp.zeros((CHUNK, LANES), F32)
    u0 = jnp.zeros((CHUNK, LANES), F32)
    rp = jnp.zeros((CHUNK, LANES), F32)
    y0 = jnp.zeros((CHUNK, LANES), F32)
    for j in range(2):
        mj = _head_mask(j)
        a_j = a_t * mj
        r_j = r_t * mj
        mm = _mm_nt(jnp.concatenate([a_j, r_j], axis=0), rhs)
        m_ab = jnp.where(strict, mm[:CHUNK, :CHUNK], 0.0)
        m_ak = jnp.where(strict, mm[:CHUNK, CHUNK:], 0.0)
        m_rb = jnp.where(causal, mm[CHUNK:, :CHUNK], 0.0)
        m_rk = jnp.where(causal, mm[CHUNK:, CHUNK:], 0.0)
        inv = eye + m_ab
        power = m_ab
        for _ in range(int(math.log2(CHUNK)) - 1):
            power = _mm(power, power)
            inv = inv + _mm(inv, power)
        mv = _mm(jnp.concatenate([m_ak, m_rk], axis=0), v * mj)
        z = _mm(inv, jnp.concatenate([a_j, mv[:CHUNK]], axis=1))
        w = _mm(m_rb, z)
        ap = ap + z[:, :LANES]
        u0 = u0 + z[:, LANES:]
        rp = rp + r_j + w[:, :LANES]
        y0 = y0 + w[:, LANES:] + mv[CHUNK:]

    s = s_ref[...]
    u = _mm_nt(ap * e_mid, s) + u0
    y = _mm_nt(rp * e_mid, s) + y0
    s_ref[...] = s * jnp.exp(last) + bd * (_mm_tn(u, b_t * to_end) + _mm_tn(v, k_t * to_end))

    mean = _mm_hi(y, bd) * (1.0 / HEAD_DIM)
    d = y - mean
    var = _mm_hi(d * d, bd) * (1.0 / HEAD_DIM)
    yn = d * lax.rsqrt(var + RWKV_LN_EPS) * lnw_ref[...] + lnb_ref[...]
    bonus = _mm_hi(r * k * rkw_ref[...], bd) * v
    o_ref[...] = ((yn + bonus) * gate).astype(BF16)


def _rwkv(r, k, v, lo, mu, w0, w_up, a0, a_up, g_up, k_k, k_a, r_k, ln_w, ln_b, batch, seq):
    t = r.shape[0]
    nc = seq // CHUNK
    per = CHUNK // HALO
    npairs = N_HEADS // 2
    rank = w_up.shape[0]

    def pair_row(col0):
        return pl.BlockSpec((CHUNK, LANES), lambda b, p, c: (b * nc + c, col0 + p))

    def pair_halo(col0):
        return pl.BlockSpec((HALO, LANES),
                            lambda b, p, c: (jnp.maximum((b * nc + c) * per - 1, 0), col0 + p))

    lo_row = pl.BlockSpec((CHUNK, 2 * LANES), lambda b, p, c: (b * nc + c, 0))
    lo_halo = pl.BlockSpec((HALO, 2 * LANES),
                           lambda b, p, c: (jnp.maximum((b * nc + c) * per - 1, 0), 0))
    pair_param = pl.BlockSpec((1, LANES), lambda b, p, c: (0, p))
    pair_weight = pl.BlockSpec((LANES, LANES), lambda b, p, c: (0, p))
    const = lambda shape: pl.BlockSpec(shape, lambda b, p, c: (0,) * len(shape))

    row1 = lambda a: a.reshape(1, -1)
    wup_pad = jnp.concatenate([w_up, jnp.zeros((LANES - rank, WIDTH), F32)], axis=0)
    aup_pad = jnp.concatenate([jnp.zeros((LANES - rank, WIDTH), F32), a_up], axis=0)
    head = jnp.arange(LANES) // HEAD_DIM
    bd = (head[:, None] == head[None, :]).astype(F32)
    args = (r, r, k, k, v, v, lo, lo,
            row1(mu[:WIDTH]), row1(mu[WIDTH:2 * WIDTH]), row1(mu[2 * WIDTH:3 * WIDTH]),
            row1(mu[3 * WIDTH:]),
            row1(w0), wup_pad, row1(a0), aup_pad, g_up.astype(BF16),
            row1(k_k), row1(k_a), row1(r_k), row1(ln_w), row1(ln_b), bd)
    return pl.pallas_call(
        _rwkv_kernel,
        grid=(batch, npairs, nc),
        in_specs=[pair_row(0), pair_halo(0), pair_row(0), pair_halo(0), pair_row(0), pair_halo(0),
                  lo_row, lo_halo,
                  pair_param, pair_param, pair_param, const((1, 2 * LANES)),
                  pair_param, pair_weight, pair_param, pair_weight, pair_weight,
                  pair_param, pair_param, pair_param, pair_param, pair_param,
                  const((LANES, LANES))],
        out_specs=pl.BlockSpec((CHUNK, LANES), lambda b, p, c: (b * nc + c, p)),
        out_shape=jax.ShapeDtypeStruct((t, WIDTH), BF16),
        scratch_shapes=[pltpu.VMEM((LANES, LANES), F32),
                        pltpu.VMEM((CHUNK + HALO, LANES), F32),
                        pltpu.VMEM((CHUNK + HALO, LANES), F32),
                        pltpu.VMEM((CHUNK + HALO, LANES), F32),
                        pltpu.VMEM((CHUNK + HALO, 2 * LANES), F32)],
        compiler_params=_cparams(("parallel", "parallel", "arbitrary")),
        name="rwkv7",
    )(*args)


FOX_PREP_ROWS = 256
FOX_TILE = 512


def _fox_prep_kernel(q_ref, k_ref, v_ref, f_ref, qw_ref, kw_ref, fb_ref, bd_ref,
                     qo_ref, ko_ref, vo_ref, carry_ref):
    @pl.when(pl.program_id(1) == 0)
    def _():
        carry_ref[...] = jnp.zeros_like(carry_ref)

    bd = bd_ref[...]

    def head_norm(x, w):
        ms = _mm(x * x, bd) * (1.0 / HEAD_DIM)
        return x * lax.rsqrt(ms + EPS) * w

    q = head_norm(q_ref[...], qw_ref[...]) * (HEAD_DIM ** -0.5)
    k = head_norm(k_ref[...], kw_ref[...])
    v = v_ref[...]

    f = f_ref[...] + fb_ref[...]
    log_f = jnp.minimum(f, 0.0) - jnp.log1p(jnp.exp(-jnp.abs(f)))
    rows = FOX_PREP_ROWS
    tri = (_iota2((rows, rows), 0) >= _iota2((rows, rows), 1)).astype(F32)
    cum = _mm_hi(tri, log_f) + carry_ref[0:1, :]
    carry_ref[...] = jnp.broadcast_to(cum[rows - 1:rows, :], carry_ref.shape)

    lane = _iota2((1, LANES), 1)
    src = _iota2((LANES, LANES), 0)
    dst = _iota2((LANES, LANES), 1)
    for h in range(N_HEADS):
        p, j = divmod(h, 2)
        ps = slice(p * LANES, (p + 1) * LANES)
        mj = _head_mask(j)
        qh, kh = q[:, ps] * mj, k[:, ps] * mj
        if j == 1:
            qh = pltpu.roll(qh, HEAD_DIM, 1)
            kh = pltpu.roll(kh, HEAD_DIM, 1)
        spread = ((src == h) & (dst >= HEAD_DIM) & (dst < HEAD_DIM + 6)).astype(F32)
        ch = _mm_hi(cum, spread)
        hi = ch.astype(BF16).astype(F32)
        mid = (ch - hi).astype(BF16).astype(F32)
        low = ch - hi - mid
        piece = lambda base: jnp.where(lane == base, hi,
                                       jnp.where(lane == base + 1, mid,
                                                 jnp.where(lane == base + 2, low, 0.0)))
        ones = lambda base: ((lane >= base) & (lane < base + 3)).astype(F32)
        qo_ref[0, h] = (qh + piece(HEAD_DIM) + ones(HEAD_DIM + 3)).astype(BF16)
        ko_ref[0, h] = (kh + ones(HEAD_DIM) - piece(HEAD_DIM + 3)).astype(BF16)
        vo_ref[0, h] = (v[:, ps] * mj).astype(BF16)


def _fox_kernel(qi_ref, ki_ref, q_ref, k_ref, v_ref, o_ref, m_ref, l_ref, acc_ref):
    t = pl.program_id(2)
    qi = qi_ref[t]
    ki = ki_ref[t]

    @pl.when(ki == 0)
    def _():
        m_ref[...] = jnp.full_like(m_ref, NEG)
        l_ref[...] = jnp.zeros_like(l_ref)
        acc_ref[...] = jnp.zeros_like(acc_ref)

    row = _iota2((FOX_TILE, FOX_TILE), 0)
    col = _iota2((FOX_TILE, FOX_TILE), 1)
    visible = (row - col + (qi - ki) * FOX_TILE) >= 0
    for j in range(2):
        s = lax.dot_general(q_ref[0, j], k_ref[0, j], (((1,), (1,)), ((), ())),
                            preferred_element_type=F32)
        s = jnp.where(visible, s, NEG)
        m_prev = m_ref[j]
        m_next = jnp.maximum(m_prev, jnp.max(s, axis=1, keepdims=True))
        alpha = jnp.exp(m_prev - m_next)
        p = jnp.exp(s - m_next[:, 0:1])
        l_ref[j] = alpha * l_ref[j] + jnp.sum(p, axis=1, keepdims=True)
        acc_ref[j] = alpha * acc_ref[j] + jnp.dot(p.astype(BF16), v_ref[0, j],
                                                  preferred_element_type=F32)
        m_ref[j] = m_next

    @pl.when(ki == qi)
    def _():
        o_ref[0] = (acc_ref[0] / l_ref[0] + acc_ref[1] / l_ref[1]).astype(BF16)


def _fox(q, k, v, f, q_norm_w, k_norm_w, f_bias, batch, seq):
    rows = FOX_PREP_ROWS
    nr = seq // rows
    head = jnp.arange(WIDTH) // HEAD_DIM
    bd = (head[:, None] == head[None, :]).astype(F32)
    row_spec = lambda width: pl.BlockSpec((rows, width), lambda b, i: (b * nr + i, 0))
    par = lambda shape: pl.BlockSpec(shape, lambda b, i: (0,) * len(shape))
    head_out = pl.BlockSpec((1, N_HEADS, rows, LANES), lambda b, i: (b, 0, i, 0))
    head_shape = jax.ShapeDtypeStruct((batch, N_HEADS, seq, LANES), BF16)
    qa, ka, va = pl.pallas_call(
        _fox_prep_kernel,
        grid=(batch, nr),
        in_specs=[row_spec(WIDTH), row_spec(WIDTH), row_spec(WIDTH), row_spec(LANES),
                  par((1, WIDTH)), par((1, WIDTH)), par((1, LANES)), par((WIDTH, WIDTH))],
        out_specs=[head_out, head_out, head_out],
        out_shape=[head_shape, head_shape, head_shape],
        scratch_shapes=[pltpu.VMEM((HALO, LANES), F32)],
        compiler_params=_cparams(("parallel", "arbitrary")),
        name="fox_prep",
    )(q, k, v, f, jnp.tile(q_norm_w, N_HEADS).reshape(1, WIDTH),
      jnp.tile(k_norm_w, N_HEADS).reshape(1, WIDTH),
      jnp.pad(f_bias, (0, LANES - N_HEADS)).reshape(1, LANES), bd)

    nt = seq // FOX_TILE
    pairs = [(i, j) for i in range(nt) for j in range(i + 1)]
    qi = jnp.asarray([i for i, _ in pairs], jnp.int32)
    ki = jnp.asarray([j for _, j in pairs], jnp.int32)
    q_spec = pl.BlockSpec((1, 2, FOX_TILE, LANES), lambda b, p, t, qi, ki: (b, p, qi[t], 0))
    kv_spec = pl.BlockSpec((1, 2, FOX_TILE, LANES), lambda b, p, t, qi, ki: (b, p, ki[t], 0))
    out = pl.pallas_call(
        _fox_kernel,
        grid_spec=pltpu.PrefetchScalarGridSpec(
            num_scalar_prefetch=2,
            grid=(batch, N_HEADS // 2, len(pairs)),
            in_specs=[q_spec, kv_spec, kv_spec],
            out_specs=pl.BlockSpec((1, FOX_TILE, LANES), lambda b, p, t, qi, ki: (b, qi[t], p)),
            scratch_shapes=[pltpu.VMEM((2, FOX_TILE, LANES), F32),
                            pltpu.VMEM((2, FOX_TILE, LANES), F32),
                            pltpu.VMEM((2, FOX_TILE, LANES), F32)]),
        out_shape=jax.ShapeDtypeStruct((batch, seq, WIDTH), BF16),
        compiler_params=_cparams(("parallel", "parallel", "arbitrary")),
        name="fox_attention",
    )(qi, ki, qa, ka, va)
    return out.reshape(batch * seq, WIDTH)


def _pad_cols(w, width):
    return jnp.pad(w, ((0, 0), (0, width - w.shape[1])))


def _even_layer(x, batch, seq, norm_w, w_in, conv_w, conv_b, dt_bias, a_log, d_skip,
                ssd_norm_w, ret_norm_w, w_out):
    ssd_x0 = WIDTH
    ssd_dt0 = WIDTH + (WIDTH + 4 * SSD_STATE)
    ret0 = ssd_dt0 + N_HEADS
    w = jnp.concatenate([w_in[:, :ssd_dt0], w_in[:, ret0:],
                         _pad_cols(w_in[:, ssd_dt0:ret0], LANES)], axis=1).astype(BF16)
    segments = ((0, WIDTH), (ssd_x0, WIDTH), (2 * WIDTH, 2 * LANES),
                (ssd_dt0, WIDTH), (ssd_dt0 + WIDTH, WIDTH), (ssd_dt0 + 2 * WIDTH, WIDTH),
                (ssd_dt0 + 3 * WIDTH, WIDTH), (ssd_dt0 + 4 * WIDTH, LANES))
    z, xs, bc, q, k, v, g, dt = _inproj(x, norm_w, w, segments)
    y_ssd = _ssd(z, xs, bc, dt, conv_w, conv_b, dt_bias, a_log, d_skip, ssd_norm_w, batch, seq)
    y_ret = _retention(q, k, v, g, ret_norm_w, batch, seq)
    return _outproj(y_ssd, y_ret, w_out.astype(BF16), x)


def _odd_layer(x, batch, seq, norm_w, w_in, mu, w0, w_up, a0, a_up, g_up, k_k, k_a, r_k,
               ln_w, ln_b, q_norm_w, k_norm_w, f_bias, w_out):
    lo_width = w_up.shape[0] + a_up.shape[0] + g_up.shape[0]
    fox0 = 3 * WIDTH + lo_width
    f0 = fox0 + 3 * WIDTH
    w = jnp.concatenate([w_in[:, :f0], _pad_cols(w_in[:, f0:], LANES)], axis=1).astype(BF16)
    segments = ((0, WIDTH), (WIDTH, WIDTH), (2 * WIDTH, WIDTH), (3 * WIDTH, lo_width),
                (fox0, WIDTH), (fox0 + WIDTH, WIDTH), (fox0 + 2 * WIDTH, WIDTH), (f0, LANES))
    r, k, v, lo, fq, fk, fv, ff = _inproj(x, norm_w, w, segments)
    y_rwkv = _rwkv(r, k, v, lo, mu, w0, w_up, a0, a_up, g_up, k_k, k_a, r_k.reshape(-1),
                   ln_w, ln_b, batch, seq)
    y_fox = _fox(fq, fk, fv, ff, q_norm_w, k_norm_w, f_bias, batch, seq)
    return _outproj(y_rwkv, y_fox, w_out.astype(BF16), x)


def kernel(x, ev_norm_w, ev_w_in, ev_ssd_conv_w, ev_ssd_conv_b, ev_ssd_dt_bias, ev_ssd_a_log,
           ev_ssd_d, ev_ssd_norm_w, ev_ret_norm_w, ev_w_out,
           od_norm_w, od_w_in, od_rwkv_mu, od_rwkv_w0, od_rwkv_w_up, od_rwkv_a0, od_rwkv_a_up,
           od_rwkv_g_up, od_rwkv_k_k, od_rwkv_k_a, od_rwkv_r_k, od_rwkv_ln_w, od_rwkv_ln_b,
           od_fox_q_norm_w, od_fox_k_norm_w, od_fox_f_bias, od_w_out,
           ffn_norm_w, ffn_w_up, ffn_conv_w, ffn_conv_b, ffn_w_down):
    batch, seq, _ = x.shape
    depth = ffn_norm_w.shape[0]
    h = x.reshape(batch * seq, D_MODEL)
    for layer in range(depth):
        i = layer // 2
        if layer % 2 == 0:
            h = _even_layer(h, batch, seq, ev_norm_w[i], ev_w_in[i], ev_ssd_conv_w[i],
                            ev_ssd_conv_b[i], ev_ssd_dt_bias[i], ev_ssd_a_log[i], ev_ssd_d[i],
                            ev_ssd_norm_w[i], ev_ret_norm_w[i], ev_w_out[i])
        else:
            h = _odd_layer(h, batch, seq, od_norm_w[i], od_w_in[i], od_rwkv_mu[i], od_rwkv_w0[i],
                           od_rwkv_w_up[i], od_rwkv_a0[i], od_rwkv_a_up[i], od_rwkv_g_up[i],
                           od_rwkv_k_k[i], od_rwkv_k_a[i], od_rwkv_r_k[i], od_rwkv_ln_w[i],
                           od_rwkv_ln_b[i], od_fox_q_norm_w[i], od_fox_k_norm_w[i],
                           od_fox_f_bias[i], od_w_out[i])
        h = _ffn(h, seq, ffn_norm_w[layer], ffn_w_up[layer], ffn_conv_w[layer],
                 ffn_conv_b[layer], ffn_w_down[layer])
    return h.reshape(batch, seq, D_MODEL)
```

```python
import functools
import math

import jax
import jax.numpy as jnp
from jax import lax
from jax.experimental import pallas as pl
from jax.experimental.pallas import tpu as pltpu

F32 = jnp.float32
BF16 = jnp.bfloat16
HIGHEST = lax.Precision.HIGHEST

D_MODEL = 1024
HEAD_DIM = 64
N_HEADS = 8
WIDTH = N_HEADS * HEAD_DIM
LANES = 128
CHUNK = 128
HALO = 8
EPS = 1e-6
SSD_CONV = 4
SSD_STATE = 64
RWKV_LN_EPS = 64e-5
RWKV_DECAY_SCALE = 0.606531
D_FF = 2816
FF_TILE = 256
FFN_HALO = 16
NEG = -1e30
VMEM_LIMIT = 56 * 1024 * 1024


def _mm(a, b):
    return jnp.dot(a.astype(BF16), b.astype(BF16), preferred_element_type=F32)


def _mm_nt(a, b):
    return lax.dot_general(a.astype(BF16), b.astype(BF16), (((1,), (1,)), ((), ())),
                           preferred_element_type=F32)


def _mm_tn(a, b):
    return lax.dot_general(a.astype(BF16), b.astype(BF16), (((0,), (0,)), ((), ())),
                           preferred_element_type=F32)


def _mm_hi(a, b):
    return jnp.dot(a, b, preferred_element_type=F32, precision=HIGHEST)


def _mm_split(lhs01, x, pieces):
    acc = None
    rest = x
    for _ in range(pieces):
        part = rest.astype(BF16)
        term = jnp.dot(lhs01, part, preferred_element_type=F32)
        acc = term if acc is None else acc + term
        rest = rest - part.astype(F32)
    return acc


def _sigmoid(x):
    return 1.0 / (1.0 + jnp.exp(-x))


def _silu(x):
    return x * _sigmoid(x)


def _softplus(x):
    return jnp.maximum(x, 0.0) + jnp.log1p(jnp.exp(-jnp.abs(x)))


def _iota2(shape, dim):
    return lax.broadcasted_iota(jnp.int32, shape, dim)


def _head_mask(j, width=LANES):
    lane = _iota2((1, width), 1)
    return ((lane % LANES) // HEAD_DIM == j).astype(F32)


def _cparams(sem):
    return pltpu.CompilerParams(dimension_semantics=sem, vmem_limit_bytes=VMEM_LIMIT)


def _const_spec(shape):
    nd = len(shape)
    return pl.BlockSpec(shape, lambda *_: (0,) * nd)


def _inproj_kernel(x_ref, nw_ref, w_ref, *o_refs, segments):
    x = x_ref[...]
    ms = jnp.mean(x * x, axis=-1, keepdims=True)
    xn = (x * lax.rsqrt(ms + EPS) * nw_ref[...]).astype(BF16)
    for o_ref, (off, width) in zip(o_refs, segments):
        o_ref[...] = jnp.dot(xn, w_ref[:, off:off + width], preferred_element_type=F32)


def _inproj(x, norm_w, w, segments, tm=512):
    t = x.shape[0]
    n = w.shape[1]
    return pl.pallas_call(
        functools.partial(_inproj_kernel, segments=segments),
        grid=(t // tm,),
        in_specs=[pl.BlockSpec((tm, D_MODEL), lambda i: (i, 0)),
                  _const_spec((1, D_MODEL)),
                  _const_spec((D_MODEL, n))],
        out_specs=[pl.BlockSpec((tm, wd), lambda i: (i, 0)) for _, wd in segments],
        out_shape=[jax.ShapeDtypeStruct((t, wd), F32) for _, wd in segments],
        compiler_params=_cparams(("parallel",)),
        name="inproj",
    )(x, norm_w.reshape(1, D_MODEL), w)


def _outproj_kernel(ya_ref, yb_ref, w_ref, res_ref, o_ref):
    acc = jnp.dot(ya_ref[...], w_ref[:WIDTH, :], preferred_element_type=F32)
    acc = acc + jnp.dot(yb_ref[...], w_ref[WIDTH:, :], preferred_element_type=F32)
    o_ref[...] = res_ref[...] + acc


def _outproj(ya, yb, w, res, tm=512):
    t = res.shape[0]
    return pl.pallas_call(
        _outproj_kernel,
        grid=(t // tm,),
        in_specs=[pl.BlockSpec((tm, WIDTH), lambda i: (i, 0)),
                  pl.BlockSpec((tm, WIDTH), lambda i: (i, 0)),
                  _const_spec((2 * WIDTH, D_MODEL)),
                  pl.BlockSpec((tm, D_MODEL), lambda i: (i, 0))],
        out_specs=pl.BlockSpec((tm, D_MODEL), lambda i: (i, 0)),
        out_shape=jax.ShapeDtypeStruct((t, D_MODEL), F32),
        compiler_params=_cparams(("parallel",)),
        name="outproj",
    )(ya, yb, w, res)


def _ffn_kernel(x_ref, xh_ref, nw_ref, wg_ref, wu_ref, cw_ref, cb_ref, wd_ref, o_ref,
                xn_ref, g_ref, acc_ref, *, tm, tiles_per_seq):
    nw = nw_ref[...]

    def norm(v):
        ms = jnp.mean(v * v, axis=-1, keepdims=True)
        return v * lax.rsqrt(ms + EPS) * nw

    x = x_ref[...]
    keep = (pl.program_id(0) % tiles_per_seq != 0).astype(F32)
    xn_ref[:FFN_HALO, :] = (norm(xh_ref[...]) * keep).astype(BF16)
    xn_ref[FFN_HALO:, :] = norm(x).astype(BF16)
    acc_ref[...] = jnp.zeros_like(acc_ref)

    def body(f, carry):
        g_ref[...] = jnp.dot(xn_ref[...], wg_ref[f], preferred_element_type=F32)
        up = jnp.dot(xn_ref[FFN_HALO:, :], wu_ref[f], preferred_element_type=F32)
        cw = cw_ref[f]
        gate = (cb_ref[f]
                + cw[0:1, :] * g_ref[FFN_HALO - 2:FFN_HALO - 2 + tm, :]
                + cw[1:2, :] * g_ref[FFN_HALO - 1:FFN_HALO - 1 + tm, :]
                + cw[2:3, :] * g_ref[FFN_HALO:, :])
        act = (_silu(gate) * up).astype(BF16)
        acc_ref[...] += jnp.dot(act, wd_ref[f], preferred_element_type=F32)
        return carry

    lax.fori_loop(0, D_FF // FF_TILE, body, 0, unroll=True)
    o_ref[...] = x + acc_ref[...]


def _ffn(x, seq, norm_w, w_up, conv_w, conv_b, w_down, tm=512):
    t = x.shape[0]
    nf = D_FF // FF_TILE
    wg = w_up[:, :D_FF].reshape(D_MODEL, nf, FF_TILE).transpose(1, 0, 2).astype(BF16)
    wu = w_up[:, D_FF:].reshape(D_MODEL, nf, FF_TILE).transpose(1, 0, 2).astype(BF16)
    wd = w_down.reshape(nf, FF_TILE, D_MODEL).astype(BF16)
    cw = conv_w.reshape(3, nf, FF_TILE).transpose(1, 0, 2)
    cb = conv_b.reshape(nf, 1, FF_TILE)
    halo_per_tile = tm // FFN_HALO
    return pl.pallas_call(
        functools.partial(_ffn_kernel, tm=tm, tiles_per_seq=seq // tm),
        grid=(t // tm,),
        in_specs=[pl.BlockSpec((tm, D_MODEL), lambda i: (i, 0)),
                  pl.BlockSpec((FFN_HALO, D_MODEL),
                               lambda i: (jnp.maximum(i * halo_per_tile - 1, 0), 0)),
                  _const_spec((1, D_MODEL)),
                  _const_spec((nf, D_MODEL, FF_TILE)),
                  _const_spec((nf, D_MODEL, FF_TILE)),
                  _const_spec((nf, 3, FF_TILE)),
                  _const_spec((nf, 1, FF_TILE)),
                  _const_spec((nf, FF_TILE, D_MODEL))],
        out_specs=pl.BlockSpec((tm, D_MODEL), lambda i: (i, 0)),
        out_shape=jax.ShapeDtypeStruct((t, D_MODEL), F32),
        scratch_shapes=[pltpu.VMEM((tm + FFN_HALO, D_MODEL), BF16),
                        pltpu.VMEM((tm + FFN_HALO, FF_TILE), F32),
                        pltpu.VMEM((tm, D_MODEL), F32)],
        compiler_params=_cparams(("parallel",)),
        name="convffn",
    )(x, x, norm_w.reshape(1, D_MODEL), wg, wu, cw, cb, wd)


def _row_spec(width, nc):
    return pl.BlockSpec((CHUNK, width), lambda b, c: (b * nc + c, 0))


def _halo_spec(width, nc):
    per = CHUNK // HALO
    return pl.BlockSpec((HALO, width), lambda b, c: (jnp.maximum((b * nc + c) * per - 1, 0), 0))


def _param_spec2(shape):
    return pl.BlockSpec(shape, lambda b, c: (0,) * len(shape))


def _with_prev_rows(scr_ref, main_ref, halo_ref, keep):
    scr_ref[:HALO, :] = halo_ref[...] * keep
    scr_ref[HALO:, :] = main_ref[...]


def _ssd_kernel(z_ref, x_ref, xh_ref, bc_ref, bch_ref, dt_ref,
                cwx_ref, cbx_ref, cwb_ref, cbb_ref, dtb_ref, alog_ref, dexp_ref, nw_ref, e_ref,
                o_ref, xe_ref, bce_ref, st_ref, y_ref):
    c = pl.program_id(1)

    @pl.when(c == 0)
    def _():
        st_ref[...] = jnp.zeros_like(st_ref)

    keep = (c > 0).astype(F32)
    _with_prev_rows(xe_ref, x_ref, xh_ref, keep)
    _with_prev_rows(bce_ref, bc_ref, bch_ref, keep)

    def conv(ref, cw_ref, cb_ref):
        out = cb_ref[...]
        for k in range(SSD_CONV):
            lo = HALO - (SSD_CONV - 1) + k
            out = out + cw_ref[k:k + 1, :] * ref[lo:lo + CHUNK, :]
        return _silu(out)

    x = conv(xe_ref, cwx_ref, cbx_ref)
    bc = conv(bce_ref, cwb_ref, cbb_ref)
    bm = bc[:, :LANES]
    cm = bc[:, LANES:]

    row = _iota2((CHUNK, CHUNK), 0)
    col = _iota2((CHUNK, CHUNK), 1)
    causal = row >= col

    dt = _softplus(dt_ref[...] + dtb_ref[...])
    a = dt * (-jnp.exp(alog_ref[...]))
    a_cum = _mm_hi(causal.astype(F32), a)
    a_cum_t = a_cum.T
    a_last = a_cum[CHUNK - 1:CHUNK, :]
    per_head = jnp.concatenate(
        [dt, jnp.exp(a_cum), jnp.exp(a_last - a_cum),
         jnp.broadcast_to(jnp.exp(a_last), (HALO, LANES))], axis=0)
    per_lane = _mm_hi(per_head, e_ref[...])
    dt_e = per_lane[:CHUNK]
    ea_e = per_lane[CHUNK:2 * CHUNK]
    te_e = per_lane[2 * CHUNK:3 * CHUNK]
    cd_e = per_lane[3 * CHUNK:3 * CHUNK + 1]

    xdt = x * dt_e
    xs = xdt * te_e
    bm_t = bm.T
    group_rows = _iota2((LANES, 1), 0) // SSD_STATE

    for g in range(2):
        cg = cm * _head_mask(g)
        cb = _mm_nt(cg, bm)
        for pp in range(2):
            p = 2 * g + pp
            xp = xdt[:, p * LANES:(p + 1) * LANES]
            yp = jnp.zeros((CHUNK, LANES), F32)
            for j in range(2):
                h = 2 * p + j
                seg = a_cum[:, h:h + 1] - a_cum_t[h:h + 1, :]
                decay = jnp.where(causal, jnp.exp(jnp.minimum(seg, 0.0)), 0.0)
                yp = yp + _mm(cb * decay, xp * _head_mask(j))
            y_ref[:, p * LANES:(p + 1) * LANES] = yp
        gs = slice(g * 2 * LANES, (g + 1) * 2 * LANES)
        st = st_ref[g]
        y_ref[:, gs] += _mm(cg, st) * ea_e[:, gs]
        st_new = _mm(bm_t, xs[:, gs])
        st_ref[g] = st * cd_e[:, gs] + jnp.where(group_rows == g, st_new, 0.0)

    y = y_ref[...] + dexp_ref[...] * x
    gated = y * _silu(z_ref[...])
    ms = jnp.mean(gated * gated, axis=-1, keepdims=True)
    o_ref[...] = (gated * lax.rsqrt(ms + EPS) * nw_ref[...]).astype(BF16)


def _ssd(z, x, bc, dt, conv_w, conv_b, dt_bias, a_log, d_skip, norm_w, batch, seq):
    t = z.shape[0]
    nc = seq // CHUNK
    pad = LANES - N_HEADS
    e = jnp.repeat(jnp.eye(LANES, N_HEADS, dtype=F32), HEAD_DIM, axis=1)
    args = (z, x, x, bc, bc, dt,
            conv_w[:, :WIDTH], conv_b[:WIDTH].reshape(1, WIDTH),
            conv_w[:, WIDTH:], conv_b[WIDTH:].reshape(1, 2 * LANES),
            jnp.pad(dt_bias, (0, pad)).reshape(1, LANES),
            jnp.pad(a_log, (0, pad)).reshape(1, LANES),
            jnp.repeat(d_skip, HEAD_DIM).reshape(1, WIDTH),
            norm_w.reshape(1, WIDTH), e)
    return pl.pallas_call(
        _ssd_kernel,
        grid=(batch, nc),
        in_specs=[_row_spec(WIDTH, nc), _row_spec(WIDTH, nc), _halo_spec(WIDTH, nc),
                  _row_spec(2 * LANES, nc), _halo_spec(2 * LANES, nc), _row_spec(LANES, nc),
                  _param_spec2((SSD_CONV, WIDTH)), _param_spec2((1, WIDTH)),
                  _param_spec2((SSD_CONV, 2 * LANES)), _param_spec2((1, 2 * LANES)),
                  _param_spec2((1, LANES)), _param_spec2((1, LANES)),
                  _param_spec2((1, WIDTH)), _param_spec2((1, WIDTH)),
                  _param_spec2((LANES, WIDTH))],
        out_specs=_row_spec(WIDTH, nc),
        out_shape=jax.ShapeDtypeStruct((t, WIDTH), BF16),
        scratch_shapes=[pltpu.VMEM((CHUNK + HALO, WIDTH), F32),
                        pltpu.VMEM((CHUNK + HALO, 2 * LANES), F32),
                        pltpu.VMEM((2, LANES, 2 * LANES), F32),
                        pltpu.VMEM((CHUNK, WIDTH), F32)],
        compiler_params=_cparams(("parallel", "arbitrary")),
        name="ssd",
    )(*args)


def _ret_kernel(q_ref, k_ref, v_ref, g_ref, cos_ref, sin_ref, intra_ref, qdec_ref, kte_ref,
                cg_ref, bd_ref, nw_ref, o_ref, r_ref, y_ref):
    c = pl.program_id(1)

    @pl.when(c == 0)
    def _():
        r_ref[...] = jnp.zeros_like(r_ref)

    lane = _iota2((1, WIDTH), 1)
    first_half = (lane % HEAD_DIM) < (HEAD_DIM // 2)
    cos = cos_ref[...]
    sin = sin_ref[...]

    def rotary(v):
        other = jnp.where(first_half,
                          pltpu.roll(v, WIDTH - HEAD_DIM // 2, 1),
                          pltpu.roll(v, HEAD_DIM // 2, 1))
        return v * cos + other * sin

    q = rotary(q_ref[...])
    k = rotary(k_ref[...]) * (HEAD_DIM ** -0.5)
    v = v_ref[...]

    for p in range(N_HEADS // 2):
        ps = slice(p * LANES, (p + 1) * LANES)
        qp, kp, vp = q[:, ps], k[:, ps], v[:, ps]
        yp = jnp.zeros((CHUNK, LANES), F32)
        for j in range(2):
            mj = _head_mask(j)
            scores = _mm_nt(qp * mj, kp) * intra_ref[2 * p + j]
            yp = yp + _mm(scores, vp * mj)
        y_ref[:, ps] = yp

    bd = bd_ref[...]
    y = y_ref[...] + _mm(q * qdec_ref[...], r_ref[...])
    r_ref[...] = r_ref[...] * cg_ref[...] + bd * _mm_tn(k * kte_ref[...], v)
    ms = _mm(y * y, bd) * (1.0 / HEAD_DIM)
    o_ref[...] = (_silu(g_ref[...]) * (y * lax.rsqrt(ms + EPS) * nw_ref[...])).astype(BF16)


def _retention_tables(seq):
    half = HEAD_DIM // 2
    inv = 1.0 / (10000.0 ** (jnp.arange(half, dtype=F32) / half))
    ang = jnp.arange(seq, dtype=F32)[:, None] * inv[None, :]
    cos = jnp.tile(jnp.concatenate([jnp.cos(ang), jnp.cos(ang)], axis=-1), (1, N_HEADS))
    sin = jnp.tile(jnp.concatenate([-jnp.sin(ang), jnp.sin(ang)], axis=-1), (1, N_HEADS))
    log_gamma = jnp.log1p(-(2.0 ** (-5.0 - jnp.arange(N_HEADS, dtype=F32))))
    idx = jnp.arange(CHUNK, dtype=F32)
    rel = idx[:, None] - idx[None, :]
    intra = jnp.where(rel[None] >= 0,
                      jnp.exp(jnp.maximum(rel, 0.0)[None] * log_gamma[:, None, None]), 0.0)
    expand = lambda m: jnp.repeat(m, HEAD_DIM, axis=1)
    qdec = expand(jnp.exp((idx + 1.0)[:, None] * log_gamma[None, :]))
    kte = expand(jnp.exp((CHUNK - 1 - idx)[:, None] * log_gamma[None, :]))
    cgam = expand(jnp.exp(CHUNK * log_gamma)[None, :])
    head = jnp.arange(WIDTH) // HEAD_DIM
    bd = (head[:, None] == head[None, :]).astype(F32)
    return cos, sin, intra, qdec, kte, cgam, bd


def _retention(q, k, v, g, norm_w, batch, seq):
    t = q.shape[0]
    nc = seq // CHUNK
    cos, sin, intra, qdec, kte, cgam, bd = _retention_tables(seq)
    table_spec = pl.BlockSpec((CHUNK, WIDTH), lambda b, c: (c, 0))
    return pl.pallas_call(
        _ret_kernel,
        grid=(batch, nc),
        in_specs=[_row_spec(WIDTH, nc)] * 4 + [table_spec, table_spec,
                  _param_spec2((N_HEADS, CHUNK, CHUNK)),
                  _param_spec2((CHUNK, WIDTH)), _param_spec2((CHUNK, WIDTH)),
                  _param_spec2((1, WIDTH)), _param_spec2((WIDTH, WIDTH)),
                  _param_spec2((1, WIDTH))],
        out_specs=_row_spec(WIDTH, nc),
        out_shape=jax.ShapeDtypeStruct((t, WIDTH), BF16),
        scratch_shapes=[pltpu.VMEM((WIDTH, WIDTH), F32), pltpu.VMEM((CHUNK, WIDTH), F32)],
        compiler_params=_cparams(("parallel", "arbitrary")),
        name="retention",
    )(q, k, v, g, cos, sin, intra, qdec, kte, cgam, bd, norm_w.reshape(1, WIDTH))


def _rwkv_kernel(r_ref, rh_ref, k_ref, kh_ref, v_ref, vh_ref, lo_ref, loh_ref,
                 mur_ref, muk_ref, muv_ref, mulo_ref, w0_ref, wup_ref, a0_ref, aup_ref, gup_ref,
                 kkw_ref, kaw_ref, rkw_ref, lnw_ref, lnb_ref, bd_ref,
                 o_ref, s_ref, sr_ref, sk_ref, sv_ref, slo_ref):
    c = pl.program_id(1)

    @pl.when(c == 0)
    def _():
        s_ref[...] = jnp.zeros_like(s_ref)

    keep = (c > 0).astype(F32)
    nb, _, width = r_ref.shape
    npair = width // LANES
    bd = bd_ref[...]
    row = _iota2((CHUNK, CHUNK), 0)
    col = _iota2((CHUNK, CHUNK), 1)
    strict = row > col
    causal = row >= col
    eye = (row == col).astype(F32)
    tri = causal.astype(BF16)
    same_head = (row // HEAD_DIM == col // HEAD_DIM).astype(F32)
    m1 = [_head_mask(j) for j in range(2)]
    m2 = [_head_mask(j, 2 * LANES) for j in range(2)]
    cat0 = lambda xs: jnp.concatenate(xs, axis=0)
    cat1 = lambda xs: jnp.concatenate(xs, axis=1)
    C = CHUNK

    def token_shift(b, main_ref, halo_ref, mu_ref, scr_ref):
        _with_prev_rows(scr_ref.at[b], main_ref.at[b], halo_ref.at[b], keep)
        cur = main_ref[b]
        prev = scr_ref[b, HALO - 1:HALO - 1 + CHUNK, :]
        return cur + (prev - cur) * mu_ref[...]

    pre = []
    for b in range(nb):
        r = token_shift(b, r_ref, rh_ref, mur_ref, sr_ref)
        k = token_shift(b, k_ref, kh_ref, muk_ref, sk_ref)
        v = token_shift(b, v_ref, vh_ref, muv_ref, sv_ref)
        lo = token_shift(b, lo_ref, loh_ref, mulo_ref, slo_ref)
        lo_wa = lo[:, :LANES]
        log_w = -RWKV_DECAY_SCALE * _sigmoid(w0_ref[...] + _mm(jnp.tanh(lo_wa), wup_ref[...]))
        a = _sigmoid(a0_ref[...] + _mm(lo_wa, aup_ref[...]))
        gate = _mm(_sigmoid(lo[:, LANES:]), gup_ref[...])
        kk = k * kkw_ref[...]
        kk = kk / jnp.maximum(jnp.sqrt(_mm(kk * kk, bd)), 1e-12)
        k = k * (1.0 + (a - 1.0) * kaw_ref[...])
        cum = _mm_split(tri, log_w, 2)
        mid = cum[C // 2 - 1:C // 2, :]
        last = cum[C - 1:C, :]
        inv_p = jnp.exp(mid - cum)
        to_end = jnp.exp(last - mid)
        pre.append(dict(
            r=r, k=k, v=v, gate=gate,
            a_t=-kk * jnp.exp(cum - log_w - mid), r_t=r * jnp.exp(cum - mid),
            b_t=kk * a * inv_p, k_t=k * inv_p, e_mid=jnp.exp(mid),
            b_end=kk * a * inv_p * to_end, k_end=k * inv_p * to_end, decay=jnp.exp(last)))

    units = [(b, q) for b in range(nb) for q in range(npair)]
    slab = lambda name: [pre[b][name][:, q * LANES:(q + 1) * LANES] for b, q in units]
    a_t, r_t, b_t, k_t, v_u = slab("a_t"), slab("r_t"), slab("b_t"), slab("k_t"), slab("v")
    e_mid, b_end, k_end, decay = slab("e_mid"), slab("b_end"), slab("k_end"), slab("decay")

    mm = [_mm_nt(cat0([a * m1[0], r * m1[0], a * m1[1], r * m1[1]]), cat0([bt, kt]))
          for a, r, bt, kt in zip(a_t, r_t, b_t, k_t)]
    chains = [(u, j) for u in range(len(units)) for j in range(2)]
    m_ab = [jnp.where(strict, mm[u][2 * j * C:(2 * j + 1) * C, :C], 0.0) for u, j in chains]
    m_ak = [jnp.where(strict, mm[u][2 * j * C:(2 * j + 1) * C, C:], 0.0) for u, j in chains]
    m_rb = [jnp.where(causal, mm[u][(2 * j + 1) * C:(2 * j + 2) * C, :C], 0.0) for u, j in chains]
    m_rk = [jnp.where(causal, mm[u][(2 * j + 1) * C:(2 * j + 2) * C, C:], 0.0) for u, j in chains]

    inv = [eye + m for m in m_ab]
    power = list(m_ab)
    for _ in range(int(math.log2(C)) - 1):
        power = [_mm(p, p) for p in power]
        inv = [i + _mm(i, p) for i, p in zip(inv, power)]

    n_units = range(len(units))
    per_head = lambda x2: x2[:C] * m2[0] + x2[C:] * m2[1]
    mv = [_mm(cat0([m_ak[2 * u], m_rk[2 * u], m_ak[2 * u + 1], m_rk[2 * u + 1]]), v_u[u])
          for u in n_units]
    mv_ak = [mv[u][:C] * m1[0] + mv[u][2 * C:3 * C] * m1[1] for u in n_units]
    mv_rk = [mv[u][C:2 * C] * m1[0] + mv[u][3 * C:] * m1[1] for u in n_units]
    z = [per_head(_mm(cat0([inv[2 * u], inv[2 * u + 1]]), cat1([a_t[u], mv_ak[u]])))
         for u in n_units]
    w = [per_head(_mm(cat0([m_rb[2 * u], m_rb[2 * u + 1]]), z[u])) for u in n_units]

    s = [s_ref[b, q] for b, q in units]
    uy = [_mm_nt(cat0([z[u][:, :LANES] * e_mid[u], (r_t[u] + w[u][:, :LANES]) * e_mid[u]]), s[u])
          for u in n_units]
    u_in = [uy[u][:C] + z[u][:, LANES:] for u in n_units]
    y = [uy[u][C:] + w[u][:, LANES:] + mv_rk[u] for u in n_units]
    s_inc = [_mm_tn(cat0([u_in[u], v_u[u]]), cat0([b_end[u], k_end[u]])) for u in n_units]
    for u, (b, q) in enumerate(units):
        s_ref[b, q] = s[u] * decay[u] + same_head * s_inc[u]

    for b in range(nb):
        yb = cat1([y[b * npair + q] for q in range(npair)])
        mean = _mm(yb, bd) * (1.0 / HEAD_DIM)
        d = yb - mean
        var = _mm(d * d, bd) * (1.0 / HEAD_DIM)
        yn = d * lax.rsqrt(var + RWKV_LN_EPS) * lnw_ref[...] + lnb_ref[...]
        bonus = _mm(pre[b]["r"] * pre[b]["k"] * rkw_ref[...], bd) * pre[b]["v"]
        o_ref[b] = ((yn + bonus) * pre[b]["gate"]).astype(BF16)


RWKV_PAIRS_PER_STEP = 2


def _rwkv(r, k, v, lo, mu, w0, w_up, a0, a_up, g_up, k_k, k_a, r_k, ln_w, ln_b, batch, seq):
    nc = seq // CHUNK
    per = CHUNK // HALO
    width = RWKV_PAIRS_PER_STEP * LANES
    lo_width = lo.shape[-1]
    rank = w_up.shape[0]
    prev_block = lambda c: jnp.maximum(c * per - 1, 0)

    slab_row = pl.BlockSpec((batch, CHUNK, width), lambda p, c: (0, c, p))
    slab_halo = pl.BlockSpec((batch, HALO, width), lambda p, c: (0, prev_block(c), p))
    lo_row = pl.BlockSpec((batch, CHUNK, lo_width), lambda p, c: (0, c, 0))
    lo_halo = pl.BlockSpec((batch, HALO, lo_width), lambda p, c: (0, prev_block(c), 0))
    slab_param = pl.BlockSpec((1, width), lambda p, c: (0, p))
    slab_weight = pl.BlockSpec((LANES, width), lambda p, c: (0, p))
    const = lambda shape: pl.BlockSpec(shape, lambda p, c: (0,) * len(shape))

    row1 = lambda a: a.reshape(1, -1)
    seq3 = lambda a: a.reshape(batch, seq, a.shape[-1])
    wup_pad = jnp.concatenate([w_up, jnp.zeros((LANES - rank, WIDTH), F32)], axis=0).astype(BF16)
    aup_pad = jnp.concatenate([jnp.zeros((LANES - rank, WIDTH), F32), a_up], axis=0).astype(BF16)
    head = jnp.arange(width) // HEAD_DIM
    bd = (head[:, None] == head[None, :]).astype(BF16)
    r, k, v, lo = seq3(r), seq3(k), seq3(v), seq3(lo)
    args = (r, r, k, k, v, v, lo, lo,
            row1(mu[:WIDTH]), row1(mu[WIDTH:2 * WIDTH]), row1(mu[2 * WIDTH:3 * WIDTH]),
            row1(mu[3 * WIDTH:]),
            row1(w0), wup_pad, row1(a0), aup_pad, g_up.astype(BF16),
            row1(k_k), row1(k_a), row1(r_k), row1(ln_w), row1(ln_b), bd)
    out = pl.pallas_call(
        _rwkv_kernel,
        grid=(WIDTH // width, nc),
        in_specs=[slab_row, slab_halo, slab_row, slab_halo, slab_row, slab_halo, lo_row, lo_halo,
                  slab_param, slab_param, slab_param, const((1, lo_width)),
                  slab_param, slab_weight, slab_param, slab_weight, slab_weight,
                  slab_param, slab_param, slab_param, slab_param, slab_param,
                  const((width, width))],
        out_specs=pl.BlockSpec((batch, CHUNK, width), lambda p, c: (0, c, p)),
        out_shape=jax.ShapeDtypeStruct((batch, seq, WIDTH), BF16),
        scratch_shapes=[pltpu.VMEM((batch, RWKV_PAIRS_PER_STEP, LANES, LANES), F32),
                        pltpu.VMEM((batch, CHUNK + HALO, width), F32),
                        pltpu.VMEM((batch, CHUNK + HALO, width), F32),
                        pltpu.VMEM((batch, CHUNK + HALO, width), F32),
                        pltpu.VMEM((batch, CHUNK + HALO, lo_width), F32)],
        compiler_params=_cparams(("parallel", "arbitrary")),
        name="rwkv7",
    )(*args)
    return out.reshape(batch * seq, WIDTH)


FOX_PREP_ROWS = 256
FOX_TILE = 1024
LOG2E = math.log2(math.e)


def _fox_prep_kernel(q_ref, k_ref, v_ref, f_ref, qw_ref, kw_ref, fb_ref, bd_ref,
                     qo_ref, ko_ref, vo_ref, carry_ref):
    @pl.when(pl.program_id(1) == 0)
    def _():
        carry_ref[...] = jnp.zeros_like(carry_ref)

    bd = bd_ref[...]

    def head_norm(x, w):
        ms = _mm(x * x, bd) * (1.0 / HEAD_DIM)
        return x * lax.rsqrt(ms + EPS) * w

    q = head_norm(q_ref[...], qw_ref[...]) * (HEAD_DIM ** -0.5 * LOG2E)
    k = head_norm(k_ref[...], kw_ref[...])
    v = v_ref[...]

    f = f_ref[...] + fb_ref[...]
    log_f = jnp.minimum(f, 0.0) - jnp.log1p(jnp.exp(-jnp.abs(f)))
    rows = FOX_PREP_ROWS
    tri = (_iota2((rows, rows), 0) >= _iota2((rows, rows), 1)).astype(F32)
    cum = _mm_hi(tri, log_f) + carry_ref[0:1, :]
    carry_ref[...] = jnp.broadcast_to(cum[rows - 1:rows, :], carry_ref.shape)
    cum = cum * LOG2E

    lane = _iota2((1, LANES), 1)
    src = _iota2((LANES, LANES), 0)
    dst = _iota2((LANES, LANES), 1)
    for h in range(N_HEADS):
        p, j = divmod(h, 2)
        ps = slice(p * LANES, (p + 1) * LANES)
        mj = _head_mask(j)
        qh, kh = q[:, ps] * mj, k[:, ps] * mj
        if j == 1:
            qh = pltpu.roll(qh, HEAD_DIM, 1)
            kh = pltpu.roll(kh, HEAD_DIM, 1)
        spread = ((src == h) & (dst >= HEAD_DIM) & (dst < HEAD_DIM + 6)).astype(F32)
        ch = _mm_hi(cum, spread)
        hi = ch.astype(BF16).astype(F32)
        mid = (ch - hi).astype(BF16).astype(F32)
        low = ch - hi - mid
        piece = lambda base: jnp.where(lane == base, hi,
                                       jnp.where(lane == base + 1, mid,
                                                 jnp.where(lane == base + 2, low, 0.0)))
        ones = lambda base: ((lane >= base) & (lane < base + 3)).astype(F32)
        qo_ref[0, h] = (qh + piece(HEAD_DIM) + ones(HEAD_DIM + 3)).astype(BF16)
        ko_ref[0, h] = (kh + ones(HEAD_DIM) - piece(HEAD_DIM + 3)).astype(BF16)
        vo_ref[0, h] = (v[:, ps] * mj).astype(BF16)


def _fox_kernel(qi_ref, ki_ref, q_ref, k_ref, v_ref, o_ref, m_ref, l_ref, acc_ref):
    t = pl.program_id(2)
    qi = qi_ref[t]
    ki = ki_ref[t]

    @pl.when(ki == 0)
    def _():
        m_ref[...] = jnp.full_like(m_ref, NEG)
        l_ref[...] = jnp.zeros_like(l_ref)
        acc_ref[...] = jnp.zeros_like(acc_ref)

    def accumulate(on_diagonal):
        scores = [lax.dot_general(q_ref[0, j], k_ref[0, j], (((1,), (1,)), ((), ())),
                                  preferred_element_type=F32) for j in range(2)]
        for j, s in enumerate(scores):
            if on_diagonal:
                visible = _iota2((FOX_TILE, FOX_TILE), 0) >= _iota2((FOX_TILE, FOX_TILE), 1)
                s = jnp.where(visible, s, NEG)
            m_prev = m_ref[j]
            m_next = jnp.maximum(m_prev, jnp.max(s, axis=1, keepdims=True))
            alpha = jnp.exp2(m_prev - m_next)
            p = jnp.exp2(s - m_next[:, 0:1])
            l_ref[j] = alpha * l_ref[j] + jnp.sum(p, axis=1, keepdims=True)
            acc_ref[j] = alpha * acc_ref[j] + jnp.dot(p.astype(BF16), v_ref[0, j],
                                                      preferred_element_type=F32)
            m_ref[j] = m_next

    @pl.when(ki < qi)
    def _():
        accumulate(False)

    @pl.when(ki == qi)
    def _():
        accumulate(True)
        o_ref[0] = (acc_ref[0] / l_ref[0] + acc_ref[1] / l_ref[1]).astype(BF16)


def _fox(q, k, v, f, q_norm_w, k_norm_w, f_bias, batch, seq):
    rows = FOX_PREP_ROWS
    nr = seq // rows
    head = jnp.arange(WIDTH) // HEAD_DIM
    bd = (head[:, None] == head[None, :]).astype(F32)
    row_spec = lambda width: pl.BlockSpec((rows, width), lambda b, i: (b * nr + i, 0))
    par = lambda shape: pl.BlockSpec(shape, lambda b, i: (0,) * len(shape))
    head_out = pl.BlockSpec((1, N_HEADS, rows, LANES), lambda b, i: (b, 0, i, 0))
    head_shape = jax.ShapeDtypeStruct((batch, N_HEADS, seq, LANES), BF16)
    qa, ka, va = pl.pallas_call(
        _fox_prep_kernel,
        grid=(batch, nr),
        in_specs=[row_spec(WIDTH), row_spec(WIDTH), row_spec(WIDTH), row_spec(LANES),
                  par((1, WIDTH)), par((1, WIDTH)), par((1, LANES)), par((WIDTH, WIDTH))],
        out_specs=[head_out, head_out, head_out],
        out_shape=[head_shape, head_shape, head_shape],
        scratch_shapes=[pltpu.VMEM((HALO, LANES), F32)],
        compiler_params=_cparams(("parallel", "arbitrary")),
        name="fox_prep",
    )(q, k, v, f, jnp.tile(q_norm_w, N_HEADS).reshape(1, WIDTH),
      jnp.tile(k_norm_w, N_HEADS).reshape(1, WIDTH),
      jnp.pad(f_bias, (0, LANES - N_HEADS)).reshape(1, LANES), bd)

    nt = seq // FOX_TILE
    pairs = [(i, j) for i in range(nt) for j in range(i + 1)]
    qi = jnp.asarray([i for i, _ in pairs], jnp.int32)
    ki = jnp.asarray([j for _, j in pairs], jnp.int32)
    q_spec = pl.BlockSpec((1, 2, FOX_TILE, LANES), lambda b, p, t, qi, ki: (b, p, qi[t], 0))
    kv_spec = pl.BlockSpec((1, 2, FOX_TILE, LANES), lambda b, p, t, qi, ki: (b, p, ki[t], 0))
    out = pl.pallas_call(
        _fox_kernel,
        grid_spec=pltpu.PrefetchScalarGridSpec(
            num_scalar_prefetch=2,
            grid=(batch, N_HEADS // 2, len(pairs)),
            in_specs=[q_spec, kv_spec, kv_spec],
            out_specs=pl.BlockSpec((1, FOX_TILE, LANES), lambda b, p, t, qi, ki: (b, qi[t], p)),
            scratch_shapes=[pltpu.VMEM((2, FOX_TILE, LANES), F32),
                            pltpu.VMEM((2, FOX_TILE, LANES), F32),
                            pltpu.VMEM((2, FOX_TILE, LANES), F32)]),
        out_shape=jax.ShapeDtypeStruct((batch, seq, WIDTH), BF16),
        compiler_params=_cparams(("parallel", "parallel", "arbitrary")),
        name="fox_attention",
    )(qi, ki, qa, ka, va)
    return out.reshape(batch * seq, WIDTH)


def _pad_cols(w, width):
    return jnp.pad(w, ((0, 0), (0, width - w.shape[1])))


def _even_layer(x, batch, seq, norm_w, w_in, conv_w, conv_b, dt_bias, a_log, d_skip,
                ssd_norm_w, ret_norm_w, w_out):
    ssd_x0 = WIDTH
    ssd_dt0 = WIDTH + (WIDTH + 4 * SSD_STATE)
    ret0 = ssd_dt0 + N_HEADS
    w = jnp.concatenate([w_in[:, :ssd_dt0], w_in[:, ret0:],
                         _pad_cols(w_in[:, ssd_dt0:ret0], LANES)], axis=1).astype(BF16)
    segments = ((0, WIDTH), (ssd_x0, WIDTH), (2 * WIDTH, 2 * LANES),
                (ssd_dt0, WIDTH), (ssd_dt0 + WIDTH, WIDTH), (ssd_dt0 + 2 * WIDTH, WIDTH),
                (ssd_dt0 + 3 * WIDTH, WIDTH), (ssd_dt0 + 4 * WIDTH, LANES))
    z, xs, bc, q, k, v, g, dt = _inproj(x, norm_w, w, segments)
    y_ssd = _ssd(z, xs, bc, dt, conv_w, conv_b, dt_bias, a_log, d_skip, ssd_norm_w, batch, seq)
    y_ret = _retention(q, k, v, g, ret_norm_w, batch, seq)
    return _outproj(y_ssd, y_ret, w_out.astype(BF16), x)


def _odd_layer(x, batch, seq, norm_w, w_in, mu, w0, w_up, a0, a_up, g_up, k_k, k_a, r_k,
               ln_w, ln_b, q_norm_w, k_norm_w, f_bias, w_out):
    lo_width = w_up.shape[0] + a_up.shape[0] + g_up.shape[0]
    fox0 = 3 * WIDTH + lo_width
    f0 = fox0 + 3 * WIDTH
    w = jnp.concatenate([w_in[:, :f0], _pad_cols(w_in[:, f0:], LANES)], axis=1).astype(BF16)
    segments = ((0, WIDTH), (WIDTH, WIDTH), (2 * WIDTH, WIDTH), (3 * WIDTH, lo_width),
                (fox0, WIDTH), (fox0 + WIDTH, WIDTH), (fox0 + 2 * WIDTH, WIDTH), (f0, LANES))
    r, k, v, lo, fq, fk, fv, ff = _inproj(x, norm_w, w, segments)
    y_rwkv = _rwkv(r, k, v, lo, mu, w0, w_up, a0, a_up, g_up, k_k, k_a, r_k.reshape(-1),
                   ln_w, ln_b, batch, seq)
    y_fox = _fox(fq, fk, fv, ff, q_norm_w, k_norm_w, f_bias, batch, seq)
    return _outproj(y_rwkv, y_fox, w_out.astype(BF16), x)


def kernel(x, ev_norm_w, ev_w_in, ev_ssd_conv_w, ev_ssd_conv_b, ev_ssd_dt_bias, ev_ssd_a_log,
           ev_ssd_d, ev_ssd_norm_w, ev_ret_norm_w, ev_w_out,
           od_norm_w, od_w_in, od_rwkv_mu, od_rwkv_w0, od_rwkv_w_up, od_rwkv_a0, od_rwkv_a_up,
           od_rwkv_g_up, od_rwkv_k_k, od_rwkv_k_a, od_rwkv_r_k, od_rwkv_ln_w, od_rwkv_ln_b,
           od_fox_q_norm_w, od_fox_k_norm_w, od_fox_f_bias, od_w_out,
           ffn_norm_w, ffn_w_up, ffn_conv_w, ffn_conv_b, ffn_w_down):
    batch, seq, _ = x.shape
    depth = ffn_norm_w.shape[0]
    h = x.reshape(batch * seq, D_MODEL)
    for layer in range(depth):
        i = layer // 2
        if layer % 2 == 0:
            h = _even_layer(h, batch, seq, ev_norm_w[i], ev_w_in[i], ev_ssd_conv_w[i],
                            ev_ssd_conv_b[i], ev_ssd_dt_bias[i], ev_ssd_a_log[i], ev_ssd_d[i],
                            ev_ssd_norm_w[i], ev_ret_norm_w[i], ev_w_out[i])
        else:
            h = _odd_layer(h, batch, seq, od_norm_w[i], od_w_in[i], od_rwkv_mu[i], od_rwkv_w0[i],
                           od_rwkv_w_up[i], od_rwkv_a0[i], od_rwkv_a_up[i], od_rwkv_g_up[i],
                           od_rwkv_k_k[i], od_rwkv_k_a[i], od_rwkv_r_k[i], od_rwkv_ln_w[i],
                           od_rwkv_ln_b[i], od_fox_q_norm_w[i], od_fox_k_norm_w[i],
                           od_fox_f_bias[i], od_w_out[i])
        h = _ffn(h, seq, ffn_norm_w[layer], ffn_w_up[layer], ffn_conv_w[layer],
                 ffn_conv_b[layer], ffn_w_down[layer])
    return h.reshape(batch, seq, D_MODEL)
```

```python
import functools
import itertools
import math

import jax
import jax.numpy as jnp
from jax import lax
from jax.experimental import pallas as pl
from jax.experimental.pallas import tpu as pltpu

F32 = jnp.float32
BF16 = jnp.bfloat16

D_MODEL = 1024
HEAD_DIM = 64
N_HEADS = 8
WIDTH = N_HEADS * HEAD_DIM
LANES = 128
CHUNK = 128
HALO = 8
EPS = 1e-6
SSD_CONV = 4
SSD_STATE = 64
RWKV_LN_EPS = 64e-5
RWKV_DECAY_SCALE = 0.606531
D_FF = 2816
FF_TILE = 256
FFN_HALO = 16
NEG = -1e30
VMEM_LIMIT = 56 * 1024 * 1024


def _mm(a, b):
    return jnp.dot(a.astype(BF16), b.astype(BF16), preferred_element_type=F32)


def _mm_nt(a, b):
    return lax.dot_general(a.astype(BF16), b.astype(BF16), (((1,), (1,)), ((), ())),
                           preferred_element_type=F32)


def _mm_tn(a, b):
    return lax.dot_general(a.astype(BF16), b.astype(BF16), (((0,), (0,)), ((), ())),
                           preferred_element_type=F32)


def _mm_split(lhs01, x, pieces):
    acc = None
    rest = x
    for _ in range(pieces):
        part = rest.astype(BF16)
        term = jnp.dot(lhs01, part, preferred_element_type=F32)
        acc = term if acc is None else acc + term
        rest = rest - part.astype(F32)
    return acc


def _mm_split_rhs(x, rhs01, pieces):
    acc = None
    rest = x
    for _ in range(pieces):
        part = rest.astype(BF16)
        term = jnp.dot(part, rhs01, preferred_element_type=F32)
        acc = term if acc is None else acc + term
        rest = rest - part.astype(F32)
    return acc


def _interleave(stage_generators):
    for _ in itertools.zip_longest(*stage_generators):
        pass


def _sigmoid(x):
    return 1.0 / (1.0 + jnp.exp(-x))


def _silu(x):
    return x * _sigmoid(x)


def _softplus(x):
    return jnp.maximum(x, 0.0) + jnp.log1p(jnp.exp(-jnp.abs(x)))


def _iota2(shape, dim):
    return lax.broadcasted_iota(jnp.int32, shape, dim)


def _head_mask(j, width=LANES):
    lane = _iota2((1, width), 1)
    return ((lane % LANES) // HEAD_DIM == j).astype(F32)


def _cparams(sem):
    return pltpu.CompilerParams(dimension_semantics=sem, vmem_limit_bytes=VMEM_LIMIT)


def _const_spec(shape):
    nd = len(shape)
    return pl.BlockSpec(shape, lambda *_: (0,) * nd, pipeline_mode=pl.Buffered(1))


def _inproj_kernel(x_ref, nw_ref, w_ref, *o_refs, segments):
    x = x_ref[...]
    ms = jnp.mean(x * x, axis=-1, keepdims=True)
    xn = (x * lax.rsqrt(ms + EPS) * nw_ref[...]).astype(BF16)
    for o_ref, (off, width) in zip(o_refs, segments):
        o_ref[...] = jnp.dot(xn, w_ref[:, off:off + width], preferred_element_type=F32)


def _inproj(x, norm_w, w, segments, tm=512):
    t = x.shape[0]
    n = w.shape[1]
    return pl.pallas_call(
        functools.partial(_inproj_kernel, segments=segments),
        grid=(t // tm,),
        in_specs=[pl.BlockSpec((tm, D_MODEL), lambda i: (i, 0)),
                  _const_spec((1, D_MODEL)),
                  _const_spec((D_MODEL, n))],
        out_specs=[pl.BlockSpec((tm, wd), lambda i: (i, 0)) for _, wd in segments],
        out_shape=[jax.ShapeDtypeStruct((t, wd), F32) for _, wd in segments],
        compiler_params=_cparams(("parallel",)),
        name="inproj",
    )(x, norm_w.reshape(1, D_MODEL), w)


def _outproj_kernel(ya_ref, yb_ref, w_ref, res_ref, o_ref):
    acc = jnp.dot(ya_ref[...], w_ref[:WIDTH, :], preferred_element_type=F32)
    acc = acc + jnp.dot(yb_ref[...], w_ref[WIDTH:, :], preferred_element_type=F32)
    o_ref[...] = res_ref[...] + acc


def _outproj(ya, yb, w, res, tm=512):
    t = res.shape[0]
    return pl.pallas_call(
        _outproj_kernel,
        grid=(t // tm,),
        in_specs=[pl.BlockSpec((tm, WIDTH), lambda i: (i, 0)),
                  pl.BlockSpec((tm, WIDTH), lambda i: (i, 0)),
                  _const_spec((2 * WIDTH, D_MODEL)),
                  pl.BlockSpec((tm, D_MODEL), lambda i: (i, 0))],
        out_specs=pl.BlockSpec((tm, D_MODEL), lambda i: (i, 0)),
        out_shape=jax.ShapeDtypeStruct((t, D_MODEL), F32),
        compiler_params=_cparams(("parallel",)),
        name="outproj",
    )(ya, yb, w, res)


def _ffn_kernel(x_ref, xh_ref, nw_ref, wg_ref, wu_ref, cw_ref, cb_ref, wd_ref, o_ref,
                xn_ref, g_ref, acc_ref, *, tm, tiles_per_seq):
    nw = nw_ref[...]

    def norm(v):
        ms = jnp.mean(v * v, axis=-1, keepdims=True)
        return v * lax.rsqrt(ms + EPS) * nw

    x = x_ref[...]
    keep = (pl.program_id(0) % tiles_per_seq != 0).astype(F32)
    xn_ref[:FFN_HALO, :] = (norm(xh_ref[...]) * keep).astype(BF16)
    xn_ref[FFN_HALO:, :] = norm(x).astype(BF16)
    acc_ref[...] = jnp.zeros_like(acc_ref)

    def up_proj(f):
        g_ref[f % 2] = jnp.dot(xn_ref[...], wg_ref[f], preferred_element_type=F32)
        return jnp.dot(xn_ref[FFN_HALO:, :], wu_ref[f], preferred_element_type=F32)

    nf = D_FF // FF_TILE
    up = up_proj(0)
    for f in range(nf):
        up_next = up_proj(f + 1) if f + 1 < nf else None
        g = g_ref.at[f % 2]
        cw = cw_ref[f]
        gate = (cb_ref[f]
                + cw[0:1, :] * g[FFN_HALO - 2:FFN_HALO - 2 + tm, :]
                + cw[1:2, :] * g[FFN_HALO - 1:FFN_HALO - 1 + tm, :]
                + cw[2:3, :] * g[FFN_HALO:, :])
        act = (_silu(gate) * up).astype(BF16)
        acc_ref[...] += jnp.dot(act, wd_ref[f], preferred_element_type=F32)
        up = up_next
    o_ref[...] = x + acc_ref[...]


def _ffn(x, seq, norm_w, w_up, conv_w, conv_b, w_down, tm=512):
    t = x.shape[0]
    nf = D_FF // FF_TILE
    wg = w_up[:, :D_FF].reshape(D_MODEL, nf, FF_TILE).transpose(1, 0, 2).astype(BF16)
    wu = w_up[:, D_FF:].reshape(D_MODEL, nf, FF_TILE).transpose(1, 0, 2).astype(BF16)
    wd = w_down.reshape(nf, FF_TILE, D_MODEL).astype(BF16)
    cw = conv_w.reshape(3, nf, FF_TILE).transpose(1, 0, 2)
    cb = conv_b.reshape(nf, 1, FF_TILE)
    halo_per_tile = tm // FFN_HALO
    return pl.pallas_call(
        functools.partial(_ffn_kernel, tm=tm, tiles_per_seq=seq // tm),
        grid=(t // tm,),
        in_specs=[pl.BlockSpec((tm, D_MODEL), lambda i: (i, 0)),
                  pl.BlockSpec((FFN_HALO, D_MODEL),
                               lambda i: (jnp.maximum(i * halo_per_tile - 1, 0), 0)),
                  _const_spec((1, D_MODEL)),
                  _const_spec((nf, D_MODEL, FF_TILE)),
                  _const_spec((nf, D_MODEL, FF_TILE)),
                  _const_spec((nf, 3, FF_TILE)),
                  _const_spec((nf, 1, FF_TILE)),
                  _const_spec((nf, FF_TILE, D_MODEL))],
        out_specs=pl.BlockSpec((tm, D_MODEL), lambda i: (i, 0)),
        out_shape=jax.ShapeDtypeStruct((t, D_MODEL), F32),
        scratch_shapes=[pltpu.VMEM((tm + FFN_HALO, D_MODEL), BF16),
                        pltpu.VMEM((2, tm + FFN_HALO, FF_TILE), F32),
                        pltpu.VMEM((tm, D_MODEL), F32)],
        compiler_params=_cparams(("parallel",)),
        name="convffn",
    )(x, x, norm_w.reshape(1, D_MODEL), wg, wu, cw, cb, wd)


def _seq_spec(batch, width):
    return pl.BlockSpec((batch, CHUNK, width), lambda c: (0, c, 0))


def _seq_halo_spec(batch, width):
    per = CHUNK // HALO
    return pl.BlockSpec((batch, HALO, width), lambda c: (0, jnp.maximum(c * per - 1, 0), 0))


def _param_spec1(shape):
    return pl.BlockSpec(shape, lambda c: (0,) * len(shape), pipeline_mode=pl.Buffered(1))


def _with_prev_rows(scr_ref, main_ref, halo_ref, keep):
    scr_ref[:HALO, :] = halo_ref[...] * keep
    scr_ref[HALO:, :] = main_ref[...]


def _ssd_chunk(b, keep, z_ref, x_ref, xh_ref, bc_ref, bch_ref, dt_ref,
               cwx_ref, cbx_ref, cwb_ref, cbb_ref, dtb_ref, alog_ref, dexp_ref, nw_ref, e_ref,
               o_ref, xe_ref, bce_ref, st_ref, y_ref):
    _with_prev_rows(xe_ref.at[b], x_ref.at[b], xh_ref.at[b], keep)
    _with_prev_rows(bce_ref.at[b], bc_ref.at[b], bch_ref.at[b], keep)

    def conv(ref, cw_ref, cb_ref):
        out = cb_ref[...]
        for k in range(SSD_CONV):
            lo = HALO - (SSD_CONV - 1) + k
            out = out + cw_ref[k:k + 1, :] * ref[b, lo:lo + CHUNK, :]
        return _silu(out)

    x = conv(xe_ref, cwx_ref, cbx_ref)
    bc = conv(bce_ref, cwb_ref, cbb_ref)
    bm = bc[:, :LANES]
    cm = bc[:, LANES:]

    row = _iota2((CHUNK, CHUNK), 0)
    col = _iota2((CHUNK, CHUNK), 1)
    causal = row >= col

    dt = _softplus(dt_ref[b] + dtb_ref[...])
    a = dt * (-jnp.exp(alog_ref[...]))
    a_cum = _mm_split(causal.astype(BF16), a, 3)
    yield
    a_cum_t = a_cum.T
    a_last = a_cum[CHUNK - 1:CHUNK, :]
    per_head = jnp.concatenate(
        [dt, jnp.exp(a_cum), jnp.exp(a_last - a_cum),
         jnp.broadcast_to(jnp.exp(a_last), (HALO, LANES))], axis=0)
    per_lane = _mm_split_rhs(per_head, e_ref[...], 2)
    dt_e = per_lane[:CHUNK]
    ea_e = per_lane[CHUNK:2 * CHUNK]
    te_e = per_lane[2 * CHUNK:3 * CHUNK]
    cd_e = per_lane[3 * CHUNK:3 * CHUNK + 1]
    yield

    xdt = x * dt_e
    xs = xdt * te_e
    bm_t = bm.T
    group_rows = _iota2((LANES, 1), 0) // SSD_STATE

    for g in range(2):
        cg = cm * _head_mask(g)
        cb = _mm_nt(cg, bm)
        gs = slice(g * 2 * LANES, (g + 1) * 2 * LANES)
        st = st_ref[b, g]
        y_off = _mm(cg, st) * ea_e[:, gs]
        st_new = _mm(bm_t, xs[:, gs])
        st_ref[b, g] = st * cd_e[:, gs] + jnp.where(group_rows == g, st_new, 0.0)
        yield
        for pp in range(2):
            p = 2 * g + pp
            xp = xdt[:, p * LANES:(p + 1) * LANES]
            yp = y_off[:, pp * LANES:(pp + 1) * LANES]
            for j in range(2):
                h = 2 * p + j
                seg = a_cum[:, h:h + 1] - a_cum_t[h:h + 1, :]
                decay = jnp.where(causal, jnp.exp(jnp.minimum(seg, 0.0)), 0.0)
                yp = yp + _mm(cb * decay, xp * _head_mask(j))
            y_ref[b, :, p * LANES:(p + 1) * LANES] = yp
            yield

    y = y_ref[b] + dexp_ref[...] * x
    gated = y * _silu(z_ref[b])
    ms = jnp.mean(gated * gated, axis=-1, keepdims=True)
    o_ref[b] = (gated * lax.rsqrt(ms + EPS) * nw_ref[...]).astype(BF16)


def _ssd_kernel(*refs):
    c = pl.program_id(0)
    st_ref = refs[-2]

    @pl.when(c == 0)
    def _():
        st_ref[...] = jnp.zeros_like(st_ref)

    keep = (c > 0).astype(F32)
    _interleave([_ssd_chunk(b, keep, *refs) for b in range(st_ref.shape[0])])


def _ssd(z, x, bc, dt, conv_w, conv_b, dt_bias, a_log, d_skip, norm_w, batch, seq):
    nc = seq // CHUNK
    pad = LANES - N_HEADS
    e = jnp.repeat(jnp.eye(LANES, N_HEADS, dtype=BF16), HEAD_DIM, axis=1)
    seq3 = lambda a: a.reshape(batch, seq, a.shape[-1])
    z, x, bc, dt = seq3(z), seq3(x), seq3(bc), seq3(dt)
    args = (z, x, x, bc, bc, dt,
            conv_w[:, :WIDTH], conv_b[:WIDTH].reshape(1, WIDTH),
            conv_w[:, WIDTH:], conv_b[WIDTH:].reshape(1, 2 * LANES),
            jnp.pad(dt_bias, (0, pad)).reshape(1, LANES),
            jnp.pad(a_log, (0, pad)).reshape(1, LANES),
            jnp.repeat(d_skip, HEAD_DIM).reshape(1, WIDTH),
            norm_w.reshape(1, WIDTH), e)
    out = pl.pallas_call(
        _ssd_kernel,
        grid=(nc,),
        in_specs=[_seq_spec(batch, WIDTH), _seq_spec(batch, WIDTH), _seq_halo_spec(batch, WIDTH),
                  _seq_spec(batch, 2 * LANES), _seq_halo_spec(batch, 2 * LANES),
                  _seq_spec(batch, LANES),
                  _param_spec1((SSD_CONV, WIDTH)), _param_spec1((1, WIDTH)),
                  _param_spec1((SSD_CONV, 2 * LANES)), _param_spec1((1, 2 * LANES)),
                  _param_spec1((1, LANES)), _param_spec1((1, LANES)),
                  _param_spec1((1, WIDTH)), _param_spec1((1, WIDTH)),
                  _param_spec1((LANES, WIDTH))],
        out_specs=_seq_spec(batch, WIDTH),
        out_shape=jax.ShapeDtypeStruct((batch, seq, WIDTH), BF16),
        scratch_shapes=[pltpu.VMEM((batch, CHUNK + HALO, WIDTH), F32),
                        pltpu.VMEM((batch, CHUNK + HALO, 2 * LANES), F32),
                        pltpu.VMEM((batch, 2, LANES, 2 * LANES), F32),
                        pltpu.VMEM((batch, CHUNK, WIDTH), F32)],
        compiler_params=_cparams(("arbitrary",)),
        name="ssd",
    )(*args)
    return out.reshape(batch * seq, WIDTH)


def _ret_chunk(b, q_ref, k_ref, v_ref, g_ref, cos_ref, sin_ref, intra_ref, qdec_ref, kte_ref,
               cg_ref, bd_ref, nw_ref, o_ref, r_ref):
    lane = _iota2((1, WIDTH), 1)
    first_half = (lane % HEAD_DIM) < (HEAD_DIM // 2)
    cos = cos_ref[...]
    sin = sin_ref[...]

    def rotary(v):
        other = jnp.where(first_half,
                          pltpu.roll(v, WIDTH - HEAD_DIM // 2, 1),
                          pltpu.roll(v, HEAD_DIM // 2, 1))
        return v * cos + other * sin

    q = rotary(q_ref[b])
    k = rotary(k_ref[b]) * (HEAD_DIM ** -0.5)
    v = v_ref[b]
    bd = bd_ref[...]
    pair = lambda t, p: t[:, p * LANES:(p + 1) * LANES]
    heads = [(p, j) for p in range(N_HEADS // 2) for j in range(2)]
    masks = [_head_mask(j) for j in range(2)]

    y_cross = _mm(q * qdec_ref[...], r_ref[b])
    r_ref[b] = r_ref[b] * cg_ref[...] + bd * _mm_tn(k * kte_ref[...], v)
    scores = [_mm_nt(pair(q, p) * masks[j], pair(k, p)) for p, j in heads]
    yield
    y_pairs = []
    for p in range(N_HEADS // 2):
        yp = pair(y_cross, p)
        for j in range(2):
            h = 2 * p + j
            yp = yp + _mm(scores[h] * intra_ref[h], pair(v, p) * masks[j])
        y_pairs.append(yp)
    yield
    y = jnp.concatenate(y_pairs, axis=1)
    ms = _mm(y * y, bd) * (1.0 / HEAD_DIM)
    yield
    o_ref[b] = (_silu(g_ref[b]) * (y * lax.rsqrt(ms + EPS) * nw_ref[...])).astype(BF16)


def _ret_kernel(*refs):
    r_ref = refs[-1]

    @pl.when(pl.program_id(0) == 0)
    def _():
        r_ref[...] = jnp.zeros_like(r_ref)

    _interleave([_ret_chunk(b, *refs) for b in range(r_ref.shape[0])])


def _retention_tables(seq):
    half = HEAD_DIM // 2
    inv = 1.0 / (10000.0 ** (jnp.arange(half, dtype=F32) / half))
    ang = jnp.arange(seq, dtype=F32)[:, None] * inv[None, :]
    cos = jnp.tile(jnp.concatenate([jnp.cos(ang), jnp.cos(ang)], axis=-1), (1, N_HEADS))
    sin = jnp.tile(jnp.concatenate([-jnp.sin(ang), jnp.sin(ang)], axis=-1), (1, N_HEADS))
    log_gamma = jnp.log1p(-(2.0 ** (-5.0 - jnp.arange(N_HEADS, dtype=F32))))
    idx = jnp.arange(CHUNK, dtype=F32)
    rel = idx[:, None] - idx[None, :]
    intra = jnp.where(rel[None] >= 0,
                      jnp.exp(jnp.maximum(rel, 0.0)[None] * log_gamma[:, None, None]), 0.0)
    expand = lambda m: jnp.repeat(m, HEAD_DIM, axis=1)
    qdec = expand(jnp.exp((idx + 1.0)[:, None] * log_gamma[None, :]))
    kte = expand(jnp.exp((CHUNK - 1 - idx)[:, None] * log_gamma[None, :]))
    cgam = expand(jnp.exp(CHUNK * log_gamma)[None, :])
    head = jnp.arange(WIDTH) // HEAD_DIM
    bd = (head[:, None] == head[None, :]).astype(BF16)
    return cos, sin, intra, qdec, kte, cgam, bd


def _retention(q, k, v, g, norm_w, batch, seq):
    nc = seq // CHUNK
    cos, sin, intra, qdec, kte, cgam, bd = _retention_tables(seq)
    seq3 = lambda a: a.reshape(batch, seq, a.shape[-1])
    table_spec = pl.BlockSpec((CHUNK, WIDTH), lambda c: (c, 0))
    out = pl.pallas_call(
        _ret_kernel,
        grid=(nc,),
        in_specs=[_seq_spec(batch, WIDTH)] * 4 + [table_spec, table_spec,
                  _param_spec1((N_HEADS, CHUNK, CHUNK)),
                  _param_spec1((CHUNK, WIDTH)), _param_spec1((CHUNK, WIDTH)),
                  _param_spec1((1, WIDTH)), _param_spec1((WIDTH, WIDTH)),
                  _param_spec1((1, WIDTH))],
        out_specs=_seq_spec(batch, WIDTH),
        out_shape=jax.ShapeDtypeStruct((batch, seq, WIDTH), BF16),
        scratch_shapes=[pltpu.VMEM((batch, WIDTH, WIDTH), F32)],
        compiler_params=_cparams(("arbitrary",)),
        name="retention",
    )(seq3(q), seq3(k), seq3(v), seq3(g), cos, sin, intra, qdec, kte, cgam, bd,
      norm_w.reshape(1, WIDTH))
    return out.reshape(batch * seq, WIDTH)


def _rwkv_kernel(r_ref, rh_ref, k_ref, kh_ref, v_ref, vh_ref, lo_ref, loh_ref,
                 mur_ref, muk_ref, muv_ref, mulo_ref, w0_ref, wup_ref, a0_ref, aup_ref, gup_ref,
                 kkw_ref, kaw_ref, rkw_ref, lnw_ref, lnb_ref, bd_ref,
                 o_ref, s_ref, sr_ref, sk_ref, sv_ref, slo_ref):
    c = pl.program_id(1)

    @pl.when(c == 0)
    def _():
        s_ref[...] = jnp.zeros_like(s_ref)

    keep = (c > 0).astype(F32)
    nb, _, width = r_ref.shape
    npair = width // LANES
    bd = bd_ref[...]
    row = _iota2((CHUNK, CHUNK), 0)
    col = _iota2((CHUNK, CHUNK), 1)
    strict = row > col
    causal = row >= col
    eye = (row == col).astype(F32)
    tri = causal.astype(BF16)
    same_head = (row // HEAD_DIM == col // HEAD_DIM).astype(F32)
    m1 = [_head_mask(j) for j in range(2)]
    m2 = [_head_mask(j, 2 * LANES) for j in range(2)]
    cat0 = lambda xs: jnp.concatenate(xs, axis=0)
    cat1 = lambda xs: jnp.concatenate(xs, axis=1)
    C = CHUNK

    def token_shift(b, main_ref, halo_ref, mu_ref, scr_ref):
        _with_prev_rows(scr_ref.at[b], main_ref.at[b], halo_ref.at[b], keep)
        cur = main_ref[b]
        prev = scr_ref[b, HALO - 1:HALO - 1 + CHUNK, :]
        return cur + (prev - cur) * mu_ref[...]

    pre = []
    for b in range(nb):
        r = token_shift(b, r_ref, rh_ref, mur_ref, sr_ref)
        k = token_shift(b, k_ref, kh_ref, muk_ref, sk_ref)
        v = token_shift(b, v_ref, vh_ref, muv_ref, sv_ref)
        lo = token_shift(b, lo_ref, loh_ref, mulo_ref, slo_ref)
        lo_wa = lo[:, :LANES]
        log_w = -RWKV_DECAY_SCALE * _sigmoid(w0_ref[...] + _mm(jnp.tanh(lo_wa), wup_ref[...]))
        a = _sigmoid(a0_ref[...] + _mm(lo_wa, aup_ref[...]))
        gate = _mm(_sigmoid(lo[:, LANES:]), gup_ref[...])
        kk = k * kkw_ref[...]
        kk = kk / jnp.maximum(jnp.sqrt(_mm(kk * kk, bd)), 1e-12)
        k = k * (1.0 + (a - 1.0) * kaw_ref[...])
        cum = _mm_split(tri, log_w, 2)
        mid = cum[C // 2 - 1:C // 2, :]
        last = cum[C - 1:C, :]
        inv_p = jnp.exp(mid - cum)
        to_end = jnp.exp(last - mid)
        pre.append(dict(
            r=r, k=k, v=v, gate=gate,
            a_t=-kk * jnp.exp(cum - log_w - mid), r_t=r * jnp.exp(cum - mid),
            b_t=kk * a * inv_p, k_t=k * inv_p, e_mid=jnp.exp(mid),
            b_end=kk * a * inv_p * to_end, k_end=k * inv_p * to_end, decay=jnp.exp(last)))

    units = [(b, q) for b in range(nb) for q in range(npair)]
    slab = lambda name: [pre[b][name][:, q * LANES:(q + 1) * LANES] for b, q in units]
    a_t, r_t, b_t, k_t, v_u = slab("a_t"), slab("r_t"), slab("b_t"), slab("k_t"), slab("v")
    e_mid, b_end, k_end, decay = slab("e_mid"), slab("b_end"), slab("k_end"), slab("decay")

    mm = [_mm_nt(cat0([a * m1[0], r * m1[0], a * m1[1], r * m1[1]]), cat0([bt, kt]))
          for a, r, bt, kt in zip(a_t, r_t, b_t, k_t)]
    chains = [(u, j) for u in range(len(units)) for j in range(2)]
    m_ab = [jnp.where(strict, mm[u][2 * j * C:(2 * j + 1) * C, :C], 0.0) for u, j in chains]
    m_ak = [jnp.where(strict, mm[u][2 * j * C:(2 * j + 1) * C, C:], 0.0) for u, j in chains]
    m_rb = [jnp.where(causal, mm[u][(2 * j + 1) * C:(2 * j + 2) * C, :C], 0.0) for u, j in chains]
    m_rk = [jnp.where(causal, mm[u][(2 * j + 1) * C:(2 * j + 2) * C, C:], 0.0) for u, j in chains]

    inv = [eye + m for m in m_ab]
    power = list(m_ab)
    for _ in range(int(math.log2(C)) - 1):
        power = [_mm(p, p) for p in power]
        inv = [i + _mm(i, p) for i, p in zip(inv, power)]

    n_units = range(len(units))
    per_head = lambda x2: x2[:C] * m2[0] + x2[C:] * m2[1]
    mv = [_mm(cat0([m_ak[2 * u], m_rk[2 * u], m_ak[2 * u + 1], m_rk[2 * u + 1]]), v_u[u])
          for u in n_units]
    mv_ak = [mv[u][:C] * m1[0] + mv[u][2 * C:3 * C] * m1[1] for u in n_units]
    mv_rk = [mv[u][C:2 * C] * m1[0] + mv[u][3 * C:] * m1[1] for u in n_units]
    z = [per_head(_mm(cat0([inv[2 * u], inv[2 * u + 1]]), cat1([a_t[u], mv_ak[u]])))
         for u in n_units]
    w = [per_head(_mm(cat0([m_rb[2 * u], m_rb[2 * u + 1]]), z[u])) for u in n_units]

    s = [s_ref[b, q] for b, q in units]
    uy = [_mm_nt(cat0([z[u][:, :LANES] * e_mid[u], (r_t[u] + w[u][:, :LANES]) * e_mid[u]]), s[u])
          for u in n_units]
    u_in = [uy[u][:C] + z[u][:, LANES:] for u in n_units]
    y = [uy[u][C:] + w[u][:, LANES:] + mv_rk[u] for u in n_units]
    s_inc = [_mm_tn(cat0([u_in[u], v_u[u]]), cat0([b_end[u], k_end[u]])) for u in n_units]
    for u, (b, q) in enumerate(units):
        s_ref[b, q] = s[u] * decay[u] + same_head * s_inc[u]

    for b in range(nb):
        yb = cat1([y[b * npair + q] for q in range(npair)])
        mean = _mm(yb, bd) * (1.0 / HEAD_DIM)
        d = yb - mean
        var = _mm(d * d, bd) * (1.0 / HEAD_DIM)
        yn = d * lax.rsqrt(var + RWKV_LN_EPS) * lnw_ref[...] + lnb_ref[...]
        bonus = _mm(pre[b]["r"] * pre[b]["k"] * rkw_ref[...], bd) * pre[b]["v"]
        o_ref[b] = ((yn + bonus) * pre[b]["gate"]).astype(BF16)


RWKV_PAIRS_PER_STEP = 2


def _rwkv(r, k, v, lo, mu, w0, w_up, a0, a_up, g_up, k_k, k_a, r_k, ln_w, ln_b, batch, seq):
    nc = seq // CHUNK
    per = CHUNK // HALO
    width = RWKV_PAIRS_PER_STEP * LANES
    lo_width = lo.shape[-1]
    rank = w_up.shape[0]
    prev_block = lambda c: jnp.maximum(c * per - 1, 0)

    slab_row = pl.BlockSpec((batch, CHUNK, width), lambda p, c: (0, c, p))
    slab_halo = pl.BlockSpec((batch, HALO, width), lambda p, c: (0, prev_block(c), p))
    lo_row = pl.BlockSpec((batch, CHUNK, lo_width), lambda p, c: (0, c, 0))
    lo_halo = pl.BlockSpec((batch, HALO, lo_width), lambda p, c: (0, prev_block(c), 0))
    slab_param = pl.BlockSpec((1, width), lambda p, c: (0, p))
    slab_weight = pl.BlockSpec((LANES, width), lambda p, c: (0, p))
    const = lambda shape: pl.BlockSpec(shape, lambda p, c: (0,) * len(shape))

    row1 = lambda a: a.reshape(1, -1)
    seq3 = lambda a: a.reshape(batch, seq, a.shape[-1])
    wup_pad = jnp.concatenate([w_up, jnp.zeros((LANES - rank, WIDTH), F32)], axis=0).astype(BF16)
    aup_pad = jnp.concatenate([jnp.zeros((LANES - rank, WIDTH), F32), a_up], axis=0).astype(BF16)
    head = jnp.arange(width) // HEAD_DIM
    bd = (head[:, None] == head[None, :]).astype(BF16)
    r, k, v, lo = seq3(r), seq3(k), seq3(v), seq3(lo)
    args = (r, r, k, k, v, v, lo, lo,
            row1(mu[:WIDTH]), row1(mu[WIDTH:2 * WIDTH]), row1(mu[2 * WIDTH:3 * WIDTH]),
            row1(mu[3 * WIDTH:]),
            row1(w0), wup_pad, row1(a0), aup_pad, g_up.astype(BF16),
            row1(k_k), row1(k_a), row1(r_k), row1(ln_w), row1(ln_b), bd)
    out = pl.pallas_call(
        _rwkv_kernel,
        grid=(WIDTH // width, nc),
        in_specs=[slab_row, slab_halo, slab_row, slab_halo, slab_row, slab_halo, lo_row, lo_halo,
                  slab_param, slab_param, slab_param, const((1, lo_width)),
                  slab_param, slab_weight, slab_param, slab_weight, slab_weight,
                  slab_param, slab_param, slab_param, slab_param, slab_param,
                  const((width, width))],
        out_specs=pl.BlockSpec((batch, CHUNK, width), lambda p, c: (0, c, p)),
        out_shape=jax.ShapeDtypeStruct((batch, seq, WIDTH), BF16),
        scratch_shapes=[pltpu.VMEM((batch, RWKV_PAIRS_PER_STEP, LANES, LANES), F32),
                        pltpu.VMEM((batch, CHUNK + HALO, width), F32),
                        pltpu.VMEM((batch, CHUNK + HALO, width), F32),
                        pltpu.VMEM((batch, CHUNK + HALO, width), F32),
                        pltpu.VMEM((batch, CHUNK + HALO, lo_width), F32)],
        compiler_params=_cparams(("parallel", "arbitrary")),
        name="rwkv7",
    )(*args)
    return out.reshape(batch * seq, WIDTH)


FOX_PREP_ROWS = 256
FOX_TILE = 1024
LOG2E = math.log2(math.e)


def _fox_prep_chunk(b, q_ref, k_ref, v_ref, f_ref, qw_ref, kw_ref, fb_ref, bd_ref, spread_ref,
                    qo_ref, ko_ref, vo_ref, carry_ref):
    bd = bd_ref[...]
    q, k, v = q_ref[b], k_ref[b], v_ref[b]
    q_ms = _mm(q * q, bd) * (1.0 / HEAD_DIM)
    k_ms = _mm(k * k, bd) * (1.0 / HEAD_DIM)

    f = f_ref[b] + fb_ref[...]
    log_f = jnp.minimum(f, 0.0) - jnp.log1p(jnp.exp(-jnp.abs(f)))
    rows = FOX_PREP_ROWS
    tri = (_iota2((rows, rows), 0) >= _iota2((rows, rows), 1)).astype(BF16)
    cum = _mm_split(tri, log_f, 3) + carry_ref[b, 0:1, :]
    carry_ref[b] = jnp.broadcast_to(cum[rows - 1:rows, :], carry_ref.shape[1:])
    yield
    q = q * lax.rsqrt(q_ms + EPS) * qw_ref[...] * (HEAD_DIM ** -0.5 * LOG2E)
    k = k * lax.rsqrt(k_ms + EPS) * kw_ref[...]
    c_all = _mm_split_rhs(cum * LOG2E, spread_ref[...], 3)
    yield

    lane = _iota2((1, LANES), 1)
    for h in range(N_HEADS):
        p, j = divmod(h, 2)
        ps = slice(p * LANES, (p + 1) * LANES)
        mj = _head_mask(j)
        qh, kh = q[:, ps] * mj, k[:, ps] * mj
        if j == 1:
            qh = pltpu.roll(qh, HEAD_DIM, 1)
            kh = pltpu.roll(kh, HEAD_DIM, 1)
        ch = c_all[:, h * LANES:(h + 1) * LANES]
        hi = ch.astype(BF16).astype(F32)
        mid = (ch - hi).astype(BF16).astype(F32)
        low = ch - hi - mid
        piece = lambda base: jnp.where(lane == base, hi,
                                       jnp.where(lane == base + 1, mid,
                                                 jnp.where(lane == base + 2, low, 0.0)))
        ones = lambda base: ((lane >= base) & (lane < base + 3)).astype(F32)
        qo_ref[b, h] = (qh + piece(HEAD_DIM) + ones(HEAD_DIM + 3)).astype(BF16)
        ko_ref[b, h] = (kh + ones(HEAD_DIM) - piece(HEAD_DIM + 3)).astype(BF16)
        vo_ref[b, h] = (v[:, ps] * mj).astype(BF16)


def _fox_prep_kernel(*refs):
    carry_ref = refs[-1]

    @pl.when(pl.program_id(0) == 0)
    def _():
        carry_ref[...] = jnp.zeros_like(carry_ref)

    _interleave([_fox_prep_chunk(b, *refs) for b in range(carry_ref.shape[0])])


def _fox_kernel(qi_ref, ki_ref, q_ref, k_ref, v_ref, o_ref, m_ref, l_ref, acc_ref):
    t = pl.program_id(2)
    qi = qi_ref[t]
    ki = ki_ref[t]

    @pl.when(ki == 0)
    def _():
        m_ref[...] = jnp.full_like(m_ref, NEG)
        l_ref[...] = jnp.zeros_like(l_ref)
        acc_ref[...] = jnp.zeros_like(acc_ref)

    def accumulate(on_diagonal):
        scores = [lax.dot_general(q_ref[0, j], k_ref[0, j], (((1,), (1,)), ((), ())),
                                  preferred_element_type=F32) for j in range(2)]
        for j, s in enumerate(scores):
            if on_diagonal:
                visible = _iota2((FOX_TILE, FOX_TILE), 0) >= _iota2((FOX_TILE, FOX_TILE), 1)
                s = jnp.where(visible, s, NEG)
            m_prev = m_ref[j]
            m_next = jnp.maximum(m_prev, jnp.max(s, axis=1, keepdims=True))
            alpha = jnp.exp2(m_prev - m_next)
            p = jnp.exp2(s - m_next[:, 0:1])
            l_ref[j] = alpha * l_ref[j] + jnp.sum(p, axis=1, keepdims=True)
            acc_ref[j] = alpha * acc_ref[j] + jnp.dot(p.astype(BF16), v_ref[0, j],
                                                      preferred_element_type=F32)
            m_ref[j] = m_next

    @pl.when(ki < qi)
    def _():
        accumulate(False)

    @pl.when(ki == qi)
    def _():
        accumulate(True)
        o_ref[0] = (acc_ref[0] / l_ref[0] + acc_ref[1] / l_ref[1]).astype(BF16)


def _fox(q, k, v, f, q_norm_w, k_norm_w, f_bias, batch, seq):
    rows = FOX_PREP_ROWS
    nr = seq // rows
    head = jnp.arange(WIDTH) // HEAD_DIM
    bd = (head[:, None] == head[None, :]).astype(BF16)
    dst = jnp.arange(N_HEADS * LANES)
    spread = ((dst[None, :] // LANES == jnp.arange(LANES)[:, None])
              & (dst[None, :] % LANES >= HEAD_DIM) & (dst[None, :] % LANES < HEAD_DIM + 6)).astype(BF16)
    seq3 = lambda a: a.reshape(batch, seq, a.shape[-1])
    row_spec = lambda width: pl.BlockSpec((batch, rows, width), lambda i: (0, i, 0))
    head_out = pl.BlockSpec((batch, N_HEADS, rows, LANES), lambda i: (0, 0, i, 0))
    head_shape = jax.ShapeDtypeStruct((batch, N_HEADS, seq, LANES), BF16)
    qa, ka, va = pl.pallas_call(
        _fox_prep_kernel,
        grid=(nr,),
        in_specs=[row_spec(WIDTH), row_spec(WIDTH), row_spec(WIDTH), row_spec(LANES),
                  _param_spec1((1, WIDTH)), _param_spec1((1, WIDTH)), _param_spec1((1, LANES)),
                  _param_spec1((WIDTH, WIDTH)), _param_spec1((LANES, N_HEADS * LANES))],
        out_specs=[head_out, head_out, head_out],
        out_shape=[head_shape, head_shape, head_shape],
        scratch_shapes=[pltpu.VMEM((batch, HALO, LANES), F32)],
        compiler_params=_cparams(("arbitrary",)),
        name="fox_prep",
    )(seq3(q), seq3(k), seq3(v), seq3(f), jnp.tile(q_norm_w, N_HEADS).reshape(1, WIDTH),
      jnp.tile(k_norm_w, N_HEADS).reshape(1, WIDTH),
      jnp.pad(f_bias, (0, LANES - N_HEADS)).reshape(1, LANES), bd, spread)

    nt = seq // FOX_TILE
    pairs = [(i, j) for i in range(nt) for j in range(i + 1)]
    qi = jnp.asarray([i for i, _ in pairs], jnp.int32)
    ki = jnp.asarray([j for _, j in pairs], jnp.int32)
    q_spec = pl.BlockSpec((1, 2, FOX_TILE, LANES), lambda b, p, t, qi, ki: (b, p, qi[t], 0))
    kv_spec = pl.BlockSpec((1, 2, FOX_TILE, LANES), lambda b, p, t, qi, ki: (b, p, ki[t], 0))
    out = pl.pallas_call(
        _fox_kernel,
        grid_spec=pltpu.PrefetchScalarGridSpec(
            num_scalar_prefetch=2,
            grid=(batch, N_HEADS // 2, len(pairs)),
            in_specs=[q_spec, kv_spec, kv_spec],
            out_specs=pl.BlockSpec((1, FOX_TILE, LANES), lambda b, p, t, qi, ki: (b, qi[t], p)),
            scratch_shapes=[pltpu.VMEM((2, FOX_TILE, LANES), F32),
                            pltpu.VMEM((2, FOX_TILE, LANES), F32),
                            pltpu.VMEM((2, FOX_TILE, LANES), F32)]),
        out_shape=jax.ShapeDtypeStruct((batch, seq, WIDTH), BF16),
        compiler_params=_cparams(("parallel", "parallel", "arbitrary")),
        name="fox_attention",
    )(qi, ki, qa, ka, va)
    return out.reshape(batch * seq, WIDTH)


def _pad_cols(w, width):
    return jnp.pad(w, ((0, 0), (0, width - w.shape[1])))


def _even_layer(x, batch, seq, norm_w, w_in, conv_w, conv_b, dt_bias, a_log, d_skip,
                ssd_norm_w, ret_norm_w, w_out):
    ssd_x0 = WIDTH
    ssd_dt0 = WIDTH + (WIDTH + 4 * SSD_STATE)
    ret0 = ssd_dt0 + N_HEADS
    w = jnp.concatenate([w_in[:, :ssd_dt0], w_in[:, ret0:],
                         _pad_cols(w_in[:, ssd_dt0:ret0], LANES)], axis=1).astype(BF16)
    segments = ((0, WIDTH), (ssd_x0, WIDTH), (2 * WIDTH, 2 * LANES),
                (ssd_dt0, WIDTH), (ssd_dt0 + WIDTH, WIDTH), (ssd_dt0 + 2 * WIDTH, WIDTH),
                (ssd_dt0 + 3 * WIDTH, WIDTH), (ssd_dt0 + 4 * WIDTH, LANES))
    z, xs, bc, q, k, v, g, dt = _inproj(x, norm_w, w, segments)
    y_ssd = _ssd(z, xs, bc, dt, conv_w, conv_b, dt_bias, a_log, d_skip, ssd_norm_w, batch, seq)
    y_ret = _retention(q, k, v, g, ret_norm_w, batch, seq)
    return _outproj(y_ssd, y_ret, w_out.astype(BF16), x)


def _odd_layer(x, batch, seq, norm_w, w_in, mu, w0, w_up, a0, a_up, g_up, k_k, k_a, r_k,
               ln_w, ln_b, q_norm_w, k_norm_w, f_bias, w_out):
    lo_width = w_up.shape[0] + a_up.shape[0] + g_up.shape[0]
    fox0 = 3 * WIDTH + lo_width
    f0 = fox0 + 3 * WIDTH
    w = jnp.concatenate([w_in[:, :f0], _pad_cols(w_in[:, f0:], LANES)], axis=1).astype(BF16)
    segments = ((0, WIDTH), (WIDTH, WIDTH), (2 * WIDTH, WIDTH), (3 * WIDTH, lo_width),
                (fox0, WIDTH), (fox0 + WIDTH, WIDTH), (fox0 + 2 * WIDTH, WIDTH), (f0, LANES))
    r, k, v, lo, fq, fk, fv, ff = _inproj(x, norm_w, w, segments)
    y_rwkv = _rwkv(r, k, v, lo, mu, w0, w_up, a0, a_up, g_up, k_k, k_a, r_k.reshape(-1),
                   ln_w, ln_b, batch, seq)
    y_fox = _fox(fq, fk, fv, ff, q_norm_w, k_norm_w, f_bias, batch, seq)
    return _outproj(y_rwkv, y_fox, w_out.astype(BF16), x)


def kernel(x, ev_norm_w, ev_w_in, ev_ssd_conv_w, ev_ssd_conv_b, ev_ssd_dt_bias, ev_ssd_a_log,
           ev_ssd_d, ev_ssd_norm_w, ev_ret_norm_w, ev_w_out,
           od_norm_w, od_w_in, od_rwkv_mu, od_rwkv_w0, od_rwkv_w_up, od_rwkv_a0, od_rwkv_a_up,
           od_rwkv_g_up, od_rwkv_k_k, od_rwkv_k_a, od_rwkv_r_k, od_rwkv_ln_w, od_rwkv_ln_b,
           od_fox_q_norm_w, od_fox_k_norm_w, od_fox_f_bias, od_w_out,
           ffn_norm_w, ffn_w_up, ffn_conv_w, ffn_conv_b, ffn_w_down):
    batch, seq, _ = x.shape
    depth = ffn_norm_w.shape[0]
    h = x.reshape(batch * seq, D_MODEL)
    for layer in range(depth):
        i = layer // 2
        if layer % 2 == 0:
            h = _even_layer(h, batch, seq, ev_norm_w[i], ev_w_in[i], ev_ssd_conv_w[i],
                            ev_ssd_conv_b[i], ev_ssd_dt_bias[i], ev_ssd_a_log[i], ev_ssd_d[i],
                            ev_ssd_norm_w[i], ev_ret_norm_w[i], ev_w_out[i])
        else:
            h = _odd_layer(h, batch, seq, od_norm_w[i], od_w_in[i], od_rwkv_mu[i], od_rwkv_w0[i],
                           od_rwkv_w_up[i], od_rwkv_a0[i], od_rwkv_a_up[i], od_rwkv_g_up[i],
                           od_rwkv_k_k[i], od_rwkv_k_a[i], od_rwkv_r_k[i], od_rwkv_ln_w[i],
                           od_rwkv_ln_b[i], od_fox_q_norm_w[i], od_fox_k_norm_w[i],
                           od_fox_f_bias[i], od_w_out[i])
        h = _ffn(h, seq, ffn_norm_w[layer], ffn_w_up[layer], ffn_conv_w[layer],
                 ffn_conv_b[layer], ffn_w_down[layer])
    return h.reshape(batch, seq, D_MODEL)
```

```python
import functools
import itertools
import math

import jax
import jax.numpy as jnp
from jax import lax
from jax.experimental import pallas as pl
from jax.experimental.pallas import tpu as pltpu

F32 = jnp.float32
BF16 = jnp.bfloat16

D_MODEL = 1024
HEAD_DIM = 64
N_HEADS = 8
WIDTH = N_HEADS * HEAD_DIM
LANES = 128
CHUNK = 128
HALO = 8
EPS = 1e-6
SSD_CONV = 4
SSD_STATE = 64
RWKV_LN_EPS = 64e-5
RWKV_DECAY_SCALE = 0.606531
D_FF = 2816
FF_TILE = 256
FFN_HALO = 16
NEG = -1e30
VMEM_LIMIT = 56 * 1024 * 1024


def _mm(a, b):
    return jnp.dot(a.astype(BF16), b.astype(BF16), preferred_element_type=F32)


def _mm_nt(a, b):
    return lax.dot_general(a.astype(BF16), b.astype(BF16), (((1,), (1,)), ((), ())),
                           preferred_element_type=F32)


def _mm_tn(a, b):
    return lax.dot_general(a.astype(BF16), b.astype(BF16), (((0,), (0,)), ((), ())),
                           preferred_element_type=F32)


def _mm_split(lhs01, x, pieces):
    acc = None
    rest = x
    for _ in range(pieces):
        part = rest.astype(BF16)
        term = jnp.dot(lhs01, part, preferred_element_type=F32)
        acc = term if acc is None else acc + term
        rest = rest - part.astype(F32)
    return acc


def _mm_split_rhs(x, rhs01, pieces):
    acc = None
    rest = x
    for _ in range(pieces):
        part = rest.astype(BF16)
        term = jnp.dot(part, rhs01, preferred_element_type=F32)
        acc = term if acc is None else acc + term
        rest = rest - part.astype(F32)
    return acc


def _interleave(stage_generators):
    for _ in itertools.zip_longest(*stage_generators):
        pass


def _sigmoid(x):
    return 1.0 / (1.0 + jnp.exp(-x))


def _silu(x):
    return x * _sigmoid(x)


def _softplus(x):
    return jnp.maximum(x, 0.0) + jnp.log1p(jnp.exp(-jnp.abs(x)))


def _iota2(shape, dim):
    return lax.broadcasted_iota(jnp.int32, shape, dim)


def _head_mask(j, width=LANES):
    lane = _iota2((1, width), 1)
    return ((lane % LANES) // HEAD_DIM == j).astype(F32)


def _cparams(sem):
    return pltpu.CompilerParams(dimension_semantics=sem, vmem_limit_bytes=VMEM_LIMIT)


def _const_spec(shape):
    nd = len(shape)
    return pl.BlockSpec(shape, lambda *_: (0,) * nd, pipeline_mode=pl.Buffered(1))


def _inproj_kernel(x_ref, nw_ref, w_ref, *o_refs, segments):
    x = x_ref[...]
    ms = jnp.mean(x * x, axis=-1, keepdims=True)
    xn = (x * lax.rsqrt(ms + EPS) * nw_ref[...]).astype(BF16)
    for o_ref, (off, width) in zip(o_refs, segments):
        o_ref[...] = jnp.dot(xn, w_ref[:, off:off + width], preferred_element_type=F32)


def _inproj(x, norm_w, w, segments, tm=512):
    t = x.shape[0]
    n = w.shape[1]
    return pl.pallas_call(
        functools.partial(_inproj_kernel, segments=segments),
        grid=(t // tm,),
        in_specs=[pl.BlockSpec((tm, D_MODEL), lambda i: (i, 0)),
                  _const_spec((1, D_MODEL)),
                  _const_spec((D_MODEL, n))],
        out_specs=[pl.BlockSpec((tm, wd), lambda i: (i, 0)) for _, wd in segments],
        out_shape=[jax.ShapeDtypeStruct((t, wd), F32) for _, wd in segments],
        compiler_params=_cparams(("parallel",)),
        name="inproj",
    )(x, norm_w.reshape(1, D_MODEL), w)


def _outproj_kernel(ya_ref, yb_ref, w_ref, res_ref, o_ref):
    acc = jnp.dot(ya_ref[...], w_ref[:WIDTH, :], preferred_element_type=F32)
    acc = acc + jnp.dot(yb_ref[...], w_ref[WIDTH:, :], preferred_element_type=F32)
    o_ref[...] = res_ref[...] + acc


def _outproj(ya, yb, w, res, tm=512):
    t = res.shape[0]
    return pl.pallas_call(
        _outproj_kernel,
        grid=(t // tm,),
        in_specs=[pl.BlockSpec((tm, WIDTH), lambda i: (i, 0)),
                  pl.BlockSpec((tm, WIDTH), lambda i: (i, 0)),
                  _const_spec((2 * WIDTH, D_MODEL)),
                  pl.BlockSpec((tm, D_MODEL), lambda i: (i, 0))],
        out_specs=pl.BlockSpec((tm, D_MODEL), lambda i: (i, 0)),
        out_shape=jax.ShapeDtypeStruct((t, D_MODEL), F32),
        compiler_params=_cparams(("parallel",)),
        name="outproj",
    )(ya, yb, w, res)


def _ffn_kernel(x_ref, xh_ref, nw_ref, wg_ref, wu_ref, cw_ref, cb_ref, wd_ref, o_ref,
                xn_ref, g_ref, acc_ref, *, tm, tiles_per_seq):
    nw = nw_ref[...]

    def norm(v):
        ms = jnp.mean(v * v, axis=-1, keepdims=True)
        return v * lax.rsqrt(ms + EPS) * nw

    x = x_ref[...]
    keep = (pl.program_id(0) % tiles_per_seq != 0).astype(F32)
    xn_ref[:FFN_HALO, :] = (norm(xh_ref[...]) * keep).astype(BF16)
    xn_ref[FFN_HALO:, :] = norm(x).astype(BF16)
    acc_ref[...] = jnp.zeros_like(acc_ref)

    def up_proj(f):
        g_ref[f % 2] = jnp.dot(xn_ref[...], wg_ref[f], preferred_element_type=F32)
        return jnp.dot(xn_ref[FFN_HALO:, :], wu_ref[f], preferred_element_type=F32)

    nf = D_FF // FF_TILE
    up = up_proj(0)
    for f in range(nf):
        up_next = up_proj(f + 1) if f + 1 < nf else None
        g = g_ref.at[f % 2]
        cw = cw_ref[f]
        gate = (cb_ref[f]
                + cw[0:1, :] * g[FFN_HALO - 2:FFN_HALO - 2 + tm, :]
                + cw[1:2, :] * g[FFN_HALO - 1:FFN_HALO - 1 + tm, :]
                + cw[2:3, :] * g[FFN_HALO:, :])
        act = (_silu(gate) * up).astype(BF16)
        acc_ref[...] += jnp.dot(act, wd_ref[f], preferred_element_type=F32)
        up = up_next
    o_ref[...] = x + acc_ref[...]


def _ffn(x, seq, norm_w, w_up, conv_w, conv_b, w_down, tm=512):
    t = x.shape[0]
    nf = D_FF // FF_TILE
    wg = w_up[:, :D_FF].reshape(D_MODEL, nf, FF_TILE).transpose(1, 0, 2).astype(BF16)
    wu = w_up[:, D_FF:].reshape(D_MODEL, nf, FF_TILE).transpose(1, 0, 2).astype(BF16)
    wd = w_down.reshape(nf, FF_TILE, D_MODEL).astype(BF16)
    cw = conv_w.reshape(3, nf, FF_TILE).transpose(1, 0, 2)
    cb = conv_b.reshape(nf, 1, FF_TILE)
    halo_per_tile = tm // FFN_HALO
    return pl.pallas_call(
        functools.partial(_ffn_kernel, tm=tm, tiles_per_seq=seq // tm),
        grid=(t // tm,),
        in_specs=[pl.BlockSpec((tm, D_MODEL), lambda i: (i, 0)),
                  pl.BlockSpec((FFN_HALO, D_MODEL),
                               lambda i: (jnp.maximum(i * halo_per_tile - 1, 0), 0)),
                  _const_spec((1, D_MODEL)),
                  _const_spec((nf, D_MODEL, FF_TILE)),
                  _const_spec((nf, D_MODEL, FF_TILE)),
                  _const_spec((nf, 3, FF_TILE)),
                  _const_spec((nf, 1, FF_TILE)),
                  _const_spec((nf, FF_TILE, D_MODEL))],
        out_specs=pl.BlockSpec((tm, D_MODEL), lambda i: (i, 0)),
        out_shape=jax.ShapeDtypeStruct((t, D_MODEL), F32),
        scratch_shapes=[pltpu.VMEM((tm + FFN_HALO, D_MODEL), BF16),
                        pltpu.VMEM((2, tm + FFN_HALO, FF_TILE), F32),
                        pltpu.VMEM((tm, D_MODEL), F32)],
        compiler_params=_cparams(("parallel",)),
        name="convffn",
    )(x, x, norm_w.reshape(1, D_MODEL), wg, wu, cw, cb, wd)


def _seq_spec(batch, width):
    return pl.BlockSpec((batch, CHUNK, width), lambda c: (0, c, 0))


def _seq_halo_spec(batch, width):
    per = CHUNK // HALO
    return pl.BlockSpec((batch, HALO, width), lambda c: (0, jnp.maximum(c * per - 1, 0), 0))


def _param_spec1(shape):
    return pl.BlockSpec(shape, lambda c: (0,) * len(shape), pipeline_mode=pl.Buffered(1))


def _with_prev_rows(scr_ref, main_ref, halo_ref, keep):
    scr_ref[:HALO, :] = halo_ref[...] * keep
    scr_ref[HALO:, :] = main_ref[...]


def _ssd_chunk(b, keep, z_ref, x_ref, xh_ref, bc_ref, bch_ref, dt_ref,
               cwx_ref, cbx_ref, cwb_ref, cbb_ref, dtb_ref, alog_ref, dexp_ref, nw_ref, e_ref,
               o_ref, xe_ref, bce_ref, st_ref, y_ref):
    _with_prev_rows(xe_ref.at[b], x_ref.at[b], xh_ref.at[b], keep)
    _with_prev_rows(bce_ref.at[b], bc_ref.at[b], bch_ref.at[b], keep)

    def conv(ref, cw_ref, cb_ref):
        out = cb_ref[...]
        for k in range(SSD_CONV):
            lo = HALO - (SSD_CONV - 1) + k
            out = out + cw_ref[k:k + 1, :] * ref[b, lo:lo + CHUNK, :]
        return _silu(out)

    x = conv(xe_ref, cwx_ref, cbx_ref)
    bc = conv(bce_ref, cwb_ref, cbb_ref)
    bm = bc[:, :LANES]
    cm = bc[:, LANES:]

    row = _iota2((CHUNK, CHUNK), 0)
    col = _iota2((CHUNK, CHUNK), 1)
    causal = row >= col

    dt = _softplus(dt_ref[b] + dtb_ref[...])
    a = dt * (-jnp.exp(alog_ref[...]))
    a_cum = _mm_split(causal.astype(BF16), a, 3)
    yield
    a_cum_t = a_cum.T
    a_last = a_cum[CHUNK - 1:CHUNK, :]
    per_head = jnp.concatenate(
        [dt, jnp.exp(a_cum), jnp.exp(a_last - a_cum),
         jnp.broadcast_to(jnp.exp(a_last), (HALO, LANES))], axis=0)
    per_lane = _mm_split_rhs(per_head, e_ref[...], 2)
    dt_e = per_lane[:CHUNK]
    ea_e = per_lane[CHUNK:2 * CHUNK]
    te_e = per_lane[2 * CHUNK:3 * CHUNK]
    cd_e = per_lane[3 * CHUNK:3 * CHUNK + 1]
    yield

    xdt = x * dt_e
    xs = xdt * te_e
    bm_t = bm.T
    group_rows = _iota2((LANES, 1), 0) // SSD_STATE

    for g in range(2):
        cg = cm * _head_mask(g)
        cb = _mm_nt(cg, bm)
        gs = slice(g * 2 * LANES, (g + 1) * 2 * LANES)
        st = st_ref[b, g]
        y_off = _mm(cg, st) * ea_e[:, gs]
        st_new = _mm(bm_t, xs[:, gs])
        st_ref[b, g] = st * cd_e[:, gs] + jnp.where(group_rows == g, st_new, 0.0)
        yield
        for pp in range(2):
            p = 2 * g + pp
            xp = xdt[:, p * LANES:(p + 1) * LANES]
            yp = y_off[:, pp * LANES:(pp + 1) * LANES]
            for j in range(2):
                h = 2 * p + j
                seg = a_cum[:, h:h + 1] - a_cum_t[h:h + 1, :]
                decay = jnp.where(causal, jnp.exp(jnp.minimum(seg, 0.0)), 0.0)
                yp = yp + _mm(cb * decay, xp * _head_mask(j))
            y_ref[b, :, p * LANES:(p + 1) * LANES] = yp
            yield

    y = y_ref[b] + dexp_ref[...] * x
    gated = y * _silu(z_ref[b])
    ms = jnp.mean(gated * gated, axis=-1, keepdims=True)
    o_ref[b] = (gated * lax.rsqrt(ms + EPS) * nw_ref[...]).astype(BF16)


def _ssd_kernel(*refs):
    c = pl.program_id(0)
    st_ref = refs[-2]

    @pl.when(c == 0)
    def _():
        st_ref[...] = jnp.zeros_like(st_ref)

    keep = (c > 0).astype(F32)
    _interleave([_ssd_chunk(b, keep, *refs) for b in range(st_ref.shape[0])])


def _ssd(z, x, bc, dt, conv_w, conv_b, dt_bias, a_log, d_skip, norm_w, batch, seq):
    nc = seq // CHUNK
    pad = LANES - N_HEADS
    e = jnp.repeat(jnp.eye(LANES, N_HEADS, dtype=BF16), HEAD_DIM, axis=1)
    seq3 = lambda a: a.reshape(batch, seq, a.shape[-1])
    z, x, bc, dt = seq3(z), seq3(x), seq3(bc), seq3(dt)
    args = (z, x, x, bc, bc, dt,
            conv_w[:, :WIDTH], conv_b[:WIDTH].reshape(1, WIDTH),
            conv_w[:, WIDTH:], conv_b[WIDTH:].reshape(1, 2 * LANES),
            jnp.pad(dt_bias, (0, pad)).reshape(1, LANES),
            jnp.pad(a_log, (0, pad)).reshape(1, LANES),
            jnp.repeat(d_skip, HEAD_DIM).reshape(1, WIDTH),
            norm_w.reshape(1, WIDTH), e)
    out = pl.pallas_call(
        _ssd_kernel,
        grid=(nc,),
        in_specs=[_seq_spec(batch, WIDTH), _seq_spec(batch, WIDTH), _seq_halo_spec(batch, WIDTH),
                  _seq_spec(batch, 2 * LANES), _seq_halo_spec(batch, 2 * LANES),
                  _seq_spec(batch, LANES),
                  _param_spec1((SSD_CONV, WIDTH)), _param_spec1((1, WIDTH)),
                  _param_spec1((SSD_CONV, 2 * LANES)), _param_spec1((1, 2 * LANES)),
                  _param_spec1((1, LANES)), _param_spec1((1, LANES)),
                  _param_spec1((1, WIDTH)), _param_spec1((1, WIDTH)),
                  _param_spec1((LANES, WIDTH))],
        out_specs=_seq_spec(batch, WIDTH),
        out_shape=jax.ShapeDtypeStruct((batch, seq, WIDTH), BF16),
        scratch_shapes=[pltpu.VMEM((batch, CHUNK + HALO, WIDTH), F32),
                        pltpu.VMEM((batch, CHUNK + HALO, 2 * LANES), F32),
                        pltpu.VMEM((batch, 2, LANES, 2 * LANES), F32),
                        pltpu.VMEM((batch, CHUNK, WIDTH), F32)],
        compiler_params=_cparams(("arbitrary",)),
        name="ssd",
    )(*args)
    return out.reshape(batch * seq, WIDTH)


def _ret_chunk(b, q_ref, k_ref, v_ref, g_ref, cos_ref, sin_ref, intra_ref, qdec_ref, kte_ref,
               cg_ref, bd_ref, nw_ref, o_ref, r_ref):
    lane = _iota2((1, WIDTH), 1)
    first_half = (lane % HEAD_DIM) < (HEAD_DIM // 2)
    cos = cos_ref[...]
    sin = sin_ref[...]

    def rotary(v):
        other = jnp.where(first_half,
                          pltpu.roll(v, WIDTH - HEAD_DIM // 2, 1),
                          pltpu.roll(v, HEAD_DIM // 2, 1))
        return v * cos + other * sin

    q = rotary(q_ref[b])
    k = rotary(k_ref[b]) * (HEAD_DIM ** -0.5)
    v = v_ref[b]
    bd = bd_ref[...]
    pair = lambda t, p: t[:, p * LANES:(p + 1) * LANES]
    heads = [(p, j) for p in range(N_HEADS // 2) for j in range(2)]
    masks = [_head_mask(j) for j in range(2)]

    y_cross = _mm(q * qdec_ref[...], r_ref[b])
    r_ref[b] = r_ref[b] * cg_ref[...] + bd * _mm_tn(k * kte_ref[...], v)
    scores = [_mm_nt(pair(q, p) * masks[j], pair(k, p)) for p, j in heads]
    yield
    y_pairs = []
    for p in range(N_HEADS // 2):
        yp = pair(y_cross, p)
        for j in range(2):
            h = 2 * p + j
            yp = yp + _mm(scores[h] * intra_ref[h], pair(v, p) * masks[j])
        y_pairs.append(yp)
    yield
    y = jnp.concatenate(y_pairs, axis=1)
    ms = _mm(y * y, bd) * (1.0 / HEAD_DIM)
    yield
    o_ref[b] = (_silu(g_ref[b]) * (y * lax.rsqrt(ms + EPS) * nw_ref[...])).astype(BF16)


def _ret_kernel(*refs):
    r_ref = refs[-1]

    @pl.when(pl.program_id(0) == 0)
    def _():
        r_ref[...] = jnp.zeros_like(r_ref)

    _interleave([_ret_chunk(b, *refs) for b in range(r_ref.shape[0])])


def _retention_tables(seq):
    half = HEAD_DIM // 2
    inv = 1.0 / (10000.0 ** (jnp.arange(half, dtype=F32) / half))
    ang = jnp.arange(seq, dtype=F32)[:, None] * inv[None, :]
    cos = jnp.tile(jnp.concatenate([jnp.cos(ang), jnp.cos(ang)], axis=-1), (1, N_HEADS))
    sin = jnp.tile(jnp.concatenate([-jnp.sin(ang), jnp.sin(ang)], axis=-1), (1, N_HEADS))
    log_gamma = jnp.log1p(-(2.0 ** (-5.0 - jnp.arange(N_HEADS, dtype=F32))))
    idx = jnp.arange(CHUNK, dtype=F32)
    rel = idx[:, None] - idx[None, :]
    intra = jnp.where(rel[None] >= 0,
                      jnp.exp(jnp.maximum(rel, 0.0)[None] * log_gamma[:, None, None]), 0.0)
    expand = lambda m: jnp.repeat(m, HEAD_DIM, axis=1)
    qdec = expand(jnp.exp((idx + 1.0)[:, None] * log_gamma[None, :]))
    kte = expand(jnp.exp((CHUNK - 1 - idx)[:, None] * log_gamma[None, :]))
    cgam = expand(jnp.exp(CHUNK * log_gamma)[None, :])
    head = jnp.arange(WIDTH) // HEAD_DIM
    bd = (head[:, None] == head[None, :]).astype(BF16)
    return cos, sin, intra, qdec, kte, cgam, bd


def _retention(q, k, v, g, norm_w, batch, seq):
    nc = seq // CHUNK
    cos, sin, intra, qdec, kte, cgam, bd = _retention_tables(seq)
    seq3 = lambda a: a.reshape(batch, seq, a.shape[-1])
    table_spec = pl.BlockSpec((CHUNK, WIDTH), lambda c: (c, 0))
    out = pl.pallas_call(
        _ret_kernel,
        grid=(nc,),
        in_specs=[_seq_spec(batch, WIDTH)] * 4 + [table_spec, table_spec,
                  _param_spec1((N_HEADS, CHUNK, CHUNK)),
                  _param_spec1((CHUNK, WIDTH)), _param_spec1((CHUNK, WIDTH)),
                  _param_spec1((1, WIDTH)), _param_spec1((WIDTH, WIDTH)),
                  _param_spec1((1, WIDTH))],
        out_specs=_seq_spec(batch, WIDTH),
        out_shape=jax.ShapeDtypeStruct((batch, seq, WIDTH), BF16),
        scratch_shapes=[pltpu.VMEM((batch, WIDTH, WIDTH), F32)],
        compiler_params=_cparams(("arbitrary",)),
        name="retention",
    )(seq3(q), seq3(k), seq3(v), seq3(g), cos, sin, intra, qdec, kte, cgam, bd,
      norm_w.reshape(1, WIDTH))
    return out.reshape(batch * seq, WIDTH)


def _rwkv_kernel(r_ref, rh_ref, k_ref, kh_ref, v_ref, vh_ref, lo_ref, loh_ref,
                 mur_ref, muk_ref, muv_ref, mulo_ref, w0_ref, wup_ref, a0_ref, aup_ref, gup_ref,
                 kkw_ref, kaw_ref, rkw_ref, lnw_ref, lnb_ref, bd_ref,
                 o_ref, s_ref, sr_ref, sk_ref, sv_ref, slo_ref):
    c = pl.program_id(1)

    @pl.when(c == 0)
    def _():
        s_ref[...] = jnp.zeros_like(s_ref)

    keep = (c > 0).astype(F32)
    nb, _, width = r_ref.shape
    npair = width // LANES
    bd = bd_ref[...]
    row = _iota2((CHUNK, CHUNK), 0)
    col = _iota2((CHUNK, CHUNK), 1)
    strict = row > col
    causal = row >= col
    eye = (row == col).astype(F32)
    tri = causal.astype(BF16)
    same_head = (row // HEAD_DIM == col // HEAD_DIM).astype(F32)
    m1 = [_head_mask(j) for j in range(2)]
    m2 = [_head_mask(j, 2 * LANES) for j in range(2)]
    cat0 = lambda xs: jnp.concatenate(xs, axis=0)
    cat1 = lambda xs: jnp.concatenate(xs, axis=1)
    C = CHUNK

    def token_shift(b, main_ref, halo_ref, mu_ref, scr_ref):
        _with_prev_rows(scr_ref.at[b], main_ref.at[b], halo_ref.at[b], keep)
        cur = main_ref[b]
        prev = scr_ref[b, HALO - 1:HALO - 1 + CHUNK, :]
        return cur + (prev - cur) * mu_ref[...]

    pre = []
    for b in range(nb):
        r = token_shift(b, r_ref, rh_ref, mur_ref, sr_ref)
        k = token_shift(b, k_ref, kh_ref, muk_ref, sk_ref)
        v = token_shift(b, v_ref, vh_ref, muv_ref, sv_ref)
        lo = token_shift(b, lo_ref, loh_ref, mulo_ref, slo_ref)
        lo_wa = lo[:, :LANES]
        log_w = -RWKV_DECAY_SCALE * _sigmoid(w0_ref[...] + _mm(jnp.tanh(lo_wa), wup_ref[...]))
        a = _sigmoid(a0_ref[...] + _mm(lo_wa, aup_ref[...]))
        gate = _mm(_sigmoid(lo[:, LANES:]), gup_ref[...])
        kk = k * kkw_ref[...]
        kk = kk / jnp.maximum(jnp.sqrt(_mm(kk * kk, bd)), 1e-12)
        k = k * (1.0 + (a - 1.0) * kaw_ref[...])
        cum = _mm_split(tri, log_w, 2)
        mid = cum[C // 2 - 1:C // 2, :]
        last = cum[C - 1:C, :]
        inv_p = jnp.exp(mid - cum)
        to_end = jnp.exp(last - mid)
        pre.append(dict(
            r=r, k=k, v=v, gate=gate,
            a_t=-kk * jnp.exp(cum - log_w - mid), r_t=r * jnp.exp(cum - mid),
            b_t=kk * a * inv_p, k_t=k * inv_p, e_mid=jnp.exp(mid),
            b_end=kk * a * inv_p * to_end, k_end=k * inv_p * to_end, decay=jnp.exp(last)))

    units = [(b, q) for b in range(nb) for q in range(npair)]
    slab = lambda name: [pre[b][name][:, q * LANES:(q + 1) * LANES] for b, q in units]
    a_t, r_t, b_t, k_t, v_u = slab("a_t"), slab("r_t"), slab("b_t"), slab("k_t"), slab("v")
    e_mid, b_end, k_end, decay = slab("e_mid"), slab("b_end"), slab("k_end"), slab("decay")

    mm = [_mm_nt(cat0([a * m1[0], r * m1[0], a * m1[1], r * m1[1]]), cat0([bt, kt]))
          for a, r, bt, kt in zip(a_t, r_t, b_t, k_t)]
    chains = [(u, j) for u in range(len(units)) for j in range(2)]
    m_ab = [jnp.where(strict, mm[u][2 * j * C:(2 * j + 1) * C, :C], 0.0) for u, j in chains]
    m_ak = [jnp.where(strict, mm[u][2 * j * C:(2 * j + 1) * C, C:], 0.0) for u, j in chains]
    m_rb = [jnp.where(causal, mm[u][(2 * j + 1) * C:(2 * j + 2) * C, :C], 0.0) for u, j in chains]
    m_rk = [jnp.where(causal, mm[u][(2 * j + 1) * C:(2 * j + 2) * C, C:], 0.0) for u, j in chains]

    inv = [eye + m for m in m_ab]
    power = list(m_ab)
    for _ in range(int(math.log2(C)) - 1):
        power = [_mm(p, p) for p in power]
        inv = [i + _mm(i, p) for i, p in zip(inv, power)]

    n_units = range(len(units))
    per_head = lambda x2: x2[:C] * m2[0] + x2[C:] * m2[1]
    mv = [_mm(cat0([m_ak[2 * u], m_rk[2 * u], m_ak[2 * u + 1], m_rk[2 * u + 1]]), v_u[u])
          for u in n_units]
    mv_ak = [mv[u][:C] * m1[0] + mv[u][2 * C:3 * C] * m1[1] for u in n_units]
    mv_rk = [mv[u][C:2 * C] * m1[0] + mv[u][3 * C:] * m1[1] for u in n_units]
    z = [per_head(_mm(cat0([inv[2 * u], inv[2 * u + 1]]), cat1([a_t[u], mv_ak[u]])))
         for u in n_units]
    w = [per_head(_mm(cat0([m_rb[2 * u], m_rb[2 * u + 1]]), z[u])) for u in n_units]

    s = [s_ref[b, q] for b, q in units]
    uy = [_mm_nt(cat0([z[u][:, :LANES] * e_mid[u], (r_t[u] + w[u][:, :LANES]) * e_mid[u]]), s[u])
          for u in n_units]
    u_in = [uy[u][:C] + z[u][:, LANES:] for u in n_units]
    y = [uy[u][C:] + w[u][:, LANES:] + mv_rk[u] for u in n_units]
    s_inc = [_mm_tn(cat0([u_in[u], v_u[u]]), cat0([b_end[u], k_end[u]])) for u in n_units]
    for u, (b, q) in enumerate(units):
        s_ref[b, q] = s[u] * decay[u] + same_head * s_inc[u]

    for b in range(nb):
        yb = cat1([y[b * npair + q] for q in range(npair)])
        mean = _mm(yb, bd) * (1.0 / HEAD_DIM)
        d = yb - mean
        var = _mm(d * d, bd) * (1.0 / HEAD_DIM)
        yn = d * lax.rsqrt(var + RWKV_LN_EPS) * lnw_ref[...] + lnb_ref[...]
        bonus = _mm(pre[b]["r"] * pre[b]["k"] * rkw_ref[...], bd) * pre[b]["v"]
        o_ref[b] = ((yn + bonus) * pre[b]["gate"]).astype(BF16)


RWKV_PAIRS_PER_STEP = 2


def _rwkv(r, k, v, lo, mu, w0, w_up, a0, a_up, g_up, k_k, k_a, r_k, ln_w, ln_b, batch, seq):
    nc = seq // CHUNK
    per = CHUNK // HALO
    width = RWKV_PAIRS_PER_STEP * LANES
    lo_width = lo.shape[-1]
    rank = w_up.shape[0]
    prev_block = lambda c: jnp.maximum(c * per - 1, 0)

    slab_row = pl.BlockSpec((batch, CHUNK, width), lambda p, c: (0, c, p))
    slab_halo = pl.BlockSpec((batch, HALO, width), lambda p, c: (0, prev_block(c), p))
    lo_row = pl.BlockSpec((batch, CHUNK, lo_width), lambda p, c: (0, c, 0))
    lo_halo = pl.BlockSpec((batch, HALO, lo_width), lambda p, c: (0, prev_block(c), 0))
    slab_param = pl.BlockSpec((1, width), lambda p, c: (0, p))
    slab_weight = pl.BlockSpec((LANES, width), lambda p, c: (0, p))
    const = lambda shape: pl.BlockSpec(shape, lambda p, c: (0,) * len(shape))

    row1 = lambda a: a.reshape(1, -1)
    seq3 = lambda a: a.reshape(batch, seq, a.shape[-1])
    wup_pad = jnp.concatenate([w_up, jnp.zeros((LANES - rank, WIDTH), F32)], axis=0).astype(BF16)
    aup_pad = jnp.concatenate([jnp.zeros((LANES - rank, WIDTH), F32), a_up], axis=0).astype(BF16)
    head = jnp.arange(width) // HEAD_DIM
    bd = (head[:, None] == head[None, :]).astype(BF16)
    r, k, v, lo = seq3(r), seq3(k), seq3(v), seq3(lo)
    args = (r, r, k, k, v, v, lo, lo,
            row1(mu[:WIDTH]), row1(mu[WIDTH:2 * WIDTH]), row1(mu[2 * WIDTH:3 * WIDTH]),
            row1(mu[3 * WIDTH:]),
            row1(w0), wup_pad, row1(a0), aup_pad, g_up.astype(BF16),
            row1(k_k), row1(k_a), row1(r_k), row1(ln_w), row1(ln_b), bd)
    out = pl.pallas_call(
        _rwkv_kernel,
        grid=(WIDTH // width, nc),
        in_specs=[slab_row, slab_halo, slab_row, slab_halo, slab_row, slab_halo, lo_row, lo_halo,
                  slab_param, slab_param, slab_param, const((1, lo_width)),
                  slab_param, slab_weight, slab_param, slab_weight, slab_weight,
                  slab_param, slab_param, slab_param, slab_param, slab_param,
                  const((width, width))],
        out_specs=pl.BlockSpec((batch, CHUNK, width), lambda p, c: (0, c, p)),
        out_shape=jax.ShapeDtypeStruct((batch, seq, WIDTH), BF16),
        scratch_shapes=[pltpu.VMEM((batch, RWKV_PAIRS_PER_STEP, LANES, LANES), F32),
                        pltpu.VMEM((batch, CHUNK + HALO, width), F32),
                        pltpu.VMEM((batch, CHUNK + HALO, width), F32),
                        pltpu.VMEM((batch, CHUNK + HALO, width), F32),
                        pltpu.VMEM((batch, CHUNK + HALO, lo_width), F32)],
        compiler_params=_cparams(("parallel", "arbitrary")),
        name="rwkv7",
    )(*args)
    return out.reshape(batch * seq, WIDTH)


FOX_PREP_ROWS = 256
FOX_TILE = 1024
FOX_Q_SPLIT = 2
FOX_GUARD = 64.0
LOG2E = math.log2(math.e)


def _fox_prep_chunk(b, q_ref, k_ref, v_ref, f_ref, qw_ref, kw_ref, fb_ref, bd_ref, spread_ref,
                    qo_ref, ko_ref, vo_ref, carry_ref):
    bd = bd_ref[...]
    q, k, v = q_ref[b], k_ref[b], v_ref[b]
    q_ms = _mm(q * q, bd) * (1.0 / HEAD_DIM)
    k_ms = _mm(k * k, bd) * (1.0 / HEAD_DIM)

    f = f_ref[b] + fb_ref[...]
    log_f = jnp.minimum(f, 0.0) - jnp.log1p(jnp.exp(-jnp.abs(f)))
    rows = FOX_PREP_ROWS
    tri = (_iota2((rows, rows), 0) >= _iota2((rows, rows), 1)).astype(BF16)
    cum = _mm_split(tri, log_f, 3) + carry_ref[b, 0:1, :]
    carry_ref[b] = jnp.broadcast_to(cum[rows - 1:rows, :], carry_ref.shape[1:])
    yield
    q = q * lax.rsqrt(q_ms + EPS) * qw_ref[...] * (HEAD_DIM ** -0.5 * LOG2E)
    k = k * lax.rsqrt(k_ms + EPS) * kw_ref[...]
    c_all = _mm_split_rhs(cum * LOG2E, spread_ref[...], 3)
    yield

    lane = _iota2((1, LANES), 1)
    for h in range(N_HEADS):
        p, j = divmod(h, 2)
        ps = slice(p * LANES, (p + 1) * LANES)
        mj = _head_mask(j)
        qh, kh = q[:, ps] * mj, k[:, ps] * mj
        if j == 1:
            qh = pltpu.roll(qh, HEAD_DIM, 1)
            kh = pltpu.roll(kh, HEAD_DIM, 1)
        ch = c_all[:, h * LANES:(h + 1) * LANES]
        hi = ch.astype(BF16).astype(F32)
        mid = (ch - hi).astype(BF16).astype(F32)
        low = ch - hi - mid
        piece = lambda base: jnp.where(lane == base, hi,
                                       jnp.where(lane == base + 1, mid,
                                                 jnp.where(lane == base + 2, low, 0.0)))
        ones = lambda base: ((lane >= base) & (lane < base + 3)).astype(F32)
        qo_ref[b, h] = (qh + piece(HEAD_DIM) + ones(HEAD_DIM + 3)).astype(BF16)
        ko_ref[b, h] = (kh + ones(HEAD_DIM) - piece(HEAD_DIM + 3)).astype(BF16)
    row_head = _iota2((LANES, 1), 0) // HEAD_DIM
    row_in_head = _iota2((LANES, 1), 0) % HEAD_DIM
    for p in range(N_HEADS // 2):
        v_t = v[:, p * LANES:(p + 1) * LANES].T
        for j in range(2):
            ones_row = ((row_head != j) & (row_in_head == 0)).astype(F32)
            vo_ref[b, 2 * p + j] = jnp.where(row_head == j, v_t, ones_row).astype(BF16)


def _fox_prep_kernel(*refs):
    carry_ref = refs[-1]

    @pl.when(pl.program_id(0) == 0)
    def _():
        carry_ref[...] = jnp.zeros_like(carry_ref)

    _interleave([_fox_prep_chunk(b, *refs) for b in range(carry_ref.shape[0])])


def _fox_kernel(qi_ref, ki_ref, q_ref, k_ref, vt_ref, o_ref, m_ref, acc_ref, redo_ref):
    t = pl.program_id(2)
    qi = qi_ref[t]
    ki = ki_ref[t]

    part = FOX_TILE // FOX_Q_SPLIT
    units = [(j, slice(h * part, (h + 1) * part)) for j in range(2) for h in range(FOX_Q_SPLIT)]

    def all_scores(on_diagonal):
        raw = [lax.dot_general(k_ref[0, j], q_ref[0, j, qs, :], (((1,), (1,)), ((), ())),
                               preferred_element_type=F32) for j, qs in units]
        if not on_diagonal:
            return raw
        key = _iota2((FOX_TILE, part), 0)
        query = _iota2((FOX_TILE, part), 1)
        return [jnp.where(query + qs.start >= key, s, NEG) for (_, qs), s in zip(units, raw)]

    def pv(j, p):
        return jnp.dot(vt_ref[0, j], p.astype(BF16), preferred_element_type=F32)

    def exact_step(on_diagonal):
        for (j, qs), s in zip(units, all_scores(on_diagonal)):
            m_prev = m_ref[j, :, qs]
            m_next = jnp.maximum(m_prev, jnp.max(s, axis=0, keepdims=True))
            alpha = jnp.exp2(m_prev - m_next)
            acc_ref[j, :, qs] = alpha * acc_ref[j, :, qs] + pv(j, jnp.exp2(s - m_next))
            m_ref[j, :, qs] = m_next

    def lagged_step(on_diagonal):
        done = []
        worst = None
        for (j, qs), s in zip(units, all_scores(on_diagonal)):
            m_used = m_ref[j, :, qs]
            block_max = jnp.max(s, axis=0, keepdims=True)
            contrib = pv(j, jnp.exp2(s - m_used))
            excess = jnp.max(block_max - m_used)
            worst = excess if worst is None else jnp.maximum(worst, excess)
            done.append((j, qs, m_used, block_max, contrib))
        ok = worst <= FOX_GUARD
        redo_ref[0] = jnp.where(ok, 0, 1)

        @pl.when(ok)
        def _():
            for j, qs, m_used, block_max, contrib in done:
                m_next = jnp.maximum(m_used, block_max)
                acc_ref[j, :, qs] = (acc_ref[j, :, qs] + contrib) * jnp.exp2(m_used - m_next)
                m_ref[j, :, qs] = m_next

    @pl.when(ki == qi)
    def _():
        m_ref[...] = jnp.full_like(m_ref, NEG)
        acc_ref[...] = jnp.zeros_like(acc_ref)
        exact_step(True)

    redo_ref[0] = 0

    @pl.when(ki < qi)
    def _():
        lagged_step(False)

    @pl.when(redo_ref[0] == 1)
    def _():
        exact_step(False)

    @pl.when(ki == 0)
    def _():
        head_rows = _iota2((LANES, 1), 0) // HEAD_DIM
        out_t = jnp.where(head_rows == 0,
                          acc_ref[0] / acc_ref[0, HEAD_DIM:HEAD_DIM + 1, :],
                          acc_ref[1] / acc_ref[1, 0:1, :])
        o_ref[0] = out_t.T.astype(BF16)


def _fox(q, k, v, f, q_norm_w, k_norm_w, f_bias, batch, seq):
    rows = FOX_PREP_ROWS
    nr = seq // rows
    head = jnp.arange(WIDTH) // HEAD_DIM
    bd = (head[:, None] == head[None, :]).astype(BF16)
    dst = jnp.arange(N_HEADS * LANES)
    spread = ((dst[None, :] // LANES == jnp.arange(LANES)[:, None])
              & (dst[None, :] % LANES >= HEAD_DIM) & (dst[None, :] % LANES < HEAD_DIM + 6)).astype(BF16)
    seq3 = lambda a: a.reshape(batch, seq, a.shape[-1])
    row_spec = lambda width: pl.BlockSpec((batch, rows, width), lambda i: (0, i, 0))
    head_out = pl.BlockSpec((batch, N_HEADS, rows, LANES), lambda i: (0, 0, i, 0))
    head_shape = jax.ShapeDtypeStruct((batch, N_HEADS, seq, LANES), BF16)
    vt_out = pl.BlockSpec((batch, N_HEADS, LANES, rows), lambda i: (0, 0, 0, i))
    vt_shape = jax.ShapeDtypeStruct((batch, N_HEADS, LANES, seq), BF16)
    qa, ka, va = pl.pallas_call(
        _fox_prep_kernel,
        grid=(nr,),
        in_specs=[row_spec(WIDTH), row_spec(WIDTH), row_spec(WIDTH), row_spec(LANES),
                  _param_spec1((1, WIDTH)), _param_spec1((1, WIDTH)), _param_spec1((1, LANES)),
                  _param_spec1((WIDTH, WIDTH)), _param_spec1((LANES, N_HEADS * LANES))],
        out_specs=[head_out, head_out, vt_out],
        out_shape=[head_shape, head_shape, vt_shape],
        scratch_shapes=[pltpu.VMEM((batch, HALO, LANES), F32)],
        compiler_params=_cparams(("arbitrary",)),
        name="fox_prep",
    )(seq3(q), seq3(k), seq3(v), seq3(f), jnp.tile(q_norm_w, N_HEADS).reshape(1, WIDTH),
      jnp.tile(k_norm_w, N_HEADS).reshape(1, WIDTH),
      jnp.pad(f_bias, (0, LANES - N_HEADS)).reshape(1, LANES), bd, spread)

    nt = seq // FOX_TILE
    pairs = [(i, j) for i in range(nt) for j in range(i, -1, -1)]
    qi = jnp.asarray([i for i, _ in pairs], jnp.int32)
    ki = jnp.asarray([j for _, j in pairs], jnp.int32)
    q_spec = pl.BlockSpec((1, 2, FOX_TILE, LANES), lambda b, p, t, qi, ki: (b, p, qi[t], 0))
    k_spec = pl.BlockSpec((1, 2, FOX_TILE, LANES), lambda b, p, t, qi, ki: (b, p, ki[t], 0))
    vt_spec = pl.BlockSpec((1, 2, LANES, FOX_TILE), lambda b, p, t, qi, ki: (b, p, 0, ki[t]))
    out = pl.pallas_call(
        _fox_kernel,
        grid_spec=pltpu.PrefetchScalarGridSpec(
            num_scalar_prefetch=2,
            grid=(batch, N_HEADS // 2, len(pairs)),
            in_specs=[q_spec, k_spec, vt_spec],
            out_specs=pl.BlockSpec((1, FOX_TILE, LANES), lambda b, p, t, qi, ki: (b, qi[t], p)),
            scratch_shapes=[pltpu.VMEM((2, 1, FOX_TILE), F32),
                            pltpu.VMEM((2, LANES, FOX_TILE), F32),
                            pltpu.SMEM((1,), jnp.int32)]),
        out_shape=jax.ShapeDtypeStruct((batch, seq, WIDTH), BF16),
        compiler_params=_cparams(("parallel", "parallel", "arbitrary")),
        name="fox_attention",
    )(qi, ki, qa, ka, va)
    return out.reshape(batch * seq, WIDTH)


def _pad_cols(w, width):
    return jnp.pad(w, ((0, 0), (0, width - w.shape[1])))


def _even_layer(x, batch, seq, norm_w, w_in, conv_w, conv_b, dt_bias, a_log, d_skip,
                ssd_norm_w, ret_norm_w, w_out):
    ssd_x0 = WIDTH
    ssd_dt0 = WIDTH + (WIDTH + 4 * SSD_STATE)
    ret0 = ssd_dt0 + N_HEADS
    w = jnp.concatenate([w_in[:, :ssd_dt0], w_in[:, ret0:],
                         _pad_cols(w_in[:, ssd_dt0:ret0], LANES)], axis=1).astype(BF16)
    segments = ((0, WIDTH), (ssd_x0, WIDTH), (2 * WIDTH, 2 * LANES),
                (ssd_dt0, WIDTH), (ssd_dt0 + WIDTH, WIDTH), (ssd_dt0 + 2 * WIDTH, WIDTH),
                (ssd_dt0 + 3 * WIDTH, WIDTH), (ssd_dt0 + 4 * WIDTH, LANES))
    z, xs, bc, q, k, v, g, dt = _inproj(x, norm_w, w, segments)
    y_ssd = _ssd(z, xs, bc, dt, conv_w, conv_b, dt_bias, a_log, d_skip, ssd_norm_w, batch, seq)
    y_ret = _retention(q, k, v, g, ret_norm_w, batch, seq)
    return _outproj(y_ssd, y_ret, w_out.astype(BF16), x)


def _odd_layer(x, batch, seq, norm_w, w_in, mu, w0, w_up, a0, a_up, g_up, k_k, k_a, r_k,
               ln_w, ln_b, q_norm_w, k_norm_w, f_bias, w_out):
    lo_width = w_up.shape[0] + a_up.shape[0] + g_up.shape[0]
    fox0 = 3 * WIDTH + lo_width
    f0 = fox0 + 3 * WIDTH
    w = jnp.concatenate([w_in[:, :f0], _pad_cols(w_in[:, f0:], LANES)], axis=1).astype(BF16)
    segments = ((0, WIDTH), (WIDTH, WIDTH), (2 * WIDTH, WIDTH), (3 * WIDTH, lo_width),
                (fox0, WIDTH), (fox0 + WIDTH, WIDTH), (fox0 + 2 * WIDTH, WIDTH), (f0, LANES))
    r, k, v, lo, fq, fk, fv, ff = _inproj(x, norm_w, w, segments)
    y_rwkv = _rwkv(r, k, v, lo, mu, w0, w_up, a0, a_up, g_up, k_k, k_a, r_k.reshape(-1),
                   ln_w, ln_b, batch, seq)
    y_fox = _fox(fq, fk, fv, ff, q_norm_w, k_norm_w, f_bias, batch, seq)
    return _outproj(y_rwkv, y_fox, w_out.astype(BF16), x)


def kernel(x, ev_norm_w, ev_w_in, ev_ssd_conv_w, ev_ssd_conv_b, ev_ssd_dt_bias, ev_ssd_a_log,
           ev_ssd_d, ev_ssd_norm_w, ev_ret_norm_w, ev_w_out,
           od_norm_w, od_w_in, od_rwkv_mu, od_rwkv_w0, od_rwkv_w_up, od_rwkv_a0, od_rwkv_a_up,
           od_rwkv_g_up, od_rwkv_k_k, od_rwkv_k_a, od_rwkv_r_k, od_rwkv_ln_w, od_rwkv_ln_b,
           od_fox_q_norm_w, od_fox_k_norm_w, od_fox_f_bias, od_w_out,
           ffn_norm_w, ffn_w_up, ffn_conv_w, ffn_conv_b, ffn_w_down):
    batch, seq, _ = x.shape
    depth = ffn_norm_w.shape[0]
    h = x.reshape(batch * seq, D_MODEL)
    for layer in range(depth):
        i = layer // 2
        if layer % 2 == 0:
            h = _even_layer(h, batch, seq, ev_norm_w[i], ev_w_in[i], ev_ssd_conv_w[i],
                            ev_ssd_conv_b[i], ev_ssd_dt_bias[i], ev_ssd_a_log[i], ev_ssd_d[i],
                            ev_ssd_norm_w[i], ev_ret_norm_w[i], ev_w_out[i])
        else:
            h = _odd_layer(h, batch, seq, od_norm_w[i], od_w_in[i], od_rwkv_mu[i], od_rwkv_w0[i],
                           od_rwkv_w_up[i], od_rwkv_a0[i], od_rwkv_a_up[i], od_rwkv_g_up[i],
                           od_rwkv_k_k[i], od_rwkv_k_a[i], od_rwkv_r_k[i], od_rwkv_ln_w[i],
                           od_rwkv_ln_b[i], od_fox_q_norm_w[i], od_fox_k_norm_w[i],
                           od_fox_f_bias[i], od_w_out[i])
        h = _ffn(h, seq, ffn_norm_w[layer], ffn_w_up[layer], ffn_conv_w[layer],
                 ffn_conv_b[layer], ffn_w_down[layer])
    return h.reshape(batch, seq, D_MODEL)
```

```python
import functools
import itertools
import math

import jax
import jax.numpy as jnp
from jax import lax
from jax.experimental import pallas as pl
from jax.experimental.pallas import tpu as pltpu

F32 = jnp.float32
BF16 = jnp.bfloat16

D_MODEL = 1024
HEAD_DIM = 64
N_HEADS = 8
WIDTH = N_HEADS * HEAD_DIM
LANES = 128
CHUNK = 128
HALO = 8
EPS = 1e-6
SSD_CONV = 4
SSD_STATE = 64
RWKV_LN_EPS = 64e-5
RWKV_DECAY_SCALE = 0.606531
D_FF = 2816
FF_TILE = 256
FFN_HALO = 16
NEG = -1e30
VMEM_LIMIT = 56 * 1024 * 1024


def _mm(a, b):
    return jnp.dot(a.astype(BF16), b.astype(BF16), preferred_element_type=F32)


def _mm_nt(a, b):
    return lax.dot_general(a.astype(BF16), b.astype(BF16), (((1,), (1,)), ((), ())),
                           preferred_element_type=F32)


def _mm_tn(a, b):
    return lax.dot_general(a.astype(BF16), b.astype(BF16), (((0,), (0,)), ((), ())),
                           preferred_element_type=F32)


def _mm_split(lhs01, x, pieces):
    acc = None
    rest = x
    for _ in range(pieces):
        part = rest.astype(BF16)
        term = jnp.dot(lhs01, part, preferred_element_type=F32)
        acc = term if acc is None else acc + term
        rest = rest - part.astype(F32)
    return acc


def _mm_split_rhs(x, rhs01, pieces):
    acc = None
    rest = x
    for _ in range(pieces):
        part = rest.astype(BF16)
        term = jnp.dot(part, rhs01, preferred_element_type=F32)
        acc = term if acc is None else acc + term
        rest = rest - part.astype(F32)
    return acc


def _interleave(stage_generators):
    for _ in itertools.zip_longest(*stage_generators):
        pass


def _sigmoid(x):
    return 1.0 / (1.0 + jnp.exp(-x))


def _silu(x):
    return x * _sigmoid(x)


def _softplus(x):
    return jnp.maximum(x, 0.0) + jnp.log1p(jnp.exp(-jnp.abs(x)))


def _iota2(shape, dim):
    return lax.broadcasted_iota(jnp.int32, shape, dim)


def _head_mask(j, width=LANES):
    lane = _iota2((1, width), 1)
    return ((lane % LANES) // HEAD_DIM == j).astype(F32)


def _cparams(sem):
    return pltpu.CompilerParams(dimension_semantics=sem, vmem_limit_bytes=VMEM_LIMIT)


def _const_spec(shape):
    nd = len(shape)
    return pl.BlockSpec(shape, lambda *_: (0,) * nd, pipeline_mode=pl.Buffered(1))


def _inproj_kernel(x_ref, nw_ref, w_ref, *o_refs, segments):
    x = x_ref[...]
    ms = jnp.mean(x * x, axis=-1, keepdims=True)
    xn = (x * lax.rsqrt(ms + EPS) * nw_ref[...]).astype(BF16)
    for o_ref, (off, width) in zip(o_refs, segments):
        o_ref[...] = jnp.dot(xn, w_ref[:, off:off + width], preferred_element_type=F32)


def _inproj(x, norm_w, w, segments, tm=512):
    t = x.shape[0]
    n = w.shape[1]
    return pl.pallas_call(
        functools.partial(_inproj_kernel, segments=segments),
        grid=(t // tm,),
        in_specs=[pl.BlockSpec((tm, D_MODEL), lambda i: (i, 0)),
                  _const_spec((1, D_MODEL)),
                  _const_spec((D_MODEL, n))],
        out_specs=[pl.BlockSpec((tm, wd), lambda i: (i, 0)) for _, wd in segments],
        out_shape=[jax.ShapeDtypeStruct((t, wd), F32) for _, wd in segments],
        compiler_params=_cparams(("parallel",)),
        name="inproj",
    )(x, norm_w.reshape(1, D_MODEL), w)


def _proj_ffn_kernel(ya_ref, yah_ref, yb_ref, ybh_ref, x_ref, xh_ref, wo_ref, nw_ref,
                     wup_ref, cw_ref, cb_ref, wd_ref, o_ref,
                     ya_ext, yb_ext, h_ref, hn_ref, g_ref, acc_ref, *, tm, tiles_per_seq):
    ya_ext[:FFN_HALO, :] = yah_ref[...]
    ya_ext[FFN_HALO:, :] = ya_ref[...]
    yb_ext[:FFN_HALO, :] = ybh_ref[...]
    yb_ext[FFN_HALO:, :] = yb_ref[...]
    h_ref[:FFN_HALO, :] = xh_ref[...]
    h_ref[FFN_HALO:, :] = x_ref[...]
    h_ref[...] += (jnp.dot(ya_ext[...], wo_ref[:WIDTH, :], preferred_element_type=F32)
                   + jnp.dot(yb_ext[...], wo_ref[WIDTH:, :], preferred_element_type=F32))

    h = h_ref[...]
    ms = jnp.mean(h * h, axis=-1, keepdims=True)
    hn = h * lax.rsqrt(ms + EPS) * nw_ref[...]
    first_of_seq = pl.program_id(0) % tiles_per_seq == 0
    halo_rows = _iota2((tm + FFN_HALO, 1), 0) < FFN_HALO
    hn_ref[...] = jnp.where(halo_rows & first_of_seq, 0.0, hn).astype(BF16)
    acc_ref[...] = jnp.zeros_like(acc_ref)

    def tile(f, base=0):
        return slice(base + f * FF_TILE, base + (f + 1) * FF_TILE)

    def up_proj(f):
        g_ref[f % 2] = jnp.dot(hn_ref[...], wup_ref[:, tile(f)], preferred_element_type=F32)
        return jnp.dot(hn_ref[FFN_HALO:, :], wup_ref[:, tile(f, D_FF)],
                       preferred_element_type=F32)

    nf = D_FF // FF_TILE
    up = up_proj(0)
    for f in range(nf):
        up_next = up_proj(f + 1) if f + 1 < nf else None
        g = g_ref.at[f % 2]
        cw = cw_ref[:, tile(f)]
        gate = (cb_ref[:, tile(f)]
                + cw[0:1, :] * g[FFN_HALO - 2:FFN_HALO - 2 + tm, :]
                + cw[1:2, :] * g[FFN_HALO - 1:FFN_HALO - 1 + tm, :]
                + cw[2:3, :] * g[FFN_HALO:, :])
        act = (_silu(gate) * up).astype(BF16)
        acc_ref[...] += jnp.dot(act, wd_ref[tile(f), :], preferred_element_type=F32)
        up = up_next
    o_ref[...] = h_ref[FFN_HALO:, :] + acc_ref[...]


def _proj_ffn(ya, yb, x, seq, w_out, norm_w, w_up, conv_w, conv_b, w_down, tm=512):
    t = x.shape[0]
    halo_per_tile = tm // FFN_HALO
    row_spec = lambda width: pl.BlockSpec((tm, width), lambda i: (i, 0))
    halo_spec = lambda width: pl.BlockSpec(
        (FFN_HALO, width), lambda i: (jnp.maximum(i * halo_per_tile - 1, 0), 0))
    ext = lambda width, dtype: pltpu.VMEM((tm + FFN_HALO, width), dtype)
    return pl.pallas_call(
        functools.partial(_proj_ffn_kernel, tm=tm, tiles_per_seq=seq // tm),
        grid=(t // tm,),
        in_specs=[row_spec(WIDTH), halo_spec(WIDTH), row_spec(WIDTH), halo_spec(WIDTH),
                  row_spec(D_MODEL), halo_spec(D_MODEL),
                  _const_spec((2 * WIDTH, D_MODEL)), _const_spec((1, D_MODEL)),
                  _const_spec((D_MODEL, 2 * D_FF)), _const_spec((3, D_FF)), _const_spec((1, D_FF)),
                  _const_spec((D_FF, D_MODEL))],
        out_specs=row_spec(D_MODEL),
        out_shape=jax.ShapeDtypeStruct((t, D_MODEL), F32),
        scratch_shapes=[ext(WIDTH, BF16), ext(WIDTH, BF16), ext(D_MODEL, F32), ext(D_MODEL, BF16),
                        pltpu.VMEM((2, tm + FFN_HALO, FF_TILE), F32),
                        pltpu.VMEM((tm, D_MODEL), F32)],
        compiler_params=_cparams(("parallel",)),
        name="proj_convffn",
    )(ya, ya, yb, yb, x, x, w_out.astype(BF16), norm_w.reshape(1, D_MODEL),
      w_up.astype(BF16), conv_w, conv_b.reshape(1, D_FF), w_down.astype(BF16))


def _seq_spec(batch, width):
    return pl.BlockSpec((batch, CHUNK, width), lambda c: (0, c, 0))


def _seq_halo_spec(batch, width):
    per = CHUNK // HALO
    return pl.BlockSpec((batch, HALO, width), lambda c: (0, jnp.maximum(c * per - 1, 0), 0))


def _param_spec1(shape):
    return pl.BlockSpec(shape, lambda c: (0,) * len(shape), pipeline_mode=pl.Buffered(1))


def _with_prev_rows(scr_ref, main_ref, halo_ref, keep):
    scr_ref[:HALO, :] = halo_ref[...] * keep
    scr_ref[HALO:, :] = main_ref[...]


def _ssd_chunk(b, keep, z_ref, x_ref, xh_ref, bc_ref, bch_ref, dt_ref,
               cwx_ref, cbx_ref, cwb_ref, cbb_ref, dtb_ref, alog_ref, dexp_ref, nw_ref, e_ref,
               o_ref, xe_ref, bce_ref, st_ref, y_ref):
    _with_prev_rows(xe_ref.at[b], x_ref.at[b], xh_ref.at[b], keep)
    _with_prev_rows(bce_ref.at[b], bc_ref.at[b], bch_ref.at[b], keep)

    def conv(ref, cw_ref, cb_ref):
        out = cb_ref[...]
        for k in range(SSD_CONV):
            lo = HALO - (SSD_CONV - 1) + k
            out = out + cw_ref[k:k + 1, :] * ref[b, lo:lo + CHUNK, :]
        return _silu(out)

    x = conv(xe_ref, cwx_ref, cbx_ref)
    bc = conv(bce_ref, cwb_ref, cbb_ref)
    bm = bc[:, :LANES]
    cm = bc[:, LANES:]

    row = _iota2((CHUNK, CHUNK), 0)
    col = _iota2((CHUNK, CHUNK), 1)
    causal = row >= col

    dt = _softplus(dt_ref[b] + dtb_ref[...])
    a = dt * (-jnp.exp(alog_ref[...]))
    a_cum = _mm_split(causal.astype(BF16), a, 3)
    yield
    a_cum_t = a_cum.T
    a_last = a_cum[CHUNK - 1:CHUNK, :]
    per_head = jnp.concatenate(
        [dt, jnp.exp(a_cum), jnp.exp(a_last - a_cum),
         jnp.broadcast_to(jnp.exp(a_last), (HALO, LANES))], axis=0)
    per_lane = _mm_split_rhs(per_head, e_ref[...], 2)
    dt_e = per_lane[:CHUNK]
    ea_e = per_lane[CHUNK:2 * CHUNK]
    te_e = per_lane[2 * CHUNK:3 * CHUNK]
    cd_e = per_lane[3 * CHUNK:3 * CHUNK + 1]
    yield

    xdt = x * dt_e
    xs = xdt * te_e
    bm_t = bm.T
    group_rows = _iota2((LANES, 1), 0) // SSD_STATE

    for g in range(2):
        cg = cm * _head_mask(g)
        cb = _mm_nt(cg, bm)
        gs = slice(g * 2 * LANES, (g + 1) * 2 * LANES)
        st = st_ref[b, g]
        y_off = _mm(cg, st) * ea_e[:, gs]
        st_new = _mm(bm_t, xs[:, gs])
        st_ref[b, g] = st * cd_e[:, gs] + jnp.where(group_rows == g, st_new, 0.0)
        yield
        for pp in range(2):
            p = 2 * g + pp
            xp = xdt[:, p * LANES:(p + 1) * LANES]
            yp = y_off[:, pp * LANES:(pp + 1) * LANES]
            for j in range(2):
                h = 2 * p + j
                seg = a_cum[:, h:h + 1] - a_cum_t[h:h + 1, :]
                decay = jnp.where(causal, jnp.exp(jnp.minimum(seg, 0.0)), 0.0)
                yp = yp + _mm(cb * decay, xp * _head_mask(j))
            y_ref[b, :, p * LANES:(p + 1) * LANES] = yp
            yield

    y = y_ref[b] + dexp_ref[...] * x
    gated = y * _silu(z_ref[b])
    ms = jnp.mean(gated * gated, axis=-1, keepdims=True)
    o_ref[b] = (gated * lax.rsqrt(ms + EPS) * nw_ref[...]).astype(BF16)


def _ssd_kernel(*refs):
    c = pl.program_id(0)
    st_ref = refs[-2]

    @pl.when(c == 0)
    def _():
        st_ref[...] = jnp.zeros_like(st_ref)

    keep = (c > 0).astype(F32)
    _interleave([_ssd_chunk(b, keep, *refs) for b in range(st_ref.shape[0])])


def _ssd(z, x, bc, dt, conv_w, conv_b, dt_bias, a_log, d_skip, norm_w, batch, seq):
    nc = seq // CHUNK
    pad = LANES - N_HEADS
    e = jnp.repeat(jnp.eye(LANES, N_HEADS, dtype=BF16), HEAD_DIM, axis=1)
    seq3 = lambda a: a.reshape(batch, seq, a.shape[-1])
    z, x, bc, dt = seq3(z), seq3(x), seq3(bc), seq3(dt)
    args = (z, x, x, bc, bc, dt,
            conv_w[:, :WIDTH], conv_b[:WIDTH].reshape(1, WIDTH),
            conv_w[:, WIDTH:], conv_b[WIDTH:].reshape(1, 2 * LANES),
            jnp.pad(dt_bias, (0, pad)).reshape(1, LANES),
            jnp.pad(a_log, (0, pad)).reshape(1, LANES),
            jnp.repeat(d_skip, HEAD_DIM).reshape(1, WIDTH),
            norm_w.reshape(1, WIDTH), e)
    out = pl.pallas_call(
        _ssd_kernel,
        grid=(nc,),
        in_specs=[_seq_spec(batch, WIDTH), _seq_spec(batch, WIDTH), _seq_halo_spec(batch, WIDTH),
                  _seq_spec(batch, 2 * LANES), _seq_halo_spec(batch, 2 * LANES),
                  _seq_spec(batch, LANES),
                  _param_spec1((SSD_CONV, WIDTH)), _param_spec1((1, WIDTH)),
                  _param_spec1((SSD_CONV, 2 * LANES)), _param_spec1((1, 2 * LANES)),
                  _param_spec1((1, LANES)), _param_spec1((1, LANES)),
                  _param_spec1((1, WIDTH)), _param_spec1((1, WIDTH)),
                  _param_spec1((LANES, WIDTH))],
        out_specs=_seq_spec(batch, WIDTH),
        out_shape=jax.ShapeDtypeStruct((batch, seq, WIDTH), BF16),
        scratch_shapes=[pltpu.VMEM((batch, CHUNK + HALO, WIDTH), F32),
                        pltpu.VMEM((batch, CHUNK + HALO, 2 * LANES), F32),
                        pltpu.VMEM((batch, 2, LANES, 2 * LANES), F32),
                        pltpu.VMEM((batch, CHUNK, WIDTH), F32)],
        compiler_params=_cparams(("arbitrary",)),
        name="ssd",
    )(*args)
    return out.reshape(batch * seq, WIDTH)


def _ret_chunk(b, q_ref, k_ref, v_ref, g_ref, cos_ref, sin_ref, intra_ref, qdec_ref, kte_ref,
               cg_ref, bd_ref, nw_ref, o_ref, r_ref):
    lane = _iota2((1, WIDTH), 1)
    first_half = (lane % HEAD_DIM) < (HEAD_DIM // 2)
    cos = jnp.concatenate([cos_ref[...]] * (WIDTH // LANES), axis=1)
    sin = jnp.concatenate([sin_ref[...]] * (WIDTH // LANES), axis=1)

    def rotary(v):
        other = jnp.where(first_half,
                          pltpu.roll(v, WIDTH - HEAD_DIM // 2, 1),
                          pltpu.roll(v, HEAD_DIM // 2, 1))
        return v * cos + other * sin

    q = rotary(q_ref[b])
    k = rotary(k_ref[b]) * (HEAD_DIM ** -0.5)
    v = v_ref[b]
    bd = bd_ref[...]
    pair = lambda t, p: t[:, p * LANES:(p + 1) * LANES]
    heads = [(p, j) for p in range(N_HEADS // 2) for j in range(2)]
    masks = [_head_mask(j) for j in range(2)]

    y_cross = _mm(q * qdec_ref[...], r_ref[b])
    r_ref[b] = r_ref[b] * cg_ref[...] + bd * _mm_tn(k * kte_ref[...], v)
    scores = [_mm_nt(pair(q, p) * masks[j], pair(k, p)) for p, j in heads]
    yield
    y_pairs = []
    for p in range(N_HEADS // 2):
        yp = pair(y_cross, p)
        for j in range(2):
            h = 2 * p + j
            yp = yp + _mm(scores[h] * intra_ref[h], pair(v, p) * masks[j])
        y_pairs.append(yp)
    yield
    y = jnp.concatenate(y_pairs, axis=1)
    ms = _mm(y * y, bd) * (1.0 / HEAD_DIM)
    yield
    o_ref[b] = (_silu(g_ref[b]) * (y * lax.rsqrt(ms + EPS) * nw_ref[...])).astype(BF16)


def _ret_kernel(*refs):
    r_ref = refs[-1]

    @pl.when(pl.program_id(0) == 0)
    def _():
        r_ref[...] = jnp.zeros_like(r_ref)

    _interleave([_ret_chunk(b, *refs) for b in range(r_ref.shape[0])])


def _retention_tables(seq):
    half = HEAD_DIM // 2
    inv = 1.0 / (10000.0 ** (jnp.arange(half, dtype=F32) / half))
    ang = jnp.arange(seq, dtype=F32)[:, None] * inv[None, :]
    cos = jnp.tile(jnp.concatenate([jnp.cos(ang), jnp.cos(ang)], axis=-1), (1, LANES // HEAD_DIM))
    sin = jnp.tile(jnp.concatenate([-jnp.sin(ang), jnp.sin(ang)], axis=-1), (1, LANES // HEAD_DIM))
    log_gamma = jnp.log1p(-(2.0 ** (-5.0 - jnp.arange(N_HEADS, dtype=F32))))
    idx = jnp.arange(CHUNK, dtype=F32)
    rel = idx[:, None] - idx[None, :]
    intra = jnp.where(rel[None] >= 0,
                      jnp.exp(jnp.maximum(rel, 0.0)[None] * log_gamma[:, None, None]), 0.0)
    expand = lambda m: jnp.repeat(m, HEAD_DIM, axis=1)
    qdec = expand(jnp.exp((idx + 1.0)[:, None] * log_gamma[None, :]))
    kte = expand(jnp.exp((CHUNK - 1 - idx)[:, None] * log_gamma[None, :]))
    cgam = expand(jnp.exp(CHUNK * log_gamma)[None, :])
    head = jnp.arange(WIDTH) // HEAD_DIM
    bd = (head[:, None] == head[None, :]).astype(BF16)
    return cos, sin, intra, qdec, kte, cgam, bd


def _retention(q, k, v, g, norm_w, batch, seq):
    nc = seq // CHUNK
    cos, sin, intra, qdec, kte, cgam, bd = _retention_tables(seq)
    seq3 = lambda a: a.reshape(batch, seq, a.shape[-1])
    table_spec = pl.BlockSpec((CHUNK, LANES), lambda c: (c, 0))
    out = pl.pallas_call(
        _ret_kernel,
        grid=(nc,),
        in_specs=[_seq_spec(batch, WIDTH)] * 4 + [table_spec, table_spec,
                  _param_spec1((N_HEADS, CHUNK, CHUNK)),
                  _param_spec1((CHUNK, WIDTH)), _param_spec1((CHUNK, WIDTH)),
                  _param_spec1((1, WIDTH)), _param_spec1((WIDTH, WIDTH)),
                  _param_spec1((1, WIDTH))],
        out_specs=_seq_spec(batch, WIDTH),
        out_shape=jax.ShapeDtypeStruct((batch, seq, WIDTH), BF16),
        scratch_shapes=[pltpu.VMEM((batch, WIDTH, WIDTH), F32)],
        compiler_params=_cparams(("arbitrary",)),
        name="retention",
    )(seq3(q), seq3(k), seq3(v), seq3(g), cos, sin, intra, qdec, kte, cgam, bd,
      norm_w.reshape(1, WIDTH))
    return out.reshape(batch * seq, WIDTH)


def _rwkv_kernel(r_ref, rh_ref, k_ref, kh_ref, v_ref, vh_ref, lo_ref, loh_ref,
                 mur_ref, muk_ref, muv_ref, mulo_ref, w0_ref, wup_ref, a0_ref, aup_ref, gup_ref,
                 kkw_ref, kaw_ref, rkw_ref, lnw_ref, lnb_ref, bd_ref,
                 o_ref, s_ref, sr_ref, sk_ref, sv_ref, slo_ref):
    c = pl.program_id(1)

    @pl.when(c == 0)
    def _():
        s_ref[...] = jnp.zeros_like(s_ref)

    keep = (c > 0).astype(F32)
    nb, _, width = r_ref.shape
    npair = width // LANES
    bd = bd_ref[...]
    row = _iota2((CHUNK, CHUNK), 0)
    col = _iota2((CHUNK, CHUNK), 1)
    strict = row > col
    causal = row >= col
    eye = (row == col).astype(F32)
    tri = causal.astype(BF16)
    same_head = (row // HEAD_DIM == col // HEAD_DIM).astype(F32)
    m1 = [_head_mask(j) for j in range(2)]
    m2 = [_head_mask(j, 2 * LANES) for j in range(2)]
    cat0 = lambda xs: jnp.concatenate(xs, axis=0)
    cat1 = lambda xs: jnp.concatenate(xs, axis=1)
    C = CHUNK

    def token_shift(b, main_ref, halo_ref, mu_ref, scr_ref):
        _with_prev_rows(scr_ref.at[b], main_ref.at[b], halo_ref.at[b], keep)
        cur = main_ref[b]
        prev = scr_ref[b, HALO - 1:HALO - 1 + CHUNK, :]
        return cur + (prev - cur) * mu_ref[...]

    pre = []
    for b in range(nb):
        r = token_shift(b, r_ref, rh_ref, mur_ref, sr_ref)
        k = token_shift(b, k_ref, kh_ref, muk_ref, sk_ref)
        v = token_shift(b, v_ref, vh_ref, muv_ref, sv_ref)
        lo = token_shift(b, lo_ref, loh_ref, mulo_ref, slo_ref)
        lo_wa = lo[:, :LANES]
        log_w = -RWKV_DECAY_SCALE * _sigmoid(w0_ref[...] + _mm(jnp.tanh(lo_wa), wup_ref[...]))
        a = _sigmoid(a0_ref[...] + _mm(lo_wa, aup_ref[...]))
        gate = _mm(_sigmoid(lo[:, LANES:]), gup_ref[...])
        kk = k * kkw_ref[...]
        kk = kk / jnp.maximum(jnp.sqrt(_mm(kk * kk, bd)), 1e-12)
        k = k * (1.0 + (a - 1.0) * kaw_ref[...])
        cum = _mm_split(tri, log_w, 2)
        mid = cum[C // 2 - 1:C // 2, :]
        last = cum[C - 1:C, :]
        inv_p = jnp.exp(mid - cum)
        to_end = jnp.exp(last - mid)
        pre.append(dict(
            r=r, k=k, v=v, gate=gate,
            a_t=-kk * jnp.exp(cum - log_w - mid), r_t=r * jnp.exp(cum - mid),
            b_t=kk * a * inv_p, k_t=k * inv_p, e_mid=jnp.exp(mid),
            b_end=kk * a * inv_p * to_end, k_end=k * inv_p * to_end, decay=jnp.exp(last)))

    units = [(b, q) for b in range(nb) for q in range(npair)]
    slab = lambda name: [pre[b][name][:, q * LANES:(q + 1) * LANES] for b, q in units]
    a_t, r_t, b_t, k_t, v_u = slab("a_t"), slab("r_t"), slab("b_t"), slab("k_t"), slab("v")
    e_mid, b_end, k_end, decay = slab("e_mid"), slab("b_end"), slab("k_end"), slab("decay")

    mm = [_mm_nt(cat0([a * m1[0], r * m1[0], a * m1[1], r * m1[1]]), cat0([bt, kt]))
          for a, r, bt, kt in zip(a_t, r_t, b_t, k_t)]
    chains = [(u, j) for u in range(len(units)) for j in range(2)]
    m_ab = [jnp.where(strict, mm[u][2 * j * C:(2 * j + 1) * C, :C], 0.0) for u, j in chains]
    m_ak = [jnp.where(strict, mm[u][2 * j * C:(2 * j + 1) * C, C:], 0.0) for u, j in chains]
    m_rb = [jnp.where(causal, mm[u][(2 * j + 1) * C:(2 * j + 2) * C, :C], 0.0) for u, j in chains]
    m_rk = [jnp.where(causal, mm[u][(2 * j + 1) * C:(2 * j + 2) * C, C:], 0.0) for u, j in chains]

    inv = [eye + m for m in m_ab]
    power = list(m_ab)
    for _ in range(int(math.log2(C)) - 1):
        power = [_mm(p, p) for p in power]
        inv = [i + _mm(i, p) for i, p in zip(inv, power)]

    n_units = range(len(units))
    per_head = lambda x2: x2[:C] * m2[0] + x2[C:] * m2[1]
    mv = [_mm(cat0([m_ak[2 * u], m_rk[2 * u], m_ak[2 * u + 1], m_rk[2 * u + 1]]), v_u[u])
          for u in n_units]
    mv_ak = [mv[u][:C] * m1[0] + mv[u][2 * C:3 * C] * m1[1] for u in n_units]
    mv_rk = [mv[u][C:2 * C] * m1[0] + mv[u][3 * C:] * m1[1] for u in n_units]
    z = [per_head(_mm(cat0([inv[2 * u], inv[2 * u + 1]]), cat1([a_t[u], mv_ak[u]])))
         for u in n_units]
    w = [per_head(_mm(cat0([m_rb[2 * u], m_rb[2 * u + 1]]), z[u])) for u in n_units]

    s = [s_ref[b, q] for b, q in units]
    uy = [_mm_nt(cat0([z[u][:, :LANES] * e_mid[u], (r_t[u] + w[u][:, :LANES]) * e_mid[u]]), s[u])
          for u in n_units]
    u_in = [uy[u][:C] + z[u][:, LANES:] for u in n_units]
    y = [uy[u][C:] + w[u][:, LANES:] + mv_rk[u] for u in n_units]
    s_inc = [_mm_tn(cat0([u_in[u], v_u[u]]), cat0([b_end[u], k_end[u]])) for u in n_units]
    for u, (b, q) in enumerate(units):
        s_ref[b, q] = s[u] * decay[u] + same_head * s_inc[u]

    for b in range(nb):
        yb = cat1([y[b * npair + q] for q in range(npair)])
        mean = _mm(yb, bd) * (1.0 / HEAD_DIM)
        d = yb - mean
        var = _mm(d * d, bd) * (1.0 / HEAD_DIM)
        yn = d * lax.rsqrt(var + RWKV_LN_EPS) * lnw_ref[...] + lnb_ref[...]
        bonus = _mm(pre[b]["r"] * pre[b]["k"] * rkw_ref[...], bd) * pre[b]["v"]
        o_ref[b] = ((yn + bonus) * pre[b]["gate"]).astype(BF16)


RWKV_PAIRS_PER_STEP = 4


def _rwkv(r, k, v, lo, mu, w0, w_up, a0, a_up, g_up, k_k, k_a, r_k, ln_w, ln_b, batch, seq):
    nc = seq // CHUNK
    per = CHUNK // HALO
    width = RWKV_PAIRS_PER_STEP * LANES
    lo_width = lo.shape[-1]
    rank = w_up.shape[0]
    prev_block = lambda c: jnp.maximum(c * per - 1, 0)

    slab_row = pl.BlockSpec((batch, CHUNK, width), lambda p, c: (0, c, p))
    slab_halo = pl.BlockSpec((batch, HALO, width), lambda p, c: (0, prev_block(c), p))
    lo_row = pl.BlockSpec((batch, CHUNK, lo_width), lambda p, c: (0, c, 0))
    lo_halo = pl.BlockSpec((batch, HALO, lo_width), lambda p, c: (0, prev_block(c), 0))
    slab_param = pl.BlockSpec((1, width), lambda p, c: (0, p))
    slab_weight = pl.BlockSpec((LANES, width), lambda p, c: (0, p))
    const = lambda shape: pl.BlockSpec(shape, lambda p, c: (0,) * len(shape))

    row1 = lambda a: a.reshape(1, -1)
    seq3 = lambda a: a.reshape(batch, seq, a.shape[-1])
    wup_pad = jnp.concatenate([w_up, jnp.zeros((LANES - rank, WIDTH), F32)], axis=0).astype(BF16)
    aup_pad = jnp.concatenate([jnp.zeros((LANES - rank, WIDTH), F32), a_up], axis=0).astype(BF16)
    head = jnp.arange(width) // HEAD_DIM
    bd = (head[:, None] == head[None, :]).astype(BF16)
    r, k, v, lo = seq3(r), seq3(k), seq3(v), seq3(lo)
    args = (r, r, k, k, v, v, lo, lo,
            row1(mu[:WIDTH]), row1(mu[WIDTH:2 * WIDTH]), row1(mu[2 * WIDTH:3 * WIDTH]),
            row1(mu[3 * WIDTH:]),
            row1(w0), wup_pad, row1(a0), aup_pad, g_up.astype(BF16),
            row1(k_k), row1(k_a), row1(r_k), row1(ln_w), row1(ln_b), bd)
    out = pl.pallas_call(
        _rwkv_kernel,
        grid=(WIDTH // width, nc),
        in_specs=[slab_row, slab_halo, slab_row, slab_halo, slab_row, slab_halo, lo_row, lo_halo,
                  slab_param, slab_param, slab_param, const((1, lo_width)),
                  slab_param, slab_weight, slab_param, slab_weight, slab_weight,
                  slab_param, slab_param, slab_param, slab_param, slab_param,
                  const((width, width))],
        out_specs=pl.BlockSpec((batch, CHUNK, width), lambda p, c: (0, c, p)),
        out_shape=jax.ShapeDtypeStruct((batch, seq, WIDTH), BF16),
        scratch_shapes=[pltpu.VMEM((batch, RWKV_PAIRS_PER_STEP, LANES, LANES), F32),
                        pltpu.VMEM((batch, CHUNK + HALO, width), F32),
                        pltpu.VMEM((batch, CHUNK + HALO, width), F32),
                        pltpu.VMEM((batch, CHUNK + HALO, width), F32),
                        pltpu.VMEM((batch, CHUNK + HALO, lo_width), F32)],
        compiler_params=_cparams(("parallel", "arbitrary")),
        name="rwkv7",
    )(*args)
    return out.reshape(batch * seq, WIDTH)


FOX_PREP_ROWS = 256
FOX_TILE = 1024
FOX_Q_SPLIT = 4
FOX_GUARD = 64.0
LOG2E = math.log2(math.e)


def _fox_prep_chunk(b, q_ref, k_ref, v_ref, f_ref, qw_ref, kw_ref, fb_ref, bd_ref, spread_ref,
                    qo_ref, ko_ref, vo_ref, carry_ref):
    bd = bd_ref[...]
    q, k, v = q_ref[b], k_ref[b], v_ref[b]
    q_ms = _mm(q * q, bd) * (1.0 / HEAD_DIM)
    k_ms = _mm(k * k, bd) * (1.0 / HEAD_DIM)

    f = f_ref[b] + fb_ref[...]
    log_f = jnp.minimum(f, 0.0) - jnp.log1p(jnp.exp(-jnp.abs(f)))
    rows = FOX_PREP_ROWS
    tri = (_iota2((rows, rows), 0) >= _iota2((rows, rows), 1)).astype(BF16)
    cum = _mm_split(tri, log_f, 3) + carry_ref[b, 0:1, :]
    carry_ref[b] = jnp.broadcast_to(cum[rows - 1:rows, :], carry_ref.shape[1:])
    yield
    q = q * lax.rsqrt(q_ms + EPS) * qw_ref[...] * (HEAD_DIM ** -0.5 * LOG2E)
    k = k * lax.rsqrt(k_ms + EPS) * kw_ref[...]
    c_all = _mm_split_rhs(cum * LOG2E, spread_ref[...], 3)
    yield

    lane = _iota2((1, LANES), 1)
    for h in range(N_HEADS):
        p, j = divmod(h, 2)
        ps = slice(p * LANES, (p + 1) * LANES)
        mj = _head_mask(j)
        qh, kh = q[:, ps] * mj, k[:, ps] * mj
        if j == 1:
            qh = pltpu.roll(qh, HEAD_DIM, 1)
            kh = pltpu.roll(kh, HEAD_DIM, 1)
        ch = c_all[:, h * LANES:(h + 1) * LANES]
        hi = ch.astype(BF16).astype(F32)
        mid = (ch - hi).astype(BF16).astype(F32)
        low = ch - hi - mid
        piece = lambda base: jnp.where(lane == base, hi,
                                       jnp.where(lane == base + 1, mid,
                                                 jnp.where(lane == base + 2, low, 0.0)))
        ones = lambda base: ((lane >= base) & (lane < base + 3)).astype(F32)
        qo_ref[b, h] = (qh + piece(HEAD_DIM) + ones(HEAD_DIM + 3)).astype(BF16)
        ko_ref[b, h] = (kh + ones(HEAD_DIM) - piece(HEAD_DIM + 3)).astype(BF16)
    row_head = _iota2((LANES, 1), 0) // HEAD_DIM
    row_in_head = _iota2((LANES, 1), 0) % HEAD_DIM
    for p in range(N_HEADS // 2):
        v_t = v[:, p * LANES:(p + 1) * LANES].T
        for j in range(2):
            ones_row = ((row_head != j) & (row_in_head == 0)).astype(F32)
            vo_ref[b, 2 * p + j] = jnp.where(row_head == j, v_t, ones_row).astype(BF16)


def _fox_prep_kernel(*refs):
    carry_ref = refs[-1]

    @pl.when(pl.program_id(0) == 0)
    def _():
        carry_ref[...] = jnp.zeros_like(carry_ref)

    _interleave([_fox_prep_chunk(b, *refs) for b in range(carry_ref.shape[0])])


def _fox_kernel(qi_ref, ki_ref, q_ref, k_ref, vt_ref, o_ref, m_ref, acc_ref, redo_ref):
    t = pl.program_id(2)
    qi = qi_ref[t]
    ki = ki_ref[t]

    part = FOX_TILE // FOX_Q_SPLIT
    units = [(j, slice(h * part, (h + 1) * part)) for j in range(2) for h in range(FOX_Q_SPLIT)]

    def all_scores(on_diagonal):
        n_keys = lambda qs: qs.stop if on_diagonal else FOX_TILE
        raw = [lax.dot_general(k_ref[0, j, :n_keys(qs), :], q_ref[0, j, qs, :],
                               (((1,), (1,)), ((), ())), preferred_element_type=F32)
               for j, qs in units]
        if not on_diagonal:
            return raw
        masked = []
        for (_, qs), s in zip(units, raw):
            key = _iota2(s.shape, 0)
            query = _iota2(s.shape, 1) + qs.start
            masked.append(jnp.where(query >= key, s, NEG))
        return masked

    def pv(j, p):
        return jnp.dot(vt_ref[0, j, :, :p.shape[0]], p.astype(BF16), preferred_element_type=F32)

    def exact_step(on_diagonal):
        for (j, qs), s in zip(units, all_scores(on_diagonal)):
            m_prev = m_ref[j, :, qs]
            m_next = jnp.maximum(m_prev, jnp.max(s, axis=0, keepdims=True))
            alpha = jnp.exp2(m_prev - m_next)
            acc_ref[j, :, qs] = alpha * acc_ref[j, :, qs] + pv(j, jnp.exp2(s - m_next))
            m_ref[j, :, qs] = m_next

    def lagged_step(on_diagonal):
        done = []
        worst = None
        for (j, qs), s in zip(units, all_scores(on_diagonal)):
            m_used = m_ref[j, :, qs]
            block_max = jnp.max(s, axis=0, keepdims=True)
            contrib = pv(j, jnp.exp2(s - m_used))
            excess = jnp.max(block_max - m_used)
            worst = excess if worst is None else jnp.maximum(worst, excess)
            done.append((j, qs, m_used, block_max, contrib))
        ok = worst <= FOX_GUARD
        redo_ref[0] = jnp.where(ok, 0, 1)

        @pl.when(ok)
        def _():
            for j, qs, m_used, block_max, contrib in done:
                m_next = jnp.maximum(m_used, block_max)
                acc_ref[j, :, qs] = (acc_ref[j, :, qs] + contrib) * jnp.exp2(m_used - m_next)
                m_ref[j, :, qs] = m_next

    @pl.when(ki == qi)
    def _():
        m_ref[...] = jnp.full_like(m_ref, NEG)
        acc_ref[...] = jnp.zeros_like(acc_ref)
        exact_step(True)

    redo_ref[0] = 0

    @pl.when(ki < qi)
    def _():
        lagged_step(False)

    @pl.when(redo_ref[0] == 1)
    def _():
        exact_step(False)

    @pl.when(ki == 0)
    def _():
        head_rows = _iota2((LANES, 1), 0) // HEAD_DIM
        out_t = jnp.where(head_rows == 0,
                          acc_ref[0] / acc_ref[0, HEAD_DIM:HEAD_DIM + 1, :],
                          acc_ref[1] / acc_ref[1, 0:1, :])
        o_ref[0] = out_t.T.astype(BF16)


def _fox(q, k, v, f, q_norm_w, k_norm_w, f_bias, batch, seq):
    rows = FOX_PREP_ROWS
    nr = seq // rows
    head = jnp.arange(WIDTH) // HEAD_DIM
    bd = (head[:, None] == head[None, :]).astype(BF16)
    dst = jnp.arange(N_HEADS * LANES)
    spread = ((dst[None, :] // LANES == jnp.arange(LANES)[:, None])
              & (dst[None, :] % LANES >= HEAD_DIM) & (dst[None, :] % LANES < HEAD_DIM + 6)).astype(BF16)
    seq3 = lambda a: a.reshape(batch, seq, a.shape[-1])
    row_spec = lambda width: pl.BlockSpec((batch, rows, width), lambda i: (0, i, 0))
    head_out = pl.BlockSpec((batch, N_HEADS, rows, LANES), lambda i: (0, 0, i, 0))
    head_shape = jax.ShapeDtypeStruct((batch, N_HEADS, seq, LANES), BF16)
    vt_out = pl.BlockSpec((batch, N_HEADS, LANES, rows), lambda i: (0, 0, 0, i))
    vt_shape = jax.ShapeDtypeStruct((batch, N_HEADS, LANES, seq), BF16)
    qa, ka, va = pl.pallas_call(
        _fox_prep_kernel,
        grid=(nr,),
        in_specs=[row_spec(WIDTH), row_spec(WIDTH), row_spec(WIDTH), row_spec(LANES),
                  _param_spec1((1, WIDTH)), _param_spec1((1, WIDTH)), _param_spec1((1, LANES)),
                  _param_spec1((WIDTH, WIDTH)), _param_spec1((LANES, N_HEADS * LANES))],
        out_specs=[head_out, head_out, vt_out],
        out_shape=[head_shape, head_shape, vt_shape],
        scratch_shapes=[pltpu.VMEM((batch, HALO, LANES), F32)],
        compiler_params=_cparams(("arbitrary",)),
        name="fox_prep",
    )(seq3(q), seq3(k), seq3(v), seq3(f), jnp.tile(q_norm_w, N_HEADS).reshape(1, WIDTH),
      jnp.tile(k_norm_w, N_HEADS).reshape(1, WIDTH),
      jnp.pad(f_bias, (0, LANES - N_HEADS)).reshape(1, LANES), bd, spread)

    nt = seq // FOX_TILE
    pairs = [(i, j) for i in range(nt) for j in range(i, -1, -1)]
    qi = jnp.asarray([i for i, _ in pairs], jnp.int32)
    ki = jnp.asarray([j for _, j in pairs], jnp.int32)
    q_spec = pl.BlockSpec((1, 2, FOX_TILE, LANES), lambda b, p, t, qi, ki: (b, p, qi[t], 0))
    k_spec = pl.BlockSpec((1, 2, FOX_TILE, LANES), lambda b, p, t, qi, ki: (b, p, ki[t], 0))
    vt_spec = pl.BlockSpec((1, 2, LANES, FOX_TILE), lambda b, p, t, qi, ki: (b, p, 0, ki[t]))
    out = pl.pallas_call(
        _fox_kernel,
        grid_spec=pltpu.PrefetchScalarGridSpec(
            num_scalar_prefetch=2,
            grid=(batch, N_HEADS // 2, len(pairs)),
            in_specs=[q_spec, k_spec, vt_spec],
            out_specs=pl.BlockSpec((1, FOX_TILE, LANES), lambda b, p, t, qi, ki: (b, qi[t], p)),
            scratch_shapes=[pltpu.VMEM((2, 1, FOX_TILE), F32),
                            pltpu.VMEM((2, LANES, FOX_TILE), F32),
                            pltpu.SMEM((1,), jnp.int32)]),
        out_shape=jax.ShapeDtypeStruct((batch, seq, WIDTH), BF16),
        compiler_params=_cparams(("parallel", "parallel", "arbitrary")),
        name="fox_attention",
    )(qi, ki, qa, ka, va)
    return out.reshape(batch * seq, WIDTH)


def _pad_cols(w, width):
    return jnp.pad(w, ((0, 0), (0, width - w.shape[1])))


def _even_mixers(x, batch, seq, norm_w, w_in, conv_w, conv_b, dt_bias, a_log, d_skip,
                 ssd_norm_w, ret_norm_w):
    ssd_x0 = WIDTH
    ssd_dt0 = WIDTH + (WIDTH + 4 * SSD_STATE)
    ret0 = ssd_dt0 + N_HEADS
    w = jnp.concatenate([w_in[:, :ssd_dt0], w_in[:, ret0:],
                         _pad_cols(w_in[:, ssd_dt0:ret0], LANES)], axis=1).astype(BF16)
    segments = ((0, WIDTH), (ssd_x0, WIDTH), (2 * WIDTH, 2 * LANES),
                (ssd_dt0, WIDTH), (ssd_dt0 + WIDTH, WIDTH), (ssd_dt0 + 2 * WIDTH, WIDTH),
                (ssd_dt0 + 3 * WIDTH, WIDTH), (ssd_dt0 + 4 * WIDTH, LANES))
    z, xs, bc, q, k, v, g, dt = _inproj(x, norm_w, w, segments)
    y_ssd = _ssd(z, xs, bc, dt, conv_w, conv_b, dt_bias, a_log, d_skip, ssd_norm_w, batch, seq)
    y_ret = _retention(q, k, v, g, ret_norm_w, batch, seq)
    return y_ssd, y_ret


def _odd_mixers(x, batch, seq, norm_w, w_in, mu, w0, w_up, a0, a_up, g_up, k_k, k_a, r_k,
                ln_w, ln_b, q_norm_w, k_norm_w, f_bias):
    lo_width = w_up.shape[0] + a_up.shape[0] + g_up.shape[0]
    fox0 = 3 * WIDTH + lo_width
    f0 = fox0 + 3 * WIDTH
    w = jnp.concatenate([w_in[:, :f0], _pad_cols(w_in[:, f0:], LANES)], axis=1).astype(BF16)
    segments = ((0, WIDTH), (WIDTH, WIDTH), (2 * WIDTH, WIDTH), (3 * WIDTH, lo_width),
                (fox0, WIDTH), (fox0 + WIDTH, WIDTH), (fox0 + 2 * WIDTH, WIDTH), (f0, LANES))
    r, k, v, lo, fq, fk, fv, ff = _inproj(x, norm_w, w, segments)
    y_rwkv = _rwkv(r, k, v, lo, mu, w0, w_up, a0, a_up, g_up, k_k, k_a, r_k.reshape(-1),
                   ln_w, ln_b, batch, seq)
    y_fox = _fox(fq, fk, fv, ff, q_norm_w, k_norm_w, f_bias, batch, seq)
    return y_rwkv, y_fox


def kernel(x, ev_norm_w, ev_w_in, ev_ssd_conv_w, ev_ssd_conv_b, ev_ssd_dt_bias, ev_ssd_a_log,
           ev_ssd_d, ev_ssd_norm_w, ev_ret_norm_w, ev_w_out,
           od_norm_w, od_w_in, od_rwkv_mu, od_rwkv_w0, od_rwkv_w_up, od_rwkv_a0, od_rwkv_a_up,
           od_rwkv_g_up, od_rwkv_k_k, od_rwkv_k_a, od_rwkv_r_k, od_rwkv_ln_w, od_rwkv_ln_b,
           od_fox_q_norm_w, od_fox_k_norm_w, od_fox_f_bias, od_w_out,
           ffn_norm_w, ffn_w_up, ffn_conv_w, ffn_conv_b, ffn_w_down):
    batch, seq, _ = x.shape
    depth = ffn_norm_w.shape[0]
    h = x.reshape(batch * seq, D_MODEL)
    for layer in range(depth):
        i = layer // 2
        if layer % 2 == 0:
            ya, yb = _even_mixers(h, batch, seq, ev_norm_w[i], ev_w_in[i], ev_ssd_conv_w[i],
                                  ev_ssd_conv_b[i], ev_ssd_dt_bias[i], ev_ssd_a_log[i],
                                  ev_ssd_d[i], ev_ssd_norm_w[i], ev_ret_norm_w[i])
            w_out = ev_w_out[i]
        else:
            ya, yb = _odd_mixers(h, batch, seq, od_norm_w[i], od_w_in[i], od_rwkv_mu[i],
                                 od_rwkv_w0[i], od_rwkv_w_up[i], od_rwkv_a0[i], od_rwkv_a_up[i],
                                 od_rwkv_g_up[i], od_rwkv_k_k[i], od_rwkv_k_a[i], od_rwkv_r_k[i],
                                 od_rwkv_ln_w[i], od_rwkv_ln_b[i], od_fox_q_norm_w[i],
                                 od_fox_k_norm_w[i], od_fox_f_bias[i])
            w_out = od_w_out[i]
        h = _proj_ffn(ya, yb, h, seq, w_out, ffn_norm_w[layer], ffn_w_up[layer],
                      ffn_conv_w[layer], ffn_conv_b[layer], ffn_w_down[layer])
    return h.reshape(batch, seq, D_MODEL)
```

```python
import functools
import itertools
import math

import jax
import jax.numpy as jnp
from jax import lax
from jax.experimental import pallas as pl
from jax.experimental.pallas import tpu as pltpu

F32 = jnp.float32
BF16 = jnp.bfloat16

D_MODEL = 1024
HEAD_DIM = 64
N_HEADS = 8
WIDTH = N_HEADS * HEAD_DIM
LANES = 128
CHUNK = 128
HALO = 8
EPS = 1e-6
SSD_CONV = 4
SSD_STATE = 64
RWKV_LN_EPS = 64e-5
RWKV_DECAY_SCALE = 0.606531
D_FF = 2816
FF_TILE = 256
FFN_HALO = 16
NEG = -1e30
VMEM_LIMIT = 56 * 1024 * 1024


def _mm(a, b):
    return jnp.dot(a.astype(BF16), b.astype(BF16), preferred_element_type=F32)


def _mm_nt(a, b):
    return lax.dot_general(a.astype(BF16), b.astype(BF16), (((1,), (1,)), ((), ())),
                           preferred_element_type=F32)


def _mm_tn(a, b):
    return lax.dot_general(a.astype(BF16), b.astype(BF16), (((0,), (0,)), ((), ())),
                           preferred_element_type=F32)


def _mm_split(lhs01, x, pieces):
    acc = None
    rest = x
    for _ in range(pieces):
        part = rest.astype(BF16)
        term = jnp.dot(lhs01, part, preferred_element_type=F32)
        acc = term if acc is None else acc + term
        rest = rest - part.astype(F32)
    return acc


def _mm_split_rhs(x, rhs01, pieces):
    acc = None
    rest = x
    for _ in range(pieces):
        part = rest.astype(BF16)
        term = jnp.dot(part, rhs01, preferred_element_type=F32)
        acc = term if acc is None else acc + term
        rest = rest - part.astype(F32)
    return acc


def _interleave(stage_generators):
    for _ in itertools.zip_longest(*stage_generators):
        pass


def _sigmoid(x):
    return 1.0 / (1.0 + jnp.exp(-x))


def _silu(x):
    return x * _sigmoid(x)


def _softplus(x):
    return jnp.maximum(x, 0.0) + jnp.log1p(jnp.exp(-jnp.abs(x)))


def _iota2(shape, dim):
    return lax.broadcasted_iota(jnp.int32, shape, dim)


def _head_mask(j, width=LANES):
    lane = _iota2((1, width), 1)
    return ((lane % LANES) // HEAD_DIM == j).astype(F32)


def _cparams(sem):
    return pltpu.CompilerParams(dimension_semantics=sem, vmem_limit_bytes=VMEM_LIMIT)


def _const_spec(shape):
    nd = len(shape)
    return pl.BlockSpec(shape, lambda *_: (0,) * nd, pipeline_mode=pl.Buffered(1))


def _inproj_kernel(x_ref, nw_ref, w_ref, *o_refs, segments):
    x = x_ref[...]
    ms = jnp.mean(x * x, axis=-1, keepdims=True)
    xn = (x * lax.rsqrt(ms + EPS) * nw_ref[...]).astype(BF16)
    for o_ref, (off, width) in zip(o_refs, segments):
        o_ref[...] = jnp.dot(xn, w_ref[:, off:off + width], preferred_element_type=F32)


def _inproj(x, norm_w, w, segments, tm=1024):
    t = x.shape[0]
    n = w.shape[1]
    return pl.pallas_call(
        functools.partial(_inproj_kernel, segments=segments),
        grid=(t // tm,),
        in_specs=[pl.BlockSpec((tm, D_MODEL), lambda i: (i, 0)),
                  _const_spec((1, D_MODEL)),
                  _const_spec((D_MODEL, n))],
        out_specs=[pl.BlockSpec((tm, wd), lambda i: (i, 0)) for _, wd in segments],
        out_shape=[jax.ShapeDtypeStruct((t, wd), F32) for _, wd in segments],
        compiler_params=_cparams(("parallel",)),
        name="inproj",
    )(x, norm_w.reshape(1, D_MODEL), w)


def _proj_ffn_kernel(ya_ref, yah_ref, yb_ref, ybh_ref, x_ref, xh_ref, wo_ref, nw_ref,
                     wup_ref, cw_ref, cb_ref, wd_ref, o_ref,
                     ya_ext, yb_ext, h_ref, hn_ref, g_ref, acc_ref, *, tm, tiles_per_seq):
    ya_ext[:FFN_HALO, :] = yah_ref[...]
    ya_ext[FFN_HALO:, :] = ya_ref[...]
    yb_ext[:FFN_HALO, :] = ybh_ref[...]
    yb_ext[FFN_HALO:, :] = yb_ref[...]
    h_ref[:FFN_HALO, :] = xh_ref[...]
    h_ref[FFN_HALO:, :] = x_ref[...]
    h_ref[...] += (jnp.dot(ya_ext[...], wo_ref[:WIDTH, :], preferred_element_type=F32)
                   + jnp.dot(yb_ext[...], wo_ref[WIDTH:, :], preferred_element_type=F32))

    h = h_ref[...]
    ms = jnp.mean(h * h, axis=-1, keepdims=True)
    hn = h * lax.rsqrt(ms + EPS) * nw_ref[...]
    first_of_seq = pl.program_id(0) % tiles_per_seq == 0
    halo_rows = _iota2((tm + FFN_HALO, 1), 0) < FFN_HALO
    hn_ref[...] = jnp.where(halo_rows & first_of_seq, 0.0, hn).astype(BF16)
    acc_ref[...] = jnp.zeros_like(acc_ref)

    def tile(f, base=0):
        return slice(base + f * FF_TILE, base + (f + 1) * FF_TILE)

    def up_proj(f):
        g_ref[f % 2] = jnp.dot(hn_ref[...], wup_ref[:, tile(f)], preferred_element_type=F32)
        return jnp.dot(hn_ref[FFN_HALO:, :], wup_ref[:, tile(f, D_FF)],
                       preferred_element_type=F32)

    nf = D_FF // FF_TILE
    up = up_proj(0)
    for f in range(nf):
        up_next = up_proj(f + 1) if f + 1 < nf else None
        g = g_ref.at[f % 2]
        cw = cw_ref[:, tile(f)]
        gate = (cb_ref[:, tile(f)]
                + cw[0:1, :] * g[FFN_HALO - 2:FFN_HALO - 2 + tm, :]
                + cw[1:2, :] * g[FFN_HALO - 1:FFN_HALO - 1 + tm, :]
                + cw[2:3, :] * g[FFN_HALO:, :])
        act = (_silu(gate) * up).astype(BF16)
        acc_ref[...] += jnp.dot(act, wd_ref[tile(f), :], preferred_element_type=F32)
        up = up_next
    o_ref[...] = h_ref[FFN_HALO:, :] + acc_ref[...]


def _proj_ffn(ya, yb, x, seq, w_out, norm_w, w_up, conv_w, conv_b, w_down, tm=512):
    t = x.shape[0]
    halo_per_tile = tm // FFN_HALO
    row_spec = lambda width: pl.BlockSpec((tm, width), lambda i: (i, 0))
    halo_spec = lambda width: pl.BlockSpec(
        (FFN_HALO, width), lambda i: (jnp.maximum(i * halo_per_tile - 1, 0), 0))
    ext = lambda width, dtype: pltpu.VMEM((tm + FFN_HALO, width), dtype)
    return pl.pallas_call(
        functools.partial(_proj_ffn_kernel, tm=tm, tiles_per_seq=seq // tm),
        grid=(t // tm,),
        in_specs=[row_spec(WIDTH), halo_spec(WIDTH), row_spec(WIDTH), halo_spec(WIDTH),
                  row_spec(D_MODEL), halo_spec(D_MODEL),
                  _const_spec((2 * WIDTH, D_MODEL)), _const_spec((1, D_MODEL)),
                  _const_spec((D_MODEL, 2 * D_FF)), _const_spec((3, D_FF)), _const_spec((1, D_FF)),
                  _const_spec((D_FF, D_MODEL))],
        out_specs=row_spec(D_MODEL),
        out_shape=jax.ShapeDtypeStruct((t, D_MODEL), F32),
        scratch_shapes=[ext(WIDTH, BF16), ext(WIDTH, BF16), ext(D_MODEL, F32), ext(D_MODEL, BF16),
                        pltpu.VMEM((2, tm + FFN_HALO, FF_TILE), F32),
                        pltpu.VMEM((tm, D_MODEL), F32)],
        compiler_params=_cparams(("parallel",)),
        name="proj_convffn",
    )(ya, ya, yb, yb, x, x, w_out.astype(BF16), norm_w.reshape(1, D_MODEL),
      w_up.astype(BF16), conv_w, conv_b.reshape(1, D_FF), w_down.astype(BF16))


def _seq_spec(batch, width):
    return pl.BlockSpec((batch, CHUNK, width), lambda c: (0, c, 0))


def _seq_halo_spec(batch, width):
    per = CHUNK // HALO
    return pl.BlockSpec((batch, HALO, width), lambda c: (0, jnp.maximum(c * per - 1, 0), 0))


def _param_spec1(shape):
    return pl.BlockSpec(shape, lambda c: (0,) * len(shape), pipeline_mode=pl.Buffered(1))


def _with_prev_rows(scr_ref, main_ref, halo_ref, keep):
    scr_ref[:HALO, :] = halo_ref[...] * keep
    scr_ref[HALO:, :] = main_ref[...]


def _ssd_chunk(b, keep, z_ref, x_ref, xh_ref, bc_ref, bch_ref, dt_ref,
               cwx_ref, cbx_ref, cwb_ref, cbb_ref, dtb_ref, alog_ref, dexp_ref, nw_ref, e_ref,
               o_ref, xe_ref, bce_ref, st_ref, y_ref):
    _with_prev_rows(xe_ref.at[b], x_ref.at[b], xh_ref.at[b], keep)
    _with_prev_rows(bce_ref.at[b], bc_ref.at[b], bch_ref.at[b], keep)

    def conv(ref, cw_ref, cb_ref):
        out = cb_ref[...]
        for k in range(SSD_CONV):
            lo = HALO - (SSD_CONV - 1) + k
            out = out + cw_ref[k:k + 1, :] * ref[b, lo:lo + CHUNK, :]
        return _silu(out)

    x = conv(xe_ref, cwx_ref, cbx_ref)
    bc = conv(bce_ref, cwb_ref, cbb_ref)
    bm = bc[:, :LANES]
    cm = bc[:, LANES:]

    row = _iota2((CHUNK, CHUNK), 0)
    col = _iota2((CHUNK, CHUNK), 1)
    causal = row >= col

    dt = _softplus(dt_ref[b] + dtb_ref[...])
    a = dt * (-jnp.exp(alog_ref[...]))
    a_cum = _mm_split(causal.astype(BF16), a, 3)
    yield
    a_cum_t = a_cum.T
    a_last = a_cum[CHUNK - 1:CHUNK, :]
    per_head = jnp.concatenate(
        [dt, jnp.exp(a_cum), jnp.exp(a_last - a_cum),
         jnp.broadcast_to(jnp.exp(a_last), (HALO, LANES))], axis=0)
    per_lane = _mm_split_rhs(per_head, e_ref[...], 2)
    dt_e = per_lane[:CHUNK]
    ea_e = per_lane[CHUNK:2 * CHUNK]
    te_e = per_lane[2 * CHUNK:3 * CHUNK]
    cd_e = per_lane[3 * CHUNK:3 * CHUNK + 1]
    yield

    xdt = x * dt_e
    xs = xdt * te_e
    bm_t = bm.T
    group_rows = _iota2((LANES, 1), 0) // SSD_STATE

    for g in range(2):
        cg = cm * _head_mask(g)
        cb = _mm_nt(cg, bm)
        gs = slice(g * 2 * LANES, (g + 1) * 2 * LANES)
        st = st_ref[b, g]
        y_off = _mm(cg, st) * ea_e[:, gs]
        st_new = _mm(bm_t, xs[:, gs])
        st_ref[b, g] = st * cd_e[:, gs] + jnp.where(group_rows == g, st_new, 0.0)
        yield
        for pp in range(2):
            p = 2 * g + pp
            xp = xdt[:, p * LANES:(p + 1) * LANES]
            yp = y_off[:, pp * LANES:(pp + 1) * LANES]
            for j in range(2):
                h = 2 * p + j
                seg = a_cum[:, h:h + 1] - a_cum_t[h:h + 1, :]
                decay = jnp.where(causal, jnp.exp(jnp.minimum(seg, 0.0)), 0.0)
                yp = yp + _mm(cb * decay, xp * _head_mask(j))
            y_ref[b, :, p * LANES:(p + 1) * LANES] = yp
            yield

    y = y_ref[b] + dexp_ref[...] * x
    gated = y * _silu(z_ref[b])
    ms = jnp.mean(gated * gated, axis=-1, keepdims=True)
    o_ref[b] = (gated * lax.rsqrt(ms + EPS) * nw_ref[...]).astype(BF16)


def _ret_chunk(b, q_ref, k_ref, v_ref, g_ref, cos_ref, sin_ref, intra_ref, qdec_ref, kte_ref,
               cg_ref, bd_ref, nw_ref, o_ref, r_ref):
    lane = _iota2((1, WIDTH), 1)
    first_half = (lane % HEAD_DIM) < (HEAD_DIM // 2)
    cos = jnp.concatenate([cos_ref[...]] * (WIDTH // LANES), axis=1)
    sin = jnp.concatenate([sin_ref[...]] * (WIDTH // LANES), axis=1)

    def rotary(v):
        other = jnp.where(first_half,
                          pltpu.roll(v, WIDTH - HEAD_DIM // 2, 1),
                          pltpu.roll(v, HEAD_DIM // 2, 1))
        return v * cos + other * sin

    q = rotary(q_ref[b])
    k = rotary(k_ref[b]) * (HEAD_DIM ** -0.5)
    v = v_ref[b]
    bd = bd_ref[...]
    pair = lambda t, p: t[:, p * LANES:(p + 1) * LANES]
    heads = [(p, j) for p in range(N_HEADS // 2) for j in range(2)]
    masks = [_head_mask(j) for j in range(2)]

    y_cross = _mm(q * qdec_ref[...], r_ref[b])
    r_ref[b] = r_ref[b] * cg_ref[...] + bd * _mm_tn(k * kte_ref[...], v)
    scores = [_mm_nt(pair(q, p) * masks[j], pair(k, p)) for p, j in heads]
    yield
    y_pairs = []
    for p in range(N_HEADS // 2):
        yp = pair(y_cross, p)
        for j in range(2):
            h = 2 * p + j
            yp = yp + _mm(scores[h] * intra_ref[h], pair(v, p) * masks[j])
        y_pairs.append(yp)
    yield
    y = jnp.concatenate(y_pairs, axis=1)
    ms = _mm(y * y, bd) * (1.0 / HEAD_DIM)
    yield
    o_ref[b] = (_silu(g_ref[b]) * (y * lax.rsqrt(ms + EPS) * nw_ref[...])).astype(BF16)


def _retention_tables(seq):
    half = HEAD_DIM // 2
    inv = 1.0 / (10000.0 ** (jnp.arange(half, dtype=F32) / half))
    ang = jnp.arange(seq, dtype=F32)[:, None] * inv[None, :]
    cos = jnp.tile(jnp.concatenate([jnp.cos(ang), jnp.cos(ang)], axis=-1), (1, LANES // HEAD_DIM))
    sin = jnp.tile(jnp.concatenate([-jnp.sin(ang), jnp.sin(ang)], axis=-1), (1, LANES // HEAD_DIM))
    log_gamma = jnp.log1p(-(2.0 ** (-5.0 - jnp.arange(N_HEADS, dtype=F32))))
    idx = jnp.arange(CHUNK, dtype=F32)
    rel = idx[:, None] - idx[None, :]
    intra = jnp.where(rel[None] >= 0,
                      jnp.exp(jnp.maximum(rel, 0.0)[None] * log_gamma[:, None, None]), 0.0)
    expand = lambda m: jnp.repeat(m, HEAD_DIM, axis=1)
    qdec = expand(jnp.exp((idx + 1.0)[:, None] * log_gamma[None, :]))
    kte = expand(jnp.exp((CHUNK - 1 - idx)[:, None] * log_gamma[None, :]))
    cgam = expand(jnp.exp(CHUNK * log_gamma)[None, :])
    head = jnp.arange(WIDTH) // HEAD_DIM
    bd = (head[:, None] == head[None, :]).astype(BF16)
    return cos, sin, intra, qdec, kte, cgam, bd


SSD_INPUTS = 15
RET_INPUTS = 12


def _ssd_retention_kernel(*refs):
    ssd_in = refs[:SSD_INPUTS]
    ret_in = refs[SSD_INPUTS:SSD_INPUTS + RET_INPUTS]
    ssd_out, ret_out, xe_ref, bce_ref, st_ref, y_ref, r_ref = refs[SSD_INPUTS + RET_INPUTS:]
    c = pl.program_id(0)

    @pl.when(c == 0)
    def _():
        st_ref[...] = jnp.zeros_like(st_ref)
        r_ref[...] = jnp.zeros_like(r_ref)

    keep = (c > 0).astype(F32)
    programs = []
    for b in range(st_ref.shape[0]):
        programs.append(_ssd_chunk(b, keep, *ssd_in, ssd_out, xe_ref, bce_ref, st_ref, y_ref))
        programs.append(_ret_chunk(b, *ret_in, ret_out, r_ref))
    _interleave(programs)


def _ssd_retention(z, x, bc, dt, conv_w, conv_b, dt_bias, a_log, d_skip, ssd_norm_w,
                   q, k, v, g, ret_norm_w, batch, seq):
    nc = seq // CHUNK
    pad = LANES - N_HEADS
    e = jnp.repeat(jnp.eye(LANES, N_HEADS, dtype=BF16), HEAD_DIM, axis=1)
    seq3 = lambda a: a.reshape(batch, seq, a.shape[-1])
    z, x, bc, dt = seq3(z), seq3(x), seq3(bc), seq3(dt)
    ssd_args = (z, x, x, bc, bc, dt,
                conv_w[:, :WIDTH], conv_b[:WIDTH].reshape(1, WIDTH),
                conv_w[:, WIDTH:], conv_b[WIDTH:].reshape(1, 2 * LANES),
                jnp.pad(dt_bias, (0, pad)).reshape(1, LANES),
                jnp.pad(a_log, (0, pad)).reshape(1, LANES),
                jnp.repeat(d_skip, HEAD_DIM).reshape(1, WIDTH),
                ssd_norm_w.reshape(1, WIDTH), e)
    ssd_specs = [_seq_spec(batch, WIDTH), _seq_spec(batch, WIDTH), _seq_halo_spec(batch, WIDTH),
                 _seq_spec(batch, 2 * LANES), _seq_halo_spec(batch, 2 * LANES),
                 _seq_spec(batch, LANES),
                 _param_spec1((SSD_CONV, WIDTH)), _param_spec1((1, WIDTH)),
                 _param_spec1((SSD_CONV, 2 * LANES)), _param_spec1((1, 2 * LANES)),
                 _param_spec1((1, LANES)), _param_spec1((1, LANES)),
                 _param_spec1((1, WIDTH)), _param_spec1((1, WIDTH)),
                 _param_spec1((LANES, WIDTH))]
    cos, sin, intra, qdec, kte, cgam, bd = _retention_tables(seq)
    table_spec = pl.BlockSpec((CHUNK, LANES), lambda c: (c, 0))
    ret_args = (seq3(q), seq3(k), seq3(v), seq3(g), cos, sin, intra, qdec, kte, cgam, bd,
                ret_norm_w.reshape(1, WIDTH))
    ret_specs = [_seq_spec(batch, WIDTH)] * 4 + [
        table_spec, table_spec, _param_spec1((N_HEADS, CHUNK, CHUNK)),
        _param_spec1((CHUNK, WIDTH)), _param_spec1((CHUNK, WIDTH)),
        _param_spec1((1, WIDTH)), _param_spec1((WIDTH, WIDTH)), _param_spec1((1, WIDTH))]
    assert len(ssd_args) == SSD_INPUTS and len(ret_args) == RET_INPUTS
    out_shape = jax.ShapeDtypeStruct((batch, seq, WIDTH), BF16)
    y_ssd, y_ret = pl.pallas_call(
        _ssd_retention_kernel,
        grid=(nc,),
        in_specs=ssd_specs + ret_specs,
        out_specs=[_seq_spec(batch, WIDTH), _seq_spec(batch, WIDTH)],
        out_shape=[out_shape, out_shape],
        scratch_shapes=[pltpu.VMEM((batch, CHUNK + HALO, WIDTH), F32),
                        pltpu.VMEM((batch, CHUNK + HALO, 2 * LANES), F32),
                        pltpu.VMEM((batch, 2, LANES, 2 * LANES), F32),
                        pltpu.VMEM((batch, CHUNK, WIDTH), F32),
                        pltpu.VMEM((batch, WIDTH, WIDTH), F32)],
        compiler_params=_cparams(("arbitrary",)),
        name="ssd_retention",
    )(*ssd_args, *ret_args)
    return y_ssd.reshape(batch * seq, WIDTH), y_ret.reshape(batch * seq, WIDTH)


def _rwkv_kernel(r_ref, rh_ref, k_ref, kh_ref, v_ref, vh_ref, lo_ref, loh_ref,
                 mur_ref, muk_ref, muv_ref, mulo_ref, w0_ref, wup_ref, a0_ref, aup_ref, gup_ref,
                 kkw_ref, kaw_ref, rkw_ref, lnw_ref, lnb_ref, bd_ref,
                 o_ref, s_ref, sr_ref, sk_ref, sv_ref, slo_ref):
    c = pl.program_id(1)

    @pl.when(c == 0)
    def _():
        s_ref[...] = jnp.zeros_like(s_ref)

    keep = (c > 0).astype(F32)
    nb, _, width = r_ref.shape
    npair = width // LANES
    bd = bd_ref[...]
    row = _iota2((CHUNK, CHUNK), 0)
    col = _iota2((CHUNK, CHUNK), 1)
    strict = row > col
    causal = row >= col
    eye = (row == col).astype(F32)
    tri = causal.astype(BF16)
    same_head = (row // HEAD_DIM == col // HEAD_DIM).astype(F32)
    m1 = [_head_mask(j) for j in range(2)]
    m2 = [_head_mask(j, 2 * LANES) for j in range(2)]
    cat0 = lambda xs: jnp.concatenate(xs, axis=0)
    cat1 = lambda xs: jnp.concatenate(xs, axis=1)
    C = CHUNK

    def token_shift(b, main_ref, halo_ref, mu_ref, scr_ref):
        _with_prev_rows(scr_ref.at[b], main_ref.at[b], halo_ref.at[b], keep)
        cur = main_ref[b]
        prev = scr_ref[b, HALO - 1:HALO - 1 + CHUNK, :]
        return cur + (prev - cur) * mu_ref[...]

    pre = []
    for b in range(nb):
        r = token_shift(b, r_ref, rh_ref, mur_ref, sr_ref)
        k = token_shift(b, k_ref, kh_ref, muk_ref, sk_ref)
        v = token_shift(b, v_ref, vh_ref, muv_ref, sv_ref)
        lo = token_shift(b, lo_ref, loh_ref, mulo_ref, slo_ref)
        lo_wa = lo[:, :LANES]
        log_w = -RWKV_DECAY_SCALE * _sigmoid(w0_ref[...] + _mm(jnp.tanh(lo_wa), wup_ref[...]))
        a = _sigmoid(a0_ref[...] + _mm(lo_wa, aup_ref[...]))
        gate = _mm(_sigmoid(lo[:, LANES:]), gup_ref[...])
        kk = k * kkw_ref[...]
        kk = kk / jnp.maximum(jnp.sqrt(_mm(kk * kk, bd)), 1e-12)
        k = k * (1.0 + (a - 1.0) * kaw_ref[...])
        cum = _mm_split(tri, log_w, 2)
        mid = cum[C // 2 - 1:C // 2, :]
        last = cum[C - 1:C, :]
        inv_p = jnp.exp(mid - cum)
        to_end = jnp.exp(last - mid)
        pre.append(dict(
            r=r, k=k, v=v, gate=gate,
            a_t=-kk * jnp.exp(cum - log_w - mid), r_t=r * jnp.exp(cum - mid),
            b_t=kk * a * inv_p, k_t=k * inv_p, e_mid=jnp.exp(mid),
            b_end=kk * a * inv_p * to_end, k_end=k * inv_p * to_end, decay=jnp.exp(last)))

    units = [(b, q) for b in range(nb) for q in range(npair)]
    slab = lambda name: [pre[b][name][:, q * LANES:(q + 1) * LANES] for b, q in units]
    a_t, r_t, b_t, k_t, v_u = slab("a_t"), slab("r_t"), slab("b_t"), slab("k_t"), slab("v")
    e_mid, b_end, k_end, decay = slab("e_mid"), slab("b_end"), slab("k_end"), slab("decay")

    mm = [_mm_nt(cat0([a * m1[0], r * m1[0], a * m1[1], r * m1[1]]), cat0([bt, kt]))
          for a, r, bt, kt in zip(a_t, r_t, b_t, k_t)]
    chains = [(u, j) for u in range(len(units)) for j in range(2)]
    m_ab = [jnp.where(strict, mm[u][2 * j * C:(2 * j + 1) * C, :C], 0.0) for u, j in chains]
    m_ak = [jnp.where(strict, mm[u][2 * j * C:(2 * j + 1) * C, C:], 0.0) for u, j in chains]
    m_rb = [jnp.where(causal, mm[u][(2 * j + 1) * C:(2 * j + 2) * C, :C], 0.0) for u, j in chains]
    m_rk = [jnp.where(causal, mm[u][(2 * j + 1) * C:(2 * j + 2) * C, C:], 0.0) for u, j in chains]

    inv = [eye + m for m in m_ab]
    power = list(m_ab)
    for _ in range(int(math.log2(C)) - 1):
        power = [_mm(p, p) for p in power]
        inv = [i + _mm(i, p) for i, p in zip(inv, power)]

    n_units = range(len(units))
    per_head = lambda x2: x2[:C] * m2[0] + x2[C:] * m2[1]
    mv = [_mm(cat0([m_ak[2 * u], m_rk[2 * u], m_ak[2 * u + 1], m_rk[2 * u + 1]]), v_u[u])
          for u in n_units]
    mv_ak = [mv[u][:C] * m1[0] + mv[u][2 * C:3 * C] * m1[1] for u in n_units]
    mv_rk = [mv[u][C:2 * C] * m1[0] + mv[u][3 * C:] * m1[1] for u in n_units]
    z = [per_head(_mm(cat0([inv[2 * u], inv[2 * u + 1]]), cat1([a_t[u], mv_ak[u]])))
         for u in n_units]
    w = [per_head(_mm(cat0([m_rb[2 * u], m_rb[2 * u + 1]]), z[u])) for u in n_units]

    s = [s_ref[b, q] for b, q in units]
    uy = [_mm_nt(cat0([z[u][:, :LANES] * e_mid[u], (r_t[u] + w[u][:, :LANES]) * e_mid[u]]), s[u])
          for u in n_units]
    u_in = [uy[u][:C] + z[u][:, LANES:] for u in n_units]
    y = [uy[u][C:] + w[u][:, LANES:] + mv_rk[u] for u in n_units]
    s_inc = [_mm_tn(cat0([u_in[u], v_u[u]]), cat0([b_end[u], k_end[u]])) for u in n_units]
    for u, (b, q) in enumerate(units):
        s_ref[b, q] = s[u] * decay[u] + same_head * s_inc[u]

    for b in range(nb):
        yb = cat1([y[b * npair + q] for q in range(npair)])
        mean = _mm(yb, bd) * (1.0 / HEAD_DIM)
        d = yb - mean
        var = _mm(d * d, bd) * (1.0 / HEAD_DIM)
        yn = d * lax.rsqrt(var + RWKV_LN_EPS) * lnw_ref[...] + lnb_ref[...]
        bonus = _mm(pre[b]["r"] * pre[b]["k"] * rkw_ref[...], bd) * pre[b]["v"]
        o_ref[b] = ((yn + bonus) * pre[b]["gate"]).astype(BF16)


RWKV_PAIRS_PER_STEP = 4


def _rwkv(r, k, v, lo, mu, w0, w_up, a0, a_up, g_up, k_k, k_a, r_k, ln_w, ln_b, batch, seq):
    nc = seq // CHUNK
    per = CHUNK // HALO
    width = RWKV_PAIRS_PER_STEP * LANES
    lo_width = lo.shape[-1]
    rank = w_up.shape[0]
    prev_block = lambda c: jnp.maximum(c * per - 1, 0)

    slab_row = pl.BlockSpec((batch, CHUNK, width), lambda p, c: (0, c, p))
    slab_halo = pl.BlockSpec((batch, HALO, width), lambda p, c: (0, prev_block(c), p))
    lo_row = pl.BlockSpec((batch, CHUNK, lo_width), lambda p, c: (0, c, 0))
    lo_halo = pl.BlockSpec((batch, HALO, lo_width), lambda p, c: (0, prev_block(c), 0))
    slab_param = pl.BlockSpec((1, width), lambda p, c: (0, p))
    slab_weight = pl.BlockSpec((LANES, width), lambda p, c: (0, p))
    const = lambda shape: pl.BlockSpec(shape, lambda p, c: (0,) * len(shape))

    row1 = lambda a: a.reshape(1, -1)
    seq3 = lambda a: a.reshape(batch, seq, a.shape[-1])
    wup_pad = jnp.concatenate([w_up, jnp.zeros((LANES - rank, WIDTH), F32)], axis=0).astype(BF16)
    aup_pad = jnp.concatenate([jnp.zeros((LANES - rank, WIDTH), F32), a_up], axis=0).astype(BF16)
    head = jnp.arange(width) // HEAD_DIM
    bd = (head[:, None] == head[None, :]).astype(BF16)
    r, k, v, lo = seq3(r), seq3(k), seq3(v), seq3(lo)
    args = (r, r, k, k, v, v, lo, lo,
            row1(mu[:WIDTH]), row1(mu[WIDTH:2 * WIDTH]), row1(mu[2 * WIDTH:3 * WIDTH]),
            row1(mu[3 * WIDTH:]),
            row1(w0), wup_pad, row1(a0), aup_pad, g_up.astype(BF16),
            row1(k_k), row1(k_a), row1(r_k), row1(ln_w), row1(ln_b), bd)
    out = pl.pallas_call(
        _rwkv_kernel,
        grid=(WIDTH // width, nc),
        in_specs=[slab_row, slab_halo, slab_row, slab_halo, slab_row, slab_halo, lo_row, lo_halo,
                  slab_param, slab_param, slab_param, const((1, lo_width)),
                  slab_param, slab_weight, slab_param, slab_weight, slab_weight,
                  slab_param, slab_param, slab_param, slab_param, slab_param,
                  const((width, width))],
        out_specs=pl.BlockSpec((batch, CHUNK, width), lambda p, c: (0, c, p)),
        out_shape=jax.ShapeDtypeStruct((batch, seq, WIDTH), BF16),
        scratch_shapes=[pltpu.VMEM((batch, RWKV_PAIRS_PER_STEP, LANES, LANES), F32),
                        pltpu.VMEM((batch, CHUNK + HALO, width), F32),
                        pltpu.VMEM((batch, CHUNK + HALO, width), F32),
                        pltpu.VMEM((batch, CHUNK + HALO, width), F32),
                        pltpu.VMEM((batch, CHUNK + HALO, lo_width), F32)],
        compiler_params=_cparams(("parallel", "arbitrary")),
        name="rwkv7",
    )(*args)
    return out.reshape(batch * seq, WIDTH)


FOX_PREP_ROWS = 256
FOX_TILE = 2048
FOX_Q_SPLIT = 8
FOX_GUARD = 64.0
LOG2E = math.log2(math.e)


def _fox_prep_chunk(b, q_ref, k_ref, v_ref, f_ref, qw_ref, kw_ref, fb_ref, bd_ref, spread_ref,
                    qo_ref, ko_ref, vo_ref, carry_ref):
    bd = bd_ref[...]
    q, k, v = q_ref[b], k_ref[b], v_ref[b]
    q_ms = _mm(q * q, bd) * (1.0 / HEAD_DIM)
    k_ms = _mm(k * k, bd) * (1.0 / HEAD_DIM)

    f = f_ref[b] + fb_ref[...]
    log_f = jnp.minimum(f, 0.0) - jnp.log1p(jnp.exp(-jnp.abs(f)))
    rows = FOX_PREP_ROWS
    tri = (_iota2((rows, rows), 0) >= _iota2((rows, rows), 1)).astype(BF16)
    cum = _mm_split(tri, log_f, 3) + carry_ref[b, 0:1, :]
    carry_ref[b] = jnp.broadcast_to(cum[rows - 1:rows, :], carry_ref.shape[1:])
    yield
    q = q * lax.rsqrt(q_ms + EPS) * qw_ref[...] * (HEAD_DIM ** -0.5 * LOG2E)
    k = k * lax.rsqrt(k_ms + EPS) * kw_ref[...]
    c_all = _mm_split_rhs(cum * LOG2E, spread_ref[...], 3)
    yield

    lane = _iota2((1, LANES), 1)
    for h in range(N_HEADS):
        p, j = divmod(h, 2)
        ps = slice(p * LANES, (p + 1) * LANES)
        mj = _head_mask(j)
        qh, kh = q[:, ps] * mj, k[:, ps] * mj
        if j == 1:
            qh = pltpu.roll(qh, HEAD_DIM, 1)
            kh = pltpu.roll(kh, HEAD_DIM, 1)
        ch = c_all[:, h * LANES:(h + 1) * LANES]
        hi = ch.astype(BF16).astype(F32)
        mid = (ch - hi).astype(BF16).astype(F32)
        low = ch - hi - mid
        piece = lambda base: jnp.where(lane == base, hi,
                                       jnp.where(lane == base + 1, mid,
                                                 jnp.where(lane == base + 2, low, 0.0)))
        ones = lambda base: ((lane >= base) & (lane < base + 3)).astype(F32)
        qo_ref[b, h] = (qh + piece(HEAD_DIM) + ones(HEAD_DIM + 3)).astype(BF16)
        ko_ref[b, h] = (kh + ones(HEAD_DIM) - piece(HEAD_DIM + 3)).astype(BF16)
    row_head = _iota2((LANES, 1), 0) // HEAD_DIM
    row_in_head = _iota2((LANES, 1), 0) % HEAD_DIM
    for p in range(N_HEADS // 2):
        v_t = v[:, p * LANES:(p + 1) * LANES].T
        for j in range(2):
            ones_row = ((row_head != j) & (row_in_head == 0)).astype(F32)
            vo_ref[b, 2 * p + j] = jnp.where(row_head == j, v_t, ones_row).astype(BF16)


def _fox_prep_kernel(*refs):
    carry_ref = refs[-1]

    @pl.when(pl.program_id(0) == 0)
    def _():
        carry_ref[...] = jnp.zeros_like(carry_ref)

    _interleave([_fox_prep_chunk(b, *refs) for b in range(carry_ref.shape[0])])


def _fox_kernel(qi_ref, ki_ref, q_ref, k_ref, vt_ref, o_ref, m_ref, acc_ref, redo_ref):
    t = pl.program_id(2)
    qi = qi_ref[t]
    ki = ki_ref[t]

    part = FOX_TILE // FOX_Q_SPLIT
    units = [(j, slice(h * part, (h + 1) * part)) for j in range(2) for h in range(FOX_Q_SPLIT)]

    def all_scores(on_diagonal):
        n_keys = lambda qs: qs.stop if on_diagonal else FOX_TILE
        raw = [lax.dot_general(k_ref[0, j, :n_keys(qs), :], q_ref[0, j, qs, :],
                               (((1,), (1,)), ((), ())), preferred_element_type=F32)
               for j, qs in units]
        if not on_diagonal:
            return raw
        masked = []
        for (_, qs), s in zip(units, raw):
            key = _iota2(s.shape, 0)
            query = _iota2(s.shape, 1) + qs.start
            masked.append(jnp.where(query >= key, s, NEG))
        return masked

    def pv(j, p):
        return jnp.dot(vt_ref[0, j, :, :p.shape[0]], p.astype(BF16), preferred_element_type=F32)

    def exact_step(on_diagonal):
        for (j, qs), s in zip(units, all_scores(on_diagonal)):
            m_prev = m_ref[j, :, qs]
            m_next = jnp.maximum(m_prev, jnp.max(s, axis=0, keepdims=True))
            alpha = jnp.exp2(m_prev - m_next)
            acc_ref[j, :, qs] = alpha * acc_ref[j, :, qs] + pv(j, jnp.exp2(s - m_next))
            m_ref[j, :, qs] = m_next

    def lagged_step(on_diagonal):
        done = []
        worst = None
        for (j, qs), s in zip(units, all_scores(on_diagonal)):
            m_used = m_ref[j, :, qs]
            block_max = jnp.max(s, axis=0, keepdims=True)
            contrib = pv(j, jnp.exp2(s - m_used))
            excess = jnp.max(block_max - m_used)
            worst = excess if worst is None else jnp.maximum(worst, excess)
            done.append((j, qs, m_used, block_max, contrib))
        ok = worst <= FOX_GUARD
        redo_ref[0] = jnp.where(ok, 0, 1)

        @pl.when(ok)
        def _():
            for j, qs, m_used, block_max, contrib in done:
                m_next = jnp.maximum(m_used, block_max)
                acc_ref[j, :, qs] = (acc_ref[j, :, qs] + contrib) * jnp.exp2(m_used - m_next)
                m_ref[j, :, qs] = m_next

    @pl.when(ki == qi)
    def _():
        m_ref[...] = jnp.full_like(m_ref, NEG)
        acc_ref[...] = jnp.zeros_like(acc_ref)
        exact_step(True)

    redo_ref[0] = 0

    @pl.when(ki < qi)
    def _():
        lagged_step(False)

    @pl.when(redo_ref[0] == 1)
    def _():
        exact_step(False)

    @pl.when(ki == 0)
    def _():
        head_rows = _iota2((LANES, 1), 0) // HEAD_DIM
        out_t = jnp.where(head_rows == 0,
                          acc_ref[0] / acc_ref[0, HEAD_DIM:HEAD_DIM + 1, :],
                          acc_ref[1] / acc_ref[1, 0:1, :])
        o_ref[0] = out_t.T.astype(BF16)


def _fox(q, k, v, f, q_norm_w, k_norm_w, f_bias, batch, seq):
    rows = FOX_PREP_ROWS
    nr = seq // rows
    head = jnp.arange(WIDTH) // HEAD_DIM
    bd = (head[:, None] == head[None, :]).astype(BF16)
    dst = jnp.arange(N_HEADS * LANES)
    spread = ((dst[None, :] // LANES == jnp.arange(LANES)[:, None])
              & (dst[None, :] % LANES >= HEAD_DIM) & (dst[None, :] % LANES < HEAD_DIM + 6)).astype(BF16)
    seq3 = lambda a: a.reshape(batch, seq, a.shape[-1])
    row_spec = lambda width: pl.BlockSpec((batch, rows, width), lambda i: (0, i, 0))
    head_out = pl.BlockSpec((batch, N_HEADS, rows, LANES), lambda i: (0, 0, i, 0))
    head_shape = jax.ShapeDtypeStruct((batch, N_HEADS, seq, LANES), BF16)
    vt_out = pl.BlockSpec((batch, N_HEADS, LANES, rows), lambda i: (0, 0, 0, i))
    vt_shape = jax.ShapeDtypeStruct((batch, N_HEADS, LANES, seq), BF16)
    qa, ka, va = pl.pallas_call(
        _fox_prep_kernel,
        grid=(nr,),
        in_specs=[row_spec(WIDTH), row_spec(WIDTH), row_spec(WIDTH), row_spec(LANES),
                  _param_spec1((1, WIDTH)), _param_spec1((1, WIDTH)), _param_spec1((1, LANES)),
                  _param_spec1((WIDTH, WIDTH)), _param_spec1((LANES, N_HEADS * LANES))],
        out_specs=[head_out, head_out, vt_out],
        out_shape=[head_shape, head_shape, vt_shape],
        scratch_shapes=[pltpu.VMEM((batch, HALO, LANES), F32)],
        compiler_params=_cparams(("arbitrary",)),
        name="fox_prep",
    )(seq3(q), seq3(k), seq3(v), seq3(f), jnp.tile(q_norm_w, N_HEADS).reshape(1, WIDTH),
      jnp.tile(k_norm_w, N_HEADS).reshape(1, WIDTH),
      jnp.pad(f_bias, (0, LANES - N_HEADS)).reshape(1, LANES), bd, spread)

    nt = seq // FOX_TILE
    pairs = [(i, j) for i in range(nt) for j in range(i, -1, -1)]
    qi = jnp.asarray([i for i, _ in pairs], jnp.int32)
    ki = jnp.asarray([j for _, j in pairs], jnp.int32)
    q_spec = pl.BlockSpec((1, 2, FOX_TILE, LANES), lambda b, p, t, qi, ki: (b, p, qi[t], 0))
    k_spec = pl.BlockSpec((1, 2, FOX_TILE, LANES), lambda b, p, t, qi, ki: (b, p, ki[t], 0))
    vt_spec = pl.BlockSpec((1, 2, LANES, FOX_TILE), lambda b, p, t, qi, ki: (b, p, 0, ki[t]))
    out = pl.pallas_call(
        _fox_kernel,
        grid_spec=pltpu.PrefetchScalarGridSpec(
            num_scalar_prefetch=2,
            grid=(batch, N_HEADS // 2, len(pairs)),
            in_specs=[q_spec, k_spec, vt_spec],
            out_specs=pl.BlockSpec((1, FOX_TILE, LANES), lambda b, p, t, qi, ki: (b, qi[t], p)),
            scratch_shapes=[pltpu.VMEM((2, 1, FOX_TILE), F32),
                            pltpu.VMEM((2, LANES, FOX_TILE), F32),
                            pltpu.SMEM((1,), jnp.int32)]),
        out_shape=jax.ShapeDtypeStruct((batch, seq, WIDTH), BF16),
        compiler_params=_cparams(("parallel", "parallel", "arbitrary")),
        name="fox_attention",
    )(qi, ki, qa, ka, va)
    return out.reshape(batch * seq, WIDTH)


def _pad_cols(w, width):
    return jnp.pad(w, ((0, 0), (0, width - w.shape[1])))


def _even_mixers(x, batch, seq, norm_w, w_in, conv_w, conv_b, dt_bias, a_log, d_skip,
                 ssd_norm_w, ret_norm_w):
    ssd_x0 = WIDTH
    ssd_dt0 = WIDTH + (WIDTH + 4 * SSD_STATE)
    ret0 = ssd_dt0 + N_HEADS
    w = jnp.concatenate([w_in[:, :ssd_dt0], w_in[:, ret0:],
                         _pad_cols(w_in[:, ssd_dt0:ret0], LANES)], axis=1).astype(BF16)
    segments = ((0, WIDTH), (ssd_x0, WIDTH), (2 * WIDTH, 2 * LANES),
                (ssd_dt0, WIDTH), (ssd_dt0 + WIDTH, WIDTH), (ssd_dt0 + 2 * WIDTH, WIDTH),
                (ssd_dt0 + 3 * WIDTH, WIDTH), (ssd_dt0 + 4 * WIDTH, LANES))
    z, xs, bc, q, k, v, g, dt = _inproj(x, norm_w, w, segments)
    return _ssd_retention(z, xs, bc, dt, conv_w, conv_b, dt_bias, a_log, d_skip, ssd_norm_w,
                          q, k, v, g, ret_norm_w, batch, seq)


def _odd_mixers(x, batch, seq, norm_w, w_in, mu, w0, w_up, a0, a_up, g_up, k_k, k_a, r_k,
                ln_w, ln_b, q_norm_w, k_norm_w, f_bias):
    lo_width = w_up.shape[0] + a_up.shape[0] + g_up.shape[0]
    fox0 = 3 * WIDTH + lo_width
    f0 = fox0 + 3 * WIDTH
    w = jnp.concatenate([w_in[:, :f0], _pad_cols(w_in[:, f0:], LANES)], axis=1).astype(BF16)
    segments = ((0, WIDTH), (WIDTH, WIDTH), (2 * WIDTH, WIDTH), (3 * WIDTH, lo_width),
                (fox0, WIDTH), (fox0 + WIDTH, WIDTH), (fox0 + 2 * WIDTH, WIDTH), (f0, LANES))
    r, k, v, lo, fq, fk, fv, ff = _inproj(x, norm_w, w, segments)
    y_rwkv = _rwkv(r, k, v, lo, mu, w0, w_up, a0, a_up, g_up, k_k, k_a, r_k.reshape(-1),
                   ln_w, ln_b, batch, seq)
    y_fox = _fox(fq, fk, fv, ff, q_norm_w, k_norm_w, f_bias, batch, seq)
    return y_rwkv, y_fox


def kernel(x, ev_norm_w, ev_w_in, ev_ssd_conv_w, ev_ssd_conv_b, ev_ssd_dt_bias, ev_ssd_a_log,
           ev_ssd_d, ev_ssd_norm_w, ev_ret_norm_w, ev_w_out,
           od_norm_w, od_w_in, od_rwkv_mu, od_rwkv_w0, od_rwkv_w_up, od_rwkv_a0, od_rwkv_a_up,
           od_rwkv_g_up, od_rwkv_k_k, od_rwkv_k_a, od_rwkv_r_k, od_rwkv_ln_w, od_rwkv_ln_b,
           od_fox_q_norm_w, od_fox_k_norm_w, od_fox_f_bias, od_w_out,
           ffn_norm_w, ffn_w_up, ffn_conv_w, ffn_conv_b, ffn_w_down):
    batch, seq, _ = x.shape
    depth = ffn_norm_w.shape[0]
    h = x.reshape(batch * seq, D_MODEL)
    for layer in range(depth):
        i = layer // 2
        if layer % 2 == 0:
            ya, yb = _even_mixers(h, batch, seq, ev_norm_w[i], ev_w_in[i], ev_ssd_conv_w[i],
                                  ev_ssd_conv_b[i], ev_ssd_dt_bias[i], ev_ssd_a_log[i],
                                  ev_ssd_d[i], ev_ssd_norm_w[i], ev_ret_norm_w[i])
            w_out = ev_w_out[i]
        else:
            ya, yb = _odd_mixers(h, batch, seq, od_norm_w[i], od_w_in[i], od_rwkv_mu[i],
                                 od_rwkv_w0[i], od_rwkv_w_up[i], od_rwkv_a0[i], od_rwkv_a_up[i],
                                 od_rwkv_g_up[i], od_rwkv_k_k[i], od_rwkv_k_a[i], od_rwkv_r_k[i],
                                 od_rwkv_ln_w[i], od_rwkv_ln_b[i], od_fox_q_norm_w[i],
                                 od_fox_k_norm_w[i], od_fox_f_bias[i])
            w_out = od_w_out[i]
        h = _proj_ffn(ya, yb, h, seq, w_out, ffn_norm_w[layer], ffn_w_up[layer],
                      ffn_conv_w[layer], ffn_conv_b[layer], ffn_w_down[layer])
    return h.reshape(batch, seq, D_MODEL)
```

```python
import functools
import itertools
import math

import jax
import jax.numpy as jnp
from jax import lax
from jax.experimental import pallas as pl
from jax.experimental.pallas import tpu as pltpu

F32 = jnp.float32
BF16 = jnp.bfloat16

D_MODEL = 1024
HEAD_DIM = 64
N_HEADS = 8
WIDTH = N_HEADS * HEAD_DIM
LANES = 128
CHUNK = 128
HALO = 8
EPS = 1e-6
SSD_CONV = 4
SSD_STATE = 64
RWKV_LN_EPS = 64e-5
RWKV_DECAY_SCALE = 0.606531
D_FF = 2816
FF_TILE = 256
FFN_HALO = 16
FFN_LOOKAHEAD = 2
NEG = -1e30
VMEM_LIMIT = 56 * 1024 * 1024


def _mm(a, b):
    return jnp.dot(a.astype(BF16), b.astype(BF16), preferred_element_type=F32)


def _mm_nt(a, b):
    return lax.dot_general(a.astype(BF16), b.astype(BF16), (((1,), (1,)), ((), ())),
                           preferred_element_type=F32)


def _mm_tn(a, b):
    return lax.dot_general(a.astype(BF16), b.astype(BF16), (((0,), (0,)), ((), ())),
                           preferred_element_type=F32)


def _mm_split(lhs01, x, pieces):
    acc = None
    rest = x
    for _ in range(pieces):
        part = rest.astype(BF16)
        term = jnp.dot(lhs01, part, preferred_element_type=F32)
        acc = term if acc is None else acc + term
        rest = rest - part.astype(F32)
    return acc


def _mm_split_rhs(x, rhs01, pieces):
    acc = None
    rest = x
    for _ in range(pieces):
        part = rest.astype(BF16)
        term = jnp.dot(part, rhs01, preferred_element_type=F32)
        acc = term if acc is None else acc + term
        rest = rest - part.astype(F32)
    return acc


def _interleave(stage_generators):
    for _ in itertools.zip_longest(*stage_generators):
        pass


def _sigmoid(x):
    return 1.0 / (1.0 + jnp.exp(-x))


def _silu(x):
    return x * _sigmoid(x)


def _softplus(x):
    return jnp.maximum(x, 0.0) + jnp.log1p(jnp.exp(-jnp.abs(x)))


def _iota2(shape, dim):
    return lax.broadcasted_iota(jnp.int32, shape, dim)


def _head_mask(j, width=LANES):
    lane = _iota2((1, width), 1)
    return ((lane % LANES) // HEAD_DIM == j).astype(F32)


def _cparams(sem):
    return pltpu.CompilerParams(dimension_semantics=sem, vmem_limit_bytes=VMEM_LIMIT)


def _const_spec(shape):
    nd = len(shape)
    return pl.BlockSpec(shape, lambda *_: (0,) * nd, pipeline_mode=pl.Buffered(1))


def _inproj_kernel(x_ref, nw_ref, w_ref, *o_refs, segments):
    x = x_ref[...]
    ms = jnp.mean(x * x, axis=-1, keepdims=True)
    xn = (x * lax.rsqrt(ms + EPS) * nw_ref[...]).astype(BF16)
    for o_ref, (off, width) in zip(o_refs, segments):
        o_ref[...] = jnp.dot(xn, w_ref[:, off:off + width], preferred_element_type=F32)


def _inproj(x, norm_w, w, segments, tm=1024):
    t = x.shape[0]
    n = w.shape[1]
    return pl.pallas_call(
        functools.partial(_inproj_kernel, segments=segments),
        grid=(t // tm,),
        in_specs=[pl.BlockSpec((tm, D_MODEL), lambda i: (i, 0)),
                  _const_spec((1, D_MODEL)),
                  _const_spec((D_MODEL, n))],
        out_specs=[pl.BlockSpec((tm, wd), lambda i: (i, 0)) for _, wd in segments],
        out_shape=[jax.ShapeDtypeStruct((t, wd), F32) for _, wd in segments],
        compiler_params=_cparams(("parallel",)),
        name="inproj",
    )(x, norm_w.reshape(1, D_MODEL), w)


def _proj_ffn_kernel(ya_ref, yah_ref, yb_ref, ybh_ref, x_ref, xh_ref, wo_ref, nw_ref,
                     wup_ref, cw_ref, cb_ref, wd_ref, o_ref,
                     ya_ext, yb_ext, h_ref, hn_ref, g_ref, acc_ref, *, tm, tiles_per_seq):
    ya_ext[:FFN_HALO, :] = yah_ref[...]
    ya_ext[FFN_HALO:, :] = ya_ref[...]
    yb_ext[:FFN_HALO, :] = ybh_ref[...]
    yb_ext[FFN_HALO:, :] = yb_ref[...]
    h_ref[:FFN_HALO, :] = xh_ref[...]
    h_ref[FFN_HALO:, :] = x_ref[...]
    h_ref[...] += (jnp.dot(ya_ext[...], wo_ref[:WIDTH, :], preferred_element_type=F32)
                   + jnp.dot(yb_ext[...], wo_ref[WIDTH:, :], preferred_element_type=F32))

    h = h_ref[...]
    ms = jnp.mean(h * h, axis=-1, keepdims=True)
    hn = h * lax.rsqrt(ms + EPS) * nw_ref[...]
    first_of_seq = pl.program_id(0) % tiles_per_seq == 0
    halo_rows = _iota2((tm + FFN_HALO, 1), 0) < FFN_HALO
    hn_ref[...] = jnp.where(halo_rows & first_of_seq, 0.0, hn).astype(BF16)
    acc_ref[...] = jnp.zeros_like(acc_ref)

    def tile(f, base=0):
        return slice(base + f * FF_TILE, base + (f + 1) * FF_TILE)

    n_buf = g_ref.shape[0]

    def up_proj(f):
        g_ref[f % n_buf] = jnp.dot(hn_ref[...], wup_ref[:, tile(f)], preferred_element_type=F32)
        return jnp.dot(hn_ref[FFN_HALO:, :], wup_ref[:, tile(f, D_FF)],
                       preferred_element_type=F32)

    nf = D_FF // FF_TILE
    ups = {f: up_proj(f) for f in range(FFN_LOOKAHEAD)}
    for f in range(nf):
        if f + FFN_LOOKAHEAD < nf:
            ups[f + FFN_LOOKAHEAD] = up_proj(f + FFN_LOOKAHEAD)
        up = ups.pop(f)
        g = g_ref.at[f % n_buf]
        cw = cw_ref[:, tile(f)]
        gate = (cb_ref[:, tile(f)]
                + cw[0:1, :] * g[FFN_HALO - 2:FFN_HALO - 2 + tm, :]
                + cw[1:2, :] * g[FFN_HALO - 1:FFN_HALO - 1 + tm, :]
                + cw[2:3, :] * g[FFN_HALO:, :])
        act = (_silu(gate) * up).astype(BF16)
        acc_ref[...] += jnp.dot(act, wd_ref[tile(f), :], preferred_element_type=F32)
    o_ref[...] = h_ref[FFN_HALO:, :] + acc_ref[...]


def _proj_ffn(ya, yb, x, seq, w_out, norm_w, w_up, conv_w, conv_b, w_down, tm=512):
    t = x.shape[0]
    halo_per_tile = tm // FFN_HALO
    row_spec = lambda width: pl.BlockSpec((tm, width), lambda i: (i, 0))
    halo_spec = lambda width: pl.BlockSpec(
        (FFN_HALO, width), lambda i: (jnp.maximum(i * halo_per_tile - 1, 0), 0))
    ext = lambda width, dtype: pltpu.VMEM((tm + FFN_HALO, width), dtype)
    return pl.pallas_call(
        functools.partial(_proj_ffn_kernel, tm=tm, tiles_per_seq=seq // tm),
        grid=(t // tm,),
        in_specs=[row_spec(WIDTH), halo_spec(WIDTH), row_spec(WIDTH), halo_spec(WIDTH),
                  row_spec(D_MODEL), halo_spec(D_MODEL),
                  _const_spec((2 * WIDTH, D_MODEL)), _const_spec((1, D_MODEL)),
                  _const_spec((D_MODEL, 2 * D_FF)), _const_spec((3, D_FF)), _const_spec((1, D_FF)),
                  _const_spec((D_FF, D_MODEL))],
        out_specs=row_spec(D_MODEL),
        out_shape=jax.ShapeDtypeStruct((t, D_MODEL), F32),
        scratch_shapes=[ext(WIDTH, BF16), ext(WIDTH, BF16), ext(D_MODEL, F32), ext(D_MODEL, BF16),
                        pltpu.VMEM((FFN_LOOKAHEAD + 1, tm + FFN_HALO, FF_TILE), F32),
                        pltpu.VMEM((tm, D_MODEL), F32)],
        compiler_params=_cparams(("parallel",)),
        name="proj_convffn",
    )(ya, ya, yb, yb, x, x, w_out.astype(BF16), norm_w.reshape(1, D_MODEL),
      w_up.astype(BF16), conv_w, conv_b.reshape(1, D_FF), w_down.astype(BF16))


def _seq_spec(batch, width):
    return pl.BlockSpec((batch, CHUNK, width), lambda c: (0, c, 0))


def _seq_halo_spec(batch, width):
    per = CHUNK // HALO
    return pl.BlockSpec((batch, HALO, width), lambda c: (0, jnp.maximum(c * per - 1, 0), 0))


def _param_spec1(shape):
    return pl.BlockSpec(shape, lambda c: (0,) * len(shape), pipeline_mode=pl.Buffered(1))


def _with_prev_rows(scr_ref, main_ref, halo_ref, keep):
    scr_ref[:HALO, :] = halo_ref[...] * keep
    scr_ref[HALO:, :] = main_ref[...]


def _ssd_chunk(b, keep, z_ref, x_ref, xh_ref, bc_ref, bch_ref, dt_ref,
               cwx_ref, cbx_ref, cwb_ref, cbb_ref, dtb_ref, alog_ref, dexp_ref, nw_ref, e_ref,
               o_ref, xe_ref, bce_ref, st_ref, y_ref):
    _with_prev_rows(xe_ref.at[b], x_ref.at[b], xh_ref.at[b], keep)
    _with_prev_rows(bce_ref.at[b], bc_ref.at[b], bch_ref.at[b], keep)

    def conv(ref, cw_ref, cb_ref):
        out = cb_ref[...]
        for k in range(SSD_CONV):
            lo = HALO - (SSD_CONV - 1) + k
            out = out + cw_ref[k:k + 1, :] * ref[b, lo:lo + CHUNK, :]
        return _silu(out)

    x = conv(xe_ref, cwx_ref, cbx_ref)
    bc = conv(bce_ref, cwb_ref, cbb_ref)
    bm = bc[:, :LANES]
    cm = bc[:, LANES:]

    row = _iota2((CHUNK, CHUNK), 0)
    col = _iota2((CHUNK, CHUNK), 1)
    causal = row >= col

    dt = _softplus(dt_ref[b] + dtb_ref[...])
    a = dt * (-jnp.exp(alog_ref[...]))
    a_cum = _mm_split(causal.astype(BF16), a, 3)
    yield
    a_cum_t = a_cum.T
    a_last = a_cum[CHUNK - 1:CHUNK, :]
    per_head = jnp.concatenate(
        [dt, jnp.exp(a_cum), jnp.exp(a_last - a_cum),
         jnp.broadcast_to(jnp.exp(a_last), (HALO, LANES))], axis=0)
    per_lane = _mm_split_rhs(per_head, e_ref[...], 2)
    dt_e = per_lane[:CHUNK]
    ea_e = per_lane[CHUNK:2 * CHUNK]
    te_e = per_lane[2 * CHUNK:3 * CHUNK]
    cd_e = per_lane[3 * CHUNK:3 * CHUNK + 1]
    yield

    xdt = x * dt_e
    xs = xdt * te_e
    bm_t = bm.T
    group_rows = _iota2((LANES, 1), 0) // SSD_STATE

    for g in range(2):
        cg = cm * _head_mask(g)
        cb = _mm_nt(cg, bm)
        gs = slice(g * 2 * LANES, (g + 1) * 2 * LANES)
        st = st_ref[b, g]
        y_off = _mm(cg, st) * ea_e[:, gs]
        st_new = _mm(bm_t, xs[:, gs])
        st_ref[b, g] = st * cd_e[:, gs] + jnp.where(group_rows == g, st_new, 0.0)
        yield
        for pp in range(2):
            p = 2 * g + pp
            xp = xdt[:, p * LANES:(p + 1) * LANES]
            yp = y_off[:, pp * LANES:(pp + 1) * LANES]
            for j in range(2):
                h = 2 * p + j
                seg = a_cum[:, h:h + 1] - a_cum_t[h:h + 1, :]
                decay = jnp.where(causal, jnp.exp(jnp.minimum(seg, 0.0)), 0.0)
                yp = yp + _mm(cb * decay, xp * _head_mask(j))
            y_ref[b, :, p * LANES:(p + 1) * LANES] = yp
            yield

    y = y_ref[b] + dexp_ref[...] * x
    gated = y * _silu(z_ref[b])
    ms = jnp.mean(gated * gated, axis=-1, keepdims=True)
    o_ref[b] = (gated * lax.rsqrt(ms + EPS) * nw_ref[...]).astype(BF16)


def _ret_chunk(b, q_ref, k_ref, v_ref, g_ref, cos_ref, sin_ref, intra_ref, qdec_ref, kte_ref,
               cg_ref, bd_ref, nw_ref, o_ref, r_ref):
    lane = _iota2((1, WIDTH), 1)
    first_half = (lane % HEAD_DIM) < (HEAD_DIM // 2)
    cos = jnp.concatenate([cos_ref[...]] * (WIDTH // LANES), axis=1)
    sin = jnp.concatenate([sin_ref[...]] * (WIDTH // LANES), axis=1)

    def rotary(v):
        other = jnp.where(first_half,
                          pltpu.roll(v, WIDTH - HEAD_DIM // 2, 1),
                          pltpu.roll(v, HEAD_DIM // 2, 1))
        return v * cos + other * sin

    q = rotary(q_ref[b])
    k = rotary(k_ref[b]) * (HEAD_DIM ** -0.5)
    v = v_ref[b]
    bd = bd_ref[...]
    pair = lambda t, p: t[:, p * LANES:(p + 1) * LANES]
    heads = [(p, j) for p in range(N_HEADS // 2) for j in range(2)]
    masks = [_head_mask(j) for j in range(2)]

    y_cross = _mm(q * qdec_ref[...], r_ref[b])
    r_ref[b] = r_ref[b] * cg_ref[...] + bd * _mm_tn(k * kte_ref[...], v)
    scores = [_mm_nt(pair(q, p) * masks[j], pair(k, p)) for p, j in heads]
    yield
    y_pairs = []
    for p in range(N_HEADS // 2):
        yp = pair(y_cross, p)
        for j in range(2):
            h = 2 * p + j
            yp = yp + _mm(scores[h] * intra_ref[h], pair(v, p) * masks[j])
        y_pairs.append(yp)
    yield
    y = jnp.concatenate(y_pairs, axis=1)
    ms = _mm(y * y, bd) * (1.0 / HEAD_DIM)
    yield
    o_ref[b] = (_silu(g_ref[b]) * (y * lax.rsqrt(ms + EPS) * nw_ref[...])).astype(BF16)


def _retention_tables(seq):
    half = HEAD_DIM // 2
    inv = 1.0 / (10000.0 ** (jnp.arange(half, dtype=F32) / half))
    ang = jnp.arange(seq, dtype=F32)[:, None] * inv[None, :]
    cos = jnp.tile(jnp.concatenate([jnp.cos(ang), jnp.cos(ang)], axis=-1), (1, LANES // HEAD_DIM))
    sin = jnp.tile(jnp.concatenate([-jnp.sin(ang), jnp.sin(ang)], axis=-1), (1, LANES // HEAD_DIM))
    log_gamma = jnp.log1p(-(2.0 ** (-5.0 - jnp.arange(N_HEADS, dtype=F32))))
    idx = jnp.arange(CHUNK, dtype=F32)
    rel = idx[:, None] - idx[None, :]
    intra = jnp.where(rel[None] >= 0,
                      jnp.exp(jnp.maximum(rel, 0.0)[None] * log_gamma[:, None, None]), 0.0)
    expand = lambda m: jnp.repeat(m, HEAD_DIM, axis=1)
    qdec = expand(jnp.exp((idx + 1.0)[:, None] * log_gamma[None, :]))
    kte = expand(jnp.exp((CHUNK - 1 - idx)[:, None] * log_gamma[None, :]))
    cgam = expand(jnp.exp(CHUNK * log_gamma)[None, :])
    head = jnp.arange(WIDTH) // HEAD_DIM
    bd = (head[:, None] == head[None, :]).astype(BF16)
    return cos, sin, intra, qdec, kte, cgam, bd


SSD_INPUTS = 15
RET_INPUTS = 12


def _ssd_retention_kernel(*refs):
    ssd_in = refs[:SSD_INPUTS]
    ret_in = refs[SSD_INPUTS:SSD_INPUTS + RET_INPUTS]
    ssd_out, ret_out, xe_ref, bce_ref, st_ref, y_ref, r_ref = refs[SSD_INPUTS + RET_INPUTS:]
    c = pl.program_id(0)

    @pl.when(c == 0)
    def _():
        st_ref[...] = jnp.zeros_like(st_ref)
        r_ref[...] = jnp.zeros_like(r_ref)

    keep = (c > 0).astype(F32)
    programs = []
    for b in range(st_ref.shape[0]):
        programs.append(_ssd_chunk(b, keep, *ssd_in, ssd_out, xe_ref, bce_ref, st_ref, y_ref))
        programs.append(_ret_chunk(b, *ret_in, ret_out, r_ref))
    _interleave(programs)


def _ssd_retention(z, x, bc, dt, conv_w, conv_b, dt_bias, a_log, d_skip, ssd_norm_w,
                   q, k, v, g, ret_norm_w, batch, seq):
    nc = seq // CHUNK
    pad = LANES - N_HEADS
    e = jnp.repeat(jnp.eye(LANES, N_HEADS, dtype=BF16), HEAD_DIM, axis=1)
    seq3 = lambda a: a.reshape(batch, seq, a.shape[-1])
    z, x, bc, dt = seq3(z), seq3(x), seq3(bc), seq3(dt)
    ssd_args = (z, x, x, bc, bc, dt,
                conv_w[:, :WIDTH], conv_b[:WIDTH].reshape(1, WIDTH),
                conv_w[:, WIDTH:], conv_b[WIDTH:].reshape(1, 2 * LANES),
                jnp.pad(dt_bias, (0, pad)).reshape(1, LANES),
                jnp.pad(a_log, (0, pad)).reshape(1, LANES),
                jnp.repeat(d_skip, HEAD_DIM).reshape(1, WIDTH),
                ssd_norm_w.reshape(1, WIDTH), e)
    ssd_specs = [_seq_spec(batch, WIDTH), _seq_spec(batch, WIDTH), _seq_halo_spec(batch, WIDTH),
                 _seq_spec(batch, 2 * LANES), _seq_halo_spec(batch, 2 * LANES),
                 _seq_spec(batch, LANES),
                 _param_spec1((SSD_CONV, WIDTH)), _param_spec1((1, WIDTH)),
                 _param_spec1((SSD_CONV, 2 * LANES)), _param_spec1((1, 2 * LANES)),
                 _param_spec1((1, LANES)), _param_spec1((1, LANES)),
                 _param_spec1((1, WIDTH)), _param_spec1((1, WIDTH)),
                 _param_spec1((LANES, WIDTH))]
    cos, sin, intra, qdec, kte, cgam, bd = _retention_tables(seq)
    table_spec = pl.BlockSpec((CHUNK, LANES), lambda c: (c, 0))
    ret_args = (seq3(q), seq3(k), seq3(v), seq3(g), cos, sin, intra, qdec, kte, cgam, bd,
                ret_norm_w.reshape(1, WIDTH))
    ret_specs = [_seq_spec(batch, WIDTH)] * 4 + [
        table_spec, table_spec, _param_spec1((N_HEADS, CHUNK, CHUNK)),
        _param_spec1((CHUNK, WIDTH)), _param_spec1((CHUNK, WIDTH)),
        _param_spec1((1, WIDTH)), _param_spec1((WIDTH, WIDTH)), _param_spec1((1, WIDTH))]
    assert len(ssd_args) == SSD_INPUTS and len(ret_args) == RET_INPUTS
    out_shape = jax.ShapeDtypeStruct((batch, seq, WIDTH), BF16)
    y_ssd, y_ret = pl.pallas_call(
        _ssd_retention_kernel,
        grid=(nc,),
        in_specs=ssd_specs + ret_specs,
        out_specs=[_seq_spec(batch, WIDTH), _seq_spec(batch, WIDTH)],
        out_shape=[out_shape, out_shape],
        scratch_shapes=[pltpu.VMEM((batch, CHUNK + HALO, WIDTH), F32),
                        pltpu.VMEM((batch, CHUNK + HALO, 2 * LANES), F32),
                        pltpu.VMEM((batch, 2, LANES, 2 * LANES), F32),
                        pltpu.VMEM((batch, CHUNK, WIDTH), F32),
                        pltpu.VMEM((batch, WIDTH, WIDTH), F32)],
        compiler_params=_cparams(("arbitrary",)),
        name="ssd_retention",
    )(*ssd_args, *ret_args)
    return y_ssd.reshape(batch * seq, WIDTH), y_ret.reshape(batch * seq, WIDTH)


def _rwkv_program(keep, r_ref, rh_ref, k_ref, kh_ref, v_ref, vh_ref, lo_ref, loh_ref,
                  mur_ref, muk_ref, muv_ref, mulo_ref, w0_ref, wup_ref, a0_ref, aup_ref, gup_ref,
                  kkw_ref, kaw_ref, rkw_ref, lnw_ref, lnb_ref, bd_ref,
                  o_ref, s_ref, sr_ref, sk_ref, sv_ref, slo_ref):
    nb, _, width = r_ref.shape
    npair = width // LANES
    bd = bd_ref[...]
    row = _iota2((CHUNK, CHUNK), 0)
    col = _iota2((CHUNK, CHUNK), 1)
    strict = row > col
    causal = row >= col
    eye = (row == col).astype(F32)
    tri = causal.astype(BF16)
    same_head = (row // HEAD_DIM == col // HEAD_DIM).astype(F32)
    m1 = [_head_mask(j) for j in range(2)]
    m2 = [_head_mask(j, 2 * LANES) for j in range(2)]
    cat0 = lambda xs: jnp.concatenate(xs, axis=0)
    cat1 = lambda xs: jnp.concatenate(xs, axis=1)
    C = CHUNK

    def token_shift(b, main_ref, halo_ref, mu_ref, scr_ref):
        _with_prev_rows(scr_ref.at[b], main_ref.at[b], halo_ref.at[b], keep)
        cur = main_ref[b]
        prev = scr_ref[b, HALO - 1:HALO - 1 + CHUNK, :]
        return cur + (prev - cur) * mu_ref[...]

    pre = []
    for b in range(nb):
        r = token_shift(b, r_ref, rh_ref, mur_ref, sr_ref)
        k = token_shift(b, k_ref, kh_ref, muk_ref, sk_ref)
        v = token_shift(b, v_ref, vh_ref, muv_ref, sv_ref)
        lo = token_shift(b, lo_ref, loh_ref, mulo_ref, slo_ref)
        lo_wa = lo[:, :LANES]
        log_w = -RWKV_DECAY_SCALE * _sigmoid(w0_ref[...] + _mm(jnp.tanh(lo_wa), wup_ref[...]))
        a = _sigmoid(a0_ref[...] + _mm(lo_wa, aup_ref[...]))
        gate = _mm(_sigmoid(lo[:, LANES:]), gup_ref[...])
        kk = k * kkw_ref[...]
        kk = kk / jnp.maximum(jnp.sqrt(_mm(kk * kk, bd)), 1e-12)
        k = k * (1.0 + (a - 1.0) * kaw_ref[...])
        cum = _mm_split(tri, log_w, 2)
        mid = cum[C // 2 - 1:C // 2, :]
        last = cum[C - 1:C, :]
        inv_p = jnp.exp(mid - cum)
        to_end = jnp.exp(last - mid)
        pre.append(dict(
            r=r, k=k, v=v, gate=gate,
            a_t=-kk * jnp.exp(cum - log_w - mid), r_t=r * jnp.exp(cum - mid),
            b_t=kk * a * inv_p, k_t=k * inv_p, e_mid=jnp.exp(mid),
            b_end=kk * a * inv_p * to_end, k_end=k * inv_p * to_end, decay=jnp.exp(last)))

    yield
    units = [(b, q) for b in range(nb) for q in range(npair)]
    slab = lambda name: [pre[b][name][:, q * LANES:(q + 1) * LANES] for b, q in units]
    a_t, r_t, b_t, k_t, v_u = slab("a_t"), slab("r_t"), slab("b_t"), slab("k_t"), slab("v")
    e_mid, b_end, k_end, decay = slab("e_mid"), slab("b_end"), slab("k_end"), slab("decay")

    mm = [_mm_nt(cat0([a * m1[0], r * m1[0], a * m1[1], r * m1[1]]), cat0([bt, kt]))
          for a, r, bt, kt in zip(a_t, r_t, b_t, k_t)]
    chains = [(u, j) for u in range(len(units)) for j in range(2)]
    m_ab = [jnp.where(strict, mm[u][2 * j * C:(2 * j + 1) * C, :C], 0.0) for u, j in chains]
    m_ak = [jnp.where(strict, mm[u][2 * j * C:(2 * j + 1) * C, C:], 0.0).astype(BF16)
            for u, j in chains]
    m_rb = [jnp.where(causal, mm[u][(2 * j + 1) * C:(2 * j + 2) * C, :C], 0.0).astype(BF16)
            for u, j in chains]
    m_rk = [jnp.where(causal, mm[u][(2 * j + 1) * C:(2 * j + 2) * C, C:], 0.0).astype(BF16)
            for u, j in chains]
    yield

    inv = [eye + m for m in m_ab]
    power = [m.astype(BF16) for m in m_ab]
    for _ in range(int(math.log2(C)) - 1):
        power = [_mm(p, p).astype(BF16) for p in power]
        inv = [i + _mm(i, p) for i, p in zip(inv, power)]
        yield

    n_units = range(len(units))
    per_head = lambda x2: x2[:C] * m2[0] + x2[C:] * m2[1]
    mv = [_mm(cat0([m_ak[2 * u], m_rk[2 * u], m_ak[2 * u + 1], m_rk[2 * u + 1]]), v_u[u])
          for u in n_units]
    mv_ak = [mv[u][:C] * m1[0] + mv[u][2 * C:3 * C] * m1[1] for u in n_units]
    mv_rk = [mv[u][C:2 * C] * m1[0] + mv[u][3 * C:] * m1[1] for u in n_units]
    z = [per_head(_mm(cat0([inv[2 * u], inv[2 * u + 1]]), cat1([a_t[u], mv_ak[u]])))
         for u in n_units]
    yield
    w = [per_head(_mm(cat0([m_rb[2 * u], m_rb[2 * u + 1]]), z[u])) for u in n_units]
    yield

    s = [s_ref[b, q] for b, q in units]
    uy = [_mm_nt(cat0([z[u][:, :LANES] * e_mid[u], (r_t[u] + w[u][:, :LANES]) * e_mid[u]]), s[u])
          for u in n_units]
    u_in = [uy[u][:C] + z[u][:, LANES:] for u in n_units]
    y = [uy[u][C:] + w[u][:, LANES:] + mv_rk[u] for u in n_units]
    s_inc = [_mm_tn(cat0([u_in[u], v_u[u]]), cat0([b_end[u], k_end[u]])) for u in n_units]
    for u, (b, q) in enumerate(units):
        s_ref[b, q] = s[u] * decay[u] + same_head * s_inc[u]

    for b in range(nb):
        yb = cat1([y[b * npair + q] for q in range(npair)])
        mean = _mm(yb, bd) * (1.0 / HEAD_DIM)
        d = yb - mean
        var = _mm(d * d, bd) * (1.0 / HEAD_DIM)
        yn = d * lax.rsqrt(var + RWKV_LN_EPS) * lnw_ref[...] + lnb_ref[...]
        bonus = _mm(pre[b]["r"] * pre[b]["k"] * rkw_ref[...], bd) * pre[b]["v"]
        o_ref[b] = ((yn + bonus) * pre[b]["gate"]).astype(BF16)


FOX_TILE = 2048
FOX_Q_SPLIT = 8
FOX_GUARD = 64.0
LOG2E = math.log2(math.e)


def _fox_prep_chunk(b, q_ref, k_ref, v_ref, f_ref, qw_ref, kw_ref, fb_ref, bd_ref, spread_ref,
                    qo_ref, ko_ref, vo_ref, carry_ref):
    bd = bd_ref[...]
    q, k, v = q_ref[b], k_ref[b], v_ref[b]
    q_ms = _mm(q * q, bd) * (1.0 / HEAD_DIM)
    k_ms = _mm(k * k, bd) * (1.0 / HEAD_DIM)

    f = f_ref[b] + fb_ref[...]
    log_f = jnp.minimum(f, 0.0) - jnp.log1p(jnp.exp(-jnp.abs(f)))
    rows = CHUNK
    tri = (_iota2((rows, rows), 0) >= _iota2((rows, rows), 1)).astype(BF16)
    cum = _mm_split(tri, log_f, 3) + carry_ref[b, 0:1, :]
    carry_ref[b] = jnp.broadcast_to(cum[rows - 1:rows, :], carry_ref.shape[1:])
    yield
    q = q * lax.rsqrt(q_ms + EPS) * qw_ref[...] * (HEAD_DIM ** -0.5 * LOG2E)
    k = k * lax.rsqrt(k_ms + EPS) * kw_ref[...]
    c_all = _mm_split_rhs(cum * LOG2E, spread_ref[...], 3)
    yield

    lane = _iota2((1, LANES), 1)
    for h in range(N_HEADS):
        p, j = divmod(h, 2)
        ps = slice(p * LANES, (p + 1) * LANES)
        mj = _head_mask(j)
        qh, kh = q[:, ps] * mj, k[:, ps] * mj
        if j == 1:
            qh = pltpu.roll(qh, HEAD_DIM, 1)
            kh = pltpu.roll(kh, HEAD_DIM, 1)
        ch = c_all[:, h * LANES:(h + 1) * LANES]
        hi = ch.astype(BF16).astype(F32)
        mid = (ch - hi).astype(BF16).astype(F32)
        low = ch - hi - mid
        piece = lambda base: jnp.where(lane == base, hi,
                                       jnp.where(lane == base + 1, mid,
                                                 jnp.where(lane == base + 2, low, 0.0)))
        ones = lambda base: ((lane >= base) & (lane < base + 3)).astype(F32)
        qo_ref[b, h] = (qh + piece(HEAD_DIM) + ones(HEAD_DIM + 3)).astype(BF16)
        ko_ref[b, h] = (kh + ones(HEAD_DIM) - piece(HEAD_DIM + 3)).astype(BF16)
    row_head = _iota2((LANES, 1), 0) // HEAD_DIM
    row_in_head = _iota2((LANES, 1), 0) % HEAD_DIM
    for p in range(N_HEADS // 2):
        v_t = v[:, p * LANES:(p + 1) * LANES].T
        for j in range(2):
            ones_row = ((row_head != j) & (row_in_head == 0)).astype(F32)
            vo_ref[b, 2 * p + j] = jnp.where(row_head == j, v_t, ones_row).astype(BF16)


RWKV_INPUTS = 23
FOX_PREP_INPUTS = 9


def _rwkv_fox_prep_kernel(*refs):
    rwkv_in = refs[:RWKV_INPUTS]
    prep_in = refs[RWKV_INPUTS:RWKV_INPUTS + FOX_PREP_INPUTS]
    (rwkv_out, qo_ref, ko_ref, vo_ref,
     s_ref, sr_ref, sk_ref, sv_ref, slo_ref, carry_ref) = refs[RWKV_INPUTS + FOX_PREP_INPUTS:]
    c = pl.program_id(0)

    @pl.when(c == 0)
    def _():
        s_ref[...] = jnp.zeros_like(s_ref)
        carry_ref[...] = jnp.zeros_like(carry_ref)

    keep = (c > 0).astype(F32)
    def after(stages, program):
        for _ in range(stages):
            yield
        yield from program

    programs = [_rwkv_program(keep, *rwkv_in, rwkv_out, s_ref, sr_ref, sk_ref, sv_ref, slo_ref)]
    for b in range(carry_ref.shape[0]):
        prep = _fox_prep_chunk(b, *prep_in, qo_ref, ko_ref, vo_ref, carry_ref)
        programs.append(after(2 + 2 * b, prep))
    _interleave(programs)


def _rwkv_fox_prep(r, k, v, lo, mu, w0, w_up, a0, a_up, g_up, k_k, k_a, r_k, ln_w, ln_b,
                   fq, fk, fv, ff, q_norm_w, k_norm_w, f_bias, batch, seq):
    nc = seq // CHUNK
    lo_width = lo.shape[-1]
    rank = w_up.shape[0]
    row1 = lambda a: a.reshape(1, -1)
    seq3 = lambda a: a.reshape(batch, seq, a.shape[-1])

    wup_pad = jnp.concatenate([w_up, jnp.zeros((LANES - rank, WIDTH), F32)], axis=0).astype(BF16)
    aup_pad = jnp.concatenate([jnp.zeros((LANES - rank, WIDTH), F32), a_up], axis=0).astype(BF16)
    head = jnp.arange(WIDTH) // HEAD_DIM
    bd = (head[:, None] == head[None, :]).astype(BF16)
    r, k, v, lo = seq3(r), seq3(k), seq3(v), seq3(lo)
    rwkv_args = (r, r, k, k, v, v, lo, lo,
                 row1(mu[:WIDTH]), row1(mu[WIDTH:2 * WIDTH]), row1(mu[2 * WIDTH:3 * WIDTH]),
                 row1(mu[3 * WIDTH:]),
                 row1(w0), wup_pad, row1(a0), aup_pad, g_up.astype(BF16),
                 row1(k_k), row1(k_a), row1(r_k), row1(ln_w), row1(ln_b), bd)
    wide, wide_halo = _seq_spec(batch, WIDTH), _seq_halo_spec(batch, WIDTH)
    param = _param_spec1((1, WIDTH))
    weight = _param_spec1((LANES, WIDTH))
    rwkv_specs = [wide, wide_halo, wide, wide_halo, wide, wide_halo,
                  _seq_spec(batch, lo_width), _seq_halo_spec(batch, lo_width),
                  param, param, param, _param_spec1((1, lo_width)),
                  param, weight, param, weight, weight,
                  param, param, param, param, param, _param_spec1((WIDTH, WIDTH))]

    dst = jnp.arange(N_HEADS * LANES)
    spread = ((dst[None, :] // LANES == jnp.arange(LANES)[:, None])
              & (dst[None, :] % LANES >= HEAD_DIM) & (dst[None, :] % LANES < HEAD_DIM + 6)).astype(BF16)
    prep_args = (seq3(fq), seq3(fk), seq3(fv), seq3(ff),
                 row1(jnp.tile(q_norm_w, N_HEADS)), row1(jnp.tile(k_norm_w, N_HEADS)),
                 row1(jnp.pad(f_bias, (0, LANES - N_HEADS))), bd, spread)
    prep_specs = [wide, wide, wide, _seq_spec(batch, LANES), param, param,
                  _param_spec1((1, LANES)), _param_spec1((WIDTH, WIDTH)),
                  _param_spec1((LANES, N_HEADS * LANES))]
    assert len(rwkv_args) == RWKV_INPUTS and len(prep_args) == FOX_PREP_INPUTS

    head_out = pl.BlockSpec((batch, N_HEADS, CHUNK, LANES), lambda c: (0, 0, c, 0))
    head_shape = jax.ShapeDtypeStruct((batch, N_HEADS, seq, LANES), BF16)
    vt_out = pl.BlockSpec((batch, N_HEADS, LANES, CHUNK), lambda c: (0, 0, 0, c))
    vt_shape = jax.ShapeDtypeStruct((batch, N_HEADS, LANES, seq), BF16)
    shifted = lambda width: pltpu.VMEM((batch, CHUNK + HALO, width), F32)
    y_rwkv, qa, ka, va = pl.pallas_call(
        _rwkv_fox_prep_kernel,
        grid=(nc,),
        in_specs=rwkv_specs + prep_specs,
        out_specs=[wide, head_out, head_out, vt_out],
        out_shape=[jax.ShapeDtypeStruct((batch, seq, WIDTH), BF16), head_shape, head_shape,
                   vt_shape],
        scratch_shapes=[pltpu.VMEM((batch, N_HEADS // 2, LANES, LANES), F32),
                        shifted(WIDTH), shifted(WIDTH), shifted(WIDTH), shifted(lo_width),
                        pltpu.VMEM((batch, HALO, LANES), F32)],
        compiler_params=_cparams(("arbitrary",)),
        name="rwkv7_fox_prep",
    )(*rwkv_args, *prep_args)
    return y_rwkv.reshape(batch * seq, WIDTH), qa, ka, va


def _fox_kernel(qi_ref, ki_ref, q_ref, k_ref, vt_ref, o_ref, m_ref, acc_ref, redo_ref):
    t = pl.program_id(2)
    qi = qi_ref[t]
    ki = ki_ref[t]

    part = FOX_TILE // FOX_Q_SPLIT
    units = [(j, slice(h * part, (h + 1) * part)) for j in range(2) for h in range(FOX_Q_SPLIT)]

    def all_scores(on_diagonal):
        n_keys = lambda qs: qs.stop if on_diagonal else FOX_TILE
        raw = [lax.dot_general(k_ref[0, j, :n_keys(qs), :], q_ref[0, j, qs, :],
                               (((1,), (1,)), ((), ())), preferred_element_type=F32)
               for j, qs in units]
        if not on_diagonal:
            return raw
        masked = []
        for (_, qs), s in zip(units, raw):
            key = _iota2(s.shape, 0)
            query = _iota2(s.shape, 1) + qs.start
            masked.append(jnp.where(query >= key, s, NEG))
        return masked

    def pv(j, p):
        return jnp.dot(vt_ref[0, j, :, :p.shape[0]], p.astype(BF16), preferred_element_type=F32)

    def exact_step(on_diagonal):
        for (j, qs), s in zip(units, all_scores(on_diagonal)):
            m_prev = m_ref[j, :, qs]
            m_next = jnp.maximum(m_prev, jnp.max(s, axis=0, keepdims=True))
            alpha = jnp.exp2(m_prev - m_next)
            acc_ref[j, :, qs] = alpha * acc_ref[j, :, qs] + pv(j, jnp.exp2(s - m_next))
            m_ref[j, :, qs] = m_next

    def lagged_step(on_diagonal):
        done = []
        worst = None
        for (j, qs), s in zip(units, all_scores(on_diagonal)):
            m_used = m_ref[j, :, qs]
            block_max = jnp.max(s, axis=0, keepdims=True)
            contrib = pv(j, jnp.exp2(s - m_used))
            excess = jnp.max(block_max - m_used)
            worst = excess if worst is None else jnp.maximum(worst, excess)
            done.append((j, qs, m_used, block_max, contrib))
        ok = worst <= FOX_GUARD
        redo_ref[0] = jnp.where(ok, 0, 1)

        @pl.when(ok)
        def _():
            for j, qs, m_used, block_max, contrib in done:
                m_next = jnp.maximum(m_used, block_max)
                acc_ref[j, :, qs] = (acc_ref[j, :, qs] + contrib) * jnp.exp2(m_used - m_next)
                m_ref[j, :, qs] = m_next

    @pl.when(ki == qi)
    def _():
        m_ref[...] = jnp.full_like(m_ref, NEG)
        acc_ref[...] = jnp.zeros_like(acc_ref)
        exact_step(True)

    redo_ref[0] = 0

    @pl.when(ki < qi)
    def _():
        lagged_step(False)

    @pl.when(redo_ref[0] == 1)
    def _():
        exact_step(False)

    @pl.when(ki == 0)
    def _():
        head_rows = _iota2((LANES, 1), 0) // HEAD_DIM
        out_t = jnp.where(head_rows == 0,
                          acc_ref[0] / acc_ref[0, HEAD_DIM:HEAD_DIM + 1, :],
                          acc_ref[1] / acc_ref[1, 0:1, :])
        o_ref[0] = out_t.T.astype(BF16)


def _fox_attention(qa, ka, va, batch, seq):
    nt = seq // FOX_TILE
    pairs = [(i, j) for i in range(nt) for j in range(i, -1, -1)]
    qi = jnp.asarray([i for i, _ in pairs], jnp.int32)
    ki = jnp.asarray([j for _, j in pairs], jnp.int32)
    q_spec = pl.BlockSpec((1, 2, FOX_TILE, LANES), lambda b, p, t, qi, ki: (b, p, qi[t], 0))
    k_spec = pl.BlockSpec((1, 2, FOX_TILE, LANES), lambda b, p, t, qi, ki: (b, p, ki[t], 0))
    vt_spec = pl.BlockSpec((1, 2, LANES, FOX_TILE), lambda b, p, t, qi, ki: (b, p, 0, ki[t]))
    out = pl.pallas_call(
        _fox_kernel,
        grid_spec=pltpu.PrefetchScalarGridSpec(
            num_scalar_prefetch=2,
            grid=(batch, N_HEADS // 2, len(pairs)),
            in_specs=[q_spec, k_spec, vt_spec],
            out_specs=pl.BlockSpec((1, FOX_TILE, LANES), lambda b, p, t, qi, ki: (b, qi[t], p)),
            scratch_shapes=[pltpu.VMEM((2, 1, FOX_TILE), F32),
                            pltpu.VMEM((2, LANES, FOX_TILE), F32),
                            pltpu.SMEM((1,), jnp.int32)]),
        out_shape=jax.ShapeDtypeStruct((batch, seq, WIDTH), BF16),
        compiler_params=_cparams(("parallel", "parallel", "arbitrary")),
        name="fox_attention",
    )(qi, ki, qa, ka, va)
    return out.reshape(batch * seq, WIDTH)


def _pad_cols(w, width):
    return jnp.pad(w, ((0, 0), (0, width - w.shape[1])))


def _even_mixers(x, batch, seq, norm_w, w_in, conv_w, conv_b, dt_bias, a_log, d_skip,
                 ssd_norm_w, ret_norm_w):
    ssd_x0 = WIDTH
    ssd_dt0 = WIDTH + (WIDTH + 4 * SSD_STATE)
    ret0 = ssd_dt0 + N_HEADS
    w = jnp.concatenate([w_in[:, :ssd_dt0], w_in[:, ret0:],
                         _pad_cols(w_in[:, ssd_dt0:ret0], LANES)], axis=1).astype(BF16)
    segments = ((0, WIDTH), (ssd_x0, WIDTH), (2 * WIDTH, 2 * LANES),
                (ssd_dt0, WIDTH), (ssd_dt0 + WIDTH, WIDTH), (ssd_dt0 + 2 * WIDTH, WIDTH),
                (ssd_dt0 + 3 * WIDTH, WIDTH), (ssd_dt0 + 4 * WIDTH, LANES))
    z, xs, bc, q, k, v, g, dt = _inproj(x, norm_w, w, segments)
    return _ssd_retention(z, xs, bc, dt, conv_w, conv_b, dt_bias, a_log, d_skip, ssd_norm_w,
                          q, k, v, g, ret_norm_w, batch, seq)


def _odd_mixers(x, batch, seq, norm_w, w_in, mu, w0, w_up, a0, a_up, g_up, k_k, k_a, r_k,
                ln_w, ln_b, q_norm_w, k_norm_w, f_bias):
    lo_width = w_up.shape[0] + a_up.shape[0] + g_up.shape[0]
    fox0 = 3 * WIDTH + lo_width
    f0 = fox0 + 3 * WIDTH
    w = jnp.concatenate([w_in[:, :f0], _pad_cols(w_in[:, f0:], LANES)], axis=1).astype(BF16)
    segments = ((0, WIDTH), (WIDTH, WIDTH), (2 * WIDTH, WIDTH), (3 * WIDTH, lo_width),
                (fox0, WIDTH), (fox0 + WIDTH, WIDTH), (fox0 + 2 * WIDTH, WIDTH), (f0, LANES))
    r, k, v, lo, fq, fk, fv, ff = _inproj(x, norm_w, w, segments)
    y_rwkv, qa, ka, va = _rwkv_fox_prep(r, k, v, lo, mu, w0, w_up, a0, a_up, g_up, k_k, k_a,
                                        r_k.reshape(-1), ln_w, ln_b, fq, fk, fv, ff,
                                        q_norm_w, k_norm_w, f_bias, batch, seq)
    return y_rwkv, _fox_attention(qa, ka, va, batch, seq)


def kernel(x, ev_norm_w, ev_w_in, ev_ssd_conv_w, ev_ssd_conv_b, ev_ssd_dt_bias, ev_ssd_a_log,
           ev_ssd_d, ev_ssd_norm_w, ev_ret_norm_w, ev_w_out,
           od_norm_w, od_w_in, od_rwkv_mu, od_rwkv_w0, od_rwkv_w_up, od_rwkv_a0, od_rwkv_a_up,
           od_rwkv_g_up, od_rwkv_k_k, od_rwkv_k_a, od_rwkv_r_k, od_rwkv_ln_w, od_rwkv_ln_b,
           od_fox_q_norm_w, od_fox_k_norm_w, od_fox_f_bias, od_w_out,
           ffn_norm_w, ffn_w_up, ffn_conv_w, ffn_conv_b, ffn_w_down):
    batch, seq, _ = x.shape
    depth = ffn_norm_w.shape[0]
    h = x.reshape(batch * seq, D_MODEL)
    for layer in range(depth):
        i = layer // 2
        if layer % 2 == 0:
            ya, yb = _even_mixers(h, batch, seq, ev_norm_w[i], ev_w_in[i], ev_ssd_conv_w[i],
                                  ev_ssd_conv_b[i], ev_ssd_dt_bias[i], ev_ssd_a_log[i],
                                  ev_ssd_d[i], ev_ssd_norm_w[i], ev_ret_norm_w[i])
            w_out = ev_w_out[i]
        else:
            ya, yb = _odd_mixers(h, batch, seq, od_norm_w[i], od_w_in[i], od_rwkv_mu[i],
                                 od_rwkv_w0[i], od_rwkv_w_up[i], od_rwkv_a0[i], od_rwkv_a_up[i],
                                 od_rwkv_g_up[i], od_rwkv_k_k[i], od_rwkv_k_a[i], od_rwkv_r_k[i],
                                 od_rwkv_ln_w[i], od_rwkv_ln_b[i], od_fox_q_norm_w[i],
                                 od_fox_k_norm_w[i], od_fox_f_bias[i])
            w_out = od_w_out[i]
        h = _proj_ffn(ya, yb, h, seq, w_out, ffn_norm_w[layer], ffn_w_up[layer],
                      ffn_conv_w[layer], ffn_conv_b[layer], ffn_w_down[layer])
    return h.reshape(batch, seq, D_MODEL)
```

```python
import functools
import itertools
import math

import jax
import jax.numpy as jnp
from jax import lax
from jax.experimental import pallas as pl
from jax.experimental.pallas import tpu as pltpu

F32 = jnp.float32
BF16 = jnp.bfloat16

D_MODEL = 1024
HEAD_DIM = 64
N_HEADS = 8
WIDTH = N_HEADS * HEAD_DIM
LANES = 128
CHUNK = 128
HALO = 8
EPS = 1e-6
SSD_CONV = 4
SSD_STATE = 64
RWKV_LN_EPS = 64e-5
RWKV_DECAY_SCALE = 0.606531
D_FF = 2816
FF_TILE = 256
FFN_HALO = 16
FFN_LOOKAHEAD = 2
NEG = -1e30
VMEM_LIMIT = 56 * 1024 * 1024


def _mm(a, b):
    return jnp.dot(a.astype(BF16), b.astype(BF16), preferred_element_type=F32)


def _mm_nt(a, b):
    return lax.dot_general(a.astype(BF16), b.astype(BF16), (((1,), (1,)), ((), ())),
                           preferred_element_type=F32)


def _mm_tn(a, b):
    return lax.dot_general(a.astype(BF16), b.astype(BF16), (((0,), (0,)), ((), ())),
                           preferred_element_type=F32)


def _mm_split(lhs01, x, pieces):
    acc = None
    rest = x
    for _ in range(pieces):
        part = rest.astype(BF16)
        term = jnp.dot(lhs01, part, preferred_element_type=F32)
        acc = term if acc is None else acc + term
        rest = rest - part.astype(F32)
    return acc


def _mm_split_rhs(x, rhs01, pieces):
    acc = None
    rest = x
    for _ in range(pieces):
        part = rest.astype(BF16)
        term = jnp.dot(part, rhs01, preferred_element_type=F32)
        acc = term if acc is None else acc + term
        rest = rest - part.astype(F32)
    return acc


def _interleave(stage_generators):
    for _ in itertools.zip_longest(*stage_generators):
        pass


def _sigmoid(x):
    return 1.0 / (1.0 + jnp.exp(-x))


def _silu(x):
    return x * _sigmoid(x)


def _softplus(x):
    return jnp.maximum(x, 0.0) + jnp.log1p(jnp.exp(-jnp.abs(x)))


def _iota2(shape, dim):
    return lax.broadcasted_iota(jnp.int32, shape, dim)


def _head_mask(j, width=LANES):
    lane = _iota2((1, width), 1)
    return ((lane % LANES) // HEAD_DIM == j).astype(F32)


def _cparams(sem):
    return pltpu.CompilerParams(dimension_semantics=sem, vmem_limit_bytes=VMEM_LIMIT)


def _const_spec(shape):
    nd = len(shape)
    return pl.BlockSpec(shape, lambda *_: (0,) * nd, pipeline_mode=pl.Buffered(1))


def _inproj_kernel(x_ref, nw_ref, w_ref, *o_refs, segments):
    x = x_ref[...]
    ms = jnp.mean(x * x, axis=-1, keepdims=True)
    xn = (x * lax.rsqrt(ms + EPS) * nw_ref[...]).astype(BF16)
    for o_ref, (off, width) in zip(o_refs, segments):
        o_ref[...] = jnp.dot(xn, w_ref[:, off:off + width], preferred_element_type=F32)


def _inproj(x, norm_w, w, segments, tm=1024):
    t = x.shape[0]
    n = w.shape[1]
    return pl.pallas_call(
        functools.partial(_inproj_kernel, segments=segments),
        grid=(t // tm,),
        in_specs=[pl.BlockSpec((tm, D_MODEL), lambda i: (i, 0)),
                  _const_spec((1, D_MODEL)),
                  _const_spec((D_MODEL, n))],
        out_specs=[pl.BlockSpec((tm, wd), lambda i: (i, 0)) for _, wd in segments],
        out_shape=[jax.ShapeDtypeStruct((t, wd), F32) for _, wd in segments],
        compiler_params=_cparams(("parallel",)),
        name="inproj",
    )(x, norm_w.reshape(1, D_MODEL), w)


def _proj_ffn_kernel(ya_ref, yah_ref, yb_ref, ybh_ref, x_ref, xh_ref, wo_ref, nw_ref,
                     wup_ref, cw_ref, cb_ref, wd_ref, o_ref,
                     ya_ext, yb_ext, h_ref, hn_ref, g_ref, acc_ref, *, tm, tiles_per_seq):
    ya_ext[:FFN_HALO, :] = yah_ref[...]
    ya_ext[FFN_HALO:, :] = ya_ref[...]
    yb_ext[:FFN_HALO, :] = ybh_ref[...]
    yb_ext[FFN_HALO:, :] = yb_ref[...]
    h_ref[:FFN_HALO, :] = xh_ref[...]
    h_ref[FFN_HALO:, :] = x_ref[...]
    h_ref[...] += (jnp.dot(ya_ext[...], wo_ref[:WIDTH, :], preferred_element_type=F32)
                   + jnp.dot(yb_ext[...], wo_ref[WIDTH:, :], preferred_element_type=F32))

    h = h_ref[...]
    ms = jnp.mean(h * h, axis=-1, keepdims=True)
    hn = h * lax.rsqrt(ms + EPS) * nw_ref[...]
    first_of_seq = pl.program_id(0) % tiles_per_seq == 0
    halo_rows = _iota2((tm + FFN_HALO, 1), 0) < FFN_HALO
    hn_ref[...] = jnp.where(halo_rows & first_of_seq, 0.0, hn).astype(BF16)
    acc_ref[...] = jnp.zeros_like(acc_ref)

    def tile(f, base=0):
        return slice(base + f * FF_TILE, base + (f + 1) * FF_TILE)

    n_buf = g_ref.shape[0]

    def up_proj(f):
        g_ref[f % n_buf] = jnp.dot(hn_ref[...], wup_ref[:, tile(f)], preferred_element_type=F32)
        return jnp.dot(hn_ref[FFN_HALO:, :], wup_ref[:, tile(f, D_FF)],
                       preferred_element_type=F32)

    nf = D_FF // FF_TILE
    ups = {f: up_proj(f) for f in range(FFN_LOOKAHEAD)}
    for f in range(nf):
        if f + FFN_LOOKAHEAD < nf:
            ups[f + FFN_LOOKAHEAD] = up_proj(f + FFN_LOOKAHEAD)
        up = ups.pop(f)
        g = g_ref.at[f % n_buf]
        cw = cw_ref[:, tile(f)]
        gate = (cb_ref[:, tile(f)]
                + cw[0:1, :] * g[FFN_HALO - 2:FFN_HALO - 2 + tm, :]
                + cw[1:2, :] * g[FFN_HALO - 1:FFN_HALO - 1 + tm, :]
                + cw[2:3, :] * g[FFN_HALO:, :])
        act = (_silu(gate) * up).astype(BF16)
        acc_ref[...] += jnp.dot(act, wd_ref[tile(f), :], preferred_element_type=F32)
    o_ref[...] = h_ref[FFN_HALO:, :] + acc_ref[...]


def _proj_ffn(ya, yb, x, seq, w_out, norm_w, layer, w_up_all, conv_w, conv_b, w_down_all, tm=512):
    t = x.shape[0]
    halo_per_tile = tm // FFN_HALO
    row_spec = lambda width: pl.BlockSpec((tm, width), lambda i: (i, 0))
    halo_spec = lambda width: pl.BlockSpec(
        (FFN_HALO, width), lambda i: (jnp.maximum(i * halo_per_tile - 1, 0), 0))
    ext = lambda width, dtype: pltpu.VMEM((tm + FFN_HALO, width), dtype)
    layer_spec = lambda shape: pl.BlockSpec((None,) + shape, lambda i: (layer, 0, 0),
                                            pipeline_mode=pl.Buffered(1))
    return pl.pallas_call(
        functools.partial(_proj_ffn_kernel, tm=tm, tiles_per_seq=seq // tm),
        grid=(t // tm,),
        in_specs=[row_spec(WIDTH), halo_spec(WIDTH), row_spec(WIDTH), halo_spec(WIDTH),
                  row_spec(D_MODEL), halo_spec(D_MODEL),
                  _const_spec((2 * WIDTH, D_MODEL)), _const_spec((1, D_MODEL)),
                  layer_spec((D_MODEL, 2 * D_FF)), _const_spec((3, D_FF)), _const_spec((1, D_FF)),
                  layer_spec((D_FF, D_MODEL))],
        out_specs=row_spec(D_MODEL),
        out_shape=jax.ShapeDtypeStruct((t, D_MODEL), F32),
        scratch_shapes=[ext(WIDTH, BF16), ext(WIDTH, BF16), ext(D_MODEL, F32), ext(D_MODEL, BF16),
                        pltpu.VMEM((FFN_LOOKAHEAD + 1, tm + FFN_HALO, FF_TILE), F32),
                        pltpu.VMEM((tm, D_MODEL), F32)],
        compiler_params=_cparams(("parallel",)),
        name="proj_convffn",
    )(ya, ya, yb, yb, x, x, w_out.astype(BF16), norm_w.reshape(1, D_MODEL),
      w_up_all, conv_w, conv_b.reshape(1, D_FF), w_down_all)


def _seq_spec(batch, width):
    return pl.BlockSpec((batch, CHUNK, width), lambda c: (0, c, 0))


def _seq_halo_spec(batch, width):
    per = CHUNK // HALO
    return pl.BlockSpec((batch, HALO, width), lambda c: (0, jnp.maximum(c * per - 1, 0), 0))


def _param_spec1(shape):
    return pl.BlockSpec(shape, lambda c: (0,) * len(shape), pipeline_mode=pl.Buffered(1))


def _with_prev_rows(scr_ref, main_ref, halo_ref, keep):
    scr_ref[:HALO, :] = halo_ref[...] * keep
    scr_ref[HALO:, :] = main_ref[...]


def _ssd_chunk(b, keep, z_ref, x_ref, xh_ref, bc_ref, bch_ref, dt_ref,
               cwx_ref, cbx_ref, cwb_ref, cbb_ref, dtb_ref, alog_ref, dexp_ref, nw_ref, e_ref,
               o_ref, xe_ref, bce_ref, st_ref, y_ref):
    _with_prev_rows(xe_ref.at[b], x_ref.at[b], xh_ref.at[b], keep)
    _with_prev_rows(bce_ref.at[b], bc_ref.at[b], bch_ref.at[b], keep)

    def conv(ref, cw_ref, cb_ref):
        out = cb_ref[...]
        for k in range(SSD_CONV):
            lo = HALO - (SSD_CONV - 1) + k
            out = out + cw_ref[k:k + 1, :] * ref[b, lo:lo + CHUNK, :]
        return _silu(out)

    x = conv(xe_ref, cwx_ref, cbx_ref)
    bc = conv(bce_ref, cwb_ref, cbb_ref)
    bm = bc[:, :LANES]
    cm = bc[:, LANES:]

    row = _iota2((CHUNK, CHUNK), 0)
    col = _iota2((CHUNK, CHUNK), 1)
    causal = row >= col

    dt = _softplus(dt_ref[b] + dtb_ref[...])
    a = dt * (-jnp.exp(alog_ref[...]))
    a_cum = _mm_split(causal.astype(BF16), a, 3)
    yield
    a_cum_t = a_cum.T
    acausal_penalty = jnp.where(causal, 0.0, NEG)
    a_last = a_cum[CHUNK - 1:CHUNK, :]
    per_head = jnp.concatenate(
        [dt, jnp.exp(a_cum), jnp.exp(a_last - a_cum),
         jnp.broadcast_to(jnp.exp(a_last), (HALO, LANES))], axis=0)
    per_lane = _mm_split_rhs(per_head, e_ref[...], 2)
    dt_e = per_lane[:CHUNK]
    ea_e = per_lane[CHUNK:2 * CHUNK]
    te_e = per_lane[2 * CHUNK:3 * CHUNK]
    cd_e = per_lane[3 * CHUNK:3 * CHUNK + 1]
    yield

    xdt = x * dt_e
    xs = xdt * te_e
    bm_t = bm.T
    group_rows = _iota2((LANES, 1), 0) // SSD_STATE

    for g in range(2):
        cg = cm * _head_mask(g)
        cb = _mm_nt(cg, bm)
        gs = slice(g * 2 * LANES, (g + 1) * 2 * LANES)
        st = st_ref[b, g]
        y_off = _mm(cg, st) * ea_e[:, gs]
        st_new = _mm(bm_t, xs[:, gs])
        st_ref[b, g] = st * cd_e[:, gs] + jnp.where(group_rows == g, st_new, 0.0)
        yield
        for pp in range(2):
            p = 2 * g + pp
            xp = xdt[:, p * LANES:(p + 1) * LANES]
            yp = y_off[:, pp * LANES:(pp + 1) * LANES]
            for j in range(2):
                h = 2 * p + j
                seg = a_cum[:, h:h + 1] - a_cum_t[h:h + 1, :]
                decay = jnp.exp(seg + acausal_penalty)
                yp = yp + _mm(cb * decay, xp * _head_mask(j))
            y_ref[b, :, p * LANES:(p + 1) * LANES] = yp
            yield

    y = y_ref[b] + dexp_ref[...] * x
    gated = y * _silu(z_ref[b])
    ms = jnp.mean(gated * gated, axis=-1, keepdims=True)
    o_ref[b] = (gated * lax.rsqrt(ms + EPS) * nw_ref[...]).astype(BF16)


def _ret_chunk(b, q_ref, k_ref, v_ref, g_ref, cos_ref, sin_ref, intra_ref, qdec_ref, kte_ref,
               cg_ref, bd_ref, nw_ref, o_ref, r_ref):
    lane = _iota2((1, WIDTH), 1)
    first_half = (lane % HEAD_DIM) < (HEAD_DIM // 2)
    cos = jnp.concatenate([cos_ref[...]] * (WIDTH // LANES), axis=1)
    sin = jnp.concatenate([sin_ref[...]] * (WIDTH // LANES), axis=1)

    def rotary(v):
        other = jnp.where(first_half,
                          pltpu.roll(v, WIDTH - HEAD_DIM // 2, 1),
                          pltpu.roll(v, HEAD_DIM // 2, 1))
        return v * cos + other * sin

    q = rotary(q_ref[b])
    k = rotary(k_ref[b]) * (HEAD_DIM ** -0.5)
    v = v_ref[b]
    bd = bd_ref[...]
    pair = lambda t, p: t[:, p * LANES:(p + 1) * LANES]
    heads = [(p, j) for p in range(N_HEADS // 2) for j in range(2)]
    masks = [_head_mask(j) for j in range(2)]

    y_cross = _mm(q * qdec_ref[...], r_ref[b])
    r_ref[b] = r_ref[b] * cg_ref[...] + bd * _mm_tn(k * kte_ref[...], v)
    scores = [_mm_nt(pair(q, p) * masks[j], pair(k, p)) for p, j in heads]
    yield
    y_pairs = []
    for p in range(N_HEADS // 2):
        yp = pair(y_cross, p)
        for j in range(2):
            h = 2 * p + j
            yp = yp + _mm(scores[h] * intra_ref[h], pair(v, p) * masks[j])
        y_pairs.append(yp)
    yield
    y = jnp.concatenate(y_pairs, axis=1)
    ms = _mm(y * y, bd) * (1.0 / HEAD_DIM)
    yield
    o_ref[b] = (_silu(g_ref[b]) * (y * lax.rsqrt(ms + EPS) * nw_ref[...])).astype(BF16)


def _retention_tables(seq):
    half = HEAD_DIM // 2
    inv = 1.0 / (10000.0 ** (jnp.arange(half, dtype=F32) / half))
    ang = jnp.arange(seq, dtype=F32)[:, None] * inv[None, :]
    cos = jnp.tile(jnp.concatenate([jnp.cos(ang), jnp.cos(ang)], axis=-1), (1, LANES // HEAD_DIM))
    sin = jnp.tile(jnp.concatenate([-jnp.sin(ang), jnp.sin(ang)], axis=-1), (1, LANES // HEAD_DIM))
    log_gamma = jnp.log1p(-(2.0 ** (-5.0 - jnp.arange(N_HEADS, dtype=F32))))
    idx = jnp.arange(CHUNK, dtype=F32)
    rel = idx[:, None] - idx[None, :]
    intra = jnp.where(rel[None] >= 0,
                      jnp.exp(jnp.maximum(rel, 0.0)[None] * log_gamma[:, None, None]), 0.0)
    expand = lambda m: jnp.repeat(m, HEAD_DIM, axis=1)
    qdec = expand(jnp.exp((idx + 1.0)[:, None] * log_gamma[None, :]))
    kte = expand(jnp.exp((CHUNK - 1 - idx)[:, None] * log_gamma[None, :]))
    cgam = expand(jnp.exp(CHUNK * log_gamma)[None, :])
    head = jnp.arange(WIDTH) // HEAD_DIM
    bd = (head[:, None] == head[None, :]).astype(BF16)
    return cos, sin, intra, qdec, kte, cgam, bd


SSD_INPUTS = 15
RET_INPUTS = 12


def _ssd_retention_kernel(*refs):
    ssd_in = refs[:SSD_INPUTS]
    ret_in = refs[SSD_INPUTS:SSD_INPUTS + RET_INPUTS]
    ssd_out, ret_out, xe_ref, bce_ref, st_ref, y_ref, r_ref = refs[SSD_INPUTS + RET_INPUTS:]
    c = pl.program_id(0)

    @pl.when(c == 0)
    def _():
        st_ref[...] = jnp.zeros_like(st_ref)
        r_ref[...] = jnp.zeros_like(r_ref)

    keep = (c > 0).astype(F32)
    programs = []
    for b in range(st_ref.shape[0]):
        programs.append(_ssd_chunk(b, keep, *ssd_in, ssd_out, xe_ref, bce_ref, st_ref, y_ref))
        programs.append(_ret_chunk(b, *ret_in, ret_out, r_ref))
    _interleave(programs)


def _ssd_retention(z, x, bc, dt, conv_w, conv_b, dt_bias, a_log, d_skip, ssd_norm_w,
                   q, k, v, g, ret_norm_w, batch, seq):
    nc = seq // CHUNK
    pad = LANES - N_HEADS
    e = jnp.repeat(jnp.eye(LANES, N_HEADS, dtype=BF16), HEAD_DIM, axis=1)
    seq3 = lambda a: a.reshape(batch, seq, a.shape[-1])
    z, x, bc, dt = seq3(z), seq3(x), seq3(bc), seq3(dt)
    ssd_args = (z, x, x, bc, bc, dt,
                conv_w[:, :WIDTH], conv_b[:WIDTH].reshape(1, WIDTH),
                conv_w[:, WIDTH:], conv_b[WIDTH:].reshape(1, 2 * LANES),
                jnp.pad(dt_bias, (0, pad)).reshape(1, LANES),
                jnp.pad(a_log, (0, pad)).reshape(1, LANES),
                jnp.repeat(d_skip, HEAD_DIM).reshape(1, WIDTH),
                ssd_norm_w.reshape(1, WIDTH), e)
    ssd_specs = [_seq_spec(batch, WIDTH), _seq_spec(batch, WIDTH), _seq_halo_spec(batch, WIDTH),
                 _seq_spec(batch, 2 * LANES), _seq_halo_spec(batch, 2 * LANES),
                 _seq_spec(batch, LANES),
                 _param_spec1((SSD_CONV, WIDTH)), _param_spec1((1, WIDTH)),
                 _param_spec1((SSD_CONV, 2 * LANES)), _param_spec1((1, 2 * LANES)),
                 _param_spec1((1, LANES)), _param_spec1((1, LANES)),
                 _param_spec1((1, WIDTH)), _param_spec1((1, WIDTH)),
                 _param_spec1((LANES, WIDTH))]
    cos, sin, intra, qdec, kte, cgam, bd = _retention_tables(seq)
    table_spec = pl.BlockSpec((CHUNK, LANES), lambda c: (c, 0))
    ret_args = (seq3(q), seq3(k), seq3(v), seq3(g), cos, sin, intra, qdec, kte, cgam, bd,
                ret_norm_w.reshape(1, WIDTH))
    ret_specs = [_seq_spec(batch, WIDTH)] * 4 + [
        table_spec, table_spec, _param_spec1((N_HEADS, CHUNK, CHUNK)),
        _param_spec1((CHUNK, WIDTH)), _param_spec1((CHUNK, WIDTH)),
        _param_spec1((1, WIDTH)), _param_spec1((WIDTH, WIDTH)), _param_spec1((1, WIDTH))]
    assert len(ssd_args) == SSD_INPUTS and len(ret_args) == RET_INPUTS
    out_shape = jax.ShapeDtypeStruct((batch, seq, WIDTH), BF16)
    y_ssd, y_ret = pl.pallas_call(
        _ssd_retention_kernel,
        grid=(nc,),
        in_specs=ssd_specs + ret_specs,
        out_specs=[_seq_spec(batch, WIDTH), _seq_spec(batch, WIDTH)],
        out_shape=[out_shape, out_shape],
        scratch_shapes=[pltpu.VMEM((batch, CHUNK + HALO, WIDTH), F32),
                        pltpu.VMEM((batch, CHUNK + HALO, 2 * LANES), F32),
                        pltpu.VMEM((batch, 2, LANES, 2 * LANES), F32),
                        pltpu.VMEM((batch, CHUNK, WIDTH), F32),
                        pltpu.VMEM((batch, WIDTH, WIDTH), F32)],
        compiler_params=_cparams(("arbitrary",)),
        name="ssd_retention",
    )(*ssd_args, *ret_args)
    return y_ssd.reshape(batch * seq, WIDTH), y_ret.reshape(batch * seq, WIDTH)


def _rwkv_program(batches, keep, r_ref, rh_ref, k_ref, kh_ref, v_ref, vh_ref, lo_ref, loh_ref,
                  mur_ref, muk_ref, muv_ref, mulo_ref, w0_ref, wup_ref, a0_ref, aup_ref, gup_ref,
                  kkw_ref, kaw_ref, rkw_ref, lnw_ref, lnb_ref, bd_ref,
                  o_ref, s_ref, sr_ref, sk_ref, sv_ref, slo_ref):
    width = r_ref.shape[2]
    npair = width // LANES
    bd = bd_ref[...]
    row = _iota2((CHUNK, CHUNK), 0)
    col = _iota2((CHUNK, CHUNK), 1)
    strict = row > col
    causal = row >= col
    eye = (row == col).astype(F32)
    tri = causal.astype(BF16)
    same_head = (row // HEAD_DIM == col // HEAD_DIM).astype(F32)
    m1 = [_head_mask(j) for j in range(2)]
    m2 = [_head_mask(j, 2 * LANES) for j in range(2)]
    cat0 = lambda xs: jnp.concatenate(xs, axis=0)
    cat1 = lambda xs: jnp.concatenate(xs, axis=1)
    C = CHUNK

    def token_shift(b, main_ref, halo_ref, mu_ref, scr_ref):
        _with_prev_rows(scr_ref.at[b], main_ref.at[b], halo_ref.at[b], keep)
        cur = main_ref[b]
        prev = scr_ref[b, HALO - 1:HALO - 1 + CHUNK, :]
        return cur + (prev - cur) * mu_ref[...]

    pre = {}
    for b in batches:
        r = token_shift(b, r_ref, rh_ref, mur_ref, sr_ref)
        k = token_shift(b, k_ref, kh_ref, muk_ref, sk_ref)
        v = token_shift(b, v_ref, vh_ref, muv_ref, sv_ref)
        lo = token_shift(b, lo_ref, loh_ref, mulo_ref, slo_ref)
        lo_wa = lo[:, :LANES]
        log_w = -RWKV_DECAY_SCALE * _sigmoid(w0_ref[...] + _mm(jnp.tanh(lo_wa), wup_ref[...]))
        a = _sigmoid(a0_ref[...] + _mm(lo_wa, aup_ref[...]))
        gate = _mm(_sigmoid(lo[:, LANES:]), gup_ref[...])
        kk = k * kkw_ref[...]
        kk = kk / jnp.maximum(jnp.sqrt(_mm(kk * kk, bd)), 1e-12)
        k = k * (1.0 + (a - 1.0) * kaw_ref[...])
        cum = _mm_split(tri, log_w, 2)
        mid = cum[C // 2 - 1:C // 2, :]
        last = cum[C - 1:C, :]
        inv_p = jnp.exp(mid - cum)
        to_end = jnp.exp(last - mid)
        pre[b] = dict(
            r=r, k=k, v=v, gate=gate,
            a_t=-kk * jnp.exp(cum - log_w - mid), r_t=r * jnp.exp(cum - mid),
            b_t=kk * a * inv_p, k_t=k * inv_p, e_mid=jnp.exp(mid),
            b_end=kk * a * inv_p * to_end, k_end=k * inv_p * to_end, decay=jnp.exp(last))

    yield
    units = [(b, q) for b in batches for q in range(npair)]
    slab = lambda name: [pre[b][name][:, q * LANES:(q + 1) * LANES] for b, q in units]
    a_t, r_t, b_t, k_t, v_u = slab("a_t"), slab("r_t"), slab("b_t"), slab("k_t"), slab("v")
    e_mid, b_end, k_end, decay = slab("e_mid"), slab("b_end"), slab("k_end"), slab("decay")

    mm = [_mm_nt(cat0([a * m1[0], r * m1[0], a * m1[1], r * m1[1]]), cat0([bt, kt]))
          for a, r, bt, kt in zip(a_t, r_t, b_t, k_t)]
    chains = [(u, j) for u in range(len(units)) for j in range(2)]
    m_ab = [jnp.where(strict, mm[u][2 * j * C:(2 * j + 1) * C, :C], 0.0) for u, j in chains]
    m_ak = [jnp.where(strict, mm[u][2 * j * C:(2 * j + 1) * C, C:], 0.0).astype(BF16)
            for u, j in chains]
    m_rb = [jnp.where(causal, mm[u][(2 * j + 1) * C:(2 * j + 2) * C, :C], 0.0).astype(BF16)
            for u, j in chains]
    m_rk = [jnp.where(causal, mm[u][(2 * j + 1) * C:(2 * j + 2) * C, C:], 0.0).astype(BF16)
            for u, j in chains]
    yield

    inv = [eye + m for m in m_ab]
    power = [m.astype(BF16) for m in m_ab]
    for _ in range(int(math.log2(C)) - 1):
        power = [_mm(p, p).astype(BF16) for p in power]
        inv = [i + _mm(i, p) for i, p in zip(inv, power)]
        yield

    n_units = range(len(units))
    per_head = lambda x2: x2[:C] * m2[0] + x2[C:] * m2[1]
    mv = [_mm(cat0([m_ak[2 * u], m_rk[2 * u], m_ak[2 * u + 1], m_rk[2 * u + 1]]), v_u[u])
          for u in n_units]
    mv_ak = [mv[u][:C] * m1[0] + mv[u][2 * C:3 * C] * m1[1] for u in n_units]
    mv_rk = [mv[u][C:2 * C] * m1[0] + mv[u][3 * C:] * m1[1] for u in n_units]
    z = [per_head(_mm(cat0([inv[2 * u], inv[2 * u + 1]]), cat1([a_t[u], mv_ak[u]])))
         for u in n_units]
    yield
    w = [per_head(_mm(cat0([m_rb[2 * u], m_rb[2 * u + 1]]), z[u])) for u in n_units]
    yield

    s = [s_ref[b, q] for b, q in units]
    uy = [_mm_nt(cat0([z[u][:, :LANES] * e_mid[u], (r_t[u] + w[u][:, :LANES]) * e_mid[u]]), s[u])
          for u in n_units]
    u_in = [uy[u][:C] + z[u][:, LANES:] for u in n_units]
    y = [uy[u][C:] + w[u][:, LANES:] + mv_rk[u] for u in n_units]
    s_inc = [_mm_tn(cat0([u_in[u], v_u[u]]), cat0([b_end[u], k_end[u]])) for u in n_units]
    for u, (b, q) in enumerate(units):
        s_ref[b, q] = s[u] * decay[u] + same_head * s_inc[u]

    for i, b in enumerate(batches):
        yb = cat1([y[i * npair + q] for q in range(npair)])
        mean = _mm(yb, bd) * (1.0 / HEAD_DIM)
        d = yb - mean
        var = _mm(d * d, bd) * (1.0 / HEAD_DIM)
        yn = d * lax.rsqrt(var + RWKV_LN_EPS) * lnw_ref[...] + lnb_ref[...]
        bonus = _mm(pre[b]["r"] * pre[b]["k"] * rkw_ref[...], bd) * pre[b]["v"]
        o_ref[b] = ((yn + bonus) * pre[b]["gate"]).astype(BF16)


FOX_TILE = 2048
FOX_Q_SPLIT = 8
FOX_GUARD = 64.0
LOG2E = math.log2(math.e)


def _fox_prep_chunk(b, q_ref, k_ref, v_ref, f_ref, qw_ref, kw_ref, fb_ref, bd_ref, spread_ref,
                    qo_ref, ko_ref, vo_ref, carry_ref):
    bd = bd_ref[...]
    q, k, v = q_ref[b], k_ref[b], v_ref[b]
    q_ms = _mm(q * q, bd) * (1.0 / HEAD_DIM)
    k_ms = _mm(k * k, bd) * (1.0 / HEAD_DIM)

    f = f_ref[b] + fb_ref[...]
    log_f = jnp.minimum(f, 0.0) - jnp.log1p(jnp.exp(-jnp.abs(f)))
    rows = CHUNK
    tri = (_iota2((rows, rows), 0) >= _iota2((rows, rows), 1)).astype(BF16)
    cum = _mm_split(tri, log_f, 3) + carry_ref[b, 0:1, :]
    carry_ref[b] = jnp.broadcast_to(cum[rows - 1:rows, :], carry_ref.shape[1:])
    yield
    q = q * lax.rsqrt(q_ms + EPS) * qw_ref[...] * (HEAD_DIM ** -0.5 * LOG2E)
    k = k * lax.rsqrt(k_ms + EPS) * kw_ref[...]
    c_all = _mm_split_rhs(cum * LOG2E, spread_ref[...], 3)
    yield

    lane = _iota2((1, LANES), 1)
    for h in range(N_HEADS):
        p, j = divmod(h, 2)
        ps = slice(p * LANES, (p + 1) * LANES)
        mj = _head_mask(j)
        qh, kh = q[:, ps] * mj, k[:, ps] * mj
        if j == 1:
            qh = pltpu.roll(qh, HEAD_DIM, 1)
            kh = pltpu.roll(kh, HEAD_DIM, 1)
        ch = c_all[:, h * LANES:(h + 1) * LANES]
        hi = ch.astype(BF16).astype(F32)
        mid = (ch - hi).astype(BF16).astype(F32)
        low = ch - hi - mid
        piece = lambda base: jnp.where(lane == base, hi,
                                       jnp.where(lane == base + 1, mid,
                                                 jnp.where(lane == base + 2, low, 0.0)))
        ones = lambda base: ((lane >= base) & (lane < base + 3)).astype(F32)
        qo_ref[b, h] = (qh + piece(HEAD_DIM) + ones(HEAD_DIM + 3)).astype(BF16)
        ko_ref[b, h] = (kh + ones(HEAD_DIM) - piece(HEAD_DIM + 3)).astype(BF16)
    row_head = _iota2((LANES, 1), 0) // HEAD_DIM
    row_in_head = _iota2((LANES, 1), 0) % HEAD_DIM
    for p in range(N_HEADS // 2):
        v_t = v[:, p * LANES:(p + 1) * LANES].T
        for j in range(2):
            ones_row = ((row_head != j) & (row_in_head == 0)).astype(F32)
            vo_ref[b, 2 * p + j] = jnp.where(row_head == j, v_t, ones_row).astype(BF16)


RWKV_INPUTS = 23
FOX_PREP_INPUTS = 9


def _rwkv_fox_prep_kernel(*refs):
    rwkv_in = refs[:RWKV_INPUTS]
    prep_in = refs[RWKV_INPUTS:RWKV_INPUTS + FOX_PREP_INPUTS]
    (rwkv_out, qo_ref, ko_ref, vo_ref,
     s_ref, sr_ref, sk_ref, sv_ref, slo_ref, carry_ref) = refs[RWKV_INPUTS + FOX_PREP_INPUTS:]
    c = pl.program_id(0)

    @pl.when(c == 0)
    def _():
        s_ref[...] = jnp.zeros_like(s_ref)
        carry_ref[...] = jnp.zeros_like(carry_ref)

    keep = (c > 0).astype(F32)
    def after(stages, program):
        for _ in range(stages):
            yield
        yield from program

    batches = list(range(carry_ref.shape[0]))
    programs = [_rwkv_program(batches, keep, *rwkv_in, rwkv_out, s_ref, sr_ref, sk_ref, sv_ref,
                              slo_ref)]
    for b in batches:
        prep = _fox_prep_chunk(b, *prep_in, qo_ref, ko_ref, vo_ref, carry_ref)
        programs.append(after(2 + 2 * b, prep))
    _interleave(programs)


def _rwkv_fox_prep(r, k, v, lo, mu, w0, w_up, a0, a_up, g_up, k_k, k_a, r_k, ln_w, ln_b,
                   fq, fk, fv, ff, q_norm_w, k_norm_w, f_bias, batch, seq):
    nc = seq // CHUNK
    lo_width = lo.shape[-1]
    rank = w_up.shape[0]
    row1 = lambda a: a.reshape(1, -1)
    seq3 = lambda a: a.reshape(batch, seq, a.shape[-1])

    wup_pad = jnp.concatenate([w_up, jnp.zeros((LANES - rank, WIDTH), F32)], axis=0).astype(BF16)
    aup_pad = jnp.concatenate([jnp.zeros((LANES - rank, WIDTH), F32), a_up], axis=0).astype(BF16)
    head = jnp.arange(WIDTH) // HEAD_DIM
    bd = (head[:, None] == head[None, :]).astype(BF16)
    r, k, v, lo = seq3(r), seq3(k), seq3(v), seq3(lo)
    rwkv_args = (r, r, k, k, v, v, lo, lo,
                 row1(mu[:WIDTH]), row1(mu[WIDTH:2 * WIDTH]), row1(mu[2 * WIDTH:3 * WIDTH]),
                 row1(mu[3 * WIDTH:]),
                 row1(w0), wup_pad, row1(a0), aup_pad, g_up.astype(BF16),
                 row1(k_k), row1(k_a), row1(r_k), row1(ln_w), row1(ln_b), bd)
    wide, wide_halo = _seq_spec(batch, WIDTH), _seq_halo_spec(batch, WIDTH)
    param = _param_spec1((1, WIDTH))
    weight = _param_spec1((LANES, WIDTH))
    rwkv_specs = [wide, wide_halo, wide, wide_halo, wide, wide_halo,
                  _seq_spec(batch, lo_width), _seq_halo_spec(batch, lo_width),
                  param, param, param, _param_spec1((1, lo_width)),
                  param, weight, param, weight, weight,
                  param, param, param, param, param, _param_spec1((WIDTH, WIDTH))]

    dst = jnp.arange(N_HEADS * LANES)
    spread = ((dst[None, :] // LANES == jnp.arange(LANES)[:, None])
              & (dst[None, :] % LANES >= HEAD_DIM) & (dst[None, :] % LANES < HEAD_DIM + 6)).astype(BF16)
    prep_args = (seq3(fq), seq3(fk), seq3(fv), seq3(ff),
                 row1(jnp.tile(q_norm_w, N_HEADS)), row1(jnp.tile(k_norm_w, N_HEADS)),
                 row1(jnp.pad(f_bias, (0, LANES - N_HEADS))), bd, spread)
    prep_specs = [wide, wide, wide, _seq_spec(batch, LANES), param, param,
                  _param_spec1((1, LANES)), _param_spec1((WIDTH, WIDTH)),
                  _param_spec1((LANES, N_HEADS * LANES))]
    assert len(rwkv_args) == RWKV_INPUTS and len(prep_args) == FOX_PREP_INPUTS

    head_out = pl.BlockSpec((batch, N_HEADS, CHUNK, LANES), lambda c: (0, 0, c, 0))
    head_shape = jax.ShapeDtypeStruct((batch, N_HEADS, seq, LANES), BF16)
    vt_out = pl.BlockSpec((batch, N_HEADS, LANES, CHUNK), lambda c: (0, 0, 0, c))
    vt_shape = jax.ShapeDtypeStruct((batch, N_HEADS, LANES, seq), BF16)
    shifted = lambda width: pltpu.VMEM((batch, CHUNK + HALO, width), F32)
    y_rwkv, qa, ka, va = pl.pallas_call(
        _rwkv_fox_prep_kernel,
        grid=(nc,),
        in_specs=rwkv_specs + prep_specs,
        out_specs=[wide, head_out, head_out, vt_out],
        out_shape=[jax.ShapeDtypeStruct((batch, seq, WIDTH), BF16), head_shape, head_shape,
                   vt_shape],
        scratch_shapes=[pltpu.VMEM((batch, N_HEADS // 2, LANES, LANES), F32),
                        shifted(WIDTH), shifted(WIDTH), shifted(WIDTH), shifted(lo_width),
                        pltpu.VMEM((batch, HALO, LANES), F32)],
        compiler_params=_cparams(("arbitrary",)),
        name="rwkv7_fox_prep",
    )(*rwkv_args, *prep_args)
    return y_rwkv.reshape(batch * seq, WIDTH), qa, ka, va


def _fox_kernel(qi_ref, ki_ref, q_ref, k_ref, vt_ref, o_ref, m_ref, acc_ref, redo_ref):
    t = pl.program_id(2)
    qi = qi_ref[t]
    ki = ki_ref[t]

    part = FOX_TILE // FOX_Q_SPLIT
    units = [(j, slice(h * part, (h + 1) * part)) for j in range(2) for h in range(FOX_Q_SPLIT)]

    def all_scores(on_diagonal):
        n_keys = lambda qs: qs.stop if on_diagonal else FOX_TILE
        raw = [lax.dot_general(k_ref[0, j, :n_keys(qs), :], q_ref[0, j, qs, :],
                               (((1,), (1,)), ((), ())), preferred_element_type=F32)
               for j, qs in units]
        if not on_diagonal:
            return raw
        masked = []
        for (_, qs), s in zip(units, raw):
            key = _iota2(s.shape, 0)
            query = _iota2(s.shape, 1) + qs.start
            masked.append(jnp.where(query >= key, s, NEG))
        return masked

    def pv(j, p):
        return jnp.dot(vt_ref[0, j, :, :p.shape[0]], p.astype(BF16), preferred_element_type=F32)

    def exact_step(on_diagonal):
        for (j, qs), s in zip(units, all_scores(on_diagonal)):
            m_prev = m_ref[j, :, qs]
            m_next = jnp.maximum(m_prev, jnp.max(s, axis=0, keepdims=True))
            alpha = jnp.exp2(m_prev - m_next)
            acc_ref[j, :, qs] = alpha * acc_ref[j, :, qs] + pv(j, jnp.exp2(s - m_next))
            m_ref[j, :, qs] = m_next

    def lagged_step(on_diagonal):
        done = []
        worst = None
        for (j, qs), s in zip(units, all_scores(on_diagonal)):
            m_used = m_ref[j, :, qs]
            block_max = jnp.max(s, axis=0, keepdims=True)
            contrib = pv(j, jnp.exp2(s - m_used))
            excess = jnp.max(block_max - m_used)
            worst = excess if worst is None else jnp.maximum(worst, excess)
            done.append((j, qs, m_used, block_max, contrib))
        ok = worst <= FOX_GUARD
        redo_ref[0] = jnp.where(ok, 0, 1)

        @pl.when(ok)
        def _():
            for j, qs, m_used, block_max, contrib in done:
                m_next = jnp.maximum(m_used, block_max)
                acc_ref[j, :, qs] = (acc_ref[j, :, qs] + contrib) * jnp.exp2(m_used - m_next)
                m_ref[j, :, qs] = m_next

    @pl.when(ki == qi)
    def _():
        m_ref[...] = jnp.full_like(m_ref, NEG)
        acc_ref[...] = jnp.zeros_like(acc_ref)
        exact_step(True)

    redo_ref[0] = 0

    @pl.when(ki < qi)
    def _():
        lagged_step(False)

    @pl.when(redo_ref[0] == 1)
    def _():
        exact_step(False)

    @pl.when(ki == 0)
    def _():
        head_rows = _iota2((LANES, 1), 0) // HEAD_DIM
        out_t = jnp.where(head_rows == 0,
                          acc_ref[0] / acc_ref[0, HEAD_DIM:HEAD_DIM + 1, :],
                          acc_ref[1] / acc_ref[1, 0:1, :])
        o_ref[0] = out_t.T.astype(BF16)


def _fox_attention(qa, ka, va, batch, seq):
    nt = seq // FOX_TILE
    pairs = [(i, j) for i in range(nt) for j in range(i, -1, -1)]
    qi = jnp.asarray([i for i, _ in pairs], jnp.int32)
    ki = jnp.asarray([j for _, j in pairs], jnp.int32)
    q_spec = pl.BlockSpec((1, 2, FOX_TILE, LANES), lambda b, p, t, qi, ki: (b, p, qi[t], 0))
    k_spec = pl.BlockSpec((1, 2, FOX_TILE, LANES), lambda b, p, t, qi, ki: (b, p, ki[t], 0))
    vt_spec = pl.BlockSpec((1, 2, LANES, FOX_TILE), lambda b, p, t, qi, ki: (b, p, 0, ki[t]))
    out = pl.pallas_call(
        _fox_kernel,
        grid_spec=pltpu.PrefetchScalarGridSpec(
            num_scalar_prefetch=2,
            grid=(batch, N_HEADS // 2, len(pairs)),
            in_specs=[q_spec, k_spec, vt_spec],
            out_specs=pl.BlockSpec((1, FOX_TILE, LANES), lambda b, p, t, qi, ki: (b, qi[t], p)),
            scratch_shapes=[pltpu.VMEM((2, 1, FOX_TILE), F32),
                            pltpu.VMEM((2, LANES, FOX_TILE), F32),
                            pltpu.SMEM((1,), jnp.int32)]),
        out_shape=jax.ShapeDtypeStruct((batch, seq, WIDTH), BF16),
        compiler_params=_cparams(("parallel", "parallel", "arbitrary")),
        name="fox_attention",
    )(qi, ki, qa, ka, va)
    return out.reshape(batch * seq, WIDTH)


def _pad_cols(w, width):
    return jnp.pad(w, ((0, 0), (0, width - w.shape[1])))


def _even_mixers(x, batch, seq, norm_w, w_in, conv_w, conv_b, dt_bias, a_log, d_skip,
                 ssd_norm_w, ret_norm_w):
    ssd_x0 = WIDTH
    ssd_dt0 = WIDTH + (WIDTH + 4 * SSD_STATE)
    ret0 = ssd_dt0 + N_HEADS
    w = jnp.concatenate([w_in[:, :ssd_dt0], w_in[:, ret0:],
                         _pad_cols(w_in[:, ssd_dt0:ret0], LANES)], axis=1).astype(BF16)
    segments = ((0, WIDTH), (ssd_x0, WIDTH), (2 * WIDTH, 2 * LANES),
                (ssd_dt0, WIDTH), (ssd_dt0 + WIDTH, WIDTH), (ssd_dt0 + 2 * WIDTH, WIDTH),
                (ssd_dt0 + 3 * WIDTH, WIDTH), (ssd_dt0 + 4 * WIDTH, LANES))
    z, xs, bc, q, k, v, g, dt = _inproj(x, norm_w, w, segments)
    return _ssd_retention(z, xs, bc, dt, conv_w, conv_b, dt_bias, a_log, d_skip, ssd_norm_w,
                          q, k, v, g, ret_norm_w, batch, seq)


def _odd_mixers(x, batch, seq, norm_w, w_in, mu, w0, w_up, a0, a_up, g_up, k_k, k_a, r_k,
                ln_w, ln_b, q_norm_w, k_norm_w, f_bias):
    lo_width = w_up.shape[0] + a_up.shape[0] + g_up.shape[0]
    fox0 = 3 * WIDTH + lo_width
    f0 = fox0 + 3 * WIDTH
    w = jnp.concatenate([w_in[:, :f0], _pad_cols(w_in[:, f0:], LANES)], axis=1).astype(BF16)
    segments = ((0, WIDTH), (WIDTH, WIDTH), (2 * WIDTH, WIDTH), (3 * WIDTH, lo_width),
                (fox0, WIDTH), (fox0 + WIDTH, WIDTH), (fox0 + 2 * WIDTH, WIDTH), (f0, LANES))
    r, k, v, lo, fq, fk, fv, ff = _inproj(x, norm_w, w, segments)
    y_rwkv, qa, ka, va = _rwkv_fox_prep(r, k, v, lo, mu, w0, w_up, a0, a_up, g_up, k_k, k_a,
                                        r_k.reshape(-1), ln_w, ln_b, fq, fk, fv, ff,
                                        q_norm_w, k_norm_w, f_bias, batch, seq)
    return y_rwkv, _fox_attention(qa, ka, va, batch, seq)


def kernel(x, ev_norm_w, ev_w_in, ev_ssd_conv_w, ev_ssd_conv_b, ev_ssd_dt_bias, ev_ssd_a_log,
           ev_ssd_d, ev_ssd_norm_w, ev_ret_norm_w, ev_w_out,
           od_norm_w, od_w_in, od_rwkv_mu, od_rwkv_w0, od_rwkv_w_up, od_rwkv_a0, od_rwkv_a_up,
           od_rwkv_g_up, od_rwkv_k_k, od_rwkv_k_a, od_rwkv_r_k, od_rwkv_ln_w, od_rwkv_ln_b,
           od_fox_q_norm_w, od_fox_k_norm_w, od_fox_f_bias, od_w_out,
           ffn_norm_w, ffn_w_up, ffn_conv_w, ffn_conv_b, ffn_w_down):
    batch, seq, _ = x.shape
    depth = ffn_norm_w.shape[0]
    h = x.reshape(batch * seq, D_MODEL)
    w_up_all = ffn_w_up.astype(BF16)
    w_down_all = ffn_w_down.astype(BF16)
    for layer in range(depth):
        i = layer // 2
        if layer % 2 == 0:
            ya, yb = _even_mixers(h, batch, seq, ev_norm_w[i], ev_w_in[i], ev_ssd_conv_w[i],
                                  ev_ssd_conv_b[i], ev_ssd_dt_bias[i], ev_ssd_a_log[i],
                                  ev_ssd_d[i], ev_ssd_norm_w[i], ev_ret_norm_w[i])
            w_out = ev_w_out[i]
        else:
            ya, yb = _odd_mixers(h, batch, seq, od_norm_w[i], od_w_in[i], od_rwkv_mu[i],
                                 od_rwkv_w0[i], od_rwkv_w_up[i], od_rwkv_a0[i], od_rwkv_a_up[i],
                                 od_rwkv_g_up[i], od_rwkv_k_k[i], od_rwkv_k_a[i], od_rwkv_r_k[i],
                                 od_rwkv_ln_w[i], od_rwkv_ln_b[i], od_fox_q_norm_w[i],
                                 od_fox_k_norm_w[i], od_fox_f_bias[i])
            w_out = od_w_out[i]
        h = _proj_ffn(ya, yb, h, seq, w_out, ffn_norm_w[layer], layer, w_up_all,
                      ffn_conv_w[layer], ffn_conv_b[layer], w_down_all)
    return h.reshape(batch, seq, D_MODEL)
```

```python
import functools
import itertools
import math

import jax
import jax.numpy as jnp
from jax import lax
from jax.experimental import pallas as pl
from jax.experimental.pallas import tpu as pltpu

F32 = jnp.float32
BF16 = jnp.bfloat16

D_MODEL = 1024
HEAD_DIM = 64
N_HEADS = 8
WIDTH = N_HEADS * HEAD_DIM
LANES = 128
CHUNK = 128
HALO = 8
EPS = 1e-6
SSD_CONV = 4
SSD_STATE = 64
RWKV_LN_EPS = 64e-5
RWKV_DECAY_SCALE = 0.606531
D_FF = 2816
FF_TILE = 256
FFN_HALO = 16
FFN_LOOKAHEAD = 2
NEG = -1e30
VMEM_LIMIT = 56 * 1024 * 1024


def _mm(a, b):
    return jnp.dot(a.astype(BF16), b.astype(BF16), preferred_element_type=F32)


def _mm_nt(a, b):
    return lax.dot_general(a.astype(BF16), b.astype(BF16), (((1,), (1,)), ((), ())),
                           preferred_element_type=F32)


def _mm_tn(a, b):
    return lax.dot_general(a.astype(BF16), b.astype(BF16), (((0,), (0,)), ((), ())),
                           preferred_element_type=F32)


def _mm_split(lhs01, x, pieces):
    acc = None
    rest = x
    for _ in range(pieces):
        part = rest.astype(BF16)
        term = jnp.dot(lhs01, part, preferred_element_type=F32)
        acc = term if acc is None else acc + term
        rest = rest - part.astype(F32)
    return acc


def _mm_split_rhs(x, rhs01, pieces):
    acc = None
    rest = x
    for _ in range(pieces):
        part = rest.astype(BF16)
        term = jnp.dot(part, rhs01, preferred_element_type=F32)
        acc = term if acc is None else acc + term
        rest = rest - part.astype(F32)
    return acc


def _interleave(stage_generators):
    for _ in itertools.zip_longest(*stage_generators):
        pass


def _sigmoid(x):
    return 1.0 / (1.0 + jnp.exp(-x))


def _silu(x):
    return x * _sigmoid(x)


def _softplus(x):
    return jnp.maximum(x, 0.0) + jnp.log1p(jnp.exp(-jnp.abs(x)))


def _iota2(shape, dim):
    return lax.broadcasted_iota(jnp.int32, shape, dim)


def _head_mask(j, width=LANES):
    lane = _iota2((1, width), 1)
    return ((lane % LANES) // HEAD_DIM == j).astype(F32)


def _cparams(sem):
    return pltpu.CompilerParams(dimension_semantics=sem, vmem_limit_bytes=VMEM_LIMIT)


def _const_spec(shape):
    nd = len(shape)
    return pl.BlockSpec(shape, lambda *_: (0,) * nd, pipeline_mode=pl.Buffered(1))


def _inproj_kernel(x_ref, nw_ref, w_ref, *o_refs, segments):
    x = x_ref[...]
    ms = jnp.mean(x * x, axis=-1, keepdims=True)
    xn = (x * lax.rsqrt(ms + EPS) * nw_ref[...]).astype(BF16)
    for o_ref, (off, width) in zip(o_refs, segments):
        o_ref[...] = jnp.dot(xn, w_ref[:, off:off + width], preferred_element_type=F32)


def _inproj(x, norm_w, w, segments, tm=1024):
    t = x.shape[0]
    n = w.shape[1]
    return pl.pallas_call(
        functools.partial(_inproj_kernel, segments=segments),
        grid=(t // tm,),
        in_specs=[pl.BlockSpec((tm, D_MODEL), lambda i: (i, 0)),
                  _const_spec((1, D_MODEL)),
                  _const_spec((D_MODEL, n))],
        out_specs=[pl.BlockSpec((tm, wd), lambda i: (i, 0)) for _, wd in segments],
        out_shape=[jax.ShapeDtypeStruct((t, wd), F32) for _, wd in segments],
        compiler_params=_cparams(("parallel",)),
        name="inproj",
    )(x, norm_w.reshape(1, D_MODEL), w)


def _proj_ffn_kernel(ya_ref, yah_ref, yb_ref, ybh_ref, x_ref, xh_ref, wo_ref, nw_ref,
                     wup_ref, cw_ref, cb_ref, wd_ref, o_ref,
                     ya_ext, yb_ext, h_ref, hn_ref, g_ref, acc_ref, *, tm, tiles_per_seq):
    ya_ext[:FFN_HALO, :] = yah_ref[...]
    ya_ext[FFN_HALO:, :] = ya_ref[...]
    yb_ext[:FFN_HALO, :] = ybh_ref[...]
    yb_ext[FFN_HALO:, :] = yb_ref[...]
    h_ref[:FFN_HALO, :] = xh_ref[...]
    h_ref[FFN_HALO:, :] = x_ref[...]
    h_ref[...] += (jnp.dot(ya_ext[...], wo_ref[:WIDTH, :], preferred_element_type=F32)
                   + jnp.dot(yb_ext[...], wo_ref[WIDTH:, :], preferred_element_type=F32))

    h = h_ref[...]
    ms = jnp.mean(h * h, axis=-1, keepdims=True)
    hn = h * lax.rsqrt(ms + EPS) * nw_ref[...]
    first_of_seq = pl.program_id(0) % tiles_per_seq == 0
    halo_rows = _iota2((tm + FFN_HALO, 1), 0) < FFN_HALO
    hn_ref[...] = jnp.where(halo_rows & first_of_seq, 0.0, hn).astype(BF16)
    acc_ref[...] = jnp.zeros_like(acc_ref)

    def tile(f, base=0):
        return slice(base + f * FF_TILE, base + (f + 1) * FF_TILE)

    n_buf = g_ref.shape[0]

    def up_proj(f):
        g_ref[f % n_buf] = jnp.dot(hn_ref[...], wup_ref[:, tile(f)], preferred_element_type=F32)
        return jnp.dot(hn_ref[FFN_HALO:, :], wup_ref[:, tile(f, D_FF)],
                       preferred_element_type=F32)

    nf = D_FF // FF_TILE
    ups = {f: up_proj(f) for f in range(FFN_LOOKAHEAD)}
    for f in range(nf):
        if f + FFN_LOOKAHEAD < nf:
            ups[f + FFN_LOOKAHEAD] = up_proj(f + FFN_LOOKAHEAD)
        up = ups.pop(f)
        g = g_ref.at[f % n_buf]
        cw = cw_ref[:, tile(f)]
        gate = (cb_ref[:, tile(f)]
                + cw[0:1, :] * g[FFN_HALO - 2:FFN_HALO - 2 + tm, :]
                + cw[1:2, :] * g[FFN_HALO - 1:FFN_HALO - 1 + tm, :]
                + cw[2:3, :] * g[FFN_HALO:, :])
        act = (_silu(gate) * up).astype(BF16)
        acc_ref[...] += jnp.dot(act, wd_ref[tile(f), :], preferred_element_type=F32)
    o_ref[...] = h_ref[FFN_HALO:, :] + acc_ref[...]


def _proj_ffn(ya, yb, x, seq, w_out, norm_w, layer, w_up_all, conv_w, conv_b, w_down_all, tm=512):
    t = x.shape[0]
    halo_per_tile = tm // FFN_HALO
    row_spec = lambda width: pl.BlockSpec((tm, width), lambda i: (i, 0))
    halo_spec = lambda width: pl.BlockSpec(
        (FFN_HALO, width), lambda i: (jnp.maximum(i * halo_per_tile - 1, 0), 0))
    ext = lambda width, dtype: pltpu.VMEM((tm + FFN_HALO, width), dtype)
    layer_spec = lambda shape: pl.BlockSpec((None,) + shape, lambda i: (layer, 0, 0),
                                            pipeline_mode=pl.Buffered(1))
    return pl.pallas_call(
        functools.partial(_proj_ffn_kernel, tm=tm, tiles_per_seq=seq // tm),
        grid=(t // tm,),
        in_specs=[row_spec(WIDTH), halo_spec(WIDTH), row_spec(WIDTH), halo_spec(WIDTH),
                  row_spec(D_MODEL), halo_spec(D_MODEL),
                  _const_spec((2 * WIDTH, D_MODEL)), _const_spec((1, D_MODEL)),
                  layer_spec((D_MODEL, 2 * D_FF)), _const_spec((3, D_FF)), _const_spec((1, D_FF)),
                  layer_spec((D_FF, D_MODEL))],
        out_specs=row_spec(D_MODEL),
        out_shape=jax.ShapeDtypeStruct((t, D_MODEL), F32),
        scratch_shapes=[ext(WIDTH, BF16), ext(WIDTH, BF16), ext(D_MODEL, F32), ext(D_MODEL, BF16),
                        pltpu.VMEM((FFN_LOOKAHEAD + 1, tm + FFN_HALO, FF_TILE), F32),
                        pltpu.VMEM((tm, D_MODEL), F32)],
        compiler_params=_cparams(("parallel",)),
        name="proj_convffn",
    )(ya, ya, yb, yb, x, x, w_out.astype(BF16), norm_w.reshape(1, D_MODEL),
      w_up_all, conv_w, conv_b.reshape(1, D_FF), w_down_all)


def _seq_spec(batch, width):
    return pl.BlockSpec((batch, CHUNK, width), lambda c: (0, c, 0))


def _seq_halo_spec(batch, width):
    per = CHUNK // HALO
    return pl.BlockSpec((batch, HALO, width), lambda c: (0, jnp.maximum(c * per - 1, 0), 0))


def _param_spec1(shape):
    return pl.BlockSpec(shape, lambda c: (0,) * len(shape), pipeline_mode=pl.Buffered(1))


def _with_prev_rows(scr_ref, main_ref, halo_ref, keep):
    scr_ref[:HALO, :] = halo_ref[...] * keep
    scr_ref[HALO:, :] = main_ref[...]


def _ssd_chunk(b, keep, z_ref, x_ref, xh_ref, bc_ref, bch_ref, dt_ref,
               cwx_ref, cbx_ref, cwb_ref, cbb_ref, dtb_ref, alog_ref, dexp_ref, nw_ref, e_ref,
               o_ref, xe_ref, bce_ref, st_ref, y_ref):
    _with_prev_rows(xe_ref.at[b], x_ref.at[b], xh_ref.at[b], keep)
    _with_prev_rows(bce_ref.at[b], bc_ref.at[b], bch_ref.at[b], keep)

    def conv(ref, cw_ref, cb_ref):
        out = cb_ref[...]
        for k in range(SSD_CONV):
            lo = HALO - (SSD_CONV - 1) + k
            out = out + cw_ref[k:k + 1, :] * ref[b, lo:lo + CHUNK, :]
        return _silu(out)

    x = conv(xe_ref, cwx_ref, cbx_ref)
    bc = conv(bce_ref, cwb_ref, cbb_ref)
    bm = bc[:, :LANES]
    cm = bc[:, LANES:]

    row = _iota2((CHUNK, CHUNK), 0)
    col = _iota2((CHUNK, CHUNK), 1)
    causal = row >= col

    dt = _softplus(dt_ref[b] + dtb_ref[...])
    a = dt * (-jnp.exp(alog_ref[...]))
    a_cum = _mm_split(causal.astype(BF16), a, 3)
    yield
    a_cum_t = a_cum.T
    acausal_penalty = jnp.where(causal, 0.0, NEG)
    a_last = a_cum[CHUNK - 1:CHUNK, :]
    per_head = jnp.concatenate(
        [dt, jnp.exp(a_cum), jnp.exp(a_last - a_cum),
         jnp.broadcast_to(jnp.exp(a_last), (HALO, LANES))], axis=0)
    per_lane = _mm_split_rhs(per_head, e_ref[...], 2)
    dt_e = per_lane[:CHUNK]
    ea_e = per_lane[CHUNK:2 * CHUNK]
    te_e = per_lane[2 * CHUNK:3 * CHUNK]
    cd_e = per_lane[3 * CHUNK:3 * CHUNK + 1]
    yield

    xdt = x * dt_e
    xs = xdt * te_e
    bm_t = bm.T
    group_rows = _iota2((LANES, 1), 0) // SSD_STATE

    for g in range(2):
        cg = cm * _head_mask(g)
        cb = _mm_nt(cg, bm)
        gs = slice(g * 2 * LANES, (g + 1) * 2 * LANES)
        st = st_ref[b, g]
        y_off = _mm(cg, st) * ea_e[:, gs]
        st_new = _mm(bm_t, xs[:, gs])
        st_ref[b, g] = st * cd_e[:, gs] + jnp.where(group_rows == g, st_new, 0.0)
        yield
        for pp in range(2):
            p = 2 * g + pp
            xp = xdt[:, p * LANES:(p + 1) * LANES]
            yp = y_off[:, pp * LANES:(pp + 1) * LANES]
            for j in range(2):
                h = 2 * p + j
                seg = a_cum[:, h:h + 1] - a_cum_t[h:h + 1, :]
                decay = jnp.exp(seg + acausal_penalty)
                yp = yp + _mm(cb * decay, xp * _head_mask(j))
            y_ref[b, :, p * LANES:(p + 1) * LANES] = yp
            yield

    y = y_ref[b] + dexp_ref[...] * x
    gated = y * _silu(z_ref[b])
    ms = jnp.mean(gated * gated, axis=-1, keepdims=True)
    o_ref[b] = (gated * lax.rsqrt(ms + EPS) * nw_ref[...]).astype(BF16)


def _ret_chunk(b, q_ref, k_ref, v_ref, g_ref, cos_ref, sin_ref, intra_ref, qdec_ref, kte_ref,
               cg_ref, bd_ref, nw_ref, o_ref, r_ref):
    lane = _iota2((1, WIDTH), 1)
    first_half = (lane % HEAD_DIM) < (HEAD_DIM // 2)
    cos = jnp.concatenate([cos_ref[...]] * (WIDTH // LANES), axis=1)
    sin = jnp.concatenate([sin_ref[...]] * (WIDTH // LANES), axis=1)

    def rotary(v):
        other = jnp.where(first_half,
                          pltpu.roll(v, WIDTH - HEAD_DIM // 2, 1),
                          pltpu.roll(v, HEAD_DIM // 2, 1))
        return v * cos + other * sin

    q = rotary(q_ref[b])
    k = rotary(k_ref[b]) * (HEAD_DIM ** -0.5)
    v = v_ref[b]
    bd = bd_ref[...]
    pair = lambda t, p: t[:, p * LANES:(p + 1) * LANES]
    heads = [(p, j) for p in range(N_HEADS // 2) for j in range(2)]
    masks = [_head_mask(j) for j in range(2)]

    y_cross = _mm(q * qdec_ref[...], r_ref[b])
    r_ref[b] = r_ref[b] * cg_ref[...] + bd * _mm_tn(k * kte_ref[...], v)
    scores = [_mm_nt(pair(q, p) * masks[j], pair(k, p)) for p, j in heads]
    yield
    y_pairs = []
    for p in range(N_HEADS // 2):
        yp = pair(y_cross, p)
        for j in range(2):
            h = 2 * p + j
            yp = yp + _mm(scores[h] * intra_ref[h], pair(v, p) * masks[j])
        y_pairs.append(yp)
    yield
    y = jnp.concatenate(y_pairs, axis=1)
    ms = _mm(y * y, bd) * (1.0 / HEAD_DIM)
    yield
    o_ref[b] = (_silu(g_ref[b]) * (y * lax.rsqrt(ms + EPS) * nw_ref[...])).astype(BF16)


def _retention_tables(seq):
    half = HEAD_DIM // 2
    inv = 1.0 / (10000.0 ** (jnp.arange(half, dtype=F32) / half))
    ang = jnp.arange(seq, dtype=F32)[:, None] * inv[None, :]
    cos = jnp.tile(jnp.concatenate([jnp.cos(ang), jnp.cos(ang)], axis=-1), (1, LANES // HEAD_DIM))
    sin = jnp.tile(jnp.concatenate([-jnp.sin(ang), jnp.sin(ang)], axis=-1), (1, LANES // HEAD_DIM))
    log_gamma = jnp.log1p(-(2.0 ** (-5.0 - jnp.arange(N_HEADS, dtype=F32))))
    idx = jnp.arange(CHUNK, dtype=F32)
    rel = idx[:, None] - idx[None, :]
    intra = jnp.where(rel[None] >= 0,
                      jnp.exp(jnp.maximum(rel, 0.0)[None] * log_gamma[:, None, None]), 0.0)
    expand = lambda m: jnp.repeat(m, HEAD_DIM, axis=1)
    qdec = expand(jnp.exp((idx + 1.0)[:, None] * log_gamma[None, :]))
    kte = expand(jnp.exp((CHUNK - 1 - idx)[:, None] * log_gamma[None, :]))
    cgam = expand(jnp.exp(CHUNK * log_gamma)[None, :])
    head = jnp.arange(WIDTH) // HEAD_DIM
    bd = (head[:, None] == head[None, :]).astype(BF16)
    return cos, sin, intra, qdec, kte, cgam, bd


EVEN_SEGMENTS = {"z": (0, WIDTH), "x": (WIDTH, WIDTH), "bc": (2 * WIDTH, 2 * LANES),
                 "q": (2 * WIDTH + 2 * LANES, WIDTH), "k": (3 * WIDTH + 2 * LANES, WIDTH),
                 "v": (4 * WIDTH + 2 * LANES, WIDTH), "g": (5 * WIDTH + 2 * LANES, WIDTH),
                 "dt": (6 * WIDTH + 2 * LANES, LANES)}
EVEN_COLS = 6 * WIDTH + 3 * LANES
SSD_PARAMS = 9
RET_PARAMS = 8


def _even_layer_kernel(x_ref, nw_ref, w_ref, *rest):
    ssd_par = rest[:SSD_PARAMS]
    ret_par = rest[SSD_PARAMS:SSD_PARAMS + RET_PARAMS]
    (ssd_out, ret_out, buf_a, buf_b, halo_x, halo_bc,
     xe_ref, bce_ref, st_ref, y_ref, r_ref) = rest[SSD_PARAMS + RET_PARAMS:]
    c = pl.program_id(0)
    nb = x_ref.shape[0]

    @pl.when(c == 0)
    def _():
        buf_a[...] = jnp.zeros_like(buf_a)
        buf_b[...] = jnp.zeros_like(buf_b)
        halo_x[...] = jnp.zeros_like(halo_x)
        halo_bc[...] = jnp.zeros_like(halo_bc)

    @pl.when(c <= 1)
    def _():
        st_ref[...] = jnp.zeros_like(st_ref)
        r_ref[...] = jnp.zeros_like(r_ref)

    keep = (c > 1).astype(F32)

    def step(read_buf, write_buf):
        def project():
            x = x_ref[...].reshape(nb * CHUNK, D_MODEL)
            ms = jnp.mean(x * x, axis=-1, keepdims=True)
            xn = (x * lax.rsqrt(ms + EPS) * nw_ref[...]).astype(BF16)
            for off, width in EVEN_SEGMENTS.values():
                res = jnp.dot(xn, w_ref[:, off:off + width], preferred_element_type=F32)
                for b in range(nb):
                    write_buf[b, :, off:off + width] = res[b * CHUNK:(b + 1) * CHUNK]
                yield

        def view(name):
            off, width = EVEN_SEGMENTS[name]
            return read_buf.at[:, :, off:off + width]

        programs = [project()]
        for b in range(nb):
            programs.append(_ssd_chunk(b, keep, view("z"), view("x"), halo_x, view("bc"), halo_bc,
                                       view("dt"), *ssd_par, ssd_out, xe_ref, bce_ref, st_ref,
                                       y_ref))
            programs.append(_ret_chunk(b, view("q"), view("k"), view("v"), view("g"), *ret_par,
                                       ret_out, r_ref))
        _interleave(programs)
        halo_x[...] = view("x")[:, CHUNK - HALO:, :]
        halo_bc[...] = view("bc")[:, CHUNK - HALO:, :]

    @pl.when(c % 2 == 0)
    def _():
        step(buf_b, buf_a)

    @pl.when(c % 2 == 1)
    def _():
        step(buf_a, buf_b)


def _even_layer_mixers(x, norm_w, w, conv_w, conv_b, dt_bias, a_log, d_skip, ssd_norm_w,
                       ret_norm_w, batch, seq):
    nc = seq // CHUNK
    pad = LANES - N_HEADS
    e = jnp.repeat(jnp.eye(LANES, N_HEADS, dtype=BF16), HEAD_DIM, axis=1)
    ssd_args = (conv_w[:, :WIDTH], conv_b[:WIDTH].reshape(1, WIDTH),
                conv_w[:, WIDTH:], conv_b[WIDTH:].reshape(1, 2 * LANES),
                jnp.pad(dt_bias, (0, pad)).reshape(1, LANES),
                jnp.pad(a_log, (0, pad)).reshape(1, LANES),
                jnp.repeat(d_skip, HEAD_DIM).reshape(1, WIDTH),
                ssd_norm_w.reshape(1, WIDTH), e)
    ssd_specs = [_param_spec1((SSD_CONV, WIDTH)), _param_spec1((1, WIDTH)),
                 _param_spec1((SSD_CONV, 2 * LANES)), _param_spec1((1, 2 * LANES)),
                 _param_spec1((1, LANES)), _param_spec1((1, LANES)),
                 _param_spec1((1, WIDTH)), _param_spec1((1, WIDTH)),
                 _param_spec1((LANES, WIDTH))]
    cos, sin, intra, qdec, kte, cgam, bd = _retention_tables(seq)
    mixed_chunk = lambda c: jnp.maximum(c - 1, 0)
    table_spec = pl.BlockSpec((CHUNK, LANES), lambda c: (mixed_chunk(c), 0))
    ret_args = (cos, sin, intra, qdec, kte, cgam, bd, ret_norm_w.reshape(1, WIDTH))
    ret_specs = [table_spec, table_spec, _param_spec1((N_HEADS, CHUNK, CHUNK)),
                 _param_spec1((CHUNK, WIDTH)), _param_spec1((CHUNK, WIDTH)),
                 _param_spec1((1, WIDTH)), _param_spec1((WIDTH, WIDTH)), _param_spec1((1, WIDTH))]
    assert len(ssd_args) == SSD_PARAMS and len(ret_args) == RET_PARAMS
    out_spec = pl.BlockSpec((batch, CHUNK, WIDTH), lambda c: (0, mixed_chunk(c), 0))
    out_shape = jax.ShapeDtypeStruct((batch, seq, WIDTH), BF16)
    proj_buf = pltpu.VMEM((batch, CHUNK, EVEN_COLS), F32)
    y_ssd, y_ret = pl.pallas_call(
        _even_layer_kernel,
        grid=(nc + 1,),
        in_specs=[pl.BlockSpec((batch, CHUNK, D_MODEL), lambda c: (0, jnp.minimum(c, nc - 1), 0)),
                  _param_spec1((1, D_MODEL)), _param_spec1((D_MODEL, EVEN_COLS))]
                 + ssd_specs + ret_specs,
        out_specs=[out_spec, out_spec],
        out_shape=[out_shape, out_shape],
        scratch_shapes=[proj_buf, proj_buf,
                        pltpu.VMEM((batch, HALO, WIDTH), F32),
                        pltpu.VMEM((batch, HALO, 2 * LANES), F32),
                        pltpu.VMEM((batch, CHUNK + HALO, WIDTH), F32),
                        pltpu.VMEM((batch, CHUNK + HALO, 2 * LANES), F32),
                        pltpu.VMEM((batch, 2, LANES, 2 * LANES), F32),
                        pltpu.VMEM((batch, CHUNK, WIDTH), F32),
                        pltpu.VMEM((batch, WIDTH, WIDTH), F32)],
        compiler_params=_cparams(("arbitrary",)),
        name="even_layer_mixers",
    )(x.reshape(batch, seq, D_MODEL), norm_w.reshape(1, D_MODEL), w, *ssd_args, *ret_args)
    return y_ssd.reshape(batch * seq, WIDTH), y_ret.reshape(batch * seq, WIDTH)


def _rwkv_program(batches, keep, r_ref, rh_ref, k_ref, kh_ref, v_ref, vh_ref, lo_ref, loh_ref,
                  mur_ref, muk_ref, muv_ref, mulo_ref, w0_ref, wup_ref, a0_ref, aup_ref, gup_ref,
                  kkw_ref, kaw_ref, rkw_ref, lnw_ref, lnb_ref, bd_ref,
                  o_ref, s_ref, sr_ref, sk_ref, sv_ref, slo_ref):
    width = r_ref.shape[2]
    npair = width // LANES
    bd = bd_ref[...]
    row = _iota2((CHUNK, CHUNK), 0)
    col = _iota2((CHUNK, CHUNK), 1)
    strict = row > col
    causal = row >= col
    eye = (row == col).astype(F32)
    tri = causal.astype(BF16)
    same_head = (row // HEAD_DIM == col // HEAD_DIM).astype(F32)
    m1 = [_head_mask(j) for j in range(2)]
    m2 = [_head_mask(j, 2 * LANES) for j in range(2)]
    cat0 = lambda xs: jnp.concatenate(xs, axis=0)
    cat1 = lambda xs: jnp.concatenate(xs, axis=1)
    C = CHUNK

    def token_shift(b, main_ref, halo_ref, mu_ref, scr_ref):
        _with_prev_rows(scr_ref.at[b], main_ref.at[b], halo_ref.at[b], keep)
        cur = main_ref[b]
        prev = scr_ref[b, HALO - 1:HALO - 1 + CHUNK, :]
        return cur + (prev - cur) * mu_ref[...]

    pre = {}
    for b in batches:
        r = token_shift(b, r_ref, rh_ref, mur_ref, sr_ref)
        k = token_shift(b, k_ref, kh_ref, muk_ref, sk_ref)
        v = token_shift(b, v_ref, vh_ref, muv_ref, sv_ref)
        lo = token_shift(b, lo_ref, loh_ref, mulo_ref, slo_ref)
        lo_wa = lo[:, :LANES]
        log_w = -RWKV_DECAY_SCALE * _sigmoid(w0_ref[...] + _mm(jnp.tanh(lo_wa), wup_ref[...]))
        a = _sigmoid(a0_ref[...] + _mm(lo_wa, aup_ref[...]))
        gate = _mm(_sigmoid(lo[:, LANES:]), gup_ref[...])
        kk = k * kkw_ref[...]
        kk = kk / jnp.maximum(jnp.sqrt(_mm(kk * kk, bd)), 1e-12)
        k = k * (1.0 + (a - 1.0) * kaw_ref[...])
        cum = _mm_split(tri, log_w, 2)
        mid = cum[C // 2 - 1:C // 2, :]
        last = cum[C - 1:C, :]
        inv_p = jnp.exp(mid - cum)
        to_end = jnp.exp(last - mid)
        pre[b] = dict(
            r=r, k=k, v=v, gate=gate,
            a_t=-kk * jnp.exp(cum - log_w - mid), r_t=r * jnp.exp(cum - mid),
            b_t=kk * a * inv_p, k_t=k * inv_p, e_mid=jnp.exp(mid),
            b_end=kk * a * inv_p * to_end, k_end=k * inv_p * to_end, decay=jnp.exp(last))

    yield
    units = [(b, q) for b in batches for q in range(npair)]
    slab = lambda name: [pre[b][name][:, q * LANES:(q + 1) * LANES] for b, q in units]
    a_t, r_t, b_t, k_t, v_u = slab("a_t"), slab("r_t"), slab("b_t"), slab("k_t"), slab("v")
    e_mid, b_end, k_end, decay = slab("e_mid"), slab("b_end"), slab("k_end"), slab("decay")

    mm = [_mm_nt(cat0([a * m1[0], r * m1[0], a * m1[1], r * m1[1]]), cat0([bt, kt]))
          for a, r, bt, kt in zip(a_t, r_t, b_t, k_t)]
    chains = [(u, j) for u in range(len(units)) for j in range(2)]
    m_ab = [jnp.where(strict, mm[u][2 * j * C:(2 * j + 1) * C, :C], 0.0) for u, j in chains]
    m_ak = [jnp.where(strict, mm[u][2 * j * C:(2 * j + 1) * C, C:], 0.0).astype(BF16)
            for u, j in chains]
    m_rb = [jnp.where(causal, mm[u][(2 * j + 1) * C:(2 * j + 2) * C, :C], 0.0).astype(BF16)
            for u, j in chains]
    m_rk = [jnp.where(causal, mm[u][(2 * j + 1) * C:(2 * j + 2) * C, C:], 0.0).astype(BF16)
            for u, j in chains]
    yield

    inv = [eye + m for m in m_ab]
    power = [m.astype(BF16) for m in m_ab]
    for _ in range(int(math.log2(C)) - 1):
        power = [_mm(p, p).astype(BF16) for p in power]
        inv = [i + _mm(i, p) for i, p in zip(inv, power)]
        yield

    n_units = range(len(units))
    per_head = lambda x2: x2[:C] * m2[0] + x2[C:] * m2[1]
    mv = [_mm(cat0([m_ak[2 * u], m_rk[2 * u], m_ak[2 * u + 1], m_rk[2 * u + 1]]), v_u[u])
          for u in n_units]
    mv_ak = [mv[u][:C] * m1[0] + mv[u][2 * C:3 * C] * m1[1] for u in n_units]
    mv_rk = [mv[u][C:2 * C] * m1[0] + mv[u][3 * C:] * m1[1] for u in n_units]
    z = [per_head(_mm(cat0([inv[2 * u], inv[2 * u + 1]]), cat1([a_t[u], mv_ak[u]])))
         for u in n_units]
    yield
    w = [per_head(_mm(cat0([m_rb[2 * u], m_rb[2 * u + 1]]), z[u])) for u in n_units]
    yield

    s = [s_ref[b, q] for b, q in units]
    uy = [_mm_nt(cat0([z[u][:, :LANES] * e_mid[u], (r_t[u] + w[u][:, :LANES]) * e_mid[u]]), s[u])
          for u in n_units]
    u_in = [uy[u][:C] + z[u][:, LANES:] for u in n_units]
    y = [uy[u][C:] + w[u][:, LANES:] + mv_rk[u] for u in n_units]
    s_inc = [_mm_tn(cat0([u_in[u], v_u[u]]), cat0([b_end[u], k_end[u]])) for u in n_units]
    for u, (b, q) in enumerate(units):
        s_ref[b, q] = s[u] * decay[u] + same_head * s_inc[u]

    for i, b in enumerate(batches):
        yb = cat1([y[i * npair + q] for q in range(npair)])
        mean = _mm(yb, bd) * (1.0 / HEAD_DIM)
        d = yb - mean
        var = _mm(d * d, bd) * (1.0 / HEAD_DIM)
        yn = d * lax.rsqrt(var + RWKV_LN_EPS) * lnw_ref[...] + lnb_ref[...]
        bonus = _mm(pre[b]["r"] * pre[b]["k"] * rkw_ref[...], bd) * pre[b]["v"]
        o_ref[b] = ((yn + bonus) * pre[b]["gate"]).astype(BF16)


FOX_TILE = 2048
FOX_Q_SPLIT = 8
FOX_GUARD = 64.0
LOG2E = math.log2(math.e)


def _fox_prep_chunk(b, q_ref, k_ref, v_ref, f_ref, qw_ref, kw_ref, fb_ref, bd_ref, spread_ref,
                    qo_ref, ko_ref, vo_ref, carry_ref):
    bd = bd_ref[...]
    q, k, v = q_ref[b], k_ref[b], v_ref[b]
    q_ms = _mm(q * q, bd) * (1.0 / HEAD_DIM)
    k_ms = _mm(k * k, bd) * (1.0 / HEAD_DIM)

    f = f_ref[b] + fb_ref[...]
    log_f = jnp.minimum(f, 0.0) - jnp.log1p(jnp.exp(-jnp.abs(f)))
    rows = CHUNK
    tri = (_iota2((rows, rows), 0) >= _iota2((rows, rows), 1)).astype(BF16)
    cum = _mm_split(tri, log_f, 3) + carry_ref[b, 0:1, :]
    carry_ref[b] = jnp.broadcast_to(cum[rows - 1:rows, :], carry_ref.shape[1:])
    yield
    q = q * lax.rsqrt(q_ms + EPS) * qw_ref[...] * (HEAD_DIM ** -0.5 * LOG2E)
    k = k * lax.rsqrt(k_ms + EPS) * kw_ref[...]
    c_all = _mm_split_rhs(cum * LOG2E, spread_ref[...], 3)
    yield

    lane = _iota2((1, LANES), 1)
    for h in range(N_HEADS):
        p, j = divmod(h, 2)
        ps = slice(p * LANES, (p + 1) * LANES)
        mj = _head_mask(j)
        qh, kh = q[:, ps] * mj, k[:, ps] * mj
        if j == 1:
            qh = pltpu.roll(qh, HEAD_DIM, 1)
            kh = pltpu.roll(kh, HEAD_DIM, 1)
        ch = c_all[:, h * LANES:(h + 1) * LANES]
        hi = ch.astype(BF16).astype(F32)
        mid = (ch - hi).astype(BF16).astype(F32)
        low = ch - hi - mid
        piece = lambda base: jnp.where(lane == base, hi,
                                       jnp.where(lane == base + 1, mid,
                                                 jnp.where(lane == base + 2, low, 0.0)))
        ones = lambda base: ((lane >= base) & (lane < base + 3)).astype(F32)
        qo_ref[b, h] = (qh + piece(HEAD_DIM) + ones(HEAD_DIM + 3)).astype(BF16)
        ko_ref[b, h] = (kh + ones(HEAD_DIM) - piece(HEAD_DIM + 3)).astype(BF16)
    row_head = _iota2((LANES, 1), 0) // HEAD_DIM
    row_in_head = _iota2((LANES, 1), 0) % HEAD_DIM
    for p in range(N_HEADS // 2):
        v_t = v[:, p * LANES:(p + 1) * LANES].T
        for j in range(2):
            ones_row = ((row_head != j) & (row_in_head == 0)).astype(F32)
            vo_ref[b, 2 * p + j] = jnp.where(row_head == j, v_t, ones_row).astype(BF16)


RWKV_INPUTS = 23
FOX_PREP_INPUTS = 9


def _rwkv_fox_prep_kernel(*refs):
    rwkv_in = refs[:RWKV_INPUTS]
    prep_in = refs[RWKV_INPUTS:RWKV_INPUTS + FOX_PREP_INPUTS]
    (rwkv_out, qo_ref, ko_ref, vo_ref,
     s_ref, sr_ref, sk_ref, sv_ref, slo_ref, carry_ref) = refs[RWKV_INPUTS + FOX_PREP_INPUTS:]
    c = pl.program_id(0)

    @pl.when(c == 0)
    def _():
        s_ref[...] = jnp.zeros_like(s_ref)
        carry_ref[...] = jnp.zeros_like(carry_ref)

    keep = (c > 0).astype(F32)
    def after(stages, program):
        for _ in range(stages):
            yield
        yield from program

    batches = list(range(carry_ref.shape[0]))
    programs = [_rwkv_program(batches, keep, *rwkv_in, rwkv_out, s_ref, sr_ref, sk_ref, sv_ref,
                              slo_ref)]
    for b in batches:
        prep = _fox_prep_chunk(b, *prep_in, qo_ref, ko_ref, vo_ref, carry_ref)
        programs.append(after(2 + 2 * b, prep))
    _interleave(programs)


def _rwkv_fox_prep(r, k, v, lo, mu, w0, w_up, a0, a_up, g_up, k_k, k_a, r_k, ln_w, ln_b,
                   fq, fk, fv, ff, q_norm_w, k_norm_w, f_bias, batch, seq):
    nc = seq // CHUNK
    lo_width = lo.shape[-1]
    rank = w_up.shape[0]
    row1 = lambda a: a.reshape(1, -1)
    seq3 = lambda a: a.reshape(batch, seq, a.shape[-1])

    wup_pad = jnp.concatenate([w_up, jnp.zeros((LANES - rank, WIDTH), F32)], axis=0).astype(BF16)
    aup_pad = jnp.concatenate([jnp.zeros((LANES - rank, WIDTH), F32), a_up], axis=0).astype(BF16)
    head = jnp.arange(WIDTH) // HEAD_DIM
    bd = (head[:, None] == head[None, :]).astype(BF16)
    r, k, v, lo = seq3(r), seq3(k), seq3(v), seq3(lo)
    rwkv_args = (r, r, k, k, v, v, lo, lo,
                 row1(mu[:WIDTH]), row1(mu[WIDTH:2 * WIDTH]), row1(mu[2 * WIDTH:3 * WIDTH]),
                 row1(mu[3 * WIDTH:]),
                 row1(w0), wup_pad, row1(a0), aup_pad, g_up.astype(BF16),
                 row1(k_k), row1(k_a), row1(r_k), row1(ln_w), row1(ln_b), bd)
    wide, wide_halo = _seq_spec(batch, WIDTH), _seq_halo_spec(batch, WIDTH)
    param = _param_spec1((1, WIDTH))
    weight = _param_spec1((LANES, WIDTH))
    rwkv_specs = [wide, wide_halo, wide, wide_halo, wide, wide_halo,
                  _seq_spec(batch, lo_width), _seq_halo_spec(batch, lo_width),
                  param, param, param, _param_spec1((1, lo_width)),
                  param, weight, param, weight, weight,
                  param, param, param, param, param, _param_spec1((WIDTH, WIDTH))]

    dst = jnp.arange(N_HEADS * LANES)
    spread = ((dst[None, :] // LANES == jnp.arange(LANES)[:, None])
              & (dst[None, :] % LANES >= HEAD_DIM) & (dst[None, :] % LANES < HEAD_DIM + 6)).astype(BF16)
    prep_args = (seq3(fq), seq3(fk), seq3(fv), seq3(ff),
                 row1(jnp.tile(q_norm_w, N_HEADS)), row1(jnp.tile(k_norm_w, N_HEADS)),
                 row1(jnp.pad(f_bias, (0, LANES - N_HEADS))), bd, spread)
    prep_specs = [wide, wide, wide, _seq_spec(batch, LANES), param, param,
                  _param_spec1((1, LANES)), _param_spec1((WIDTH, WIDTH)),
                  _param_spec1((LANES, N_HEADS * LANES))]
    assert len(rwkv_args) == RWKV_INPUTS and len(prep_args) == FOX_PREP_INPUTS

    head_out = pl.BlockSpec((batch, N_HEADS, CHUNK, LANES), lambda c: (0, 0, c, 0))
    head_shape = jax.ShapeDtypeStruct((batch, N_HEADS, seq, LANES), BF16)
    vt_out = pl.BlockSpec((batch, N_HEADS, LANES, CHUNK), lambda c: (0, 0, 0, c))
    vt_shape = jax.ShapeDtypeStruct((batch, N_HEADS, LANES, seq), BF16)
    shifted = lambda width: pltpu.VMEM((batch, CHUNK + HALO, width), F32)
    y_rwkv, qa, ka, va = pl.pallas_call(
        _rwkv_fox_prep_kernel,
        grid=(nc,),
        in_specs=rwkv_specs + prep_specs,
        out_specs=[wide, head_out, head_out, vt_out],
        out_shape=[jax.ShapeDtypeStruct((batch, seq, WIDTH), BF16), head_shape, head_shape,
                   vt_shape],
        scratch_shapes=[pltpu.VMEM((batch, N_HEADS // 2, LANES, LANES), F32),
                        shifted(WIDTH), shifted(WIDTH), shifted(WIDTH), shifted(lo_width),
                        pltpu.VMEM((batch, HALO, LANES), F32)],
        compiler_params=_cparams(("arbitrary",)),
        name="rwkv7_fox_prep",
    )(*rwkv_args, *prep_args)
    return y_rwkv.reshape(batch * seq, WIDTH), qa, ka, va


def _fox_kernel(qi_ref, ki_ref, q_ref, k_ref, vt_ref, o_ref, m_ref, acc_ref, redo_ref):
    t = pl.program_id(2)
    qi = qi_ref[t]
    ki = ki_ref[t]

    part = FOX_TILE // FOX_Q_SPLIT
    units = [(j, slice(h * part, (h + 1) * part)) for j in range(2) for h in range(FOX_Q_SPLIT)]

    def all_scores(on_diagonal):
        n_keys = lambda qs: qs.stop if on_diagonal else FOX_TILE
        raw = [lax.dot_general(k_ref[0, j, :n_keys(qs), :], q_ref[0, j, qs, :],
                               (((1,), (1,)), ((), ())), preferred_element_type=F32)
               for j, qs in units]
        if not on_diagonal:
            return raw
        masked = []
        for (_, qs), s in zip(units, raw):
            key = _iota2(s.shape, 0)
            query = _iota2(s.shape, 1) + qs.start
            masked.append(jnp.where(query >= key, s, NEG))
        return masked

    def pv(j, p):
        return jnp.dot(vt_ref[0, j, :, :p.shape[0]], p.astype(BF16), preferred_element_type=F32)

    def exact_step(on_diagonal):
        for (j, qs), s in zip(units, all_scores(on_diagonal)):
            m_prev = m_ref[j, :, qs]
            m_next = jnp.maximum(m_prev, jnp.max(s, axis=0, keepdims=True))
            alpha = jnp.exp2(m_prev - m_next)
            acc_ref[j, :, qs] = alpha * acc_ref[j, :, qs] + pv(j, jnp.exp2(s - m_next))
            m_ref[j, :, qs] = m_next

    def lagged_step(on_diagonal):
        done = []
        worst = None
        for (j, qs), s in zip(units, all_scores(on_diagonal)):
            m_used = m_ref[j, :, qs]
            block_max = jnp.max(s, axis=0, keepdims=True)
            contrib = pv(j, jnp.exp2(s - m_used))
            excess = jnp.max(block_max - m_used)
            worst = excess if worst is None else jnp.maximum(worst, excess)
            done.append((j, qs, m_used, block_max, contrib))
        ok = worst <= FOX_GUARD
        redo_ref[0] = jnp.where(ok, 0, 1)

        @pl.when(ok)
        def _():
            for j, qs, m_used, block_max, contrib in done:
                m_next = jnp.maximum(m_used, block_max)
                acc_ref[j, :, qs] = (acc_ref[j, :, qs] + contrib) * jnp.exp2(m_used - m_next)
                m_ref[j, :, qs] = m_next

    @pl.when(ki == qi)
    def _():
        m_ref[...] = jnp.full_like(m_ref, NEG)
        acc_ref[...] = jnp.zeros_like(acc_ref)
        exact_step(True)

    redo_ref[0] = 0

    @pl.when(ki < qi)
    def _():
        lagged_step(False)

    @pl.when(redo_ref[0] == 1)
    def _():
        exact_step(False)

    @pl.when(ki == 0)
    def _():
        head_rows = _iota2((LANES, 1), 0) // HEAD_DIM
        out_t = jnp.where(head_rows == 0,
                          acc_ref[0] / acc_ref[0, HEAD_DIM:HEAD_DIM + 1, :],
                          acc_ref[1] / acc_ref[1, 0:1, :])
        o_ref[0] = out_t.T.astype(BF16)


def _fox_attention(qa, ka, va, batch, seq):
    nt = seq // FOX_TILE
    pairs = [(i, j) for i in range(nt) for j in range(i, -1, -1)]
    qi = jnp.asarray([i for i, _ in pairs], jnp.int32)
    ki = jnp.asarray([j for _, j in pairs], jnp.int32)
    q_spec = pl.BlockSpec((1, 2, FOX_TILE, LANES), lambda b, p, t, qi, ki: (b, p, qi[t], 0))
    k_spec = pl.BlockSpec((1, 2, FOX_TILE, LANES), lambda b, p, t, qi, ki: (b, p, ki[t], 0))
    vt_spec = pl.BlockSpec((1, 2, LANES, FOX_TILE), lambda b, p, t, qi, ki: (b, p, 0, ki[t]))
    out = pl.pallas_call(
        _fox_kernel,
        grid_spec=pltpu.PrefetchScalarGridSpec(
            num_scalar_prefetch=2,
            grid=(batch, N_HEADS // 2, len(pairs)),
            in_specs=[q_spec, k_spec, vt_spec],
            out_specs=pl.BlockSpec((1, FOX_TILE, LANES), lambda b, p, t, qi, ki: (b, qi[t], p)),
            scratch_shapes=[pltpu.VMEM((2, 1, FOX_TILE), F32),
                            pltpu.VMEM((2, LANES, FOX_TILE), F32),
                            pltpu.SMEM((1,), jnp.int32)]),
        out_shape=jax.ShapeDtypeStruct((batch, seq, WIDTH), BF16),
        compiler_params=_cparams(("parallel", "parallel", "arbitrary")),
        name="fox_attention",
    )(qi, ki, qa, ka, va)
    return out.reshape(batch * seq, WIDTH)


def _pad_cols(w, width):
    return jnp.pad(w, ((0, 0), (0, width - w.shape[1])))


def _even_mixers(x, batch, seq, norm_w, w_in, conv_w, conv_b, dt_bias, a_log, d_skip,
                 ssd_norm_w, ret_norm_w):
    ssd_dt0 = WIDTH + (WIDTH + 4 * SSD_STATE)
    ret0 = ssd_dt0 + N_HEADS
    w = jnp.concatenate([w_in[:, :ssd_dt0], w_in[:, ret0:],
                         _pad_cols(w_in[:, ssd_dt0:ret0], LANES)], axis=1).astype(BF16)
    return _even_layer_mixers(x, norm_w, w, conv_w, conv_b, dt_bias, a_log, d_skip, ssd_norm_w,
                              ret_norm_w, batch, seq)


def _odd_mixers(x, batch, seq, norm_w, w_in, mu, w0, w_up, a0, a_up, g_up, k_k, k_a, r_k,
                ln_w, ln_b, q_norm_w, k_norm_w, f_bias):
    lo_width = w_up.shape[0] + a_up.shape[0] + g_up.shape[0]
    fox0 = 3 * WIDTH + lo_width
    f0 = fox0 + 3 * WIDTH
    w = jnp.concatenate([w_in[:, :f0], _pad_cols(w_in[:, f0:], LANES)], axis=1).astype(BF16)
    segments = ((0, WIDTH), (WIDTH, WIDTH), (2 * WIDTH, WIDTH), (3 * WIDTH, lo_width),
                (fox0, WIDTH), (fox0 + WIDTH, WIDTH), (fox0 + 2 * WIDTH, WIDTH), (f0, LANES))
    r, k, v, lo, fq, fk, fv, ff = _inproj(x, norm_w, w, segments)
    y_rwkv, qa, ka, va = _rwkv_fox_prep(r, k, v, lo, mu, w0, w_up, a0, a_up, g_up, k_k, k_a,
                                        r_k.reshape(-1), ln_w, ln_b, fq, fk, fv, ff,
                                        q_norm_w, k_norm_w, f_bias, batch, seq)
    return y_rwkv, _fox_attention(qa, ka, va, batch, seq)


def kernel(x, ev_norm_w, ev_w_in, ev_ssd_conv_w, ev_ssd_conv_b, ev_ssd_dt_bias, ev_ssd_a_log,
           ev_ssd_d, ev_ssd_norm_w, ev_ret_norm_w, ev_w_out,
           od_norm_w, od_w_in, od_rwkv_mu, od_rwkv_w0, od_rwkv_w_up, od_rwkv_a0, od_rwkv_a_up,
           od_rwkv_g_up, od_rwkv_k_k, od_rwkv_k_a, od_rwkv_r_k, od_rwkv_ln_w, od_rwkv_ln_b,
           od_fox_q_norm_w, od_fox_k_norm_w, od_fox_f_bias, od_w_out,
           ffn_norm_w, ffn_w_up, ffn_conv_w, ffn_conv_b, ffn_w_down):
    batch, seq, _ = x.shape
    depth = ffn_norm_w.shape[0]
    h = x.reshape(batch * seq, D_MODEL)
    w_up_all = ffn_w_up.astype(BF16)
    w_down_all = ffn_w_down.astype(BF16)
    for layer in range(depth):
        i = layer // 2
        if layer % 2 == 0:
            ya, yb = _even_mixers(h, batch, seq, ev_norm_w[i], ev_w_in[i], ev_ssd_conv_w[i],
                                  ev_ssd_conv_b[i], ev_ssd_dt_bias[i], ev_ssd_a_log[i],
                                  ev_ssd_d[i], ev_ssd_norm_w[i], ev_ret_norm_w[i])
            w_out = ev_w_out[i]
        else:
            ya, yb = _odd_mixers(h, batch, seq, od_norm_w[i], od_w_in[i], od_rwkv_mu[i],
                                 od_rwkv_w0[i], od_rwkv_w_up[i], od_rwkv_a0[i], od_rwkv_a_up[i],
                                 od_rwkv_g_up[i], od_rwkv_k_k[i], od_rwkv_k_a[i], od_rwkv_r_k[i],
                                 od_rwkv_ln_w[i], od_rwkv_ln_b[i], od_fox_q_norm_w[i],
                                 od_fox_k_norm_w[i], od_fox_f_bias[i])
            w_out = od_w_out[i]
        h = _proj_ffn(ya, yb, h, seq, w_out, ffn_norm_w[layer], layer, w_up_all,
                      ffn_conv_w[layer], ffn_conv_b[layer], w_down_all)
    return h.reshape(batch, seq, D_MODEL)
```

```python
import functools
import itertools
import math

import jax
import jax.numpy as jnp
from jax import lax
from jax.experimental import pallas as pl
from jax.experimental.pallas import tpu as pltpu

F32 = jnp.float32
BF16 = jnp.bfloat16

D_MODEL = 1024
HEAD_DIM = 64
N_HEADS = 8
WIDTH = N_HEADS * HEAD_DIM
LANES = 128
CHUNK = 128
HALO = 8
EPS = 1e-6
SSD_CONV = 4
SSD_STATE = 64
RWKV_LN_EPS = 64e-5
RWKV_DECAY_SCALE = 0.606531
D_FF = 2816
FF_TILE = 256
FFN_HALO = 16
FFN_LOOKAHEAD = 2
NEG = -1e30
VMEM_LIMIT = 56 * 1024 * 1024


def _mm(a, b):
    return jnp.dot(a.astype(BF16), b.astype(BF16), preferred_element_type=F32)


def _mm_nt(a, b):
    return lax.dot_general(a.astype(BF16), b.astype(BF16), (((1,), (1,)), ((), ())),
                           preferred_element_type=F32)


def _mm_tn(a, b):
    return lax.dot_general(a.astype(BF16), b.astype(BF16), (((0,), (0,)), ((), ())),
                           preferred_element_type=F32)


def _mm_split(lhs01, x, pieces):
    acc = None
    rest = x
    for _ in range(pieces):
        part = rest.astype(BF16)
        term = jnp.dot(lhs01, part, preferred_element_type=F32)
        acc = term if acc is None else acc + term
        rest = rest - part.astype(F32)
    return acc


def _mm_split_rhs(x, rhs01, pieces):
    acc = None
    rest = x
    for _ in range(pieces):
        part = rest.astype(BF16)
        term = jnp.dot(part, rhs01, preferred_element_type=F32)
        acc = term if acc is None else acc + term
        rest = rest - part.astype(F32)
    return acc


def _interleave(stage_generators):
    for _ in itertools.zip_longest(*stage_generators):
        pass


def _after(stages, program):
    for _ in range(stages):
        yield
    yield from program


def _projection_program(x_ref, nw_ref, w_ref, write_buf, segments):
    nb = x_ref.shape[0]
    x = x_ref[...].reshape(nb * CHUNK, D_MODEL)
    ms = jnp.mean(x * x, axis=-1, keepdims=True)
    xn = (x * lax.rsqrt(ms + EPS) * nw_ref[...]).astype(BF16)
    for off, width in segments:
        res = jnp.dot(xn, w_ref[:, off:off + width], preferred_element_type=F32)
        for b in range(nb):
            write_buf[b, :, off:off + width] = res[b * CHUNK:(b + 1) * CHUNK]
        yield


def _sigmoid(x):
    return 1.0 / (1.0 + jnp.exp(-x))


def _silu(x):
    return x * _sigmoid(x)


def _softplus(x):
    return jnp.maximum(x, 0.0) + jnp.log1p(jnp.exp(-jnp.abs(x)))


def _iota2(shape, dim):
    return lax.broadcasted_iota(jnp.int32, shape, dim)


def _head_mask(j, width=LANES):
    lane = _iota2((1, width), 1)
    return ((lane % LANES) // HEAD_DIM == j).astype(F32)


def _cparams(sem):
    return pltpu.CompilerParams(dimension_semantics=sem, vmem_limit_bytes=VMEM_LIMIT)


def _const_spec(shape):
    nd = len(shape)
    return pl.BlockSpec(shape, lambda *_: (0,) * nd, pipeline_mode=pl.Buffered(1))


def _proj_ffn_kernel(ya_ref, yah_ref, yb_ref, ybh_ref, x_ref, xh_ref, wo_ref, nw_ref,
                     wup_ref, cw_ref, cb_ref, wd_ref, o_ref,
                     ya_ext, yb_ext, h_ref, hn_ref, g_ref, acc_ref, *, tm, tiles_per_seq):
    ya_ext[:FFN_HALO, :] = yah_ref[...]
    ya_ext[FFN_HALO:, :] = ya_ref[...]
    yb_ext[:FFN_HALO, :] = ybh_ref[...]
    yb_ext[FFN_HALO:, :] = yb_ref[...]
    h_ref[:FFN_HALO, :] = xh_ref[...]
    h_ref[FFN_HALO:, :] = x_ref[...]
    h_ref[...] += (jnp.dot(ya_ext[...], wo_ref[:WIDTH, :], preferred_element_type=F32)
                   + jnp.dot(yb_ext[...], wo_ref[WIDTH:, :], preferred_element_type=F32))

    h = h_ref[...]
    ms = jnp.mean(h * h, axis=-1, keepdims=True)
    hn = h * lax.rsqrt(ms + EPS) * nw_ref[...]
    first_of_seq = pl.program_id(0) % tiles_per_seq == 0
    halo_rows = _iota2((tm + FFN_HALO, 1), 0) < FFN_HALO
    hn_ref[...] = jnp.where(halo_rows & first_of_seq, 0.0, hn).astype(BF16)
    acc_ref[...] = jnp.zeros_like(acc_ref)

    def tile(f, base=0):
        return slice(base + f * FF_TILE, base + (f + 1) * FF_TILE)

    n_buf = g_ref.shape[0]

    def up_proj(f):
        g_ref[f % n_buf] = jnp.dot(hn_ref[...], wup_ref[:, tile(f)], preferred_element_type=F32)
        return jnp.dot(hn_ref[FFN_HALO:, :], wup_ref[:, tile(f, D_FF)],
                       preferred_element_type=F32)

    nf = D_FF // FF_TILE
    ups = {f: up_proj(f) for f in range(FFN_LOOKAHEAD)}
    for f in range(nf):
        if f + FFN_LOOKAHEAD < nf:
            ups[f + FFN_LOOKAHEAD] = up_proj(f + FFN_LOOKAHEAD)
        up = ups.pop(f)
        g = g_ref.at[f % n_buf]
        cw = cw_ref[:, tile(f)]
        gate = (cb_ref[:, tile(f)]
                + cw[0:1, :] * g[FFN_HALO - 2:FFN_HALO - 2 + tm, :]
                + cw[1:2, :] * g[FFN_HALO - 1:FFN_HALO - 1 + tm, :]
                + cw[2:3, :] * g[FFN_HALO:, :])
        act = (_silu(gate) * up).astype(BF16)
        acc_ref[...] += jnp.dot(act, wd_ref[tile(f), :], preferred_element_type=F32)
    o_ref[...] = h_ref[FFN_HALO:, :] + acc_ref[...]


def _proj_ffn(ya, yb, x, seq, w_out, norm_w, layer, w_up_all, conv_w, conv_b, w_down_all, tm=512):
    t = x.shape[0]
    halo_per_tile = tm // FFN_HALO
    row_spec = lambda width: pl.BlockSpec((tm, width), lambda i: (i, 0))
    halo_spec = lambda width: pl.BlockSpec(
        (FFN_HALO, width), lambda i: (jnp.maximum(i * halo_per_tile - 1, 0), 0))
    ext = lambda width, dtype: pltpu.VMEM((tm + FFN_HALO, width), dtype)
    layer_spec = lambda shape: pl.BlockSpec((None,) + shape, lambda i: (layer, 0, 0),
                                            pipeline_mode=pl.Buffered(1))
    return pl.pallas_call(
        functools.partial(_proj_ffn_kernel, tm=tm, tiles_per_seq=seq // tm),
        grid=(t // tm,),
        in_specs=[row_spec(WIDTH), halo_spec(WIDTH), row_spec(WIDTH), halo_spec(WIDTH),
                  row_spec(D_MODEL), halo_spec(D_MODEL),
                  _const_spec((2 * WIDTH, D_MODEL)), _const_spec((1, D_MODEL)),
                  layer_spec((D_MODEL, 2 * D_FF)), _const_spec((3, D_FF)), _const_spec((1, D_FF)),
                  layer_spec((D_FF, D_MODEL))],
        out_specs=row_spec(D_MODEL),
        out_shape=jax.ShapeDtypeStruct((t, D_MODEL), F32),
        scratch_shapes=[ext(WIDTH, BF16), ext(WIDTH, BF16), ext(D_MODEL, F32), ext(D_MODEL, BF16),
                        pltpu.VMEM((FFN_LOOKAHEAD + 1, tm + FFN_HALO, FF_TILE), F32),
                        pltpu.VMEM((tm, D_MODEL), F32)],
        compiler_params=_cparams(("parallel",)),
        name="proj_convffn",
    )(ya, ya, yb, yb, x, x, w_out.astype(BF16), norm_w.reshape(1, D_MODEL),
      w_up_all, conv_w, conv_b.reshape(1, D_FF), w_down_all)


def _param_spec1(shape):
    return pl.BlockSpec(shape, lambda c: (0,) * len(shape), pipeline_mode=pl.Buffered(1))


def _with_prev_rows(scr_ref, main_ref, halo_ref, keep):
    scr_ref[:HALO, :] = halo_ref[...] * keep
    scr_ref[HALO:, :] = main_ref[...]


def _ssd_chunk(b, keep, z_ref, x_ref, xh_ref, bc_ref, bch_ref, dt_ref,
               cwx_ref, cbx_ref, cwb_ref, cbb_ref, dtb_ref, alog_ref, dexp_ref, nw_ref, e_ref,
               o_ref, xe_ref, bce_ref, st_ref, y_ref):
    _with_prev_rows(xe_ref.at[b], x_ref.at[b], xh_ref.at[b], keep)
    _with_prev_rows(bce_ref.at[b], bc_ref.at[b], bch_ref.at[b], keep)

    def conv(ref, cw_ref, cb_ref):
        out = cb_ref[...]
        for k in range(SSD_CONV):
            lo = HALO - (SSD_CONV - 1) + k
            out = out + cw_ref[k:k + 1, :] * ref[b, lo:lo + CHUNK, :]
        return _silu(out)

    x = conv(xe_ref, cwx_ref, cbx_ref)
    bc = conv(bce_ref, cwb_ref, cbb_ref)
    bm = bc[:, :LANES]
    cm = bc[:, LANES:]

    row = _iota2((CHUNK, CHUNK), 0)
    col = _iota2((CHUNK, CHUNK), 1)
    causal = row >= col

    dt = _softplus(dt_ref[b] + dtb_ref[...])
    a = dt * (-jnp.exp(alog_ref[...]))
    a_cum = _mm_split(causal.astype(BF16), a, 3)
    yield
    a_cum_t = a_cum.T
    acausal_penalty = jnp.where(causal, 0.0, NEG)
    a_last = a_cum[CHUNK - 1:CHUNK, :]
    per_head = jnp.concatenate(
        [dt, jnp.exp(a_cum), jnp.exp(a_last - a_cum),
         jnp.broadcast_to(jnp.exp(a_last), (HALO, LANES))], axis=0)
    per_lane = _mm_split_rhs(per_head, e_ref[...], 2)
    dt_e = per_lane[:CHUNK]
    ea_e = per_lane[CHUNK:2 * CHUNK]
    te_e = per_lane[2 * CHUNK:3 * CHUNK]
    cd_e = per_lane[3 * CHUNK:3 * CHUNK + 1]
    yield

    xdt = x * dt_e
    xs = xdt * te_e
    bm_t = bm.T
    group_rows = _iota2((LANES, 1), 0) // SSD_STATE

    for g in range(2):
        cg = cm * _head_mask(g)
        cb = _mm_nt(cg, bm)
        gs = slice(g * 2 * LANES, (g + 1) * 2 * LANES)
        st = st_ref[b, g]
        y_off = _mm(cg, st) * ea_e[:, gs]
        st_new = _mm(bm_t, xs[:, gs])
        st_ref[b, g] = st * cd_e[:, gs] + jnp.where(group_rows == g, st_new, 0.0)
        yield
        for pp in range(2):
            p = 2 * g + pp
            xp = xdt[:, p * LANES:(p + 1) * LANES]
            yp = y_off[:, pp * LANES:(pp + 1) * LANES]
            for j in range(2):
                h = 2 * p + j
                seg = a_cum[:, h:h + 1] - a_cum_t[h:h + 1, :]
                decay = jnp.exp(seg + acausal_penalty)
                yp = yp + _mm(cb * decay, xp * _head_mask(j))
            y_ref[b, :, p * LANES:(p + 1) * LANES] = yp
            yield

    y = y_ref[b] + dexp_ref[...] * x
    gated = y * _silu(z_ref[b])
    ms = jnp.mean(gated * gated, axis=-1, keepdims=True)
    o_ref[b] = (gated * lax.rsqrt(ms + EPS) * nw_ref[...]).astype(BF16)


def _ret_chunk(b, q_ref, k_ref, v_ref, g_ref, cos_ref, sin_ref, intra_ref, qdec_ref, kte_ref,
               cg_ref, bd_ref, nw_ref, o_ref, r_ref):
    lane = _iota2((1, WIDTH), 1)
    first_half = (lane % HEAD_DIM) < (HEAD_DIM // 2)
    cos = jnp.concatenate([cos_ref[...]] * (WIDTH // LANES), axis=1)
    sin = jnp.concatenate([sin_ref[...]] * (WIDTH // LANES), axis=1)

    def rotary(v):
        other = jnp.where(first_half,
                          pltpu.roll(v, WIDTH - HEAD_DIM // 2, 1),
                          pltpu.roll(v, HEAD_DIM // 2, 1))
        return v * cos + other * sin

    q = rotary(q_ref[b])
    k = rotary(k_ref[b]) * (HEAD_DIM ** -0.5)
    v = v_ref[b]
    bd = bd_ref[...]
    pair = lambda t, p: t[:, p * LANES:(p + 1) * LANES]
    heads = [(p, j) for p in range(N_HEADS // 2) for j in range(2)]
    masks = [_head_mask(j) for j in range(2)]

    y_cross = _mm(q * qdec_ref[...], r_ref[b])
    r_ref[b] = r_ref[b] * cg_ref[...] + bd * _mm_tn(k * kte_ref[...], v)
    scores = [_mm_nt(pair(q, p) * masks[j], pair(k, p)) for p, j in heads]
    yield
    y_pairs = []
    for p in range(N_HEADS // 2):
        yp = pair(y_cross, p)
        for j in range(2):
            h = 2 * p + j
            yp = yp + _mm(scores[h] * intra_ref[h], pair(v, p) * masks[j])
        y_pairs.append(yp)
    yield
    y = jnp.concatenate(y_pairs, axis=1)
    ms = _mm(y * y, bd) * (1.0 / HEAD_DIM)
    yield
    o_ref[b] = (_silu(g_ref[b]) * (y * lax.rsqrt(ms + EPS) * nw_ref[...])).astype(BF16)


def _retention_tables(seq):
    half = HEAD_DIM // 2
    inv = 1.0 / (10000.0 ** (jnp.arange(half, dtype=F32) / half))
    ang = jnp.arange(seq, dtype=F32)[:, None] * inv[None, :]
    cos = jnp.tile(jnp.concatenate([jnp.cos(ang), jnp.cos(ang)], axis=-1), (1, LANES // HEAD_DIM))
    sin = jnp.tile(jnp.concatenate([-jnp.sin(ang), jnp.sin(ang)], axis=-1), (1, LANES // HEAD_DIM))
    log_gamma = jnp.log1p(-(2.0 ** (-5.0 - jnp.arange(N_HEADS, dtype=F32))))
    idx = jnp.arange(CHUNK, dtype=F32)
    rel = idx[:, None] - idx[None, :]
    intra = jnp.where(rel[None] >= 0,
                      jnp.exp(jnp.maximum(rel, 0.0)[None] * log_gamma[:, None, None]), 0.0)
    expand = lambda m: jnp.repeat(m, HEAD_DIM, axis=1)
    qdec = expand(jnp.exp((idx + 1.0)[:, None] * log_gamma[None, :]))
    kte = expand(jnp.exp((CHUNK - 1 - idx)[:, None] * log_gamma[None, :]))
    cgam = expand(jnp.exp(CHUNK * log_gamma)[None, :])
    head = jnp.arange(WIDTH) // HEAD_DIM
    bd = (head[:, None] == head[None, :]).astype(BF16)
    return cos, sin, intra, qdec, kte, cgam, bd


EVEN_SEGMENTS = {"z": (0, WIDTH), "x": (WIDTH, WIDTH), "bc": (2 * WIDTH, 2 * LANES),
                 "q": (2 * WIDTH + 2 * LANES, WIDTH), "k": (3 * WIDTH + 2 * LANES, WIDTH),
                 "v": (4 * WIDTH + 2 * LANES, WIDTH), "g": (5 * WIDTH + 2 * LANES, WIDTH),
                 "dt": (6 * WIDTH + 2 * LANES, LANES)}
EVEN_COLS = 6 * WIDTH + 3 * LANES
SSD_PARAMS = 9
RET_PARAMS = 8


def _even_layer_kernel(x_ref, nw_ref, w_ref, *rest):
    ssd_par = rest[:SSD_PARAMS]
    ret_par = rest[SSD_PARAMS:SSD_PARAMS + RET_PARAMS]
    (ssd_out, ret_out, buf_a, buf_b, halo_x, halo_bc,
     xe_ref, bce_ref, st_ref, y_ref, r_ref) = rest[SSD_PARAMS + RET_PARAMS:]
    c = pl.program_id(0)
    nb = x_ref.shape[0]

    @pl.when(c == 0)
    def _():
        buf_a[...] = jnp.zeros_like(buf_a)
        buf_b[...] = jnp.zeros_like(buf_b)
        halo_x[...] = jnp.zeros_like(halo_x)
        halo_bc[...] = jnp.zeros_like(halo_bc)

    @pl.when(c <= 1)
    def _():
        st_ref[...] = jnp.zeros_like(st_ref)
        r_ref[...] = jnp.zeros_like(r_ref)

    keep = (c > 1).astype(F32)

    def step(read_buf, write_buf):
        def view(name):
            off, width = EVEN_SEGMENTS[name]
            return read_buf.at[:, :, off:off + width]

        programs = [_projection_program(x_ref, nw_ref, w_ref, write_buf,
                                        EVEN_SEGMENTS.values())]
        for b in range(nb):
            programs.append(_ssd_chunk(b, keep, view("z"), view("x"), halo_x, view("bc"), halo_bc,
                                       view("dt"), *ssd_par, ssd_out, xe_ref, bce_ref, st_ref,
                                       y_ref))
            programs.append(_ret_chunk(b, view("q"), view("k"), view("v"), view("g"), *ret_par,
                                       ret_out, r_ref))
        _interleave(programs)
        halo_x[...] = view("x")[:, CHUNK - HALO:, :]
        halo_bc[...] = view("bc")[:, CHUNK - HALO:, :]

    @pl.when(c % 2 == 0)
    def _():
        step(buf_b, buf_a)

    @pl.when(c % 2 == 1)
    def _():
        step(buf_a, buf_b)


def _even_layer_mixers(x, norm_w, w, conv_w, conv_b, dt_bias, a_log, d_skip, ssd_norm_w,
                       ret_norm_w, batch, seq):
    nc = seq // CHUNK
    pad = LANES - N_HEADS
    e = jnp.repeat(jnp.eye(LANES, N_HEADS, dtype=BF16), HEAD_DIM, axis=1)
    ssd_args = (conv_w[:, :WIDTH], conv_b[:WIDTH].reshape(1, WIDTH),
                conv_w[:, WIDTH:], conv_b[WIDTH:].reshape(1, 2 * LANES),
                jnp.pad(dt_bias, (0, pad)).reshape(1, LANES),
                jnp.pad(a_log, (0, pad)).reshape(1, LANES),
                jnp.repeat(d_skip, HEAD_DIM).reshape(1, WIDTH),
                ssd_norm_w.reshape(1, WIDTH), e)
    ssd_specs = [_param_spec1((SSD_CONV, WIDTH)), _param_spec1((1, WIDTH)),
                 _param_spec1((SSD_CONV, 2 * LANES)), _param_spec1((1, 2 * LANES)),
                 _param_spec1((1, LANES)), _param_spec1((1, LANES)),
                 _param_spec1((1, WIDTH)), _param_spec1((1, WIDTH)),
                 _param_spec1((LANES, WIDTH))]
    cos, sin, intra, qdec, kte, cgam, bd = _retention_tables(seq)
    mixed_chunk = lambda c: jnp.maximum(c - 1, 0)
    table_spec = pl.BlockSpec((CHUNK, LANES), lambda c: (mixed_chunk(c), 0))
    ret_args = (cos, sin, intra, qdec, kte, cgam, bd, ret_norm_w.reshape(1, WIDTH))
    ret_specs = [table_spec, table_spec, _param_spec1((N_HEADS, CHUNK, CHUNK)),
                 _param_spec1((CHUNK, WIDTH)), _param_spec1((CHUNK, WIDTH)),
                 _param_spec1((1, WIDTH)), _param_spec1((WIDTH, WIDTH)), _param_spec1((1, WIDTH))]
    assert len(ssd_args) == SSD_PARAMS and len(ret_args) == RET_PARAMS
    out_spec = pl.BlockSpec((batch, CHUNK, WIDTH), lambda c: (0, mixed_chunk(c), 0))
    out_shape = jax.ShapeDtypeStruct((batch, seq, WIDTH), BF16)
    proj_buf = pltpu.VMEM((batch, CHUNK, EVEN_COLS), F32)
    y_ssd, y_ret = pl.pallas_call(
        _even_layer_kernel,
        grid=(nc + 1,),
        in_specs=[pl.BlockSpec((batch, CHUNK, D_MODEL), lambda c: (0, jnp.minimum(c, nc - 1), 0)),
                  _param_spec1((1, D_MODEL)), _param_spec1((D_MODEL, EVEN_COLS))]
                 + ssd_specs + ret_specs,
        out_specs=[out_spec, out_spec],
        out_shape=[out_shape, out_shape],
        scratch_shapes=[proj_buf, proj_buf,
                        pltpu.VMEM((batch, HALO, WIDTH), F32),
                        pltpu.VMEM((batch, HALO, 2 * LANES), F32),
                        pltpu.VMEM((batch, CHUNK + HALO, WIDTH), F32),
                        pltpu.VMEM((batch, CHUNK + HALO, 2 * LANES), F32),
                        pltpu.VMEM((batch, 2, LANES, 2 * LANES), F32),
                        pltpu.VMEM((batch, CHUNK, WIDTH), F32),
                        pltpu.VMEM((batch, WIDTH, WIDTH), F32)],
        compiler_params=_cparams(("arbitrary",)),
        name="even_layer_mixers",
    )(x.reshape(batch, seq, D_MODEL), norm_w.reshape(1, D_MODEL), w, *ssd_args, *ret_args)
    return y_ssd.reshape(batch * seq, WIDTH), y_ret.reshape(batch * seq, WIDTH)


def _rwkv_program(batches, keep, r_ref, rh_ref, k_ref, kh_ref, v_ref, vh_ref, lo_ref, loh_ref,
                  mur_ref, muk_ref, muv_ref, mulo_ref, w0_ref, wup_ref, a0_ref, aup_ref, gup_ref,
                  kkw_ref, kaw_ref, rkw_ref, lnw_ref, lnb_ref, bd_ref,
                  o_ref, s_ref, sr_ref, sk_ref, sv_ref, slo_ref):
    width = r_ref.shape[2]
    npair = width // LANES
    bd = bd_ref[...]
    row = _iota2((CHUNK, CHUNK), 0)
    col = _iota2((CHUNK, CHUNK), 1)
    strict = row > col
    causal = row >= col
    eye = (row == col).astype(F32)
    tri = causal.astype(BF16)
    same_head = (row // HEAD_DIM == col // HEAD_DIM).astype(F32)
    m1 = [_head_mask(j) for j in range(2)]
    m2 = [_head_mask(j, 2 * LANES) for j in range(2)]
    cat0 = lambda xs: jnp.concatenate(xs, axis=0)
    cat1 = lambda xs: jnp.concatenate(xs, axis=1)
    C = CHUNK

    def token_shift(b, main_ref, halo_ref, mu_ref, scr_ref):
        _with_prev_rows(scr_ref.at[b], main_ref.at[b], halo_ref.at[b], keep)
        cur = main_ref[b]
        prev = scr_ref[b, HALO - 1:HALO - 1 + CHUNK, :]
        return cur + (prev - cur) * mu_ref[...]

    pre = {}
    for b in batches:
        r = token_shift(b, r_ref, rh_ref, mur_ref, sr_ref)
        k = token_shift(b, k_ref, kh_ref, muk_ref, sk_ref)
        v = token_shift(b, v_ref, vh_ref, muv_ref, sv_ref)
        lo = token_shift(b, lo_ref, loh_ref, mulo_ref, slo_ref)
        lo_wa = lo[:, :LANES]
        log_w = -RWKV_DECAY_SCALE * _sigmoid(w0_ref[...] + _mm(jnp.tanh(lo_wa), wup_ref[...]))
        a = _sigmoid(a0_ref[...] + _mm(lo_wa, aup_ref[...]))
        gate = _mm(_sigmoid(lo[:, LANES:]), gup_ref[...])
        kk = k * kkw_ref[...]
        kk = kk / jnp.maximum(jnp.sqrt(_mm(kk * kk, bd)), 1e-12)
        k = k * (1.0 + (a - 1.0) * kaw_ref[...])
        cum = _mm_split(tri, log_w, 2)
        mid = cum[C // 2 - 1:C // 2, :]
        last = cum[C - 1:C, :]
        inv_p = jnp.exp(mid - cum)
        to_end = jnp.exp(last - mid)
        pre[b] = dict(
            r=r, k=k, v=v, gate=gate,
            a_t=-kk * jnp.exp(cum - log_w - mid), r_t=r * jnp.exp(cum - mid),
            b_t=kk * a * inv_p, k_t=k * inv_p, e_mid=jnp.exp(mid),
            b_end=kk * a * inv_p * to_end, k_end=k * inv_p * to_end, decay=jnp.exp(last))

    yield
    units = [(b, q) for b in batches for q in range(npair)]
    slab = lambda name: [pre[b][name][:, q * LANES:(q + 1) * LANES] for b, q in units]
    a_t, r_t, b_t, k_t, v_u = slab("a_t"), slab("r_t"), slab("b_t"), slab("k_t"), slab("v")
    e_mid, b_end, k_end, decay = slab("e_mid"), slab("b_end"), slab("k_end"), slab("decay")

    mm = [_mm_nt(cat0([a * m1[0], r * m1[0], a * m1[1], r * m1[1]]), cat0([bt, kt]))
          for a, r, bt, kt in zip(a_t, r_t, b_t, k_t)]
    chains = [(u, j) for u in range(len(units)) for j in range(2)]
    m_ab = [jnp.where(strict, mm[u][2 * j * C:(2 * j + 1) * C, :C], 0.0) for u, j in chains]
    m_ak = [jnp.where(strict, mm[u][2 * j * C:(2 * j + 1) * C, C:], 0.0).astype(BF16)
            for u, j in chains]
    m_rb = [jnp.where(causal, mm[u][(2 * j + 1) * C:(2 * j + 2) * C, :C], 0.0).astype(BF16)
            for u, j in chains]
    m_rk = [jnp.where(causal, mm[u][(2 * j + 1) * C:(2 * j + 2) * C, C:], 0.0).astype(BF16)
            for u, j in chains]
    yield

    inv = [eye + m for m in m_ab]
    power = [m.astype(BF16) for m in m_ab]
    for _ in range(int(math.log2(C)) - 1):
        power = [_mm(p, p).astype(BF16) for p in power]
        inv = [i + _mm(i, p) for i, p in zip(inv, power)]
        yield

    n_units = range(len(units))
    per_head = lambda x2: x2[:C] * m2[0] + x2[C:] * m2[1]
    mv = [_mm(cat0([m_ak[2 * u], m_rk[2 * u], m_ak[2 * u + 1], m_rk[2 * u + 1]]), v_u[u])
          for u in n_units]
    mv_ak = [mv[u][:C] * m1[0] + mv[u][2 * C:3 * C] * m1[1] for u in n_units]
    mv_rk = [mv[u][C:2 * C] * m1[0] + mv[u][3 * C:] * m1[1] for u in n_units]
    z = [per_head(_mm(cat0([inv[2 * u], inv[2 * u + 1]]), cat1([a_t[u], mv_ak[u]])))
         for u in n_units]
    yield
    w = [per_head(_mm(cat0([m_rb[2 * u], m_rb[2 * u + 1]]), z[u])) for u in n_units]
    yield

    s = [s_ref[b, q] for b, q in units]
    uy = [_mm_nt(cat0([z[u][:, :LANES] * e_mid[u], (r_t[u] + w[u][:, :LANES]) * e_mid[u]]), s[u])
          for u in n_units]
    u_in = [uy[u][:C] + z[u][:, LANES:] for u in n_units]
    y = [uy[u][C:] + w[u][:, LANES:] + mv_rk[u] for u in n_units]
    s_inc = [_mm_tn(cat0([u_in[u], v_u[u]]), cat0([b_end[u], k_end[u]])) for u in n_units]
    for u, (b, q) in enumerate(units):
        s_ref[b, q] = s[u] * decay[u] + same_head * s_inc[u]

    for i, b in enumerate(batches):
        yb = cat1([y[i * npair + q] for q in range(npair)])
        mean = _mm(yb, bd) * (1.0 / HEAD_DIM)
        d = yb - mean
        var = _mm(d * d, bd) * (1.0 / HEAD_DIM)
        yn = d * lax.rsqrt(var + RWKV_LN_EPS) * lnw_ref[...] + lnb_ref[...]
        bonus = _mm(pre[b]["r"] * pre[b]["k"] * rkw_ref[...], bd) * pre[b]["v"]
        o_ref[b] = ((yn + bonus) * pre[b]["gate"]).astype(BF16)


FOX_TILE = 2048
FOX_Q_SPLIT = 8
FOX_GUARD = 64.0
LOG2E = math.log2(math.e)


def _fox_prep_chunk(b, q_ref, k_ref, v_ref, f_ref, qw_ref, kw_ref, fb_ref, bd_ref, spread_ref,
                    qo_ref, ko_ref, vo_ref, carry_ref):
    bd = bd_ref[...]
    q, k, v = q_ref[b], k_ref[b], v_ref[b]
    q_ms = _mm(q * q, bd) * (1.0 / HEAD_DIM)
    k_ms = _mm(k * k, bd) * (1.0 / HEAD_DIM)

    f = f_ref[b] + fb_ref[...]
    log_f = jnp.minimum(f, 0.0) - jnp.log1p(jnp.exp(-jnp.abs(f)))
    rows = CHUNK
    tri = (_iota2((rows, rows), 0) >= _iota2((rows, rows), 1)).astype(BF16)
    cum = _mm_split(tri, log_f, 3) + carry_ref[b, 0:1, :]
    carry_ref[b] = jnp.broadcast_to(cum[rows - 1:rows, :], carry_ref.shape[1:])
    yield
    q = q * lax.rsqrt(q_ms + EPS) * qw_ref[...] * (HEAD_DIM ** -0.5 * LOG2E)
    k = k * lax.rsqrt(k_ms + EPS) * kw_ref[...]
    c_all = _mm_split_rhs(cum * LOG2E, spread_ref[...], 3)
    yield

    lane = _iota2((1, LANES), 1)
    for h in range(N_HEADS):
        p, j = divmod(h, 2)
        ps = slice(p * LANES, (p + 1) * LANES)
        mj = _head_mask(j)
        qh, kh = q[:, ps] * mj, k[:, ps] * mj
        if j == 1:
            qh = pltpu.roll(qh, HEAD_DIM, 1)
            kh = pltpu.roll(kh, HEAD_DIM, 1)
        ch = c_all[:, h * LANES:(h + 1) * LANES]
        hi = ch.astype(BF16).astype(F32)
        mid = (ch - hi).astype(BF16).astype(F32)
        low = ch - hi - mid
        piece = lambda base: jnp.where(lane == base, hi,
                                       jnp.where(lane == base + 1, mid,
                                                 jnp.where(lane == base + 2, low, 0.0)))
        ones = lambda base: ((lane >= base) & (lane < base + 3)).astype(F32)
        qo_ref[b, h] = (qh + piece(HEAD_DIM) + ones(HEAD_DIM + 3)).astype(BF16)
        ko_ref[b, h] = (kh + ones(HEAD_DIM) - piece(HEAD_DIM + 3)).astype(BF16)
    row_head = _iota2((LANES, 1), 0) // HEAD_DIM
    row_in_head = _iota2((LANES, 1), 0) % HEAD_DIM
    for p in range(N_HEADS // 2):
        v_t = v[:, p * LANES:(p + 1) * LANES].T
        for j in range(2):
            ones_row = ((row_head != j) & (row_in_head == 0)).astype(F32)
            vo_ref[b, 2 * p + j] = jnp.where(row_head == j, v_t, ones_row).astype(BF16)


ODD_SEGMENTS = {"r": (0, WIDTH), "k": (WIDTH, WIDTH), "v": (2 * WIDTH, WIDTH),
                "lo": (3 * WIDTH, 2 * LANES),
                "fq": (3 * WIDTH + 2 * LANES, WIDTH), "fk": (4 * WIDTH + 2 * LANES, WIDTH),
                "fv": (5 * WIDTH + 2 * LANES, WIDTH), "ff": (6 * WIDTH + 2 * LANES, LANES)}
ODD_COLS = 6 * WIDTH + 3 * LANES
RWKV_PARAMS = 15
FOX_PREP_PARAMS = 5


def _odd_layer_kernel(x_ref, nw_ref, w_ref, *rest):
    rwkv_par = rest[:RWKV_PARAMS]
    prep_par = rest[RWKV_PARAMS:RWKV_PARAMS + FOX_PREP_PARAMS]
    (rwkv_out, qo_ref, ko_ref, vo_ref, buf_a, buf_b, halo_r, halo_k, halo_v, halo_lo,
     s_ref, sr_ref, sk_ref, sv_ref, slo_ref, carry_ref) = rest[RWKV_PARAMS + FOX_PREP_PARAMS:]
    halos = {"r": halo_r, "k": halo_k, "v": halo_v, "lo": halo_lo}
    c = pl.program_id(0)

    @pl.when(c == 0)
    def _():
        for ref in (buf_a, buf_b, halo_r, halo_k, halo_v, halo_lo):
            ref[...] = jnp.zeros_like(ref)

    @pl.when(c <= 1)
    def _():
        s_ref[...] = jnp.zeros_like(s_ref)
        carry_ref[...] = jnp.zeros_like(carry_ref)

    keep = (c > 1).astype(F32)
    batches = list(range(x_ref.shape[0]))

    def step(read_buf, write_buf):
        def view(name):
            off, width = ODD_SEGMENTS[name]
            return read_buf.at[:, :, off:off + width]

        shifted_inputs = [ref for name in halos for ref in (view(name), halos[name])]
        programs = [_projection_program(x_ref, nw_ref, w_ref, write_buf, ODD_SEGMENTS.values()),
                    _rwkv_program(batches, keep, *shifted_inputs, *rwkv_par, rwkv_out,
                                  s_ref, sr_ref, sk_ref, sv_ref, slo_ref)]
        for b in batches:
            prep = _fox_prep_chunk(b, view("fq"), view("fk"), view("fv"), view("ff"), *prep_par,
                                   qo_ref, ko_ref, vo_ref, carry_ref)
            programs.append(_after(2 + 2 * b, prep))
        _interleave(programs)
        for name, halo in halos.items():
            halo[...] = view(name)[:, CHUNK - HALO:, :]

    @pl.when(c % 2 == 0)
    def _():
        step(buf_b, buf_a)

    @pl.when(c % 2 == 1)
    def _():
        step(buf_a, buf_b)


def _odd_layer_mixers(x, norm_w, w, mu, w0, w_up, a0, a_up, g_up, k_k, k_a, r_k, ln_w, ln_b,
                      q_norm_w, k_norm_w, f_bias, batch, seq):
    nc = seq // CHUNK
    lo_width = ODD_SEGMENTS["lo"][1]
    rank = w_up.shape[0]
    row1 = lambda a: a.reshape(1, -1)

    wup_pad = jnp.concatenate([w_up, jnp.zeros((LANES - rank, WIDTH), F32)], axis=0).astype(BF16)
    aup_pad = jnp.concatenate([jnp.zeros((LANES - rank, WIDTH), F32), a_up], axis=0).astype(BF16)
    head = jnp.arange(WIDTH) // HEAD_DIM
    bd = (head[:, None] == head[None, :]).astype(BF16)
    rwkv_args = (row1(mu[:WIDTH]), row1(mu[WIDTH:2 * WIDTH]), row1(mu[2 * WIDTH:3 * WIDTH]),
                 row1(mu[3 * WIDTH:]),
                 row1(w0), wup_pad, row1(a0), aup_pad, g_up.astype(BF16),
                 row1(k_k), row1(k_a), row1(r_k), row1(ln_w), row1(ln_b), bd)
    param = _param_spec1((1, WIDTH))
    weight = _param_spec1((LANES, WIDTH))
    rwkv_specs = [param, param, param, _param_spec1((1, lo_width)),
                  param, weight, param, weight, weight,
                  param, param, param, param, param, _param_spec1((WIDTH, WIDTH))]

    dst = jnp.arange(N_HEADS * LANES)
    spread = ((dst[None, :] // LANES == jnp.arange(LANES)[:, None])
              & (dst[None, :] % LANES >= HEAD_DIM) & (dst[None, :] % LANES < HEAD_DIM + 6)).astype(BF16)
    prep_args = (row1(jnp.tile(q_norm_w, N_HEADS)), row1(jnp.tile(k_norm_w, N_HEADS)),
                 row1(jnp.pad(f_bias, (0, LANES - N_HEADS))), bd, spread)
    prep_specs = [param, param, _param_spec1((1, LANES)), _param_spec1((WIDTH, WIDTH)),
                  _param_spec1((LANES, N_HEADS * LANES))]
    assert len(rwkv_args) == RWKV_PARAMS and len(prep_args) == FOX_PREP_PARAMS

    mixed_chunk = lambda c: jnp.maximum(c - 1, 0)
    wide_out = pl.BlockSpec((batch, CHUNK, WIDTH), lambda c: (0, mixed_chunk(c), 0))
    head_out = pl.BlockSpec((batch, N_HEADS, CHUNK, LANES), lambda c: (0, 0, mixed_chunk(c), 0))
    head_shape = jax.ShapeDtypeStruct((batch, N_HEADS, seq, LANES), BF16)
    vt_out = pl.BlockSpec((batch, N_HEADS, LANES, CHUNK), lambda c: (0, 0, 0, mixed_chunk(c)))
    vt_shape = jax.ShapeDtypeStruct((batch, N_HEADS, LANES, seq), BF16)
    proj_buf = pltpu.VMEM((batch, CHUNK, ODD_COLS), F32)
    halo = lambda width: pltpu.VMEM((batch, HALO, width), F32)
    shifted = lambda width: pltpu.VMEM((batch, CHUNK + HALO, width), F32)
    y_rwkv, qa, ka, va = pl.pallas_call(
        _odd_layer_kernel,
        grid=(nc + 1,),
        in_specs=[pl.BlockSpec((batch, CHUNK, D_MODEL), lambda c: (0, jnp.minimum(c, nc - 1), 0)),
                  _param_spec1((1, D_MODEL)), _param_spec1((D_MODEL, ODD_COLS))]
                 + rwkv_specs + prep_specs,
        out_specs=[wide_out, head_out, head_out, vt_out],
        out_shape=[jax.ShapeDtypeStruct((batch, seq, WIDTH), BF16), head_shape, head_shape,
                   vt_shape],
        scratch_shapes=[proj_buf, proj_buf, halo(WIDTH), halo(WIDTH), halo(WIDTH), halo(lo_width),
                        pltpu.VMEM((batch, N_HEADS // 2, LANES, LANES), F32),
                        shifted(WIDTH), shifted(WIDTH), shifted(WIDTH), shifted(lo_width),
                        pltpu.VMEM((batch, HALO, LANES), F32)],
        compiler_params=_cparams(("arbitrary",)),
        name="odd_layer_mixers",
    )(x.reshape(batch, seq, D_MODEL), norm_w.reshape(1, D_MODEL), w, *rwkv_args, *prep_args)
    return y_rwkv.reshape(batch * seq, WIDTH), qa, ka, va


def _fox_kernel(qi_ref, ki_ref, q_ref, k_ref, vt_ref, o_ref, m_ref, acc_ref, redo_ref):
    t = pl.program_id(2)
    qi = qi_ref[t]
    ki = ki_ref[t]

    part = FOX_TILE // FOX_Q_SPLIT
    units = [(j, slice(h * part, (h + 1) * part)) for j in range(2) for h in range(FOX_Q_SPLIT)]

    def all_scores(on_diagonal):
        n_keys = lambda qs: qs.stop if on_diagonal else FOX_TILE
        raw = [lax.dot_general(k_ref[0, j, :n_keys(qs), :], q_ref[0, j, qs, :],
                               (((1,), (1,)), ((), ())), preferred_element_type=F32)
               for j, qs in units]
        if not on_diagonal:
            return raw
        masked = []
        for (_, qs), s in zip(units, raw):
            key = _iota2(s.shape, 0)
            query = _iota2(s.shape, 1) + qs.start
            masked.append(jnp.where(query >= key, s, NEG))
        return masked

    def pv(j, p):
        return jnp.dot(vt_ref[0, j, :, :p.shape[0]], p.astype(BF16), preferred_element_type=F32)

    def exact_step(on_diagonal):
        for (j, qs), s in zip(units, all_scores(on_diagonal)):
            m_prev = m_ref[j, :, qs]
            m_next = jnp.maximum(m_prev, jnp.max(s, axis=0, keepdims=True))
            alpha = jnp.exp2(m_prev - m_next)
            acc_ref[j, :, qs] = alpha * acc_ref[j, :, qs] + pv(j, jnp.exp2(s - m_next))
            m_ref[j, :, qs] = m_next

    def lagged_step(on_diagonal):
        done = []
        worst = None
        for (j, qs), s in zip(units, all_scores(on_diagonal)):
            m_used = m_ref[j, :, qs]
            block_max = jnp.max(s, axis=0, keepdims=True)
            contrib = pv(j, jnp.exp2(s - m_used))
            excess = jnp.max(block_max - m_used)
            worst = excess if worst is None else jnp.maximum(worst, excess)
            done.append((j, qs, m_used, block_max, contrib))
        ok = worst <= FOX_GUARD
        redo_ref[0] = jnp.where(ok, 0, 1)

        @pl.when(ok)
        def _():
            for j, qs, m_used, block_max, contrib in done:
                m_next = jnp.maximum(m_used, block_max)
                acc_ref[j, :, qs] = (acc_ref[j, :, qs] + contrib) * jnp.exp2(m_used - m_next)
                m_ref[j, :, qs] = m_next

    @pl.when(ki == qi)
    def _():
        m_ref[...] = jnp.full_like(m_ref, NEG)
        acc_ref[...] = jnp.zeros_like(acc_ref)
        exact_step(True)

    redo_ref[0] = 0

    @pl.when(ki < qi)
    def _():
        lagged_step(False)

    @pl.when(redo_ref[0] == 1)
    def _():
        exact_step(False)

    @pl.when(ki == 0)
    def _():
        head_rows = _iota2((LANES, 1), 0) // HEAD_DIM
        out_t = jnp.where(head_rows == 0,
                          acc_ref[0] / acc_ref[0, HEAD_DIM:HEAD_DIM + 1, :],
                          acc_ref[1] / acc_ref[1, 0:1, :])
        o_ref[0] = out_t.T.astype(BF16)


def _fox_attention(qa, ka, va, batch, seq):
    nt = seq // FOX_TILE
    pairs = [(i, j) for i in range(nt) for j in range(i, -1, -1)]
    qi = jnp.asarray([i for i, _ in pairs], jnp.int32)
    ki = jnp.asarray([j for _, j in pairs], jnp.int32)
    q_spec = pl.BlockSpec((1, 2, FOX_TILE, LANES), lambda b, p, t, qi, ki: (b, p, qi[t], 0))
    k_spec = pl.BlockSpec((1, 2, FOX_TILE, LANES), lambda b, p, t, qi, ki: (b, p, ki[t], 0))
    vt_spec = pl.BlockSpec((1, 2, LANES, FOX_TILE), lambda b, p, t, qi, ki: (b, p, 0, ki[t]))
    out = pl.pallas_call(
        _fox_kernel,
        grid_spec=pltpu.PrefetchScalarGridSpec(
            num_scalar_prefetch=2,
            grid=(batch, N_HEADS // 2, len(pairs)),
            in_specs=[q_spec, k_spec, vt_spec],
            out_specs=pl.BlockSpec((1, FOX_TILE, LANES), lambda b, p, t, qi, ki: (b, qi[t], p)),
            scratch_shapes=[pltpu.VMEM((2, 1, FOX_TILE), F32),
                            pltpu.VMEM((2, LANES, FOX_TILE), F32),
                            pltpu.SMEM((1,), jnp.int32)]),
        out_shape=jax.ShapeDtypeStruct((batch, seq, WIDTH), BF16),
        compiler_params=_cparams(("parallel", "parallel", "arbitrary")),
        name="fox_attention",
    )(qi, ki, qa, ka, va)
    return out.reshape(batch * seq, WIDTH)


def _pad_cols(w, width):
    return jnp.pad(w, ((0, 0), (0, width - w.shape[1])))


def _even_mixers(x, batch, seq, norm_w, w_in, conv_w, conv_b, dt_bias, a_log, d_skip,
                 ssd_norm_w, ret_norm_w):
    ssd_dt0 = WIDTH + (WIDTH + 4 * SSD_STATE)
    ret0 = ssd_dt0 + N_HEADS
    w = jnp.concatenate([w_in[:, :ssd_dt0], w_in[:, ret0:],
                         _pad_cols(w_in[:, ssd_dt0:ret0], LANES)], axis=1).astype(BF16)
    return _even_layer_mixers(x, norm_w, w, conv_w, conv_b, dt_bias, a_log, d_skip, ssd_norm_w,
                              ret_norm_w, batch, seq)


def _odd_mixers(x, batch, seq, norm_w, w_in, mu, w0, w_up, a0, a_up, g_up, k_k, k_a, r_k,
                ln_w, ln_b, q_norm_w, k_norm_w, f_bias):
    f0 = ODD_SEGMENTS["ff"][0]
    w = jnp.concatenate([w_in[:, :f0], _pad_cols(w_in[:, f0:], LANES)], axis=1).astype(BF16)
    y_rwkv, qa, ka, va = _odd_layer_mixers(x, norm_w, w, mu, w0, w_up, a0, a_up, g_up, k_k, k_a,
                                           r_k.reshape(-1), ln_w, ln_b, q_norm_w, k_norm_w,
                                           f_bias, batch, seq)
    return y_rwkv, _fox_attention(qa, ka, va, batch, seq)


def kernel(x, ev_norm_w, ev_w_in, ev_ssd_conv_w, ev_ssd_conv_b, ev_ssd_dt_bias, ev_ssd_a_log,
           ev_ssd_d, ev_ssd_norm_w, ev_ret_norm_w, ev_w_out,
           od_norm_w, od_w_in, od_rwkv_mu, od_rwkv_w0, od_rwkv_w_up, od_rwkv_a0, od_rwkv_a_up,
           od_rwkv_g_up, od_rwkv_k_k, od_rwkv_k_a, od_rwkv_r_k, od_rwkv_ln_w, od_rwkv_ln_b,
           od_fox_q_norm_w, od_fox_k_norm_w, od_fox_f_bias, od_w_out,
           ffn_norm_w, ffn_w_up, ffn_conv_w, ffn_conv_b, ffn_w_down):
    batch, seq, _ = x.shape
    depth = ffn_norm_w.shape[0]
    h = x.reshape(batch * seq, D_MODEL)
    w_up_all = ffn_w_up.astype(BF16)
    w_down_all = ffn_w_down.astype(BF16)
    for layer in range(depth):
        i = layer // 2
        if layer % 2 == 0:
            ya, yb = _even_mixers(h, batch, seq, ev_norm_w[i], ev_w_in[i], ev_ssd_conv_w[i],
                                  ev_ssd_conv_b[i], ev_ssd_dt_bias[i], ev_ssd_a_log[i],
                                  ev_ssd_d[i], ev_ssd_norm_w[i], ev_ret_norm_w[i])
            w_out = ev_w_out[i]
        else:
            ya, yb = _odd_mixers(h, batch, seq, od_norm_w[i], od_w_in[i], od_rwkv_mu[i],
                                 od_rwkv_w0[i], od_rwkv_w_up[i], od_rwkv_a0[i], od_rwkv_a_up[i],
                                 od_rwkv_g_up[i], od_rwkv_k_k[i], od_rwkv_k_a[i], od_rwkv_r_k[i],
                                 od_rwkv_ln_w[i], od_rwkv_ln_b[i], od_fox_q_norm_w[i],
                                 od_fox_k_norm_w[i], od_fox_f_bias[i])
            w_out = od_w_out[i]
        h = _proj_ffn(ya, yb, h, seq, w_out, ffn_norm_w[layer], layer, w_up_all,
                      ffn_conv_w[layer], ffn_conv_b[layer], w_down_all)
    return h.reshape(batch, seq, D_MODEL)
```

```python
import functools
import itertools
import math

import jax
import jax.numpy as jnp
from jax import lax
from jax.experimental import pallas as pl
from jax.experimental.pallas import tpu as pltpu

F32 = jnp.float32
BF16 = jnp.bfloat16

D_MODEL = 1024
HEAD_DIM = 64
N_HEADS = 8
WIDTH = N_HEADS * HEAD_DIM
LANES = 128
CHUNK = 128
HALO = 8
EPS = 1e-6
SSD_CONV = 4
SSD_STATE = 64
RWKV_LN_EPS = 64e-5
RWKV_DECAY_SCALE = 0.606531
D_FF = 2816
FF_TILE = 256
FFN_HALO = 16
FFN_LOOKAHEAD = 2
NEG = -1e30
VMEM_LIMIT = 56 * 1024 * 1024


def _mm(a, b):
    return jnp.dot(a.astype(BF16), b.astype(BF16), preferred_element_type=F32)


def _mm_nt(a, b):
    return lax.dot_general(a.astype(BF16), b.astype(BF16), (((1,), (1,)), ((), ())),
                           preferred_element_type=F32)


def _mm_tn(a, b):
    return lax.dot_general(a.astype(BF16), b.astype(BF16), (((0,), (0,)), ((), ())),
                           preferred_element_type=F32)


def _mm_split(lhs01, x, pieces):
    acc = None
    rest = x
    for _ in range(pieces):
        part = rest.astype(BF16)
        term = jnp.dot(lhs01, part, preferred_element_type=F32)
        acc = term if acc is None else acc + term
        rest = rest - part.astype(F32)
    return acc


def _mm_split_rhs(x, rhs01, pieces):
    acc = None
    rest = x
    for _ in range(pieces):
        part = rest.astype(BF16)
        term = jnp.dot(part, rhs01, preferred_element_type=F32)
        acc = term if acc is None else acc + term
        rest = rest - part.astype(F32)
    return acc


def _interleave(stage_generators):
    for _ in itertools.zip_longest(*stage_generators):
        pass


def _sigmoid(x):
    return 1.0 / (1.0 + jnp.exp(-x))


def _silu(x):
    return x * _sigmoid(x)


def _softplus(x):
    return jnp.maximum(x, 0.0) + jnp.log1p(jnp.exp(-jnp.abs(x)))


def _iota2(shape, dim):
    return lax.broadcasted_iota(jnp.int32, shape, dim)


def _head_mask(j, width=LANES):
    lane = _iota2((1, width), 1)
    return ((lane % LANES) // HEAD_DIM == j).astype(F32)


def _cparams(sem):
    return pltpu.CompilerParams(dimension_semantics=sem, vmem_limit_bytes=VMEM_LIMIT)


def _const_spec(shape):
    nd = len(shape)
    return pl.BlockSpec(shape, lambda *_: (0,) * nd, pipeline_mode=pl.Buffered(1))


def _inproj_kernel(x_ref, nw_ref, w_ref, *o_refs, segments):
    x = x_ref[...]
    ms = jnp.mean(x * x, axis=-1, keepdims=True)
    xn = (x * lax.rsqrt(ms + EPS) * nw_ref[...]).astype(BF16)
    for o_ref, (off, width) in zip(o_refs, segments):
        o_ref[...] = jnp.dot(xn, w_ref[:, off:off + width], preferred_element_type=F32)


def _inproj(x, norm_w, w, segments, tm=1024):
    t = x.shape[0]
    n = w.shape[1]
    return pl.pallas_call(
        functools.partial(_inproj_kernel, segments=segments),
        grid=(t // tm,),
        in_specs=[pl.BlockSpec((tm, D_MODEL), lambda i: (i, 0)),
                  _const_spec((1, D_MODEL)),
                  _const_spec((D_MODEL, n))],
        out_specs=[pl.BlockSpec((tm, wd), lambda i: (i, 0)) for _, wd in segments],
        out_shape=[jax.ShapeDtypeStruct((t, wd), F32) for _, wd in segments],
        compiler_params=_cparams(("parallel",)),
        name="inproj",
    )(x, norm_w.reshape(1, D_MODEL), w)


def _proj_ffn_kernel(ya_ref, yah_ref, yb_ref, ybh_ref, x_ref, xh_ref, wo_ref, nw_ref,
                     wup_ref, cw_ref, cb_ref, wd_ref, o_ref,
                     ya_ext, yb_ext, h_ref, hn_ref, g_ref, acc_ref, *, tm, tiles_per_seq):
    ya_ext[:FFN_HALO, :] = yah_ref[...]
    ya_ext[FFN_HALO:, :] = ya_ref[...]
    yb_ext[:FFN_HALO, :] = ybh_ref[...]
    yb_ext[FFN_HALO:, :] = yb_ref[...]
    h_ref[:FFN_HALO, :] = xh_ref[...]
    h_ref[FFN_HALO:, :] = x_ref[...]
    h_ref[...] += (jnp.dot(ya_ext[...], wo_ref[:WIDTH, :], preferred_element_type=F32)
                   + jnp.dot(yb_ext[...], wo_ref[WIDTH:, :], preferred_element_type=F32))

    h = h_ref[...]
    ms = jnp.mean(h * h, axis=-1, keepdims=True)
    hn = h * lax.rsqrt(ms + EPS) * nw_ref[...]
    first_of_seq = pl.program_id(0) % tiles_per_seq == 0
    halo_rows = _iota2((tm + FFN_HALO, 1), 0) < FFN_HALO
    hn_ref[...] = jnp.where(halo_rows & first_of_seq, 0.0, hn).astype(BF16)
    acc_ref[...] = jnp.zeros_like(acc_ref)

    def tile(f, base=0):
        return slice(base + f * FF_TILE, base + (f + 1) * FF_TILE)

    n_buf = g_ref.shape[0]

    def up_proj(f):
        g_ref[f % n_buf] = jnp.dot(hn_ref[...], wup_ref[:, tile(f)], preferred_element_type=F32)
        return jnp.dot(hn_ref[FFN_HALO:, :], wup_ref[:, tile(f, D_FF)],
                       preferred_element_type=F32)

    nf = D_FF // FF_TILE
    ups = {f: up_proj(f) for f in range(FFN_LOOKAHEAD)}
    for f in range(nf):
        if f + FFN_LOOKAHEAD < nf:
            ups[f + FFN_LOOKAHEAD] = up_proj(f + FFN_LOOKAHEAD)
        up = ups.pop(f)
        g = g_ref.at[f % n_buf]
        cw = cw_ref[:, tile(f)]
        gate = (cb_ref[:, tile(f)]
                + cw[0:1, :] * g[FFN_HALO - 2:FFN_HALO - 2 + tm, :]
                + cw[1:2, :] * g[FFN_HALO - 1:FFN_HALO - 1 + tm, :]
                + cw[2:3, :] * g[FFN_HALO:, :])
        act = (_silu(gate) * up).astype(BF16)
        acc_ref[...] += jnp.dot(act, wd_ref[tile(f), :], preferred_element_type=F32)
    o_ref[...] = h_ref[FFN_HALO:, :] + acc_ref[...]


def _proj_ffn(ya, yb, x, seq, w_out, norm_w, layer, w_up_all, conv_w, conv_b, w_down_all, tm=512):
    t = x.shape[0]
    halo_per_tile = tm // FFN_HALO
    row_spec = lambda width: pl.BlockSpec((tm, width), lambda i: (i, 0))
    halo_spec = lambda width: pl.BlockSpec(
        (FFN_HALO, width), lambda i: (jnp.maximum(i * halo_per_tile - 1, 0), 0))
    ext = lambda width, dtype: pltpu.VMEM((tm + FFN_HALO, width), dtype)
    layer_spec = lambda shape: pl.BlockSpec((None,) + shape, lambda i: (layer, 0, 0),
                                            pipeline_mode=pl.Buffered(1))
    return pl.pallas_call(
        functools.partial(_proj_ffn_kernel, tm=tm, tiles_per_seq=seq // tm),
        grid=(t // tm,),
        in_specs=[row_spec(WIDTH), halo_spec(WIDTH), row_spec(WIDTH), halo_spec(WIDTH),
                  row_spec(D_MODEL), halo_spec(D_MODEL),
                  _const_spec((2 * WIDTH, D_MODEL)), _const_spec((1, D_MODEL)),
                  layer_spec((D_MODEL, 2 * D_FF)), _const_spec((3, D_FF)), _const_spec((1, D_FF)),
                  layer_spec((D_FF, D_MODEL))],
        out_specs=row_spec(D_MODEL),
        out_shape=jax.ShapeDtypeStruct((t, D_MODEL), F32),
        scratch_shapes=[ext(WIDTH, BF16), ext(WIDTH, BF16), ext(D_MODEL, F32), ext(D_MODEL, BF16),
                        pltpu.VMEM((FFN_LOOKAHEAD + 1, tm + FFN_HALO, FF_TILE), F32),
                        pltpu.VMEM((tm, D_MODEL), F32)],
        compiler_params=_cparams(("parallel",)),
        name="proj_convffn",
    )(ya, ya, yb, yb, x, x, w_out.astype(BF16), norm_w.reshape(1, D_MODEL),
      w_up_all, conv_w, conv_b.reshape(1, D_FF), w_down_all)


def _seq_spec(batch, width):
    return pl.BlockSpec((batch, CHUNK, width), lambda c: (0, c, 0))


def _seq_halo_spec(batch, width):
    per = CHUNK // HALO
    return pl.BlockSpec((batch, HALO, width), lambda c: (0, jnp.maximum(c * per - 1, 0), 0))


def _param_spec1(shape):
    return pl.BlockSpec(shape, lambda c: (0,) * len(shape), pipeline_mode=pl.Buffered(1))


def _with_prev_rows(scr_ref, main_ref, halo_ref, keep):
    scr_ref[:HALO, :] = halo_ref[...] * keep
    scr_ref[HALO:, :] = main_ref[...]


def _ssd_chunk(b, keep, z_ref, x_ref, xh_ref, bc_ref, bch_ref, dt_ref,
               cwx_ref, cbx_ref, cwb_ref, cbb_ref, dtb_ref, alog_ref, dexp_ref, nw_ref, e_ref,
               o_ref, xe_ref, bce_ref, st_ref, y_ref):
    _with_prev_rows(xe_ref.at[b], x_ref.at[b], xh_ref.at[b], keep)
    _with_prev_rows(bce_ref.at[b], bc_ref.at[b], bch_ref.at[b], keep)

    def conv(ref, cw_ref, cb_ref):
        out = cb_ref[...]
        for k in range(SSD_CONV):
            lo = HALO - (SSD_CONV - 1) + k
            out = out + cw_ref[k:k + 1, :] * ref[b, lo:lo + CHUNK, :]
        return _silu(out)

    x = conv(xe_ref, cwx_ref, cbx_ref)
    bc = conv(bce_ref, cwb_ref, cbb_ref)
    bm = bc[:, :LANES]
    cm = bc[:, LANES:]

    row = _iota2((CHUNK, CHUNK), 0)
    col = _iota2((CHUNK, CHUNK), 1)
    causal = row >= col

    dt = _softplus(dt_ref[b] + dtb_ref[...])
    a = dt * (-jnp.exp(alog_ref[...]))
    a_cum = _mm_split(causal.astype(BF16), a, 3)
    yield
    a_cum_t = a_cum.T
    acausal_penalty = jnp.where(causal, 0.0, NEG)
    a_last = a_cum[CHUNK - 1:CHUNK, :]
    per_head = jnp.concatenate(
        [dt, jnp.exp(a_cum), jnp.exp(a_last - a_cum),
         jnp.broadcast_to(jnp.exp(a_last), (HALO, LANES))], axis=0)
    per_lane = _mm_split_rhs(per_head, e_ref[...], 2)
    dt_e = per_lane[:CHUNK]
    ea_e = per_lane[CHUNK:2 * CHUNK]
    te_e = per_lane[2 * CHUNK:3 * CHUNK]
    cd_e = per_lane[3 * CHUNK:3 * CHUNK + 1]
    yield

    xdt = x * dt_e
    xs = xdt * te_e
    bm_t = bm.T
    group_rows = _iota2((LANES, 1), 0) // SSD_STATE

    for g in range(2):
        cg = cm * _head_mask(g)
        cb = _mm_nt(cg, bm)
        gs = slice(g * 2 * LANES, (g + 1) * 2 * LANES)
        st = st_ref[b, g]
        y_off = _mm(cg, st) * ea_e[:, gs]
        st_new = _mm(bm_t, xs[:, gs])
        st_ref[b, g] = st * cd_e[:, gs] + jnp.where(group_rows == g, st_new, 0.0)
        yield
        for pp in range(2):
            p = 2 * g + pp
            xp = xdt[:, p * LANES:(p + 1) * LANES]
            yp = y_off[:, pp * LANES:(pp + 1) * LANES]
            for j in range(2):
                h = 2 * p + j
                seg = a_cum[:, h:h + 1] - a_cum_t[h:h + 1, :]
                decay = jnp.exp(seg + acausal_penalty)
                yp = yp + _mm(cb * decay, xp * _head_mask(j))
            y_ref[b, :, p * LANES:(p + 1) * LANES] = yp
            yield

    y = y_ref[b] + dexp_ref[...] * x
    gated = y * _silu(z_ref[b])
    ms = jnp.mean(gated * gated, axis=-1, keepdims=True)
    o_ref[b] = (gated * lax.rsqrt(ms + EPS) * nw_ref[...]).astype(BF16)


def _ret_chunk(b, q_ref, k_ref, v_ref, g_ref, cos_ref, sin_ref, intra_ref, qdec_ref, kte_ref,
               cg_ref, bd_ref, nw_ref, o_ref, r_ref):
    lane = _iota2((1, WIDTH), 1)
    first_half = (lane % HEAD_DIM) < (HEAD_DIM // 2)
    cos = jnp.concatenate([cos_ref[...]] * (WIDTH // LANES), axis=1)
    sin = jnp.concatenate([sin_ref[...]] * (WIDTH // LANES), axis=1)

    def rotary(v):
        other = jnp.where(first_half,
                          pltpu.roll(v, WIDTH - HEAD_DIM // 2, 1),
                          pltpu.roll(v, HEAD_DIM // 2, 1))
        return v * cos + other * sin

    q = rotary(q_ref[b])
    k = rotary(k_ref[b]) * (HEAD_DIM ** -0.5)
    v = v_ref[b]
    bd = bd_ref[...]
    pair = lambda t, p: t[:, p * LANES:(p + 1) * LANES]
    heads = [(p, j) for p in range(N_HEADS // 2) for j in range(2)]
    masks = [_head_mask(j) for j in range(2)]

    y_cross = _mm(q * qdec_ref[...], r_ref[b])
    r_ref[b] = r_ref[b] * cg_ref[...] + bd * _mm_tn(k * kte_ref[...], v)
    scores = [_mm_nt(pair(q, p) * masks[j], pair(k, p)) for p, j in heads]
    yield
    y_pairs = []
    for p in range(N_HEADS // 2):
        yp = pair(y_cross, p)
        for j in range(2):
            h = 2 * p + j
            yp = yp + _mm(scores[h] * intra_ref[h], pair(v, p) * masks[j])
        y_pairs.append(yp)
    yield
    y = jnp.concatenate(y_pairs, axis=1)
    ms = _mm(y * y, bd) * (1.0 / HEAD_DIM)
    yield
    o_ref[b] = (_silu(g_ref[b]) * (y * lax.rsqrt(ms + EPS) * nw_ref[...])).astype(BF16)


def _retention_tables(seq):
    half = HEAD_DIM // 2
    inv = 1.0 / (10000.0 ** (jnp.arange(half, dtype=F32) / half))
    ang = jnp.arange(seq, dtype=F32)[:, None] * inv[None, :]
    cos = jnp.tile(jnp.concatenate([jnp.cos(ang), jnp.cos(ang)], axis=-1), (1, LANES // HEAD_DIM))
    sin = jnp.tile(jnp.concatenate([-jnp.sin(ang), jnp.sin(ang)], axis=-1), (1, LANES // HEAD_DIM))
    log_gamma = jnp.log1p(-(2.0 ** (-5.0 - jnp.arange(N_HEADS, dtype=F32))))
    idx = jnp.arange(CHUNK, dtype=F32)
    rel = idx[:, None] - idx[None, :]
    intra = jnp.where(rel[None] >= 0,
                      jnp.exp(jnp.maximum(rel, 0.0)[None] * log_gamma[:, None, None]), 0.0)
    expand = lambda m: jnp.repeat(m, HEAD_DIM, axis=1)
    qdec = expand(jnp.exp((idx + 1.0)[:, None] * log_gamma[None, :]))
    kte = expand(jnp.exp((CHUNK - 1 - idx)[:, None] * log_gamma[None, :]))
    cgam = expand(jnp.exp(CHUNK * log_gamma)[None, :])
    head = jnp.arange(WIDTH) // HEAD_DIM
    bd = (head[:, None] == head[None, :]).astype(BF16)
    return cos, sin, intra, qdec, kte, cgam, bd


EVEN_SEGMENTS = {"z": (0, WIDTH), "x": (WIDTH, WIDTH), "bc": (2 * WIDTH, 2 * LANES),
                 "q": (2 * WIDTH + 2 * LANES, WIDTH), "k": (3 * WIDTH + 2 * LANES, WIDTH),
                 "v": (4 * WIDTH + 2 * LANES, WIDTH), "g": (5 * WIDTH + 2 * LANES, WIDTH),
                 "dt": (6 * WIDTH + 2 * LANES, LANES)}
EVEN_COLS = 6 * WIDTH + 3 * LANES
SSD_PARAMS = 9
RET_PARAMS = 8


def _even_layer_kernel(x_ref, nw_ref, w32_ref, *rest):
    ssd_par = rest[:SSD_PARAMS]
    ret_par = rest[SSD_PARAMS:SSD_PARAMS + RET_PARAMS]
    (ssd_out, ret_out, w_ref, buf_a, buf_b, halo_x, halo_bc,
     xe_ref, bce_ref, st_ref, y_ref, r_ref) = rest[SSD_PARAMS + RET_PARAMS:]
    c = pl.program_id(0)
    nb = x_ref.shape[0]

    @pl.when(c == 0)
    def _():
        dt0 = EVEN_SEGMENTS["q"][0]
        dt_new = EVEN_SEGMENTS["dt"][0]
        w_ref[:, :dt0] = w32_ref[:, :dt0].astype(BF16)
        w_ref[:, dt0:dt_new] = w32_ref[:, dt0 + N_HEADS:].astype(BF16)
        w_ref[:, dt_new:] = jnp.zeros((D_MODEL, LANES), BF16)
        w_ref[:, dt_new:dt_new + N_HEADS] = w32_ref[:, dt0:dt0 + N_HEADS].astype(BF16)
        buf_a[...] = jnp.zeros_like(buf_a)
        buf_b[...] = jnp.zeros_like(buf_b)
        halo_x[...] = jnp.zeros_like(halo_x)
        halo_bc[...] = jnp.zeros_like(halo_bc)

    @pl.when(c <= 1)
    def _():
        st_ref[...] = jnp.zeros_like(st_ref)
        r_ref[...] = jnp.zeros_like(r_ref)

    keep = (c > 1).astype(F32)

    def step(read_buf, write_buf):
        def project():
            x = x_ref[...].reshape(nb * CHUNK, D_MODEL)
            ms = jnp.mean(x * x, axis=-1, keepdims=True)
            xn = (x * lax.rsqrt(ms + EPS) * nw_ref[...]).astype(BF16)
            for off, width in EVEN_SEGMENTS.values():
                res = jnp.dot(xn, w_ref[:, off:off + width], preferred_element_type=F32)
                for b in range(nb):
                    write_buf[b, :, off:off + width] = res[b * CHUNK:(b + 1) * CHUNK]
                yield

        def view(name):
            off, width = EVEN_SEGMENTS[name]
            return read_buf.at[:, :, off:off + width]

        programs = [project()]
        for b in range(nb):
            programs.append(_ssd_chunk(b, keep, view("z"), view("x"), halo_x, view("bc"), halo_bc,
                                       view("dt"), *ssd_par, ssd_out, xe_ref, bce_ref, st_ref,
                                       y_ref))
            programs.append(_ret_chunk(b, view("q"), view("k"), view("v"), view("g"), *ret_par,
                                       ret_out, r_ref))
        _interleave(programs)
        halo_x[...] = view("x")[:, CHUNK - HALO:, :]
        halo_bc[...] = view("bc")[:, CHUNK - HALO:, :]

    @pl.when(c % 2 == 0)
    def _():
        step(buf_b, buf_a)

    @pl.when(c % 2 == 1)
    def _():
        step(buf_a, buf_b)


def _even_layer_mixers(x, norm_w, w, conv_w, conv_b, dt_bias, a_log, d_skip, ssd_norm_w,
                       ret_norm_w, batch, seq):
    nc = seq // CHUNK
    pad = LANES - N_HEADS
    e = jnp.repeat(jnp.eye(LANES, N_HEADS, dtype=BF16), HEAD_DIM, axis=1)
    ssd_args = (conv_w[:, :WIDTH], conv_b[:WIDTH].reshape(1, WIDTH),
                conv_w[:, WIDTH:], conv_b[WIDTH:].reshape(1, 2 * LANES),
                jnp.pad(dt_bias, (0, pad)).reshape(1, LANES),
                jnp.pad(a_log, (0, pad)).reshape(1, LANES),
                jnp.repeat(d_skip, HEAD_DIM).reshape(1, WIDTH),
                ssd_norm_w.reshape(1, WIDTH), e)
    ssd_specs = [_param_spec1((SSD_CONV, WIDTH)), _param_spec1((1, WIDTH)),
                 _param_spec1((SSD_CONV, 2 * LANES)), _param_spec1((1, 2 * LANES)),
                 _param_spec1((1, LANES)), _param_spec1((1, LANES)),
                 _param_spec1((1, WIDTH)), _param_spec1((1, WIDTH)),
                 _param_spec1((LANES, WIDTH))]
    cos, sin, intra, qdec, kte, cgam, bd = _retention_tables(seq)
    mixed_chunk = lambda c: jnp.maximum(c - 1, 0)
    table_spec = pl.BlockSpec((CHUNK, LANES), lambda c: (mixed_chunk(c), 0))
    ret_args = (cos, sin, intra, qdec, kte, cgam, bd, ret_norm_w.reshape(1, WIDTH))
    ret_specs = [table_spec, table_spec, _param_spec1((N_HEADS, CHUNK, CHUNK)),
                 _param_spec1((CHUNK, WIDTH)), _param_spec1((CHUNK, WIDTH)),
                 _param_spec1((1, WIDTH)), _param_spec1((WIDTH, WIDTH)), _param_spec1((1, WIDTH))]
    assert len(ssd_args) == SSD_PARAMS and len(ret_args) == RET_PARAMS
    out_spec = pl.BlockSpec((batch, CHUNK, WIDTH), lambda c: (0, mixed_chunk(c), 0))
    out_shape = jax.ShapeDtypeStruct((batch, seq, WIDTH), BF16)
    proj_buf = pltpu.VMEM((batch, CHUNK, EVEN_COLS), F32)
    y_ssd, y_ret = pl.pallas_call(
        _even_layer_kernel,
        grid=(nc + 1,),
        in_specs=[pl.BlockSpec((batch, CHUNK, D_MODEL), lambda c: (0, jnp.minimum(c, nc - 1), 0)),
                  _param_spec1((1, D_MODEL)), _param_spec1(tuple(w.shape))]
                 + ssd_specs + ret_specs,
        out_specs=[out_spec, out_spec],
        out_shape=[out_shape, out_shape],
        scratch_shapes=[pltpu.VMEM((D_MODEL, EVEN_COLS), BF16), proj_buf, proj_buf,
                        pltpu.VMEM((batch, HALO, WIDTH), F32),
                        pltpu.VMEM((batch, HALO, 2 * LANES), F32),
                        pltpu.VMEM((batch, CHUNK + HALO, WIDTH), F32),
                        pltpu.VMEM((batch, CHUNK + HALO, 2 * LANES), F32),
                        pltpu.VMEM((batch, 2, LANES, 2 * LANES), F32),
                        pltpu.VMEM((batch, CHUNK, WIDTH), F32),
                        pltpu.VMEM((batch, WIDTH, WIDTH), F32)],
        compiler_params=_cparams(("arbitrary",)),
        name="even_layer_mixers",
    )(x.reshape(batch, seq, D_MODEL), norm_w.reshape(1, D_MODEL), w, *ssd_args, *ret_args)
    return y_ssd.reshape(batch * seq, WIDTH), y_ret.reshape(batch * seq, WIDTH)


def _rwkv_program(batches, keep, r_ref, rh_ref, k_ref, kh_ref, v_ref, vh_ref, lo_ref, loh_ref,
                  mur_ref, muk_ref, muv_ref, mulo_ref, w0_ref, wup_ref, a0_ref, aup_ref, gup_ref,
                  kkw_ref, kaw_ref, rkw_ref, lnw_ref, lnb_ref, bd_ref,
                  o_ref, s_ref, sr_ref, sk_ref, sv_ref, slo_ref):
    width = r_ref.shape[2]
    npair = width // LANES
    bd = bd_ref[...]
    row = _iota2((CHUNK, CHUNK), 0)
    col = _iota2((CHUNK, CHUNK), 1)
    strict = row > col
    causal = row >= col
    eye = (row == col).astype(F32)
    tri = causal.astype(BF16)
    same_head = (row // HEAD_DIM == col // HEAD_DIM).astype(F32)
    m1 = [_head_mask(j) for j in range(2)]
    m2 = [_head_mask(j, 2 * LANES) for j in range(2)]
    cat0 = lambda xs: jnp.concatenate(xs, axis=0)
    cat1 = lambda xs: jnp.concatenate(xs, axis=1)
    C = CHUNK

    def token_shift(b, main_ref, halo_ref, mu_ref, scr_ref):
        _with_prev_rows(scr_ref.at[b], main_ref.at[b], halo_ref.at[b], keep)
        cur = main_ref[b]
        prev = scr_ref[b, HALO - 1:HALO - 1 + CHUNK, :]
        return cur + (prev - cur) * mu_ref[...]

    pre = {}
    for b in batches:
        r = token_shift(b, r_ref, rh_ref, mur_ref, sr_ref)
        k = token_shift(b, k_ref, kh_ref, muk_ref, sk_ref)
        v = token_shift(b, v_ref, vh_ref, muv_ref, sv_ref)
        lo = token_shift(b, lo_ref, loh_ref, mulo_ref, slo_ref)
        lo_wa = lo[:, :LANES]
        log_w = -RWKV_DECAY_SCALE * _sigmoid(w0_ref[...] + _mm(jnp.tanh(lo_wa), wup_ref[...]))
        a = _sigmoid(a0_ref[...] + _mm(lo_wa, aup_ref[...]))
        gate = _mm(_sigmoid(lo[:, LANES:]), gup_ref[...])
        kk = k * kkw_ref[...]
        kk = kk / jnp.maximum(jnp.sqrt(_mm(kk * kk, bd)), 1e-12)
        k = k * (1.0 + (a - 1.0) * kaw_ref[...])
        cum = _mm_split(tri, log_w, 2)
        mid = cum[C // 2 - 1:C // 2, :]
        last = cum[C - 1:C, :]
        inv_p = jnp.exp(mid - cum)
        to_end = jnp.exp(last - mid)
        pre[b] = dict(
            r=r, k=k, v=v, gate=gate,
            a_t=-kk * jnp.exp(cum - log_w - mid), r_t=r * jnp.exp(cum - mid),
            b_t=kk * a * inv_p, k_t=k * inv_p, e_mid=jnp.exp(mid),
            b_end=kk * a * inv_p * to_end, k_end=k * inv_p * to_end, decay=jnp.exp(last))

    yield
    units = [(b, q) for b in batches for q in range(npair)]
    slab = lambda name: [pre[b][name][:, q * LANES:(q + 1) * LANES] for b, q in units]
    a_t, r_t, b_t, k_t, v_u = slab("a_t"), slab("r_t"), slab("b_t"), slab("k_t"), slab("v")
    e_mid, b_end, k_end, decay = slab("e_mid"), slab("b_end"), slab("k_end"), slab("decay")

    mm = [_mm_nt(cat0([a * m1[0], r * m1[0], a * m1[1], r * m1[1]]), cat0([bt, kt]))
          for a, r, bt, kt in zip(a_t, r_t, b_t, k_t)]
    chains = [(u, j) for u in range(len(units)) for j in range(2)]
    m_ab = [jnp.where(strict, mm[u][2 * j * C:(2 * j + 1) * C, :C], 0.0) for u, j in chains]
    m_ak = [jnp.where(strict, mm[u][2 * j * C:(2 * j + 1) * C, C:], 0.0).astype(BF16)
            for u, j in chains]
    m_rb = [jnp.where(causal, mm[u][(2 * j + 1) * C:(2 * j + 2) * C, :C], 0.0).astype(BF16)
            for u, j in chains]
    m_rk = [jnp.where(causal, mm[u][(2 * j + 1) * C:(2 * j + 2) * C, C:], 0.0).astype(BF16)
            for u, j in chains]
    yield

    inv = [eye + m for m in m_ab]
    power = [m.astype(BF16) for m in m_ab]
    for _ in range(int(math.log2(C)) - 1):
        power = [_mm(p, p).astype(BF16) for p in power]
        inv = [i + _mm(i, p) for i, p in zip(inv, power)]
        yield

    n_units = range(len(units))
    per_head = lambda x2: x2[:C] * m2[0] + x2[C:] * m2[1]
    mv = [_mm(cat0([m_ak[2 * u], m_rk[2 * u], m_ak[2 * u + 1], m_rk[2 * u + 1]]), v_u[u])
          for u in n_units]
    mv_ak = [mv[u][:C] * m1[0] + mv[u][2 * C:3 * C] * m1[1] for u in n_units]
    mv_rk = [mv[u][C:2 * C] * m1[0] + mv[u][3 * C:] * m1[1] for u in n_units]
    z = [per_head(_mm(cat0([inv[2 * u], inv[2 * u + 1]]), cat1([a_t[u], mv_ak[u]])))
         for u in n_units]
    yield
    w = [per_head(_mm(cat0([m_rb[2 * u], m_rb[2 * u + 1]]), z[u])) for u in n_units]
    yield

    s = [s_ref[b, q] for b, q in units]
    uy = [_mm_nt(cat0([z[u][:, :LANES] * e_mid[u], (r_t[u] + w[u][:, :LANES]) * e_mid[u]]), s[u])
          for u in n_units]
    u_in = [uy[u][:C] + z[u][:, LANES:] for u in n_units]
    y = [uy[u][C:] + w[u][:, LANES:] + mv_rk[u] for u in n_units]
    s_inc = [_mm_tn(cat0([u_in[u], v_u[u]]), cat0([b_end[u], k_end[u]])) for u in n_units]
    for u, (b, q) in enumerate(units):
        s_ref[b, q] = s[u] * decay[u] + same_head * s_inc[u]

    for i, b in enumerate(batches):
        yb = cat1([y[i * npair + q] for q in range(npair)])
        mean = _mm(yb, bd) * (1.0 / HEAD_DIM)
        d = yb - mean
        var = _mm(d * d, bd) * (1.0 / HEAD_DIM)
        yn = d * lax.rsqrt(var + RWKV_LN_EPS) * lnw_ref[...] + lnb_ref[...]
        bonus = _mm(pre[b]["r"] * pre[b]["k"] * rkw_ref[...], bd) * pre[b]["v"]
        o_ref[b] = ((yn + bonus) * pre[b]["gate"]).astype(BF16)


FOX_TILE = 2048
FOX_Q_SPLIT = 8
FOX_GUARD = 64.0
LOG2E = math.log2(math.e)


def _fox_prep_chunk(b, q_ref, k_ref, v_ref, f_ref, qw_ref, kw_ref, fb_ref, bd_ref, spread_ref,
                    qo_ref, ko_ref, vo_ref, carry_ref):
    bd = bd_ref[...]
    q, k, v = q_ref[b], k_ref[b], v_ref[b]
    q_ms = _mm(q * q, bd) * (1.0 / HEAD_DIM)
    k_ms = _mm(k * k, bd) * (1.0 / HEAD_DIM)

    f = f_ref[b] + fb_ref[...]
    log_f = jnp.minimum(f, 0.0) - jnp.log1p(jnp.exp(-jnp.abs(f)))
    rows = CHUNK
    tri = (_iota2((rows, rows), 0) >= _iota2((rows, rows), 1)).astype(BF16)
    cum = _mm_split(tri, log_f, 3) + carry_ref[b, 0:1, :]
    carry_ref[b] = jnp.broadcast_to(cum[rows - 1:rows, :], carry_ref.shape[1:])
    yield
    q = q * lax.rsqrt(q_ms + EPS) * qw_ref[...] * (HEAD_DIM ** -0.5 * LOG2E)
    k = k * lax.rsqrt(k_ms + EPS) * kw_ref[...]
    c_all = _mm_split_rhs(cum * LOG2E, spread_ref[...], 3)
    yield

    lane = _iota2((1, LANES), 1)
    for h in range(N_HEADS):
        p, j = divmod(h, 2)
        ps = slice(p * LANES, (p + 1) * LANES)
        mj = _head_mask(j)
        qh, kh = q[:, ps] * mj, k[:, ps] * mj
        if j == 1:
            qh = pltpu.roll(qh, HEAD_DIM, 1)
            kh = pltpu.roll(kh, HEAD_DIM, 1)
        ch = c_all[:, h * LANES:(h + 1) * LANES]
        hi = ch.astype(BF16).astype(F32)
        mid = (ch - hi).astype(BF16).astype(F32)
        low = ch - hi - mid
        piece = lambda base: jnp.where(lane == base, hi,
                                       jnp.where(lane == base + 1, mid,
                                                 jnp.where(lane == base + 2, low, 0.0)))
        ones = lambda base: ((lane >= base) & (lane < base + 3)).astype(F32)
        qo_ref[b, h] = (qh + piece(HEAD_DIM) + ones(HEAD_DIM + 3)).astype(BF16)
        ko_ref[b, h] = (kh + ones(HEAD_DIM) - piece(HEAD_DIM + 3)).astype(BF16)
    row_head = _iota2((LANES, 1), 0) // HEAD_DIM
    row_in_head = _iota2((LANES, 1), 0) % HEAD_DIM
    for p in range(N_HEADS // 2):
        v_t = v[:, p * LANES:(p + 1) * LANES].T
        for j in range(2):
            ones_row = ((row_head != j) & (row_in_head == 0)).astype(F32)
            vo_ref[b, 2 * p + j] = jnp.where(row_head == j, v_t, ones_row).astype(BF16)


RWKV_INPUTS = 23
FOX_PREP_INPUTS = 9


def _rwkv_fox_prep_kernel(*refs):
    rwkv_in = refs[:RWKV_INPUTS]
    prep_in = refs[RWKV_INPUTS:RWKV_INPUTS + FOX_PREP_INPUTS]
    (rwkv_out, qo_ref, ko_ref, vo_ref,
     s_ref, sr_ref, sk_ref, sv_ref, slo_ref, carry_ref) = refs[RWKV_INPUTS + FOX_PREP_INPUTS:]
    c = pl.program_id(0)

    @pl.when(c == 0)
    def _():
        s_ref[...] = jnp.zeros_like(s_ref)
        carry_ref[...] = jnp.zeros_like(carry_ref)

    keep = (c > 0).astype(F32)
    def after(stages, program):
        for _ in range(stages):
            yield
        yield from program

    batches = list(range(carry_ref.shape[0]))
    programs = [_rwkv_program(batches, keep, *rwkv_in, rwkv_out, s_ref, sr_ref, sk_ref, sv_ref,
                              slo_ref)]
    for b in batches:
        prep = _fox_prep_chunk(b, *prep_in, qo_ref, ko_ref, vo_ref, carry_ref)
        programs.append(after(2 + 2 * b, prep))
    _interleave(programs)


def _rwkv_fox_prep(r, k, v, lo, mu, w0, w_up, a0, a_up, g_up, k_k, k_a, r_k, ln_w, ln_b,
                   fq, fk, fv, ff, q_norm_w, k_norm_w, f_bias, batch, seq):
    nc = seq // CHUNK
    lo_width = lo.shape[-1]
    rank = w_up.shape[0]
    row1 = lambda a: a.reshape(1, -1)
    seq3 = lambda a: a.reshape(batch, seq, a.shape[-1])

    wup_pad = jnp.concatenate([w_up, jnp.zeros((LANES - rank, WIDTH), F32)], axis=0).astype(BF16)
    aup_pad = jnp.concatenate([jnp.zeros((LANES - rank, WIDTH), F32), a_up], axis=0).astype(BF16)
    head = jnp.arange(WIDTH) // HEAD_DIM
    bd = (head[:, None] == head[None, :]).astype(BF16)
    r, k, v, lo = seq3(r), seq3(k), seq3(v), seq3(lo)
    rwkv_args = (r, r, k, k, v, v, lo, lo,
                 row1(mu[:WIDTH]), row1(mu[WIDTH:2 * WIDTH]), row1(mu[2 * WIDTH:3 * WIDTH]),
                 row1(mu[3 * WIDTH:]),
                 row1(w0), wup_pad, row1(a0), aup_pad, g_up.astype(BF16),
                 row1(k_k), row1(k_a), row1(r_k), row1(ln_w), row1(ln_b), bd)
    wide, wide_halo = _seq_spec(batch, WIDTH), _seq_halo_spec(batch, WIDTH)
    param = _param_spec1((1, WIDTH))
    weight = _param_spec1((LANES, WIDTH))
    rwkv_specs = [wide, wide_halo, wide, wide_halo, wide, wide_halo,
                  _seq_spec(batch, lo_width), _seq_halo_spec(batch, lo_width),
                  param, param, param, _param_spec1((1, lo_width)),
                  param, weight, param, weight, weight,
                  param, param, param, param, param, _param_spec1((WIDTH, WIDTH))]

    dst = jnp.arange(N_HEADS * LANES)
    spread = ((dst[None, :] // LANES == jnp.arange(LANES)[:, None])
              & (dst[None, :] % LANES >= HEAD_DIM) & (dst[None, :] % LANES < HEAD_DIM + 6)).astype(BF16)
    prep_args = (seq3(fq), seq3(fk), seq3(fv), seq3(ff),
                 row1(jnp.tile(q_norm_w, N_HEADS)), row1(jnp.tile(k_norm_w, N_HEADS)),
                 row1(jnp.pad(f_bias, (0, LANES - N_HEADS))), bd, spread)
    prep_specs = [wide, wide, wide, _seq_spec(batch, LANES), param, param,
                  _param_spec1((1, LANES)), _param_spec1((WIDTH, WIDTH)),
                  _param_spec1((LANES, N_HEADS * LANES))]
    assert len(rwkv_args) == RWKV_INPUTS and len(prep_args) == FOX_PREP_INPUTS

    head_out = pl.BlockSpec((batch, N_HEADS, CHUNK, LANES), lambda c: (0, 0, c, 0))
    head_shape = jax.ShapeDtypeStruct((batch, N_HEADS, seq, LANES), BF16)
    vt_out = pl.BlockSpec((batch, N_HEADS, LANES, CHUNK), lambda c: (0, 0, 0, c))
    vt_shape = jax.ShapeDtypeStruct((batch, N_HEADS, LANES, seq), BF16)
    shifted = lambda width: pltpu.VMEM((batch, CHUNK + HALO, width), F32)
    y_rwkv, qa, ka, va = pl.pallas_call(
        _rwkv_fox_prep_kernel,
        grid=(nc,),
        in_specs=rwkv_specs + prep_specs,
        out_specs=[wide, head_out, head_out, vt_out],
        out_shape=[jax.ShapeDtypeStruct((batch, seq, WIDTH), BF16), head_shape, head_shape,
                   vt_shape],
        scratch_shapes=[pltpu.VMEM((batch, N_HEADS // 2, LANES, LANES), F32),
                        shifted(WIDTH), shifted(WIDTH), shifted(WIDTH), shifted(lo_width),
                        pltpu.VMEM((batch, HALO, LANES), F32)],
        compiler_params=_cparams(("arbitrary",)),
        name="rwkv7_fox_prep",
    )(*rwkv_args, *prep_args)
    return y_rwkv.reshape(batch * seq, WIDTH), qa, ka, va


def _fox_kernel(qi_ref, ki_ref, q_ref, k_ref, vt_ref, o_ref, m_ref, acc_ref, redo_ref):
    t = pl.program_id(2)
    qi = qi_ref[t]
    ki = ki_ref[t]

    part = FOX_TILE // FOX_Q_SPLIT
    units = [(j, slice(h * part, (h + 1) * part)) for j in range(2) for h in range(FOX_Q_SPLIT)]

    def all_scores(on_diagonal):
        n_keys = lambda qs: qs.stop if on_diagonal else FOX_TILE
        raw = [lax.dot_general(k_ref[0, j, :n_keys(qs), :], q_ref[0, j, qs, :],
                               (((1,), (1,)), ((), ())), preferred_element_type=F32)
               for j, qs in units]
        if not on_diagonal:
            return raw
        masked = []
        for (_, qs), s in zip(units, raw):
            key = _iota2(s.shape, 0)
            query = _iota2(s.shape, 1) + qs.start
            masked.append(jnp.where(query >= key, s, NEG))
        return masked

    def pv(j, p):
        return jnp.dot(vt_ref[0, j, :, :p.shape[0]], p.astype(BF16), preferred_element_type=F32)

    def exact_step(on_diagonal):
        for (j, qs), s in zip(units, all_scores(on_diagonal)):
            m_prev = m_ref[j, :, qs]
            m_next = jnp.maximum(m_prev, jnp.max(s, axis=0, keepdims=True))
            alpha = jnp.exp2(m_prev - m_next)
            acc_ref[j, :, qs] = alpha * acc_ref[j, :, qs] + pv(j, jnp.exp2(s - m_next))
            m_ref[j, :, qs] = m_next

    def lagged_step(on_diagonal):
        done = []
        worst = None
        for (j, qs), s in zip(units, all_scores(on_diagonal)):
            m_used = m_ref[j, :, qs]
            block_max = jnp.max(s, axis=0, keepdims=True)
            contrib = pv(j, jnp.exp2(s - m_used))
            excess = jnp.max(block_max - m_used)
            worst = excess if worst is None else jnp.maximum(worst, excess)
            done.append((j, qs, m_used, block_max, contrib))
        ok = worst <= FOX_GUARD
        redo_ref[0] = jnp.where(ok, 0, 1)

        @pl.when(ok)
        def _():
            for j, qs, m_used, block_max, contrib in done:
                m_next = jnp.maximum(m_used, block_max)
                acc_ref[j, :, qs] = (acc_ref[j, :, qs] + contrib) * jnp.exp2(m_used - m_next)
                m_ref[j, :, qs] = m_next

    @pl.when(ki == qi)
    def _():
        m_ref[...] = jnp.full_like(m_ref, NEG)
        acc_ref[...] = jnp.zeros_like(acc_ref)
        exact_step(True)

    redo_ref[0] = 0

    @pl.when(ki < qi)
    def _():
        lagged_step(False)

    @pl.when(redo_ref[0] == 1)
    def _():
        exact_step(False)

    @pl.when(ki == 0)
    def _():
        head_rows = _iota2((LANES, 1), 0) // HEAD_DIM
        out_t = jnp.where(head_rows == 0,
                          acc_ref[0] / acc_ref[0, HEAD_DIM:HEAD_DIM + 1, :],
                          acc_ref[1] / acc_ref[1, 0:1, :])
        o_ref[0] = out_t.T.astype(BF16)


def _fox_attention(qa, ka, va, batch, seq):
    nt = seq // FOX_TILE
    pairs = [(i, j) for i in range(nt) for j in range(i, -1, -1)]
    qi = jnp.asarray([i for i, _ in pairs], jnp.int32)
    ki = jnp.asarray([j for _, j in pairs], jnp.int32)
    q_spec = pl.BlockSpec((1, 2, FOX_TILE, LANES), lambda b, p, t, qi, ki: (b, p, qi[t], 0))
    k_spec = pl.BlockSpec((1, 2, FOX_TILE, LANES), lambda b, p, t, qi, ki: (b, p, ki[t], 0))
    vt_spec = pl.BlockSpec((1, 2, LANES, FOX_TILE), lambda b, p, t, qi, ki: (b, p, 0, ki[t]))
    out = pl.pallas_call(
        _fox_kernel,
        grid_spec=pltpu.PrefetchScalarGridSpec(
            num_scalar_prefetch=2,
            grid=(batch, N_HEADS // 2, len(pairs)),
            in_specs=[q_spec, k_spec, vt_spec],
            out_specs=pl.BlockSpec((1, FOX_TILE, LANES), lambda b, p, t, qi, ki: (b, qi[t], p)),
            scratch_shapes=[pltpu.VMEM((2, 1, FOX_TILE), F32),
                            pltpu.VMEM((2, LANES, FOX_TILE), F32),
                            pltpu.SMEM((1,), jnp.int32)]),
        out_shape=jax.ShapeDtypeStruct((batch, seq, WIDTH), BF16),
        compiler_params=_cparams(("parallel", "parallel", "arbitrary")),
        name="fox_attention",
    )(qi, ki, qa, ka, va)
    return out.reshape(batch * seq, WIDTH)


def _pad_cols(w, width):
    return jnp.pad(w, ((0, 0), (0, width - w.shape[1])))


def _even_mixers(x, batch, seq, norm_w, w_in, conv_w, conv_b, dt_bias, a_log, d_skip,
                 ssd_norm_w, ret_norm_w):
    return _even_layer_mixers(x, norm_w, w_in, conv_w, conv_b, dt_bias, a_log, d_skip, ssd_norm_w,
                              ret_norm_w, batch, seq)


def _odd_mixers(x, batch, seq, norm_w, w_in, mu, w0, w_up, a0, a_up, g_up, k_k, k_a, r_k,
                ln_w, ln_b, q_norm_w, k_norm_w, f_bias):
    lo_width = w_up.shape[0] + a_up.shape[0] + g_up.shape[0]
    fox0 = 3 * WIDTH + lo_width
    f0 = fox0 + 3 * WIDTH
    w = jnp.concatenate([w_in[:, :f0], _pad_cols(w_in[:, f0:], LANES)], axis=1).astype(BF16)
    segments = ((0, WIDTH), (WIDTH, WIDTH), (2 * WIDTH, WIDTH), (3 * WIDTH, lo_width),
                (fox0, WIDTH), (fox0 + WIDTH, WIDTH), (fox0 + 2 * WIDTH, WIDTH), (f0, LANES))
    r, k, v, lo, fq, fk, fv, ff = _inproj(x, norm_w, w, segments)
    y_rwkv, qa, ka, va = _rwkv_fox_prep(r, k, v, lo, mu, w0, w_up, a0, a_up, g_up, k_k, k_a,
                                        r_k.reshape(-1), ln_w, ln_b, fq, fk, fv, ff,
                                        q_norm_w, k_norm_w, f_bias, batch, seq)
    return y_rwkv, _fox_attention(qa, ka, va, batch, seq)


def kernel(x, ev_norm_w, ev_w_in, ev_ssd_conv_w, ev_ssd_conv_b, ev_ssd_dt_bias, ev_ssd_a_log,
           ev_ssd_d, ev_ssd_norm_w, ev_ret_norm_w, ev_w_out,
           od_norm_w, od_w_in, od_rwkv_mu, od_rwkv_w0, od_rwkv_w_up, od_rwkv_a0, od_rwkv_a_up,
           od_rwkv_g_up, od_rwkv_k_k, od_rwkv_k_a, od_rwkv_r_k, od_rwkv_ln_w, od_rwkv_ln_b,
           od_fox_q_norm_w, od_fox_k_norm_w, od_fox_f_bias, od_w_out,
           ffn_norm_w, ffn_w_up, ffn_conv_w, ffn_conv_b, ffn_w_down):
    batch, seq, _ = x.shape
    depth = ffn_norm_w.shape[0]
    h = x.reshape(batch * seq, D_MODEL)
    w_up_all = ffn_w_up.astype(BF16)
    w_down_all = ffn_w_down.astype(BF16)
    for layer in range(depth):
        i = layer // 2
        if layer % 2 == 0:
            ya, yb = _even_mixers(h, batch, seq, ev_norm_w[i], ev_w_in[i], ev_ssd_conv_w[i],
                                  ev_ssd_conv_b[i], ev_ssd_dt_bias[i], ev_ssd_a_log[i],
                                  ev_ssd_d[i], ev_ssd_norm_w[i], ev_ret_norm_w[i])
            w_out = ev_w_out[i]
        else:
            ya, yb = _odd_mixers(h, batch, seq, od_norm_w[i], od_w_in[i], od_rwkv_mu[i],
                                 od_rwkv_w0[i], od_rwkv_w_up[i], od_rwkv_a0[i], od_rwkv_a_up[i],
                                 od_rwkv_g_up[i], od_rwkv_k_k[i], od_rwkv_k_a[i], od_rwkv_r_k[i],
                                 od_rwkv_ln_w[i], od_rwkv_ln_b[i], od_fox_q_norm_w[i],
                                 od_fox_k_norm_w[i], od_fox_f_bias[i])
            w_out = od_w_out[i]
        h = _proj_ffn(ya, yb, h, seq, w_out, ffn_norm_w[layer], layer, w_up_all,
                      ffn_conv_w[layer], ffn_conv_b[layer], w_down_all)
    return h.reshape(batch, seq, D_MODEL)
```

```python
import functools
import itertools
import math

import jax
import jax.numpy as jnp
from jax import lax
from jax.experimental import pallas as pl
from jax.experimental.pallas import tpu as pltpu

F32 = jnp.float32
BF16 = jnp.bfloat16

D_MODEL = 1024
HEAD_DIM = 64
N_HEADS = 8
WIDTH = N_HEADS * HEAD_DIM
LANES = 128
CHUNK = 128
HALO = 8
EPS = 1e-6
SSD_CONV = 4
SSD_STATE = 64
RWKV_LN_EPS = 64e-5
RWKV_DECAY_SCALE = 0.606531
D_FF = 2816
FF_TILE = 256
FFN_HALO = 16
FFN_LOOKAHEAD = 2
NEG = -1e30
VMEM_LIMIT = 56 * 1024 * 1024


def _mm(a, b):
    return jnp.dot(a.astype(BF16), b.astype(BF16), preferred_element_type=F32)


def _mm_nt(a, b):
    return lax.dot_general(a.astype(BF16), b.astype(BF16), (((1,), (1,)), ((), ())),
                           preferred_element_type=F32)


def _mm_tn(a, b):
    return lax.dot_general(a.astype(BF16), b.astype(BF16), (((0,), (0,)), ((), ())),
                           preferred_element_type=F32)


def _mm_split(lhs01, x, pieces):
    acc = None
    rest = x
    for _ in range(pieces):
        part = rest.astype(BF16)
        term = jnp.dot(lhs01, part, preferred_element_type=F32)
        acc = term if acc is None else acc + term
        rest = rest - part.astype(F32)
    return acc


def _mm_split_rhs(x, rhs01, pieces):
    acc = None
    rest = x
    for _ in range(pieces):
        part = rest.astype(BF16)
        term = jnp.dot(part, rhs01, preferred_element_type=F32)
        acc = term if acc is None else acc + term
        rest = rest - part.astype(F32)
    return acc


def _interleave(stage_generators):
    for _ in itertools.zip_longest(*stage_generators):
        pass


def _sigmoid(x):
    return 1.0 / (1.0 + jnp.exp(-x))


def _silu(x):
    return x * _sigmoid(x)


def _softplus(x):
    return jnp.maximum(x, 0.0) + jnp.log1p(jnp.exp(-jnp.abs(x)))


def _iota2(shape, dim):
    return lax.broadcasted_iota(jnp.int32, shape, dim)


def _head_mask(j, width=LANES):
    lane = _iota2((1, width), 1)
    return ((lane % LANES) // HEAD_DIM == j).astype(F32)


def _cparams(sem):
    return pltpu.CompilerParams(dimension_semantics=sem, vmem_limit_bytes=VMEM_LIMIT)


def _const_spec(shape):
    nd = len(shape)
    return pl.BlockSpec(shape, lambda *_: (0,) * nd, pipeline_mode=pl.Buffered(1))


def _inproj_kernel(x_ref, nw_ref, w_ref, *o_refs, segments):
    x = x_ref[...]
    ms = jnp.mean(x * x, axis=-1, keepdims=True)
    xn = (x * lax.rsqrt(ms + EPS) * nw_ref[...]).astype(BF16)
    for o_ref, (off, width) in zip(o_refs, segments):
        o_ref[...] = jnp.dot(xn, w_ref[:, off:off + width], preferred_element_type=F32)


def _inproj(x, norm_w, w, segments, tm=1024):
    t = x.shape[0]
    n = w.shape[1]
    return pl.pallas_call(
        functools.partial(_inproj_kernel, segments=segments),
        grid=(t // tm,),
        in_specs=[pl.BlockSpec((tm, D_MODEL), lambda i: (i, 0)),
                  _const_spec((1, D_MODEL)),
                  _const_spec((D_MODEL, n))],
        out_specs=[pl.BlockSpec((tm, wd), lambda i: (i, 0)) for _, wd in segments],
        out_shape=[jax.ShapeDtypeStruct((t, wd), F32) for _, wd in segments],
        compiler_params=_cparams(("parallel",)),
        name="inproj",
    )(x, norm_w.reshape(1, D_MODEL), w)


def _proj_ffn_kernel(ya_ref, yah_ref, yb_ref, ybh_ref, x_ref, xh_ref, wo_ref, nw_ref,
                     wup_ref, cw_ref, cb_ref, wd_ref, o_ref,
                     ya_ext, yb_ext, h_ref, hn_ref, g_ref, acc_ref, *, tm, tiles_per_seq):
    ya_ext[:FFN_HALO, :] = yah_ref[...]
    ya_ext[FFN_HALO:, :] = ya_ref[...]
    yb_ext[:FFN_HALO, :] = ybh_ref[...]
    yb_ext[FFN_HALO:, :] = yb_ref[...]
    h_ref[:FFN_HALO, :] = xh_ref[...]
    h_ref[FFN_HALO:, :] = x_ref[...]
    h_ref[...] += (jnp.dot(ya_ext[...], wo_ref[:WIDTH, :], preferred_element_type=F32)
                   + jnp.dot(yb_ext[...], wo_ref[WIDTH:, :], preferred_element_type=F32))

    h = h_ref[...]
    ms = jnp.mean(h * h, axis=-1, keepdims=True)
    hn = h * lax.rsqrt(ms + EPS) * nw_ref[...]
    first_of_seq = pl.program_id(0) % tiles_per_seq == 0
    halo_rows = _iota2((tm + FFN_HALO, 1), 0) < FFN_HALO
    hn_ref[...] = jnp.where(halo_rows & first_of_seq, 0.0, hn).astype(BF16)
    acc_ref[...] = jnp.zeros_like(acc_ref)

    def tile(f, base=0):
        return slice(base + f * FF_TILE, base + (f + 1) * FF_TILE)

    n_buf = g_ref.shape[0]

    def up_proj(f):
        g_ref[f % n_buf] = jnp.dot(hn_ref[...], wup_ref[:, tile(f)], preferred_element_type=F32)
        return jnp.dot(hn_ref[FFN_HALO:, :], wup_ref[:, tile(f, D_FF)],
                       preferred_element_type=F32)

    nf = D_FF // FF_TILE
    ups = {f: up_proj(f) for f in range(FFN_LOOKAHEAD)}
    for f in range(nf):
        if f + FFN_LOOKAHEAD < nf:
            ups[f + FFN_LOOKAHEAD] = up_proj(f + FFN_LOOKAHEAD)
        up = ups.pop(f)
        g = g_ref.at[f % n_buf]
        cw = cw_ref[:, tile(f)]
        gate = (cb_ref[:, tile(f)]
                + cw[0:1, :] * g[FFN_HALO - 2:FFN_HALO - 2 + tm, :]
                + cw[1:2, :] * g[FFN_HALO - 1:FFN_HALO - 1 + tm, :]
                + cw[2:3, :] * g[FFN_HALO:, :])
        act = (_silu(gate) * up).astype(BF16)
        acc_ref[...] += jnp.dot(act, wd_ref[tile(f), :], preferred_element_type=F32)
    o_ref[...] = h_ref[FFN_HALO:, :] + acc_ref[...]


def _proj_ffn(ya, yb, x, seq, w_out, norm_w, layer, w_up_all, conv_w, conv_b, w_down_all, tm=512):
    t = x.shape[0]
    halo_per_tile = tm // FFN_HALO
    row_spec = lambda width: pl.BlockSpec((tm, width), lambda i: (i, 0))
    halo_spec = lambda width: pl.BlockSpec(
        (FFN_HALO, width), lambda i: (jnp.maximum(i * halo_per_tile - 1, 0), 0))
    ext = lambda width, dtype: pltpu.VMEM((tm + FFN_HALO, width), dtype)
    layer_spec = lambda shape: pl.BlockSpec((None,) + shape, lambda i: (layer, 0, 0),
                                            pipeline_mode=pl.Buffered(1))
    return pl.pallas_call(
        functools.partial(_proj_ffn_kernel, tm=tm, tiles_per_seq=seq // tm),
        grid=(t // tm,),
        in_specs=[row_spec(WIDTH), halo_spec(WIDTH), row_spec(WIDTH), halo_spec(WIDTH),
                  row_spec(D_MODEL), halo_spec(D_MODEL),
                  _const_spec((2 * WIDTH, D_MODEL)), _const_spec((1, D_MODEL)),
                  layer_spec((D_MODEL, 2 * D_FF)), _const_spec((3, D_FF)), _const_spec((1, D_FF)),
                  layer_spec((D_FF, D_MODEL))],
        out_specs=row_spec(D_MODEL),
        out_shape=jax.ShapeDtypeStruct((t, D_MODEL), F32),
        scratch_shapes=[ext(WIDTH, BF16), ext(WIDTH, BF16), ext(D_MODEL, F32), ext(D_MODEL, BF16),
                        pltpu.VMEM((FFN_LOOKAHEAD + 1, tm + FFN_HALO, FF_TILE), F32),
                        pltpu.VMEM((tm, D_MODEL), F32)],
        compiler_params=_cparams(("parallel",)),
        name="proj_convffn",
    )(ya, ya, yb, yb, x, x, w_out.astype(BF16), norm_w.reshape(1, D_MODEL),
      w_up_all, conv_w, conv_b.reshape(1, D_FF), w_down_all)


def _seq_spec(batch, width):
    return pl.BlockSpec((batch, CHUNK, width), lambda c: (0, c, 0))


def _seq_halo_spec(batch, width):
    per = CHUNK // HALO
    return pl.BlockSpec((batch, HALO, width), lambda c: (0, jnp.maximum(c * per - 1, 0), 0))


def _param_spec1(shape):
    return pl.BlockSpec(shape, lambda c: (0,) * len(shape), pipeline_mode=pl.Buffered(1))


def _with_prev_rows(scr_ref, main_ref, halo_ref, keep):
    scr_ref[:HALO, :] = halo_ref[...] * keep
    scr_ref[HALO:, :] = main_ref[...]


def _ssd_chunk(b, keep, z_ref, x_ref, xh_ref, bc_ref, bch_ref, dt_ref,
               cwx_ref, cbx_ref, cwb_ref, cbb_ref, dtb_ref, alog_ref, dexp_ref, nw_ref, e_ref,
               o_ref, xe_ref, bce_ref, st_ref, y_ref):
    _with_prev_rows(xe_ref.at[b], x_ref.at[b], xh_ref.at[b], keep)
    _with_prev_rows(bce_ref.at[b], bc_ref.at[b], bch_ref.at[b], keep)

    def conv(ref, cw_ref, cb_ref):
        out = cb_ref[...]
        for k in range(SSD_CONV):
            lo = HALO - (SSD_CONV - 1) + k
            out = out + cw_ref[k:k + 1, :] * ref[b, lo:lo + CHUNK, :]
        return _silu(out)

    x = conv(xe_ref, cwx_ref, cbx_ref)
    bc = conv(bce_ref, cwb_ref, cbb_ref)
    bm = bc[:, :LANES]
    cm = bc[:, LANES:]

    row = _iota2((CHUNK, CHUNK), 0)
    col = _iota2((CHUNK, CHUNK), 1)
    causal = row >= col

    dt = _softplus(dt_ref[b] + dtb_ref[...])
    a = dt * (-jnp.exp(alog_ref[...]))
    a_cum = _mm_split(causal.astype(BF16), a, 3)
    yield
    a_cum_t = a_cum.T
    acausal_penalty = jnp.where(causal, 0.0, NEG)
    a_last = a_cum[CHUNK - 1:CHUNK, :]
    per_head = jnp.concatenate(
        [dt, jnp.exp(a_cum), jnp.exp(a_last - a_cum),
         jnp.broadcast_to(jnp.exp(a_last), (HALO, LANES))], axis=0)
    per_lane = _mm_split_rhs(per_head, e_ref[...], 2)
    dt_e = per_lane[:CHUNK]
    ea_e = per_lane[CHUNK:2 * CHUNK]
    te_e = per_lane[2 * CHUNK:3 * CHUNK]
    cd_e = per_lane[3 * CHUNK:3 * CHUNK + 1]
    yield

    xdt = x * dt_e
    xs = xdt * te_e
    bm_t = bm.T
    group_rows = _iota2((LANES, 1), 0) // SSD_STATE

    for g in range(2):
        cg = cm * _head_mask(g)
        cb = _mm_nt(cg, bm)
        gs = slice(g * 2 * LANES, (g + 1) * 2 * LANES)
        st = st_ref[b, g]
        y_off = _mm(cg, st) * ea_e[:, gs]
        st_new = _mm(bm_t, xs[:, gs])
        st_ref[b, g] = st * cd_e[:, gs] + jnp.where(group_rows == g, st_new, 0.0)
        yield
        for pp in range(2):
            p = 2 * g + pp
            xp = xdt[:, p * LANES:(p + 1) * LANES]
            yp = y_off[:, pp * LANES:(pp + 1) * LANES]
            for j in range(2):
                h = 2 * p + j
                seg = a_cum[:, h:h + 1] - a_cum_t[h:h + 1, :]
                decay = jnp.exp(seg + acausal_penalty)
                yp = yp + _mm(cb * decay, xp * _head_mask(j))
            y_ref[b, :, p * LANES:(p + 1) * LANES] = yp
            yield

    y = y_ref[b] + dexp_ref[...] * x
    gated = y * _silu(z_ref[b])
    ms = jnp.mean(gated * gated, axis=-1, keepdims=True)
    o_ref[b] = (gated * lax.rsqrt(ms + EPS) * nw_ref[...]).astype(BF16)


def _ret_chunk(b, q_ref, k_ref, v_ref, g_ref, cos_ref, sin_ref, intra_ref, qdec_ref, kte_ref,
               cg_ref, bd_ref, nw_ref, o_ref, r_ref):
    lane = _iota2((1, WIDTH), 1)
    first_half = (lane % HEAD_DIM) < (HEAD_DIM // 2)
    cos = jnp.concatenate([cos_ref[...]] * (WIDTH // LANES), axis=1)
    sin = jnp.concatenate([sin_ref[...]] * (WIDTH // LANES), axis=1)

    def rotary(v):
        other = jnp.where(first_half,
                          pltpu.roll(v, WIDTH - HEAD_DIM // 2, 1),
                          pltpu.roll(v, HEAD_DIM // 2, 1))
        return v * cos + other * sin

    q = rotary(q_ref[b])
    k = rotary(k_ref[b]) * (HEAD_DIM ** -0.5)
    v = v_ref[b]
    bd = bd_ref[...]
    pair = lambda t, p: t[:, p * LANES:(p + 1) * LANES]
    heads = [(p, j) for p in range(N_HEADS // 2) for j in range(2)]
    masks = [_head_mask(j) for j in range(2)]

    y_cross = _mm(q * qdec_ref[...], r_ref[b])
    r_ref[b] = r_ref[b] * cg_ref[...] + bd * _mm_tn(k * kte_ref[...], v)
    scores = [_mm_nt(pair(q, p) * masks[j], pair(k, p)) for p, j in heads]
    yield
    y_pairs = []
    for p in range(N_HEADS // 2):
        yp = pair(y_cross, p)
        for j in range(2):
            h = 2 * p + j
            yp = yp + _mm(scores[h] * intra_ref[h], pair(v, p) * masks[j])
        y_pairs.append(yp)
    yield
    y = jnp.concatenate(y_pairs, axis=1)
    ms = _mm(y * y, bd) * (1.0 / HEAD_DIM)
    yield
    o_ref[b] = (_silu(g_ref[b]) * (y * lax.rsqrt(ms + EPS) * nw_ref[...])).astype(BF16)


def _retention_tables(seq):
    half = HEAD_DIM // 2
    inv = 1.0 / (10000.0 ** (jnp.arange(half, dtype=F32) / half))
    ang = jnp.arange(seq, dtype=F32)[:, None] * inv[None, :]
    cos = jnp.tile(jnp.concatenate([jnp.cos(ang), jnp.cos(ang)], axis=-1), (1, LANES // HEAD_DIM))
    sin = jnp.tile(jnp.concatenate([-jnp.sin(ang), jnp.sin(ang)], axis=-1), (1, LANES // HEAD_DIM))
    log_gamma = jnp.log1p(-(2.0 ** (-5.0 - jnp.arange(N_HEADS, dtype=F32))))
    idx = jnp.arange(CHUNK, dtype=F32)
    rel = idx[:, None] - idx[None, :]
    intra = jnp.where(rel[None] >= 0,
                      jnp.exp(jnp.maximum(rel, 0.0)[None] * log_gamma[:, None, None]), 0.0)
    expand = lambda m: jnp.repeat(m, HEAD_DIM, axis=1)
    qdec = expand(jnp.exp((idx + 1.0)[:, None] * log_gamma[None, :]))
    kte = expand(jnp.exp((CHUNK - 1 - idx)[:, None] * log_gamma[None, :]))
    cgam = expand(jnp.exp(CHUNK * log_gamma)[None, :])
    head = jnp.arange(WIDTH) // HEAD_DIM
    bd = (head[:, None] == head[None, :]).astype(BF16)
    return cos, sin, intra, qdec, kte, cgam, bd


EVEN_SEGMENTS = {"z": (0, WIDTH), "x": (WIDTH, WIDTH), "bc": (2 * WIDTH, 2 * LANES),
                 "q": (2 * WIDTH + 2 * LANES, WIDTH), "k": (3 * WIDTH + 2 * LANES, WIDTH),
                 "v": (4 * WIDTH + 2 * LANES, WIDTH), "g": (5 * WIDTH + 2 * LANES, WIDTH),
                 "dt": (6 * WIDTH + 2 * LANES, LANES)}
EVEN_COLS = 6 * WIDTH + 3 * LANES
SSD_PARAMS = 9
RET_PARAMS = 8


def _even_layer_kernel(x_ref, nw_ref, w_ref, *rest):
    ssd_par = rest[:SSD_PARAMS]
    ret_par = rest[SSD_PARAMS:SSD_PARAMS + RET_PARAMS]
    (ssd_out, ret_out, buf_a, buf_b, halo_x, halo_bc,
     xe_ref, bce_ref, st_ref, y_ref, r_ref) = rest[SSD_PARAMS + RET_PARAMS:]
    c = pl.program_id(0)
    nb = x_ref.shape[0]

    @pl.when(c == 0)
    def _():
        buf_a[...] = jnp.zeros_like(buf_a)
        buf_b[...] = jnp.zeros_like(buf_b)
        halo_x[...] = jnp.zeros_like(halo_x)
        halo_bc[...] = jnp.zeros_like(halo_bc)

    @pl.when(c <= 1)
    def _():
        st_ref[...] = jnp.zeros_like(st_ref)
        r_ref[...] = jnp.zeros_like(r_ref)

    keep = (c > 1).astype(F32)

    def step(read_buf, write_buf):
        def project():
            x = x_ref[...].reshape(nb * CHUNK, D_MODEL)
            ms = jnp.mean(x * x, axis=-1, keepdims=True)
            xn = (x * lax.rsqrt(ms + EPS) * nw_ref[...]).astype(BF16)
            for off, width in EVEN_SEGMENTS.values():
                res = jnp.dot(xn, w_ref[:, off:off + width], preferred_element_type=F32)
                for b in range(nb):
                    write_buf[b, :, off:off + width] = res[b * CHUNK:(b + 1) * CHUNK]
                yield

        def view(name):
            off, width = EVEN_SEGMENTS[name]
            return read_buf.at[:, :, off:off + width]

        programs = [project()]
        for b in range(nb):
            programs.append(_ssd_chunk(b, keep, view("z"), view("x"), halo_x, view("bc"), halo_bc,
                                       view("dt"), *ssd_par, ssd_out, xe_ref, bce_ref, st_ref,
                                       y_ref))
            programs.append(_ret_chunk(b, view("q"), view("k"), view("v"), view("g"), *ret_par,
                                       ret_out, r_ref))
        _interleave(programs)
        halo_x[...] = view("x")[:, CHUNK - HALO:, :]
        halo_bc[...] = view("bc")[:, CHUNK - HALO:, :]

    @pl.when(c % 2 == 0)
    def _():
        step(buf_b, buf_a)

    @pl.when(c % 2 == 1)
    def _():
        step(buf_a, buf_b)


def _even_layer_mixers(x, norm_w, w, conv_w, conv_b, dt_bias, a_log, d_skip, ssd_norm_w,
                       ret_norm_w, batch, seq):
    nc = seq // CHUNK
    pad = LANES - N_HEADS
    e = jnp.repeat(jnp.eye(LANES, N_HEADS, dtype=BF16), HEAD_DIM, axis=1)
    ssd_args = (conv_w[:, :WIDTH], conv_b[:WIDTH].reshape(1, WIDTH),
                conv_w[:, WIDTH:], conv_b[WIDTH:].reshape(1, 2 * LANES),
                jnp.pad(dt_bias, (0, pad)).reshape(1, LANES),
                jnp.pad(a_log, (0, pad)).reshape(1, LANES),
                jnp.repeat(d_skip, HEAD_DIM).reshape(1, WIDTH),
                ssd_norm_w.reshape(1, WIDTH), e)
    ssd_specs = [_param_spec1((SSD_CONV, WIDTH)), _param_spec1((1, WIDTH)),
                 _param_spec1((SSD_CONV, 2 * LANES)), _param_spec1((1, 2 * LANES)),
                 _param_spec1((1, LANES)), _param_spec1((1, LANES)),
                 _param_spec1((1, WIDTH)), _param_spec1((1, WIDTH)),
                 _param_spec1((LANES, WIDTH))]
    cos, sin, intra, qdec, kte, cgam, bd = _retention_tables(seq)
    mixed_chunk = lambda c: jnp.maximum(c - 1, 0)
    table_spec = pl.BlockSpec((CHUNK, LANES), lambda c: (mixed_chunk(c), 0))
    ret_args = (cos, sin, intra, qdec, kte, cgam, bd, ret_norm_w.reshape(1, WIDTH))
    ret_specs = [table_spec, table_spec, _param_spec1((N_HEADS, CHUNK, CHUNK)),
                 _param_spec1((CHUNK, WIDTH)), _param_spec1((CHUNK, WIDTH)),
                 _param_spec1((1, WIDTH)), _param_spec1((WIDTH, WIDTH)), _param_spec1((1, WIDTH))]
    assert len(ssd_args) == SSD_PARAMS and len(ret_args) == RET_PARAMS
    out_spec = pl.BlockSpec((batch, CHUNK, WIDTH), lambda c: (0, mixed_chunk(c), 0))
    out_shape = jax.ShapeDtypeStruct((batch, seq, WIDTH), BF16)
    proj_buf = pltpu.VMEM((batch, CHUNK, EVEN_COLS), F32)
    y_ssd, y_ret = pl.pallas_call(
        _even_layer_kernel,
        grid=(nc + 1,),
        in_specs=[pl.BlockSpec((batch, CHUNK, D_MODEL), lambda c: (0, jnp.minimum(c, nc - 1), 0)),
                  _param_spec1((1, D_MODEL)), _param_spec1((D_MODEL, EVEN_COLS))]
                 + ssd_specs + ret_specs,
        out_specs=[out_spec, out_spec],
        out_shape=[out_shape, out_shape],
        scratch_shapes=[proj_buf, proj_buf,
                        pltpu.VMEM((batch, HALO, WIDTH), F32),
                        pltpu.VMEM((batch, HALO, 2 * LANES), F32),
                        pltpu.VMEM((batch, CHUNK + HALO, WIDTH), F32),
                        pltpu.VMEM((batch, CHUNK + HALO, 2 * LANES), F32),
                        pltpu.VMEM((batch, 2, LANES, 2 * LANES), F32),
                        pltpu.VMEM((batch, CHUNK, WIDTH), F32),
                        pltpu.VMEM((batch, WIDTH, WIDTH), F32)],
        compiler_params=_cparams(("arbitrary",)),
        name="even_layer_mixers",
    )(x.reshape(batch, seq, D_MODEL), norm_w.reshape(1, D_MODEL), w, *ssd_args, *ret_args)
    return y_ssd.reshape(batch * seq, WIDTH), y_ret.reshape(batch * seq, WIDTH)


def _rwkv_program(batches, keep, r_ref, rh_ref, k_ref, kh_ref, v_ref, vh_ref, lo_ref, loh_ref,
                  mur_ref, muk_ref, muv_ref, mulo_ref, w0_ref, wup_ref, a0_ref, aup_ref, gup_ref,
                  kkw_ref, kaw_ref, rkw_ref, lnw_ref, lnb_ref, bd_ref,
                  o_ref, s_ref, sr_ref, sk_ref, sv_ref, slo_ref):
    width = r_ref.shape[2]
    npair = width // LANES
    bd = bd_ref[...]
    row = _iota2((CHUNK, CHUNK), 0)
    col = _iota2((CHUNK, CHUNK), 1)
    strict = row > col
    causal = row >= col
    eye = (row == col).astype(F32)
    tri = causal.astype(BF16)
    same_head = (row // HEAD_DIM == col // HEAD_DIM).astype(F32)
    m1 = [_head_mask(j) for j in range(2)]
    m2 = [_head_mask(j, 2 * LANES) for j in range(2)]
    cat0 = lambda xs: jnp.concatenate(xs, axis=0)
    cat1 = lambda xs: jnp.concatenate(xs, axis=1)
    C = CHUNK

    def token_shift(b, main_ref, halo_ref, mu_ref, scr_ref):
        _with_prev_rows(scr_ref.at[b], main_ref.at[b], halo_ref.at[b], keep)
        cur = main_ref[b]
        prev = scr_ref[b, HALO - 1:HALO - 1 + CHUNK, :]
        return cur + (prev - cur) * mu_ref[...]

    pre = {}
    for b in batches:
        r = token_shift(b, r_ref, rh_ref, mur_ref, sr_ref)
        k = token_shift(b, k_ref, kh_ref, muk_ref, sk_ref)
        v = token_shift(b, v_ref, vh_ref, muv_ref, sv_ref)
        lo = token_shift(b, lo_ref, loh_ref, mulo_ref, slo_ref)
        lo_wa = lo[:, :LANES]
        log_w = -RWKV_DECAY_SCALE * _sigmoid(w0_ref[...] + _mm(jnp.tanh(lo_wa), wup_ref[...]))
        a = _sigmoid(a0_ref[...] + _mm(lo_wa, aup_ref[...]))
        gate = _mm(_sigmoid(lo[:, LANES:]), gup_ref[...])
        kk = k * kkw_ref[...]
        kk = kk / jnp.maximum(jnp.sqrt(_mm(kk * kk, bd)), 1e-12)
        k = k * (1.0 + (a - 1.0) * kaw_ref[...])
        cum = _mm_split(tri, log_w, 2)
        mid = cum[C // 2 - 1:C // 2, :]
        last = cum[C - 1:C, :]
        inv_p = jnp.exp(mid - cum)
        to_end = jnp.exp(last - mid)
        pre[b] = dict(
            r=r, k=k, v=v, gate=gate,
            a_t=-kk * jnp.exp(cum - log_w - mid), r_t=r * jnp.exp(cum - mid),
            b_t=kk * a * inv_p, k_t=k * inv_p, e_mid=jnp.exp(mid),
            b_end=kk * a * inv_p * to_end, k_end=k * inv_p * to_end, decay=jnp.exp(last))

    yield
    units = [(b, q) for b in batches for q in range(npair)]
    slab = lambda name: [pre[b][name][:, q * LANES:(q + 1) * LANES] for b, q in units]
    a_t, r_t, b_t, k_t, v_u = slab("a_t"), slab("r_t"), slab("b_t"), slab("k_t"), slab("v")
    e_mid, b_end, k_end, decay = slab("e_mid"), slab("b_end"), slab("k_end"), slab("decay")

    mm = [_mm_nt(cat0([a * m1[0], r * m1[0], a * m1[1], r * m1[1]]), cat0([bt, kt]))
          for a, r, bt, kt in zip(a_t, r_t, b_t, k_t)]
    chains = [(u, j) for u in range(len(units)) for j in range(2)]
    m_ab = [jnp.where(strict, mm[u][2 * j * C:(2 * j + 1) * C, :C], 0.0) for u, j in chains]
    m_ak = [jnp.where(strict, mm[u][2 * j * C:(2 * j + 1) * C, C:], 0.0).astype(BF16)
            for u, j in chains]
    m_rb = [jnp.where(causal, mm[u][(2 * j + 1) * C:(2 * j + 2) * C, :C], 0.0).astype(BF16)
            for u, j in chains]
    m_rk = [jnp.where(causal, mm[u][(2 * j + 1) * C:(2 * j + 2) * C, C:], 0.0).astype(BF16)
            for u, j in chains]
    yield

    inv = [eye + m for m in m_ab]
    power = [m.astype(BF16) for m in m_ab]
    power = [_mm(p, p).astype(BF16) for p in power]
    yield
    for _ in range(int(math.log2(C)) - 2):
        both = [_mm(cat0([i.astype(BF16), p]), p) for i, p in zip(inv, power)]
        inv = [i + b[:C] for i, b in zip(inv, both)]
        power = [b[C:].astype(BF16) for b in both]
        yield
    half = C // 2
    inv = [cat0([i[:half], i[half:] + _mm(i[half:], p)]) for i, p in zip(inv, power)]
    yield

    n_units = range(len(units))
    per_head = lambda x2: x2[:C] * m2[0] + x2[C:] * m2[1]
    mv = [_mm(cat0([m_ak[2 * u], m_rk[2 * u], m_ak[2 * u + 1], m_rk[2 * u + 1]]), v_u[u])
          for u in n_units]
    mv_ak = [mv[u][:C] * m1[0] + mv[u][2 * C:3 * C] * m1[1] for u in n_units]
    mv_rk = [mv[u][C:2 * C] * m1[0] + mv[u][3 * C:] * m1[1] for u in n_units]
    z = [per_head(_mm(cat0([inv[2 * u], inv[2 * u + 1]]), cat1([a_t[u], mv_ak[u]])))
         for u in n_units]
    yield
    w = [per_head(_mm(cat0([m_rb[2 * u], m_rb[2 * u + 1]]), z[u])) for u in n_units]
    yield

    s = [s_ref[b, q] for b, q in units]
    uy = [_mm_nt(cat0([z[u][:, :LANES] * e_mid[u], (r_t[u] + w[u][:, :LANES]) * e_mid[u]]), s[u])
          for u in n_units]
    u_in = [uy[u][:C] + z[u][:, LANES:] for u in n_units]
    y = [uy[u][C:] + w[u][:, LANES:] + mv_rk[u] for u in n_units]
    s_inc = [_mm_tn(cat0([u_in[u], v_u[u]]), cat0([b_end[u], k_end[u]])) for u in n_units]
    for u, (b, q) in enumerate(units):
        s_ref[b, q] = s[u] * decay[u] + same_head * s_inc[u]

    for i, b in enumerate(batches):
        yb = cat1([y[i * npair + q] for q in range(npair)])
        mean = _mm(yb, bd) * (1.0 / HEAD_DIM)
        d = yb - mean
        var = _mm(d * d, bd) * (1.0 / HEAD_DIM)
        yn = d * lax.rsqrt(var + RWKV_LN_EPS) * lnw_ref[...] + lnb_ref[...]
        bonus = _mm(pre[b]["r"] * pre[b]["k"] * rkw_ref[...], bd) * pre[b]["v"]
        o_ref[b] = ((yn + bonus) * pre[b]["gate"]).astype(BF16)


FOX_TILE = 2048
FOX_Q_SPLIT = 8
FOX_GUARD = 64.0
LOG2E = math.log2(math.e)


def _fox_prep_chunk(b, q_ref, k_ref, v_ref, f_ref, qw_ref, kw_ref, fb_ref, bd_ref, spread_ref,
                    qo_ref, ko_ref, vo_ref, carry_ref):
    bd = bd_ref[...]
    q, k, v = q_ref[b], k_ref[b], v_ref[b]
    q_ms = _mm(q * q, bd) * (1.0 / HEAD_DIM)
    k_ms = _mm(k * k, bd) * (1.0 / HEAD_DIM)

    f = f_ref[b] + fb_ref[...]
    log_f = jnp.minimum(f, 0.0) - jnp.log1p(jnp.exp(-jnp.abs(f)))
    rows = CHUNK
    tri = (_iota2((rows, rows), 0) >= _iota2((rows, rows), 1)).astype(BF16)
    cum = _mm_split(tri, log_f, 3) + carry_ref[b, 0:1, :]
    carry_ref[b] = jnp.broadcast_to(cum[rows - 1:rows, :], carry_ref.shape[1:])
    yield
    q = q * lax.rsqrt(q_ms + EPS) * qw_ref[...] * (HEAD_DIM ** -0.5 * LOG2E)
    k = k * lax.rsqrt(k_ms + EPS) * kw_ref[...]
    c_all = _mm_split_rhs(cum * LOG2E, spread_ref[...], 3)
    yield

    lane = _iota2((1, LANES), 1)
    for h in range(N_HEADS):
        p, j = divmod(h, 2)
        ps = slice(p * LANES, (p + 1) * LANES)
        mj = _head_mask(j)
        qh, kh = q[:, ps] * mj, k[:, ps] * mj
        if j == 1:
            qh = pltpu.roll(qh, HEAD_DIM, 1)
            kh = pltpu.roll(kh, HEAD_DIM, 1)
        ch = c_all[:, h * LANES:(h + 1) * LANES]
        hi = ch.astype(BF16).astype(F32)
        mid = (ch - hi).astype(BF16).astype(F32)
        low = ch - hi - mid
        piece = lambda base: jnp.where(lane == base, hi,
                                       jnp.where(lane == base + 1, mid,
                                                 jnp.where(lane == base + 2, low, 0.0)))
        ones = lambda base: ((lane >= base) & (lane < base + 3)).astype(F32)
        qo_ref[b, h] = (qh + piece(HEAD_DIM) + ones(HEAD_DIM + 3)).astype(BF16)
        ko_ref[b, h] = (kh + ones(HEAD_DIM) - piece(HEAD_DIM + 3)).astype(BF16)
    row_head = _iota2((LANES, 1), 0) // HEAD_DIM
    row_in_head = _iota2((LANES, 1), 0) % HEAD_DIM
    for p in range(N_HEADS // 2):
        v_t = v[:, p * LANES:(p + 1) * LANES].T
        for j in range(2):
            ones_row = ((row_head != j) & (row_in_head == 0)).astype(F32)
            vo_ref[b, 2 * p + j] = jnp.where(row_head == j, v_t, ones_row).astype(BF16)


RWKV_INPUTS = 23
FOX_PREP_INPUTS = 9


ODD_CHUNKS_PER_STEP = 2


def _rwkv_fox_prep_kernel(*refs):
    rwkv_in = refs[:RWKV_INPUTS]
    prep_in = refs[RWKV_INPUTS:RWKV_INPUTS + FOX_PREP_INPUTS]
    (rwkv_out, qo_ref, ko_ref, vo_ref,
     s_ref, sr_ref, sk_ref, sv_ref, slo_ref, carry_ref) = refs[RWKV_INPUTS + FOX_PREP_INPUTS:]
    c = pl.program_id(0)

    @pl.when(c == 0)
    def _():
        s_ref[...] = jnp.zeros_like(s_ref)
        carry_ref[...] = jnp.zeros_like(carry_ref)

    def after(stages, program):
        for _ in range(stages):
            yield
        yield from program

    batches = list(range(carry_ref.shape[0]))
    programs = []
    for ci in range(ODD_CHUNKS_PER_STEP):
        rows = slice(ci * CHUNK, (ci + 1) * CHUNK)
        prev = slice(ci * CHUNK - HALO, ci * CHUNK)
        shifted = []
        for main_ref, halo_ref in zip(rwkv_in[0:8:2], rwkv_in[1:8:2]):
            shifted += [main_ref.at[:, rows, :], halo_ref if ci == 0 else main_ref.at[:, prev, :]]
        keep = (c > 0).astype(F32) if ci == 0 else jnp.float32(1.0)
        programs.append(_rwkv_program(
            batches, keep, *shifted, *rwkv_in[8:], rwkv_out.at[:, rows, :], s_ref,
            sr_ref.at[ci], sk_ref.at[ci], sv_ref.at[ci], slo_ref.at[ci]))
        for b in batches:
            prep = _fox_prep_chunk(b, *[ref.at[:, rows, :] for ref in prep_in[:4]], *prep_in[4:],
                                   qo_ref.at[:, :, rows, :], ko_ref.at[:, :, rows, :],
                                   vo_ref.at[:, :, :, rows], carry_ref)
            programs.append(after(2 + 2 * b, prep))
    _interleave(programs)


def _rwkv_fox_prep(r, k, v, lo, mu, w0, w_up, a0, a_up, g_up, k_k, k_a, r_k, ln_w, ln_b,
                   fq, fk, fv, ff, q_norm_w, k_norm_w, f_bias, batch, seq):
    nc = seq // CHUNK
    lo_width = lo.shape[-1]
    rank = w_up.shape[0]
    row1 = lambda a: a.reshape(1, -1)
    seq3 = lambda a: a.reshape(batch, seq, a.shape[-1])

    wup_pad = jnp.concatenate([w_up, jnp.zeros((LANES - rank, WIDTH), F32)], axis=0).astype(BF16)
    aup_pad = jnp.concatenate([jnp.zeros((LANES - rank, WIDTH), F32), a_up], axis=0).astype(BF16)
    head = jnp.arange(WIDTH) // HEAD_DIM
    bd = (head[:, None] == head[None, :]).astype(BF16)
    r, k, v, lo = seq3(r), seq3(k), seq3(v), seq3(lo)
    rwkv_args = (r, r, k, k, v, v, lo, lo,
                 row1(mu[:WIDTH]), row1(mu[WIDTH:2 * WIDTH]), row1(mu[2 * WIDTH:3 * WIDTH]),
                 row1(mu[3 * WIDTH:]),
                 row1(w0), wup_pad, row1(a0), aup_pad, g_up.astype(BF16),
                 row1(k_k), row1(k_a), row1(r_k), row1(ln_w), row1(ln_b), bd)
    rows = ODD_CHUNKS_PER_STEP * CHUNK
    seq_spec = lambda width: pl.BlockSpec((batch, rows, width), lambda c: (0, c, 0))
    halo_spec = lambda width: pl.BlockSpec(
        (batch, HALO, width), lambda c: (0, jnp.maximum(c * (rows // HALO) - 1, 0), 0))
    wide, wide_halo = seq_spec(WIDTH), halo_spec(WIDTH)
    param = _param_spec1((1, WIDTH))
    weight = _param_spec1((LANES, WIDTH))
    rwkv_specs = [wide, wide_halo, wide, wide_halo, wide, wide_halo,
                  seq_spec(lo_width), halo_spec(lo_width),
                  param, param, param, _param_spec1((1, lo_width)),
                  param, weight, param, weight, weight,
                  param, param, param, param, param, _param_spec1((WIDTH, WIDTH))]

    dst = jnp.arange(N_HEADS * LANES)
    spread = ((dst[None, :] // LANES == jnp.arange(LANES)[:, None])
              & (dst[None, :] % LANES >= HEAD_DIM) & (dst[None, :] % LANES < HEAD_DIM + 6)).astype(BF16)
    prep_args = (seq3(fq), seq3(fk), seq3(fv), seq3(ff),
                 row1(jnp.tile(q_norm_w, N_HEADS)), row1(jnp.tile(k_norm_w, N_HEADS)),
                 row1(jnp.pad(f_bias, (0, LANES - N_HEADS))), bd, spread)
    prep_specs = [wide, wide, wide, seq_spec(LANES), param, param,
                  _param_spec1((1, LANES)), _param_spec1((WIDTH, WIDTH)),
                  _param_spec1((LANES, N_HEADS * LANES))]
    assert len(rwkv_args) == RWKV_INPUTS and len(prep_args) == FOX_PREP_INPUTS

    head_out = pl.BlockSpec((batch, N_HEADS, rows, LANES), lambda c: (0, 0, c, 0))
    head_shape = jax.ShapeDtypeStruct((batch, N_HEADS, seq, LANES), BF16)
    vt_out = pl.BlockSpec((batch, N_HEADS, LANES, rows), lambda c: (0, 0, 0, c))
    vt_shape = jax.ShapeDtypeStruct((batch, N_HEADS, LANES, seq), BF16)
    shifted = lambda width: pltpu.VMEM((ODD_CHUNKS_PER_STEP, batch, CHUNK + HALO, width), F32)
    y_rwkv, qa, ka, va = pl.pallas_call(
        _rwkv_fox_prep_kernel,
        grid=(nc // ODD_CHUNKS_PER_STEP,),
        in_specs=rwkv_specs + prep_specs,
        out_specs=[wide, head_out, head_out, vt_out],
        out_shape=[jax.ShapeDtypeStruct((batch, seq, WIDTH), BF16), head_shape, head_shape,
                   vt_shape],
        scratch_shapes=[pltpu.VMEM((batch, N_HEADS // 2, LANES, LANES), F32),
                        shifted(WIDTH), shifted(WIDTH), shifted(WIDTH), shifted(lo_width),
                        pltpu.VMEM((batch, HALO, LANES), F32)],
        compiler_params=_cparams(("arbitrary",)),
        name="rwkv7_fox_prep",
    )(*rwkv_args, *prep_args)
    return y_rwkv.reshape(batch * seq, WIDTH), qa, ka, va


def _fox_kernel(qi_ref, ki_ref, q_ref, k_ref, vt_ref, o_ref, m_ref, acc_ref, redo_ref):
    t = pl.program_id(2)
    qi = qi_ref[t]
    ki = ki_ref[t]

    part = FOX_TILE // FOX_Q_SPLIT
    units = [(j, slice(h * part, (h + 1) * part)) for j in range(2) for h in range(FOX_Q_SPLIT)]

    def all_scores(on_diagonal):
        n_keys = lambda qs: qs.stop if on_diagonal else FOX_TILE
        raw = [lax.dot_general(k_ref[0, j, :n_keys(qs), :], q_ref[0, j, qs, :],
                               (((1,), (1,)), ((), ())), preferred_element_type=F32)
               for j, qs in units]
        if not on_diagonal:
            return raw
        masked = []
        for (_, qs), s in zip(units, raw):
            key = _iota2(s.shape, 0)
            query = _iota2(s.shape, 1) + qs.start
            masked.append(jnp.where(query >= key, s, NEG))
        return masked

    def pv(j, p):
        return jnp.dot(vt_ref[0, j, :, :p.shape[0]], p.astype(BF16), preferred_element_type=F32)

    def exact_step(on_diagonal):
        for (j, qs), s in zip(units, all_scores(on_diagonal)):
            m_prev = m_ref[j, :, qs]
            m_next = jnp.maximum(m_prev, jnp.max(s, axis=0, keepdims=True))
            alpha = jnp.exp2(m_prev - m_next)
            acc_ref[j, :, qs] = alpha * acc_ref[j, :, qs] + pv(j, jnp.exp2(s - m_next))
            m_ref[j, :, qs] = m_next

    def lagged_step(on_diagonal):
        done = []
        worst = None
        for (j, qs), s in zip(units, all_scores(on_diagonal)):
            m_used = m_ref[j, :, qs]
            block_max = jnp.max(s, axis=0, keepdims=True)
            contrib = pv(j, jnp.exp2(s - m_used))
            excess = jnp.max(block_max - m_used)
            worst = excess if worst is None else jnp.maximum(worst, excess)
            done.append((j, qs, m_used, block_max, contrib))
        ok = worst <= FOX_GUARD
        redo_ref[0] = jnp.where(ok, 0, 1)

        @pl.when(ok)
        def _():
            for j, qs, m_used, block_max, contrib in done:
                m_next = jnp.maximum(m_used, block_max)
                acc_ref[j, :, qs] = (acc_ref[j, :, qs] + contrib) * jnp.exp2(m_used - m_next)
                m_ref[j, :, qs] = m_next

    @pl.when(ki == qi)
    def _():
        m_ref[...] = jnp.full_like(m_ref, NEG)
        acc_ref[...] = jnp.zeros_like(acc_ref)
        exact_step(True)

    redo_ref[0] = 0

    @pl.when(ki < qi)
    def _():
        lagged_step(False)

    @pl.when(redo_ref[0] == 1)
    def _():
        exact_step(False)

    @pl.when(ki == 0)
    def _():
        head_rows = _iota2((LANES, 1), 0) // HEAD_DIM
        out_t = jnp.where(head_rows == 0,
                          acc_ref[0] / acc_ref[0, HEAD_DIM:HEAD_DIM + 1, :],
                          acc_ref[1] / acc_ref[1, 0:1, :])
        o_ref[0] = out_t.T.astype(BF16)


def _fox_attention(qa, ka, va, batch, seq):
    nt = seq // FOX_TILE
    pairs = [(i, j) for i in range(nt) for j in range(i, -1, -1)]
    qi = jnp.asarray([i for i, _ in pairs], jnp.int32)
    ki = jnp.asarray([j for _, j in pairs], jnp.int32)
    q_spec = pl.BlockSpec((1, 2, FOX_TILE, LANES), lambda b, p, t, qi, ki: (b, p, qi[t], 0))
    k_spec = pl.BlockSpec((1, 2, FOX_TILE, LANES), lambda b, p, t, qi, ki: (b, p, ki[t], 0))
    vt_spec = pl.BlockSpec((1, 2, LANES, FOX_TILE), lambda b, p, t, qi, ki: (b, p, 0, ki[t]))
    out = pl.pallas_call(
        _fox_kernel,
        grid_spec=pltpu.PrefetchScalarGridSpec(
            num_scalar_prefetch=2,
            grid=(batch, N_HEADS // 2, len(pairs)),
            in_specs=[q_spec, k_spec, vt_spec],
            out_specs=pl.BlockSpec((1, FOX_TILE, LANES), lambda b, p, t, qi, ki: (b, qi[t], p)),
            scratch_shapes=[pltpu.VMEM((2, 1, FOX_TILE), F32),
                            pltpu.VMEM((2, LANES, FOX_TILE), F32),
                            pltpu.SMEM((1,), jnp.int32)]),
        out_shape=jax.ShapeDtypeStruct((batch, seq, WIDTH), BF16),
        compiler_params=_cparams(("parallel", "parallel", "arbitrary")),
        name="fox_attention",
    )(qi, ki, qa, ka, va)
    return out.reshape(batch * seq, WIDTH)


def _pad_cols(w, width):
    return jnp.pad(w, ((0, 0), (0, width - w.shape[1])))


def _even_mixers(x, batch, seq, norm_w, w_in, conv_w, conv_b, dt_bias, a_log, d_skip,
                 ssd_norm_w, ret_norm_w):
    ssd_dt0 = WIDTH + (WIDTH + 4 * SSD_STATE)
    ret0 = ssd_dt0 + N_HEADS
    w = jnp.concatenate([w_in[:, :ssd_dt0], w_in[:, ret0:],
                         _pad_cols(w_in[:, ssd_dt0:ret0], LANES)], axis=1).astype(BF16)
    return _even_layer_mixers(x, norm_w, w, conv_w, conv_b, dt_bias, a_log, d_skip, ssd_norm_w,
                              ret_norm_w, batch, seq)


def _odd_mixers(x, batch, seq, norm_w, w_in, mu, w0, w_up, a0, a_up, g_up, k_k, k_a, r_k,
                ln_w, ln_b, q_norm_w, k_norm_w, f_bias):
    lo_width = w_up.shape[0] + a_up.shape[0] + g_up.shape[0]
    fox0 = 3 * WIDTH + lo_width
    f0 = fox0 + 3 * WIDTH
    w = jnp.concatenate([w_in[:, :f0], _pad_cols(w_in[:, f0:], LANES)], axis=1).astype(BF16)
    segments = ((0, WIDTH), (WIDTH, WIDTH), (2 * WIDTH, WIDTH), (3 * WIDTH, lo_width),
                (fox0, WIDTH), (fox0 + WIDTH, WIDTH), (fox0 + 2 * WIDTH, WIDTH), (f0, LANES))
    r, k, v, lo, fq, fk, fv, ff = _inproj(x, norm_w, w, segments)
    y_rwkv, qa, ka, va = _rwkv_fox_prep(r, k, v, lo, mu, w0, w_up, a0, a_up, g_up, k_k, k_a,
                                        r_k.reshape(-1), ln_w, ln_b, fq, fk, fv, ff,
                                        q_norm_w, k_norm_w, f_bias, batch, seq)
    return y_rwkv, _fox_attention(qa, ka, va, batch, seq)


def kernel(x, ev_norm_w, ev_w_in, ev_ssd_conv_w, ev_ssd_conv_b, ev_ssd_dt_bias, ev_ssd_a_log,
           ev_ssd_d, ev_ssd_norm_w, ev_ret_norm_w, ev_w_out,
           od_norm_w, od_w_in, od_rwkv_mu, od_rwkv_w0, od_rwkv_w_up, od_rwkv_a0, od_rwkv_a_up,
           od_rwkv_g_up, od_rwkv_k_k, od_rwkv_k_a, od_rwkv_r_k, od_rwkv_ln_w, od_rwkv_ln_b,
           od_fox_q_norm_w, od_fox_k_norm_w, od_fox_f_bias, od_w_out,
           ffn_norm_w, ffn_w_up, ffn_conv_w, ffn_conv_b, ffn_w_down):
    batch, seq, _ = x.shape
    depth = ffn_norm_w.shape[0]
    h = x.reshape(batch * seq, D_MODEL)
    w_up_all = ffn_w_up.astype(BF16)
    w_down_all = ffn_w_down.astype(BF16)
    for layer in range(depth):
        i = layer // 2
        if layer % 2 == 0:
            ya, yb = _even_mixers(h, batch, seq, ev_norm_w[i], ev_w_in[i], ev_ssd_conv_w[i],
                                  ev_ssd_conv_b[i], ev_ssd_dt_bias[i], ev_ssd_a_log[i],
                                  ev_ssd_d[i], ev_ssd_norm_w[i], ev_ret_norm_w[i])
            w_out = ev_w_out[i]
        else:
            ya, yb = _odd_mixers(h, batch, seq, od_norm_w[i], od_w_in[i], od_rwkv_mu[i],
                                 od_rwkv_w0[i], od_rwkv_w_up[i], od_rwkv_a0[i], od_rwkv_a_up[i],
                                 od_rwkv_g_up[i], od_rwkv_k_k[i], od_rwkv_k_a[i], od_rwkv_r_k[i],
                                 od_rwkv_ln_w[i], od_rwkv_ln_b[i], od_fox_q_norm_w[i],
                                 od_fox_k_norm_w[i], od_fox_f_bias[i])
            w_out = od_w_out[i]
        h = _proj_ffn(ya, yb, h, seq, w_out, ffn_norm_w[layer], layer, w_up_all,
                      ffn_conv_w[layer], ffn_conv_b[layer], w_down_all)
    return h.reshape(batch, seq, D_MODEL)
```

```python
import functools
import itertools
import math

import jax
import jax.numpy as jnp
from jax import lax
from jax.experimental import pallas as pl
from jax.experimental.pallas import tpu as pltpu

F32 = jnp.float32
BF16 = jnp.bfloat16

D_MODEL = 1024
HEAD_DIM = 64
N_HEADS = 8
WIDTH = N_HEADS * HEAD_DIM
LANES = 128
CHUNK = 128
HALO = 8
EPS = 1e-6
SSD_CONV = 4
SSD_STATE = 64
RWKV_LN_EPS = 64e-5
RWKV_DECAY_SCALE = 0.606531
D_FF = 2816
FF_TILE = 256
FFN_HALO = 16
FFN_LOOKAHEAD = 2
NEG = -1e30
VMEM_LIMIT = 56 * 1024 * 1024


def _mm(a, b):
    return jnp.dot(a.astype(BF16), b.astype(BF16), preferred_element_type=F32)


def _mm_nt(a, b):
    return lax.dot_general(a.astype(BF16), b.astype(BF16), (((1,), (1,)), ((), ())),
                           preferred_element_type=F32)


def _mm_tn(a, b):
    return lax.dot_general(a.astype(BF16), b.astype(BF16), (((0,), (0,)), ((), ())),
                           preferred_element_type=F32)


def _mm_split(lhs01, x, pieces):
    acc = None
    rest = x
    for _ in range(pieces):
        part = rest.astype(BF16)
        term = jnp.dot(lhs01, part, preferred_element_type=F32)
        acc = term if acc is None else acc + term
        rest = rest - part.astype(F32)
    return acc


def _mm_split_rhs(x, rhs01, pieces):
    acc = None
    rest = x
    for _ in range(pieces):
        part = rest.astype(BF16)
        term = jnp.dot(part, rhs01, preferred_element_type=F32)
        acc = term if acc is None else acc + term
        rest = rest - part.astype(F32)
    return acc


def _interleave(stage_generators):
    for _ in itertools.zip_longest(*stage_generators):
        pass


def _sigmoid(x):
    return 0.5 * jnp.tanh(0.5 * x) + 0.5


def _silu(x):
    return x * _sigmoid(x)


def _softplus(x):
    return jnp.maximum(x, 0.0) + jnp.log1p(jnp.exp(-jnp.abs(x)))


def _iota2(shape, dim):
    return lax.broadcasted_iota(jnp.int32, shape, dim)


def _head_mask(j, width=LANES):
    lane = _iota2((1, width), 1)
    return ((lane % LANES) // HEAD_DIM == j).astype(F32)


def _cparams(sem):
    return pltpu.CompilerParams(dimension_semantics=sem, vmem_limit_bytes=VMEM_LIMIT)


def _const_spec(shape):
    nd = len(shape)
    return pl.BlockSpec(shape, lambda *_: (0,) * nd, pipeline_mode=pl.Buffered(1))


def _inproj_kernel(x_ref, nw_ref, w_ref, *o_refs, segments):
    x = x_ref[...]
    ms = jnp.mean(x * x, axis=-1, keepdims=True)
    xn = (x * lax.rsqrt(ms + EPS) * nw_ref[...]).astype(BF16)
    for o_ref, (off, width) in zip(o_refs, segments):
        o_ref[...] = jnp.dot(xn, w_ref[:, off:off + width], preferred_element_type=F32)


def _inproj(x, norm_w, w, segments, tm=1024):
    t = x.shape[0]
    n = w.shape[1]
    return pl.pallas_call(
        functools.partial(_inproj_kernel, segments=segments),
        grid=(t // tm,),
        in_specs=[pl.BlockSpec((tm, D_MODEL), lambda i: (i, 0)),
                  _const_spec((1, D_MODEL)),
                  _const_spec((D_MODEL, n))],
        out_specs=[pl.BlockSpec((tm, wd), lambda i: (i, 0)) for _, wd in segments],
        out_shape=[jax.ShapeDtypeStruct((t, wd), F32) for _, wd in segments],
        compiler_params=_cparams(("parallel",)),
        name="inproj",
    )(x, norm_w.reshape(1, D_MODEL), w)


def _proj_ffn_kernel(ya_ref, yah_ref, yb_ref, ybh_ref, x_ref, xh_ref, wo_ref, nw_ref,
                     wup_ref, cw_ref, cb_ref, wd_ref, o_ref,
                     ya_ext, yb_ext, h_ref, hn_ref, g_ref, acc_ref, *, tm, tiles_per_seq):
    ya_ext[:FFN_HALO, :] = yah_ref[...]
    ya_ext[FFN_HALO:, :] = ya_ref[...]
    yb_ext[:FFN_HALO, :] = ybh_ref[...]
    yb_ext[FFN_HALO:, :] = yb_ref[...]
    h_ref[:FFN_HALO, :] = xh_ref[...]
    h_ref[FFN_HALO:, :] = x_ref[...]
    h_ref[...] += (jnp.dot(ya_ext[...], wo_ref[:WIDTH, :], preferred_element_type=F32)
                   + jnp.dot(yb_ext[...], wo_ref[WIDTH:, :], preferred_element_type=F32))

    h = h_ref[...]
    ms = jnp.mean(h * h, axis=-1, keepdims=True)
    hn = h * lax.rsqrt(ms + EPS) * nw_ref[...]
    first_of_seq = pl.program_id(0) % tiles_per_seq == 0
    halo_rows = _iota2((tm + FFN_HALO, 1), 0) < FFN_HALO
    hn_ref[...] = jnp.where(halo_rows & first_of_seq, 0.0, hn).astype(BF16)
    acc_ref[...] = jnp.zeros_like(acc_ref)

    def tile(f, base=0):
        return slice(base + f * FF_TILE, base + (f + 1) * FF_TILE)

    n_buf = g_ref.shape[0]

    def up_proj(f):
        g_ref[f % n_buf] = jnp.dot(hn_ref[...], wup_ref[:, tile(f)], preferred_element_type=F32)
        return jnp.dot(hn_ref[FFN_HALO:, :], wup_ref[:, tile(f, D_FF)],
                       preferred_element_type=F32)

    nf = D_FF // FF_TILE
    ups = {f: up_proj(f) for f in range(FFN_LOOKAHEAD)}
    for f in range(nf):
        if f + FFN_LOOKAHEAD < nf:
            ups[f + FFN_LOOKAHEAD] = up_proj(f + FFN_LOOKAHEAD)
        up = ups.pop(f)
        g = g_ref.at[f % n_buf]
        cw = cw_ref[:, tile(f)]
        gate = (cb_ref[:, tile(f)]
                + cw[0:1, :] * g[FFN_HALO - 2:FFN_HALO - 2 + tm, :]
                + cw[1:2, :] * g[FFN_HALO - 1:FFN_HALO - 1 + tm, :]
                + cw[2:3, :] * g[FFN_HALO:, :])
        act = (_silu(gate) * up).astype(BF16)
        acc_ref[...] += jnp.dot(act, wd_ref[tile(f), :], preferred_element_type=F32)
    o_ref[...] = h_ref[FFN_HALO:, :] + acc_ref[...]


def _proj_ffn(ya, yb, x, seq, w_out, norm_w, layer, w_up_all, conv_w, conv_b, w_down_all, tm=512):
    t = x.shape[0]
    halo_per_tile = tm // FFN_HALO
    row_spec = lambda width: pl.BlockSpec((tm, width), lambda i: (i, 0))
    halo_spec = lambda width: pl.BlockSpec(
        (FFN_HALO, width), lambda i: (jnp.maximum(i * halo_per_tile - 1, 0), 0))
    ext = lambda width, dtype: pltpu.VMEM((tm + FFN_HALO, width), dtype)
    layer_spec = lambda shape: pl.BlockSpec((None,) + shape, lambda i: (layer, 0, 0),
                                            pipeline_mode=pl.Buffered(1))
    return pl.pallas_call(
        functools.partial(_proj_ffn_kernel, tm=tm, tiles_per_seq=seq // tm),
        grid=(t // tm,),
        in_specs=[row_spec(WIDTH), halo_spec(WIDTH), row_spec(WIDTH), halo_spec(WIDTH),
                  row_spec(D_MODEL), halo_spec(D_MODEL),
                  _const_spec((2 * WIDTH, D_MODEL)), _const_spec((1, D_MODEL)),
                  layer_spec((D_MODEL, 2 * D_FF)), _const_spec((3, D_FF)), _const_spec((1, D_FF)),
                  layer_spec((D_FF, D_MODEL))],
        out_specs=row_spec(D_MODEL),
        out_shape=jax.ShapeDtypeStruct((t, D_MODEL), F32),
        scratch_shapes=[ext(WIDTH, BF16), ext(WIDTH, BF16), ext(D_MODEL, F32), ext(D_MODEL, BF16),
                        pltpu.VMEM((FFN_LOOKAHEAD + 1, tm + FFN_HALO, FF_TILE), F32),
                        pltpu.VMEM((tm, D_MODEL), F32)],
        compiler_params=_cparams(("parallel",)),
        name="proj_convffn",
    )(ya, ya, yb, yb, x, x, w_out.astype(BF16), norm_w.reshape(1, D_MODEL),
      w_up_all, conv_w, conv_b.reshape(1, D_FF), w_down_all)


def _seq_spec(batch, width):
    return pl.BlockSpec((batch, CHUNK, width), lambda c: (0, c, 0))


def _seq_halo_spec(batch, width):
    per = CHUNK // HALO
    return pl.BlockSpec((batch, HALO, width), lambda c: (0, jnp.maximum(c * per - 1, 0), 0))


def _param_spec1(shape):
    return pl.BlockSpec(shape, lambda c: (0,) * len(shape), pipeline_mode=pl.Buffered(1))


def _with_prev_rows(scr_ref, main_ref, halo_ref, keep):
    scr_ref[:HALO, :] = halo_ref[...] * keep
    scr_ref[HALO:, :] = main_ref[...]


def _ssd_chunk(b, keep, z_ref, x_ref, xh_ref, bc_ref, bch_ref, dt_ref,
               cwx_ref, cbx_ref, cwb_ref, cbb_ref, dtb_ref, alog_ref, dexp_ref, nw_ref, e_ref,
               o_ref, xe_ref, bce_ref, st_ref, y_ref):
    _with_prev_rows(xe_ref.at[b], x_ref.at[b], xh_ref.at[b], keep)
    _with_prev_rows(bce_ref.at[b], bc_ref.at[b], bch_ref.at[b], keep)

    def conv(ref, cw_ref, cb_ref):
        out = cb_ref[...]
        for k in range(SSD_CONV):
            lo = HALO - (SSD_CONV - 1) + k
            out = out + cw_ref[k:k + 1, :] * ref[b, lo:lo + CHUNK, :]
        return _silu(out)

    x = conv(xe_ref, cwx_ref, cbx_ref)
    bc = conv(bce_ref, cwb_ref, cbb_ref)
    bm = bc[:, :LANES]
    cm = bc[:, LANES:]

    row = _iota2((CHUNK, CHUNK), 0)
    col = _iota2((CHUNK, CHUNK), 1)
    causal = row >= col

    dt = _softplus(dt_ref[b] + dtb_ref[...])
    a = dt * (-jnp.exp(alog_ref[...]))
    a_cum = _mm_split(causal.astype(BF16), a, 3)
    yield
    a_cum2 = a_cum * LOG2E
    a_cum2_t = a_cum2.T
    acausal_penalty = jnp.where(causal, 0.0, NEG)
    a_last = a_cum[CHUNK - 1:CHUNK, :]
    per_head = jnp.concatenate(
        [dt, jnp.exp(a_cum), jnp.exp(a_last - a_cum),
         jnp.broadcast_to(jnp.exp(a_last), (HALO, LANES))], axis=0)
    per_lane = _mm_split_rhs(per_head, e_ref[...], 2)
    dt_e = per_lane[:CHUNK]
    ea_e = per_lane[CHUNK:2 * CHUNK]
    te_e = per_lane[2 * CHUNK:3 * CHUNK]
    cd_e = per_lane[3 * CHUNK:3 * CHUNK + 1]
    yield

    xdt = x * dt_e
    xs = xdt * te_e
    bm_t = bm.T
    group_rows = _iota2((LANES, 1), 0) // SSD_STATE

    for g in range(2):
        cg = cm * _head_mask(g)
        cb = _mm_nt(cg, bm)
        gs = slice(g * 2 * LANES, (g + 1) * 2 * LANES)
        st = st_ref[b, g]
        y_off = _mm(cg, st) * ea_e[:, gs]
        st_new = _mm(bm_t, xs[:, gs])
        st_ref[b, g] = st * cd_e[:, gs] + jnp.where(group_rows == g, st_new, 0.0)
        yield
        for pp in range(2):
            p = 2 * g + pp
            xp = xdt[:, p * LANES:(p + 1) * LANES]
            yp = y_off[:, pp * LANES:(pp + 1) * LANES]
            for j in range(2):
                h = 2 * p + j
                seg = a_cum2[:, h:h + 1] - a_cum2_t[h:h + 1, :]
                decay = jnp.exp2(seg + acausal_penalty)
                yp = yp + _mm(cb * decay, xp * _head_mask(j))
            y_ref[b, :, p * LANES:(p + 1) * LANES] = yp
            yield

    y = y_ref[b] + dexp_ref[...] * x
    gated = y * _silu(z_ref[b])
    ms = jnp.mean(gated * gated, axis=-1, keepdims=True)
    o_ref[b] = (gated * lax.rsqrt(ms + EPS) * nw_ref[...]).astype(BF16)


def _ret_chunk(b, q_ref, k_ref, v_ref, g_ref, cos_ref, sin_ref, intra_ref, qdec_ref, kte_ref,
               cg_ref, bd_ref, nw_ref, o_ref, r_ref):
    lane = _iota2((1, WIDTH), 1)
    first_half = (lane % HEAD_DIM) < (HEAD_DIM // 2)
    cos = jnp.concatenate([cos_ref[...]] * (WIDTH // LANES), axis=1)
    sin = jnp.concatenate([sin_ref[...]] * (WIDTH // LANES), axis=1)

    def rotary(v):
        other = jnp.where(first_half,
                          pltpu.roll(v, WIDTH - HEAD_DIM // 2, 1),
                          pltpu.roll(v, HEAD_DIM // 2, 1))
        return v * cos + other * sin

    q = rotary(q_ref[b])
    k = rotary(k_ref[b]) * (HEAD_DIM ** -0.5)
    v = v_ref[b]
    bd = bd_ref[...]
    pair = lambda t, p: t[:, p * LANES:(p + 1) * LANES]
    heads = [(p, j) for p in range(N_HEADS // 2) for j in range(2)]
    masks = [_head_mask(j) for j in range(2)]

    y_cross = _mm(q * qdec_ref[...], r_ref[b])
    r_ref[b] = r_ref[b] * cg_ref[...] + bd * _mm_tn(k * kte_ref[...], v)
    scores = [_mm_nt(pair(q, p) * masks[j], pair(k, p)) for p, j in heads]
    yield
    y_pairs = []
    for p in range(N_HEADS // 2):
        yp = pair(y_cross, p)
        for j in range(2):
            h = 2 * p + j
            yp = yp + _mm(scores[h] * intra_ref[h], pair(v, p) * masks[j])
        y_pairs.append(yp)
    yield
    y = jnp.concatenate(y_pairs, axis=1)
    ms = _mm(y * y, bd) * (1.0 / HEAD_DIM)
    yield
    o_ref[b] = (_silu(g_ref[b]) * (y * lax.rsqrt(ms + EPS) * nw_ref[...])).astype(BF16)


def _retention_tables(seq):
    half = HEAD_DIM // 2
    inv = 1.0 / (10000.0 ** (jnp.arange(half, dtype=F32) / half))
    ang = jnp.arange(seq, dtype=F32)[:, None] * inv[None, :]
    cos = jnp.tile(jnp.concatenate([jnp.cos(ang), jnp.cos(ang)], axis=-1), (1, LANES // HEAD_DIM))
    sin = jnp.tile(jnp.concatenate([-jnp.sin(ang), jnp.sin(ang)], axis=-1), (1, LANES // HEAD_DIM))
    log_gamma = jnp.log1p(-(2.0 ** (-5.0 - jnp.arange(N_HEADS, dtype=F32))))
    idx = jnp.arange(CHUNK, dtype=F32)
    rel = idx[:, None] - idx[None, :]
    intra = jnp.where(rel[None] >= 0,
                      jnp.exp(jnp.maximum(rel, 0.0)[None] * log_gamma[:, None, None]), 0.0)
    expand = lambda m: jnp.repeat(m, HEAD_DIM, axis=1)
    qdec = expand(jnp.exp((idx + 1.0)[:, None] * log_gamma[None, :]))
    kte = expand(jnp.exp((CHUNK - 1 - idx)[:, None] * log_gamma[None, :]))
    cgam = expand(jnp.exp(CHUNK * log_gamma)[None, :])
    head = jnp.arange(WIDTH) // HEAD_DIM
    bd = (head[:, None] == head[None, :]).astype(BF16)
    return cos, sin, intra, qdec, kte, cgam, bd


EVEN_SEGMENTS = {"z": (0, WIDTH), "x": (WIDTH, WIDTH), "bc": (2 * WIDTH, 2 * LANES),
                 "q": (2 * WIDTH + 2 * LANES, WIDTH), "k": (3 * WIDTH + 2 * LANES, WIDTH),
                 "v": (4 * WIDTH + 2 * LANES, WIDTH), "g": (5 * WIDTH + 2 * LANES, WIDTH),
                 "dt": (6 * WIDTH + 2 * LANES, LANES)}
EVEN_COLS = 6 * WIDTH + 3 * LANES
SSD_PARAMS = 9
RET_PARAMS = 8


def _even_layer_kernel(x_ref, nw_ref, w_ref, *rest):
    ssd_par = rest[:SSD_PARAMS]
    ret_par = rest[SSD_PARAMS:SSD_PARAMS + RET_PARAMS]
    (ssd_out, ret_out, buf_a, buf_b, halo_x, halo_bc,
     xe_ref, bce_ref, st_ref, y_ref, r_ref) = rest[SSD_PARAMS + RET_PARAMS:]
    c = pl.program_id(0)
    nb = x_ref.shape[0]

    @pl.when(c == 0)
    def _():
        buf_a[...] = jnp.zeros_like(buf_a)
        buf_b[...] = jnp.zeros_like(buf_b)
        halo_x[...] = jnp.zeros_like(halo_x)
        halo_bc[...] = jnp.zeros_like(halo_bc)

    @pl.when(c <= 1)
    def _():
        st_ref[...] = jnp.zeros_like(st_ref)
        r_ref[...] = jnp.zeros_like(r_ref)

    keep = (c > 1).astype(F32)

    def step(read_buf, write_buf):
        def project():
            x = x_ref[...].reshape(nb * CHUNK, D_MODEL)
            ms = jnp.mean(x * x, axis=-1, keepdims=True)
            xn = (x * lax.rsqrt(ms + EPS) * nw_ref[...]).astype(BF16)
            for off, width in EVEN_SEGMENTS.values():
                res = jnp.dot(xn, w_ref[:, off:off + width], preferred_element_type=F32)
                for b in range(nb):
                    write_buf[b, :, off:off + width] = res[b * CHUNK:(b + 1) * CHUNK]
                yield

        def view(name):
            off, width = EVEN_SEGMENTS[name]
            return read_buf.at[:, :, off:off + width]

        programs = [project()]
        for b in range(nb):
            programs.append(_ssd_chunk(b, keep, view("z"), view("x"), halo_x, view("bc"), halo_bc,
                                       view("dt"), *ssd_par, ssd_out, xe_ref, bce_ref, st_ref,
                                       y_ref))
            programs.append(_ret_chunk(b, view("q"), view("k"), view("v"), view("g"), *ret_par,
                                       ret_out, r_ref))
        _interleave(programs)
        halo_x[...] = view("x")[:, CHUNK - HALO:, :]
        halo_bc[...] = view("bc")[:, CHUNK - HALO:, :]

    @pl.when(c % 2 == 0)
    def _():
        step(buf_b, buf_a)

    @pl.when(c % 2 == 1)
    def _():
        step(buf_a, buf_b)


def _even_layer_mixers(x, norm_w, w, conv_w, conv_b, dt_bias, a_log, d_skip, ssd_norm_w,
                       ret_norm_w, batch, seq):
    nc = seq // CHUNK
    pad = LANES - N_HEADS
    e = jnp.repeat(jnp.eye(LANES, N_HEADS, dtype=BF16), HEAD_DIM, axis=1)
    ssd_args = (conv_w[:, :WIDTH], conv_b[:WIDTH].reshape(1, WIDTH),
                conv_w[:, WIDTH:], conv_b[WIDTH:].reshape(1, 2 * LANES),
                jnp.pad(dt_bias, (0, pad)).reshape(1, LANES),
                jnp.pad(a_log, (0, pad)).reshape(1, LANES),
                jnp.repeat(d_skip, HEAD_DIM).reshape(1, WIDTH),
                ssd_norm_w.reshape(1, WIDTH), e)
    ssd_specs = [_param_spec1((SSD_CONV, WIDTH)), _param_spec1((1, WIDTH)),
                 _param_spec1((SSD_CONV, 2 * LANES)), _param_spec1((1, 2 * LANES)),
                 _param_spec1((1, LANES)), _param_spec1((1, LANES)),
                 _param_spec1((1, WIDTH)), _param_spec1((1, WIDTH)),
                 _param_spec1((LANES, WIDTH))]
    cos, sin, intra, qdec, kte, cgam, bd = _retention_tables(seq)
    mixed_chunk = lambda c: jnp.maximum(c - 1, 0)
    table_spec = pl.BlockSpec((CHUNK, LANES), lambda c: (mixed_chunk(c), 0))
    ret_args = (cos, sin, intra, qdec, kte, cgam, bd, ret_norm_w.reshape(1, WIDTH))
    ret_specs = [table_spec, table_spec, _param_spec1((N_HEADS, CHUNK, CHUNK)),
                 _param_spec1((CHUNK, WIDTH)), _param_spec1((CHUNK, WIDTH)),
                 _param_spec1((1, WIDTH)), _param_spec1((WIDTH, WIDTH)), _param_spec1((1, WIDTH))]
    assert len(ssd_args) == SSD_PARAMS and len(ret_args) == RET_PARAMS
    out_spec = pl.BlockSpec((batch, CHUNK, WIDTH), lambda c: (0, mixed_chunk(c), 0))
    out_shape = jax.ShapeDtypeStruct((batch, seq, WIDTH), BF16)
    proj_buf = pltpu.VMEM((batch, CHUNK, EVEN_COLS), F32)
    y_ssd, y_ret = pl.pallas_call(
        _even_layer_kernel,
        grid=(nc + 1,),
        in_specs=[pl.BlockSpec((batch, CHUNK, D_MODEL), lambda c: (0, jnp.minimum(c, nc - 1), 0)),
                  _param_spec1((1, D_MODEL)), _param_spec1((D_MODEL, EVEN_COLS))]
                 + ssd_specs + ret_specs,
        out_specs=[out_spec, out_spec],
        out_shape=[out_shape, out_shape],
        scratch_shapes=[proj_buf, proj_buf,
                        pltpu.VMEM((batch, HALO, WIDTH), F32),
                        pltpu.VMEM((batch, HALO, 2 * LANES), F32),
                        pltpu.VMEM((batch, CHUNK + HALO, WIDTH), F32),
                        pltpu.VMEM((batch, CHUNK + HALO, 2 * LANES), F32),
                        pltpu.VMEM((batch, 2, LANES, 2 * LANES), F32),
                        pltpu.VMEM((batch, CHUNK, WIDTH), F32),
                        pltpu.VMEM((batch, WIDTH, WIDTH), F32)],
        compiler_params=_cparams(("arbitrary",)),
        name="even_layer_mixers",
    )(x.reshape(batch, seq, D_MODEL), norm_w.reshape(1, D_MODEL), w, *ssd_args, *ret_args)
    return y_ssd.reshape(batch * seq, WIDTH), y_ret.reshape(batch * seq, WIDTH)


def _rwkv_program(batches, keep, r_ref, rh_ref, k_ref, kh_ref, v_ref, vh_ref, lo_ref, loh_ref,
                  mur_ref, muk_ref, muv_ref, mulo_ref, w0_ref, wup_ref, a0_ref, aup_ref, gup_ref,
                  kkw_ref, kaw_ref, rkw_ref, lnw_ref, lnb_ref, bd_ref,
                  o_ref, s_ref, sr_ref, sk_ref, sv_ref, slo_ref):
    width = r_ref.shape[2]
    npair = width // LANES
    bd = bd_ref[...]
    row = _iota2((CHUNK, CHUNK), 0)
    col = _iota2((CHUNK, CHUNK), 1)
    strict = row > col
    causal = row >= col
    eye = (row == col).astype(F32)
    tri = causal.astype(BF16)
    same_head = (row // HEAD_DIM == col // HEAD_DIM).astype(F32)
    m1 = [_head_mask(j) for j in range(2)]
    m2 = [_head_mask(j, 2 * LANES) for j in range(2)]
    cat0 = lambda xs: jnp.concatenate(xs, axis=0)
    cat1 = lambda xs: jnp.concatenate(xs, axis=1)
    C = CHUNK

    def token_shift(b, main_ref, halo_ref, mu_ref, scr_ref):
        _with_prev_rows(scr_ref.at[b], main_ref.at[b], halo_ref.at[b], keep)
        cur = main_ref[b]
        prev = scr_ref[b, HALO - 1:HALO - 1 + CHUNK, :]
        return cur + (prev - cur) * mu_ref[...]

    pre = {}
    for b in batches:
        r = token_shift(b, r_ref, rh_ref, mur_ref, sr_ref)
        k = token_shift(b, k_ref, kh_ref, muk_ref, sk_ref)
        v = token_shift(b, v_ref, vh_ref, muv_ref, sv_ref)
        lo = token_shift(b, lo_ref, loh_ref, mulo_ref, slo_ref)
        lo_wa = lo[:, :LANES]
        log_w = -RWKV_DECAY_SCALE * _sigmoid(w0_ref[...] + _mm(jnp.tanh(lo_wa), wup_ref[...]))
        a = _sigmoid(a0_ref[...] + _mm(lo_wa, aup_ref[...]))
        gate = _mm(_sigmoid(lo[:, LANES:]), gup_ref[...])
        kk = k * kkw_ref[...]
        kk = kk / jnp.maximum(jnp.sqrt(_mm(kk * kk, bd)), 1e-12)
        k = k * (1.0 + (a - 1.0) * kaw_ref[...])
        cum = _mm_split(tri, log_w, 2)
        mid = cum[C // 2 - 1:C // 2, :]
        last = cum[C - 1:C, :]
        inv_p = jnp.exp(mid - cum)
        to_end = jnp.exp(last - mid)
        pre[b] = dict(
            r=r, k=k, v=v, gate=gate,
            a_t=-kk * jnp.exp(cum - log_w - mid), r_t=r * jnp.exp(cum - mid),
            b_t=kk * a * inv_p, k_t=k * inv_p, e_mid=jnp.exp(mid),
            b_end=kk * a * inv_p * to_end, k_end=k * inv_p * to_end, decay=jnp.exp(last))

    yield
    units = [(b, q) for b in batches for q in range(npair)]
    slab = lambda name: [pre[b][name][:, q * LANES:(q + 1) * LANES] for b, q in units]
    a_t, r_t, b_t, k_t, v_u = slab("a_t"), slab("r_t"), slab("b_t"), slab("k_t"), slab("v")
    e_mid, b_end, k_end, decay = slab("e_mid"), slab("b_end"), slab("k_end"), slab("decay")

    mm = [_mm_nt(cat0([a * m1[0], r * m1[0], a * m1[1], r * m1[1]]), cat0([bt, kt]))
          for a, r, bt, kt in zip(a_t, r_t, b_t, k_t)]
    chains = [(u, j) for u in range(len(units)) for j in range(2)]
    m_ab = [jnp.where(strict, mm[u][2 * j * C:(2 * j + 1) * C, :C], 0.0) for u, j in chains]
    m_ak = [jnp.where(strict, mm[u][2 * j * C:(2 * j + 1) * C, C:], 0.0).astype(BF16)
            for u, j in chains]
    m_rb = [jnp.where(causal, mm[u][(2 * j + 1) * C:(2 * j + 2) * C, :C], 0.0).astype(BF16)
            for u, j in chains]
    m_rk = [jnp.where(causal, mm[u][(2 * j + 1) * C:(2 * j + 2) * C, C:], 0.0).astype(BF16)
            for u, j in chains]
    yield

    inv = [eye + m for m in m_ab]
    power = [m.astype(BF16) for m in m_ab]
    for _ in range(int(math.log2(C)) - 1):
        power = [_mm(p, p).astype(BF16) for p in power]
        inv = [i + _mm(i, p) for i, p in zip(inv, power)]
        yield

    n_units = range(len(units))
    per_head = lambda x2: x2[:C] * m2[0] + x2[C:] * m2[1]
    mv = [_mm(cat0([m_ak[2 * u], m_rk[2 * u], m_ak[2 * u + 1], m_rk[2 * u + 1]]), v_u[u])
          for u in n_units]
    mv_ak = [mv[u][:C] * m1[0] + mv[u][2 * C:3 * C] * m1[1] for u in n_units]
    mv_rk = [mv[u][C:2 * C] * m1[0] + mv[u][3 * C:] * m1[1] for u in n_units]
    z = [per_head(_mm(cat0([inv[2 * u], inv[2 * u + 1]]), cat1([a_t[u], mv_ak[u]])))
         for u in n_units]
    yield
    w = [per_head(_mm(cat0([m_rb[2 * u], m_rb[2 * u + 1]]), z[u])) for u in n_units]
    yield

    s = [s_ref[b, q] for b, q in units]
    uy = [_mm_nt(cat0([z[u][:, :LANES] * e_mid[u], (r_t[u] + w[u][:, :LANES]) * e_mid[u]]), s[u])
          for u in n_units]
    u_in = [uy[u][:C] + z[u][:, LANES:] for u in n_units]
    y = [uy[u][C:] + w[u][:, LANES:] + mv_rk[u] for u in n_units]
    s_inc = [_mm_tn(cat0([u_in[u], v_u[u]]), cat0([b_end[u], k_end[u]])) for u in n_units]
    for u, (b, q) in enumerate(units):
        s_ref[b, q] = s[u] * decay[u] + same_head * s_inc[u]

    for i, b in enumerate(batches):
        yb = cat1([y[i * npair + q] for q in range(npair)])
        mean = _mm(yb, bd) * (1.0 / HEAD_DIM)
        d = yb - mean
        var = _mm(d * d, bd) * (1.0 / HEAD_DIM)
        yn = d * lax.rsqrt(var + RWKV_LN_EPS) * lnw_ref[...] + lnb_ref[...]
        bonus = _mm(pre[b]["r"] * pre[b]["k"] * rkw_ref[...], bd) * pre[b]["v"]
        o_ref[b] = ((yn + bonus) * pre[b]["gate"]).astype(BF16)


FOX_TILE = 2048
FOX_Q_SPLIT = 8
FOX_GUARD = 64.0
LOG2E = math.log2(math.e)


def _fox_prep_chunk(b, q_ref, k_ref, v_ref, f_ref, qw_ref, kw_ref, fb_ref, bd_ref, spread_ref,
                    qo_ref, ko_ref, vo_ref, carry_ref):
    bd = bd_ref[...]
    q, k, v = q_ref[b], k_ref[b], v_ref[b]
    q_ms = _mm(q * q, bd) * (1.0 / HEAD_DIM)
    k_ms = _mm(k * k, bd) * (1.0 / HEAD_DIM)

    f = f_ref[b] + fb_ref[...]
    log_f = jnp.minimum(f, 0.0) - jnp.log1p(jnp.exp(-jnp.abs(f)))
    rows = CHUNK
    tri = (_iota2((rows, rows), 0) >= _iota2((rows, rows), 1)).astype(BF16)
    cum = _mm_split(tri, log_f, 3) + carry_ref[b, 0:1, :]
    carry_ref[b] = jnp.broadcast_to(cum[rows - 1:rows, :], carry_ref.shape[1:])
    yield
    q = q * lax.rsqrt(q_ms + EPS) * qw_ref[...] * (HEAD_DIM ** -0.5 * LOG2E)
    k = k * lax.rsqrt(k_ms + EPS) * kw_ref[...]
    c_all = _mm_split_rhs(cum * LOG2E, spread_ref[...], 3)
    yield

    lane = _iota2((1, LANES), 1)
    for h in range(N_HEADS):
        p, j = divmod(h, 2)
        ps = slice(p * LANES, (p + 1) * LANES)
        mj = _head_mask(j)
        qh, kh = q[:, ps] * mj, k[:, ps] * mj
        if j == 1:
            qh = pltpu.roll(qh, HEAD_DIM, 1)
            kh = pltpu.roll(kh, HEAD_DIM, 1)
        ch = c_all[:, h * LANES:(h + 1) * LANES]
        hi = ch.astype(BF16).astype(F32)
        mid = (ch - hi).astype(BF16).astype(F32)
        low = ch - hi - mid
        piece = lambda base: jnp.where(lane == base, hi,
                                       jnp.where(lane == base + 1, mid,
                                                 jnp.where(lane == base + 2, low, 0.0)))
        ones = lambda base: ((lane >= base) & (lane < base + 3)).astype(F32)
        qo_ref[b, h] = (qh + piece(HEAD_DIM) + ones(HEAD_DIM + 3)).astype(BF16)
        ko_ref[b, h] = (kh + ones(HEAD_DIM) - piece(HEAD_DIM + 3)).astype(BF16)
    row_head = _iota2((LANES, 1), 0) // HEAD_DIM
    row_in_head = _iota2((LANES, 1), 0) % HEAD_DIM
    for p in range(N_HEADS // 2):
        v_t = v[:, p * LANES:(p + 1) * LANES].T
        for j in range(2):
            ones_row = ((row_head != j) & (row_in_head == 0)).astype(F32)
            vo_ref[b, 2 * p + j] = jnp.where(row_head == j, v_t, ones_row).astype(BF16)


RWKV_INPUTS = 23
FOX_PREP_INPUTS = 9


def _rwkv_fox_prep_kernel(*refs):
    rwkv_in = refs[:RWKV_INPUTS]
    prep_in = refs[RWKV_INPUTS:RWKV_INPUTS + FOX_PREP_INPUTS]
    (rwkv_out, qo_ref, ko_ref, vo_ref,
     s_ref, sr_ref, sk_ref, sv_ref, slo_ref, carry_ref) = refs[RWKV_INPUTS + FOX_PREP_INPUTS:]
    c = pl.program_id(0)

    @pl.when(c == 0)
    def _():
        s_ref[...] = jnp.zeros_like(s_ref)
        carry_ref[...] = jnp.zeros_like(carry_ref)

    keep = (c > 0).astype(F32)
    def after(stages, program):
        for _ in range(stages):
            yield
        yield from program

    batches = list(range(carry_ref.shape[0]))
    programs = [_rwkv_program(batches, keep, *rwkv_in, rwkv_out, s_ref, sr_ref, sk_ref, sv_ref,
                              slo_ref)]
    for b in batches:
        prep = _fox_prep_chunk(b, *prep_in, qo_ref, ko_ref, vo_ref, carry_ref)
        programs.append(after(2 + 2 * b, prep))
    _interleave(programs)


def _rwkv_fox_prep(r, k, v, lo, mu, w0, w_up, a0, a_up, g_up, k_k, k_a, r_k, ln_w, ln_b,
                   fq, fk, fv, ff, q_norm_w, k_norm_w, f_bias, batch, seq):
    nc = seq // CHUNK
    lo_width = lo.shape[-1]
    rank = w_up.shape[0]
    row1 = lambda a: a.reshape(1, -1)
    seq3 = lambda a: a.reshape(batch, seq, a.shape[-1])

    wup_pad = jnp.concatenate([w_up, jnp.zeros((LANES - rank, WIDTH), F32)], axis=0).astype(BF16)
    aup_pad = jnp.concatenate([jnp.zeros((LANES - rank, WIDTH), F32), a_up], axis=0).astype(BF16)
    head = jnp.arange(WIDTH) // HEAD_DIM
    bd = (head[:, None] == head[None, :]).astype(BF16)
    r, k, v, lo = seq3(r), seq3(k), seq3(v), seq3(lo)
    rwkv_args = (r, r, k, k, v, v, lo, lo,
                 row1(mu[:WIDTH]), row1(mu[WIDTH:2 * WIDTH]), row1(mu[2 * WIDTH:3 * WIDTH]),
                 row1(mu[3 * WIDTH:]),
                 row1(w0), wup_pad, row1(a0), aup_pad, g_up.astype(BF16),
                 row1(k_k), row1(k_a), row1(r_k), row1(ln_w), row1(ln_b), bd)
    wide, wide_halo = _seq_spec(batch, WIDTH), _seq_halo_spec(batch, WIDTH)
    param = _param_spec1((1, WIDTH))
    weight = _param_spec1((LANES, WIDTH))
    rwkv_specs = [wide, wide_halo, wide, wide_halo, wide, wide_halo,
                  _seq_spec(batch, lo_width), _seq_halo_spec(batch, lo_width),
                  param, param, param, _param_spec1((1, lo_width)),
                  param, weight, param, weight, weight,
                  param, param, param, param, param, _param_spec1((WIDTH, WIDTH))]

    dst = jnp.arange(N_HEADS * LANES)
    spread = ((dst[None, :] // LANES == jnp.arange(LANES)[:, None])
              & (dst[None, :] % LANES >= HEAD_DIM) & (dst[None, :] % LANES < HEAD_DIM + 6)).astype(BF16)
    prep_args = (seq3(fq), seq3(fk), seq3(fv), seq3(ff),
                 row1(jnp.tile(q_norm_w, N_HEADS)), row1(jnp.tile(k_norm_w, N_HEADS)),
                 row1(jnp.pad(f_bias, (0, LANES - N_HEADS))), bd, spread)
    prep_specs = [wide, wide, wide, _seq_spec(batch, LANES), param, param,
                  _param_spec1((1, LANES)), _param_spec1((WIDTH, WIDTH)),
                  _param_spec1((LANES, N_HEADS * LANES))]
    assert len(rwkv_args) == RWKV_INPUTS and len(prep_args) == FOX_PREP_INPUTS

    head_out = pl.BlockSpec((batch, N_HEADS, CHUNK, LANES), lambda c: (0, 0, c, 0))
    head_shape = jax.ShapeDtypeStruct((batch, N_HEADS, seq, LANES), BF16)
    vt_out = pl.BlockSpec((batch, N_HEADS, LANES, CHUNK), lambda c: (0, 0, 0, c))
    vt_shape = jax.ShapeDtypeStruct((batch, N_HEADS, LANES, seq), BF16)
    shifted = lambda width: pltpu.VMEM((batch, CHUNK + HALO, width), F32)
    y_rwkv, qa, ka, va = pl.pallas_call(
        _rwkv_fox_prep_kernel,
        grid=(nc,),
        in_specs=rwkv_specs + prep_specs,
        out_specs=[wide, head_out, head_out, vt_out],
        out_shape=[jax.ShapeDtypeStruct((batch, seq, WIDTH), BF16), head_shape, head_shape,
                   vt_shape],
        scratch_shapes=[pltpu.VMEM((batch, N_HEADS // 2, LANES, LANES), F32),
                        shifted(WIDTH), shifted(WIDTH), shifted(WIDTH), shifted(lo_width),
                        pltpu.VMEM((batch, HALO, LANES), F32)],
        compiler_params=_cparams(("arbitrary",)),
        name="rwkv7_fox_prep",
    )(*rwkv_args, *prep_args)
    return y_rwkv.reshape(batch * seq, WIDTH), qa, ka, va


def _fox_kernel(qi_ref, ki_ref, q_ref, k_ref, vt_ref, o_ref, m_ref, acc_ref, redo_ref):
    t = pl.program_id(2)
    qi = qi_ref[t]
    ki = ki_ref[t]

    part = FOX_TILE // FOX_Q_SPLIT
    units = [(j, slice(h * part, (h + 1) * part)) for j in range(2) for h in range(FOX_Q_SPLIT)]

    def all_scores(on_diagonal):
        n_keys = lambda qs: qs.stop if on_diagonal else FOX_TILE
        raw = [lax.dot_general(k_ref[0, j, :n_keys(qs), :], q_ref[0, j, qs, :],
                               (((1,), (1,)), ((), ())), preferred_element_type=F32)
               for j, qs in units]
        if not on_diagonal:
            return raw
        masked = []
        for (_, qs), s in zip(units, raw):
            key = _iota2(s.shape, 0)
            query = _iota2(s.shape, 1) + qs.start
            masked.append(jnp.where(query >= key, s, NEG))
        return masked

    def pv(j, p):
        return jnp.dot(vt_ref[0, j, :, :p.shape[0]], p.astype(BF16), preferred_element_type=F32)

    def exact_step(on_diagonal):
        for (j, qs), s in zip(units, all_scores(on_diagonal)):
            m_prev = m_ref[j, :, qs]
            m_next = jnp.maximum(m_prev, jnp.max(s, axis=0, keepdims=True))
            alpha = jnp.exp2(m_prev - m_next)
            acc_ref[j, :, qs] = alpha * acc_ref[j, :, qs] + pv(j, jnp.exp2(s - m_next))
            m_ref[j, :, qs] = m_next

    def lagged_step(on_diagonal):
        done = []
        worst = None
        for (j, qs), s in zip(units, all_scores(on_diagonal)):
            m_used = m_ref[j, :, qs]
            block_max = jnp.max(s, axis=0, keepdims=True)
            contrib = pv(j, jnp.exp2(s - m_used))
            excess = jnp.max(block_max - m_used)
            worst = excess if worst is None else jnp.maximum(worst, excess)
            done.append((j, qs, m_used, block_max, contrib))
        ok = worst <= FOX_GUARD
        redo_ref[0] = jnp.where(ok, 0, 1)

        @pl.when(ok)
        def _():
            for j, qs, m_used, block_max, contrib in done:
                m_next = jnp.maximum(m_used, block_max)
                acc_ref[j, :, qs] = (acc_ref[j, :, qs] + contrib) * jnp.exp2(m_used - m_next)
                m_ref[j, :, qs] = m_next

    @pl.when(ki == qi)
    def _():
        m_ref[...] = jnp.full_like(m_ref, NEG)
        acc_ref[...] = jnp.zeros_like(acc_ref)
        exact_step(True)

    redo_ref[0] = 0

    @pl.when(ki < qi)
    def _():
        lagged_step(False)

    @pl.when(redo_ref[0] == 1)
    def _():
        exact_step(False)

    @pl.when(ki == 0)
    def _():
        head_rows = _iota2((LANES, 1), 0) // HEAD_DIM
        out_t = jnp.where(head_rows == 0,
                          acc_ref[0] / acc_ref[0, HEAD_DIM:HEAD_DIM + 1, :],
                          acc_ref[1] / acc_ref[1, 0:1, :])
        o_ref[0] = out_t.T.astype(BF16)


def _fox_attention(qa, ka, va, batch, seq):
    nt = seq // FOX_TILE
    pairs = [(i, j) for i in range(nt) for j in range(i, -1, -1)]
    qi = jnp.asarray([i for i, _ in pairs], jnp.int32)
    ki = jnp.asarray([j for _, j in pairs], jnp.int32)
    q_spec = pl.BlockSpec((1, 2, FOX_TILE, LANES), lambda b, p, t, qi, ki: (b, p, qi[t], 0))
    k_spec = pl.BlockSpec((1, 2, FOX_TILE, LANES), lambda b, p, t, qi, ki: (b, p, ki[t], 0))
    vt_spec = pl.BlockSpec((1, 2, LANES, FOX_TILE), lambda b, p, t, qi, ki: (b, p, 0, ki[t]))
    out = pl.pallas_call(
        _fox_kernel,
        grid_spec=pltpu.PrefetchScalarGridSpec(
            num_scalar_prefetch=2,
            grid=(batch, N_HEADS // 2, len(pairs)),
            in_specs=[q_spec, k_spec, vt_spec],
            out_specs=pl.BlockSpec((1, FOX_TILE, LANES), lambda b, p, t, qi, ki: (b, qi[t], p)),
            scratch_shapes=[pltpu.VMEM((2, 1, FOX_TILE), F32),
                            pltpu.VMEM((2, LANES, FOX_TILE), F32),
                            pltpu.SMEM((1,), jnp.int32)]),
        out_shape=jax.ShapeDtypeStruct((batch, seq, WIDTH), BF16),
        compiler_params=_cparams(("parallel", "parallel", "arbitrary")),
        name="fox_attention",
    )(qi, ki, qa, ka, va)
    return out.reshape(batch * seq, WIDTH)


def _pad_cols(w, width):
    return jnp.pad(w, ((0, 0), (0, width - w.shape[1])))


def _even_mixers(x, batch, seq, norm_w, w_in, conv_w, conv_b, dt_bias, a_log, d_skip,
                 ssd_norm_w, ret_norm_w):
    ssd_dt0 = WIDTH + (WIDTH + 4 * SSD_STATE)
    ret0 = ssd_dt0 + N_HEADS
    w = jnp.concatenate([w_in[:, :ssd_dt0], w_in[:, ret0:],
                         _pad_cols(w_in[:, ssd_dt0:ret0], LANES)], axis=1).astype(BF16)
    return _even_layer_mixers(x, norm_w, w, conv_w, conv_b, dt_bias, a_log, d_skip, ssd_norm_w,
                              ret_norm_w, batch, seq)


def _odd_mixers(x, batch, seq, norm_w, w_in, mu, w0, w_up, a0, a_up, g_up, k_k, k_a, r_k,
                ln_w, ln_b, q_norm_w, k_norm_w, f_bias):
    lo_width = w_up.shape[0] + a_up.shape[0] + g_up.shape[0]
    fox0 = 3 * WIDTH + lo_width
    f0 = fox0 + 3 * WIDTH
    w = jnp.concatenate([w_in[:, :f0], _pad_cols(w_in[:, f0:], LANES)], axis=1).astype(BF16)
    segments = ((0, WIDTH), (WIDTH, WIDTH), (2 * WIDTH, WIDTH), (3 * WIDTH, lo_width),
                (fox0, WIDTH), (fox0 + WIDTH, WIDTH), (fox0 + 2 * WIDTH, WIDTH), (f0, LANES))
    r, k, v, lo, fq, fk, fv, ff = _inproj(x, norm_w, w, segments)
    y_rwkv, qa, ka, va = _rwkv_fox_prep(r, k, v, lo, mu, w0, w_up, a0, a_up, g_up, k_k, k_a,
                                        r_k.reshape(-1), ln_w, ln_b, fq, fk, fv, ff,
                                        q_norm_w, k_norm_w, f_bias, batch, seq)
    return y_rwkv, _fox_attention(qa, ka, va, batch, seq)


def kernel(x, ev_norm_w, ev_w_in, ev_ssd_conv_w, ev_ssd_conv_b, ev_ssd_dt_bias, ev_ssd_a_log,
           ev_ssd_d, ev_ssd_norm_w, ev_ret_norm_w, ev_w_out,
           od_norm_w, od_w_in, od_rwkv_mu, od_rwkv_w0, od_rwkv_w_up, od_rwkv_a0, od_rwkv_a_up,
           od_rwkv_g_up, od_rwkv_k_k, od_rwkv_k_a, od_rwkv_r_k, od_rwkv_ln_w, od_rwkv_ln_b,
           od_fox_q_norm_w, od_fox_k_norm_w, od_fox_f_bias, od_w_out,
           ffn_norm_w, ffn_w_up, ffn_conv_w, ffn_conv_b, ffn_w_down):
    batch, seq, _ = x.shape
    depth = ffn_norm_w.shape[0]
    h = x.reshape(batch * seq, D_MODEL)
    w_up_all = ffn_w_up.astype(BF16)
    w_down_all = ffn_w_down.astype(BF16)
    for layer in range(depth):
        i = layer // 2
        if layer % 2 == 0:
            ya, yb = _even_mixers(h, batch, seq, ev_norm_w[i], ev_w_in[i], ev_ssd_conv_w[i],
                                  ev_ssd_conv_b[i], ev_ssd_dt_bias[i], ev_ssd_a_log[i],
                                  ev_ssd_d[i], ev_ssd_norm_w[i], ev_ret_norm_w[i])
            w_out = ev_w_out[i]
        else:
            ya, yb = _odd_mixers(h, batch, seq, od_norm_w[i], od_w_in[i], od_rwkv_mu[i],
                                 od_rwkv_w0[i], od_rwkv_w_up[i], od_rwkv_a0[i], od_rwkv_a_up[i],
                                 od_rwkv_g_up[i], od_rwkv_k_k[i], od_rwkv_k_a[i], od_rwkv_r_k[i],
                                 od_rwkv_ln_w[i], od_rwkv_ln_b[i], od_fox_q_norm_w[i],
                                 od_fox_k_norm_w[i], od_fox_f_bias[i])
            w_out = od_w_out[i]
        h = _proj_ffn(ya, yb, h, seq, w_out, ffn_norm_w[layer], layer, w_up_all,
                      ffn_conv_w[layer], ffn_conv_b[layer], w_down_all)
    return h.reshape(batch, seq, D_MODEL)
```

```python
import functools
import itertools
import math

import jax
import jax.numpy as jnp
from jax import lax
from jax.experimental import pallas as pl
from jax.experimental.pallas import tpu as pltpu

F32 = jnp.float32
BF16 = jnp.bfloat16

D_MODEL = 1024
HEAD_DIM = 64
N_HEADS = 8
WIDTH = N_HEADS * HEAD_DIM
LANES = 128
CHUNK = 128
HALO = 8
EPS = 1e-6
SSD_CONV = 4
SSD_STATE = 64
RWKV_LN_EPS = 64e-5
RWKV_DECAY_SCALE = 0.606531
D_FF = 2816
FF_TILE = 256
FFN_HALO = 16
FFN_LOOKAHEAD = 2
NEG = -1e30
VMEM_LIMIT = 56 * 1024 * 1024


def _mm(a, b):
    return jnp.dot(a.astype(BF16), b.astype(BF16), preferred_element_type=F32)


def _mm_nt(a, b):
    return lax.dot_general(a.astype(BF16), b.astype(BF16), (((1,), (1,)), ((), ())),
                           preferred_element_type=F32)


def _mm_tn(a, b):
    return lax.dot_general(a.astype(BF16), b.astype(BF16), (((0,), (0,)), ((), ())),
                           preferred_element_type=F32)


def _mm_split(lhs01, x, pieces):
    acc = None
    rest = x
    for _ in range(pieces):
        part = rest.astype(BF16)
        term = jnp.dot(lhs01, part, preferred_element_type=F32)
        acc = term if acc is None else acc + term
        rest = rest - part.astype(F32)
    return acc


def _mm_split_rhs(x, rhs01, pieces):
    acc = None
    rest = x
    for _ in range(pieces):
        part = rest.astype(BF16)
        term = jnp.dot(part, rhs01, preferred_element_type=F32)
        acc = term if acc is None else acc + term
        rest = rest - part.astype(F32)
    return acc


def _interleave(stage_generators):
    for _ in itertools.zip_longest(*stage_generators):
        pass


def _sigmoid(x):
    return 0.5 * jnp.tanh(0.5 * x) + 0.5


def _silu(x):
    return x * _sigmoid(x)


def _softplus(x):
    return jnp.maximum(x, 0.0) + jnp.log1p(jnp.exp(-jnp.abs(x)))


def _iota2(shape, dim):
    return lax.broadcasted_iota(jnp.int32, shape, dim)


def _head_mask(j, width=LANES):
    lane = _iota2((1, width), 1)
    return ((lane % LANES) // HEAD_DIM == j).astype(F32)


def _cparams(sem):
    return pltpu.CompilerParams(dimension_semantics=sem, vmem_limit_bytes=VMEM_LIMIT)


def _const_spec(shape):
    nd = len(shape)
    return pl.BlockSpec(shape, lambda *_: (0,) * nd, pipeline_mode=pl.Buffered(1))


def _inproj_kernel(x_ref, nw_ref, w_ref, *o_refs, segments):
    x = x_ref[...]
    ms = jnp.mean(x * x, axis=-1, keepdims=True)
    xn = (x * lax.rsqrt(ms + EPS) * nw_ref[...]).astype(BF16)
    for o_ref, (off, width) in zip(o_refs, segments):
        o_ref[...] = jnp.dot(xn, w_ref[:, off:off + width], preferred_element_type=F32)


def _inproj(x, norm_w, w, segments, tm=1024):
    t = x.shape[0]
    n = w.shape[1]
    return pl.pallas_call(
        functools.partial(_inproj_kernel, segments=segments),
        grid=(t // tm,),
        in_specs=[pl.BlockSpec((tm, D_MODEL), lambda i: (i, 0)),
                  _const_spec((1, D_MODEL)),
                  _const_spec((D_MODEL, n))],
        out_specs=[pl.BlockSpec((tm, wd), lambda i: (i, 0)) for _, wd in segments],
        out_shape=[jax.ShapeDtypeStruct((t, wd), F32) for _, wd in segments],
        compiler_params=_cparams(("parallel",)),
        name="inproj",
    )(x, norm_w.reshape(1, D_MODEL), w)


def _proj_ffn_kernel(ya_ref, yah_ref, yb_ref, ybh_ref, x_ref, xh_ref, wo_ref, nw_ref,
                     wup_ref, cw_ref, cb_ref, wd_ref, o_ref,
                     ya_ext, yb_ext, h_ref, hn_ref, g_ref, acc_ref, *, tm, tiles_per_seq):
    ya_ext[:FFN_HALO, :] = yah_ref[...]
    ya_ext[FFN_HALO:, :] = ya_ref[...]
    yb_ext[:FFN_HALO, :] = ybh_ref[...]
    yb_ext[FFN_HALO:, :] = yb_ref[...]
    h_ref[:FFN_HALO, :] = xh_ref[...]
    h_ref[FFN_HALO:, :] = x_ref[...]
    h_ref[...] += (jnp.dot(ya_ext[...], wo_ref[:WIDTH, :], preferred_element_type=F32)
                   + jnp.dot(yb_ext[...], wo_ref[WIDTH:, :], preferred_element_type=F32))

    h = h_ref[...]
    ms = jnp.mean(h * h, axis=-1, keepdims=True)
    hn = h * lax.rsqrt(ms + EPS) * nw_ref[...]
    first_of_seq = pl.program_id(0) % tiles_per_seq == 0
    halo_rows = _iota2((tm + FFN_HALO, 1), 0) < FFN_HALO
    hn_ref[...] = jnp.where(halo_rows & first_of_seq, 0.0, hn).astype(BF16)
    acc_ref[...] = jnp.zeros_like(acc_ref)

    def tile(f, base=0):
        return slice(base + f * FF_TILE, base + (f + 1) * FF_TILE)

    n_buf = g_ref.shape[0]

    def up_proj(f):
        g_ref[f % n_buf] = jnp.dot(hn_ref[...], wup_ref[:, tile(f)], preferred_element_type=F32)
        return jnp.dot(hn_ref[FFN_HALO:, :], wup_ref[:, tile(f, D_FF)],
                       preferred_element_type=F32)

    nf = D_FF // FF_TILE
    ups = {f: up_proj(f) for f in range(FFN_LOOKAHEAD)}
    for f in range(nf):
        if f + FFN_LOOKAHEAD < nf:
            ups[f + FFN_LOOKAHEAD] = up_proj(f + FFN_LOOKAHEAD)
        up = ups.pop(f)
        g = g_ref.at[f % n_buf]
        cw = cw_ref[:, tile(f)]
        gate = (cb_ref[:, tile(f)]
                + cw[0:1, :] * g[FFN_HALO - 2:FFN_HALO - 2 + tm, :]
                + cw[1:2, :] * g[FFN_HALO - 1:FFN_HALO - 1 + tm, :]
                + cw[2:3, :] * g[FFN_HALO:, :])
        act = (_silu(gate) * up).astype(BF16)
        acc_ref[...] += jnp.dot(act, wd_ref[tile(f), :], preferred_element_type=F32)
    o_ref[...] = h_ref[FFN_HALO:, :] + acc_ref[...]


def _proj_ffn(ya, yb, x, seq, w_out, norm_w, layer, w_up_all, conv_w, conv_b, w_down_all, tm=512):
    t = x.shape[0]
    halo_per_tile = tm // FFN_HALO
    row_spec = lambda width: pl.BlockSpec((tm, width), lambda i: (i, 0))
    halo_spec = lambda width: pl.BlockSpec(
        (FFN_HALO, width), lambda i: (jnp.maximum(i * halo_per_tile - 1, 0), 0))
    ext = lambda width, dtype: pltpu.VMEM((tm + FFN_HALO, width), dtype)
    layer_spec = lambda shape: pl.BlockSpec((None,) + shape, lambda i: (layer, 0, 0),
                                            pipeline_mode=pl.Buffered(1))
    return pl.pallas_call(
        functools.partial(_proj_ffn_kernel, tm=tm, tiles_per_seq=seq // tm),
        grid=(t // tm,),
        in_specs=[row_spec(WIDTH), halo_spec(WIDTH), row_spec(WIDTH), halo_spec(WIDTH),
                  row_spec(D_MODEL), halo_spec(D_MODEL),
                  _const_spec((2 * WIDTH, D_MODEL)), _const_spec((1, D_MODEL)),
                  layer_spec((D_MODEL, 2 * D_FF)), _const_spec((3, D_FF)), _const_spec((1, D_FF)),
                  layer_spec((D_FF, D_MODEL))],
        out_specs=row_spec(D_MODEL),
        out_shape=jax.ShapeDtypeStruct((t, D_MODEL), F32),
        scratch_shapes=[ext(WIDTH, BF16), ext(WIDTH, BF16), ext(D_MODEL, F32), ext(D_MODEL, BF16),
                        pltpu.VMEM((FFN_LOOKAHEAD + 1, tm + FFN_HALO, FF_TILE), F32),
                        pltpu.VMEM((tm, D_MODEL), F32)],
        compiler_params=_cparams(("parallel",)),
        name="proj_convffn",
    )(ya, ya, yb, yb, x, x, w_out.astype(BF16), norm_w.reshape(1, D_MODEL),
      w_up_all, conv_w, conv_b.reshape(1, D_FF), w_down_all)


def _seq_spec(batch, width):
    return pl.BlockSpec((batch, CHUNK, width), lambda c: (0, c, 0))


def _seq_halo_spec(batch, width):
    per = CHUNK // HALO
    return pl.BlockSpec((batch, HALO, width), lambda c: (0, jnp.maximum(c * per - 1, 0), 0))


def _param_spec1(shape):
    return pl.BlockSpec(shape, lambda c: (0,) * len(shape), pipeline_mode=pl.Buffered(1))


def _with_prev_rows(scr_ref, main_ref, halo_ref, keep):
    scr_ref[:HALO, :] = halo_ref[...] * keep
    scr_ref[HALO:, :] = main_ref[...]


def _ssd_chunk(b, keep, z_ref, x_ref, xh_ref, bc_ref, bch_ref, dt_ref,
               cwx_ref, cbx_ref, cwb_ref, cbb_ref, dtb_ref, alog_ref, dexp_ref, nw_ref, e_ref,
               o_ref, xe_ref, bce_ref, st_ref, y_ref):
    _with_prev_rows(xe_ref.at[b], x_ref.at[b], xh_ref.at[b], keep)
    _with_prev_rows(bce_ref.at[b], bc_ref.at[b], bch_ref.at[b], keep)

    def conv(ref, cw_ref, cb_ref):
        out = cb_ref[...]
        for k in range(SSD_CONV):
            lo = HALO - (SSD_CONV - 1) + k
            out = out + cw_ref[k:k + 1, :] * ref[b, lo:lo + CHUNK, :]
        return _silu(out)

    x = conv(xe_ref, cwx_ref, cbx_ref)
    bc = conv(bce_ref, cwb_ref, cbb_ref)
    bm = bc[:, :LANES]
    cm = bc[:, LANES:]

    row = _iota2((CHUNK, CHUNK), 0)
    col = _iota2((CHUNK, CHUNK), 1)
    causal = row >= col

    dt = _softplus(dt_ref[b] + dtb_ref[...])
    a = dt * (-jnp.exp(alog_ref[...]))
    a_cum = _mm_split(causal.astype(BF16), a, 3)
    yield
    a_cum2 = a_cum * LOG2E
    a_cum2_t = a_cum2.T
    acausal_penalty = jnp.where(causal, 0.0, NEG)
    a_last = a_cum[CHUNK - 1:CHUNK, :]
    per_head = jnp.concatenate(
        [dt, jnp.exp(a_cum), jnp.exp(a_last - a_cum),
         jnp.broadcast_to(jnp.exp(a_last), (HALO, LANES))], axis=0)
    per_lane = _mm_split_rhs(per_head, e_ref[...], 2)
    dt_e = per_lane[:CHUNK]
    ea_e = per_lane[CHUNK:2 * CHUNK]
    te_e = per_lane[2 * CHUNK:3 * CHUNK]
    cd_e = per_lane[3 * CHUNK:3 * CHUNK + 1]
    yield

    xdt = x * dt_e
    xs = xdt * te_e
    bm_t = bm.T
    group_rows = _iota2((LANES, 1), 0) // SSD_STATE

    for g in range(2):
        cg = cm * _head_mask(g)
        cb = _mm_nt(cg, bm)
        gs = slice(g * 2 * LANES, (g + 1) * 2 * LANES)
        st = st_ref[b, g]
        y_off = _mm(cg, st) * ea_e[:, gs]
        st_new = _mm(bm_t, xs[:, gs])
        st_ref[b, g] = st * cd_e[:, gs] + jnp.where(group_rows == g, st_new, 0.0)
        yield
        for pp in range(2):
            p = 2 * g + pp
            xp = xdt[:, p * LANES:(p + 1) * LANES]
            yp = y_off[:, pp * LANES:(pp + 1) * LANES]
            for j in range(2):
                h = 2 * p + j
                seg = a_cum2[:, h:h + 1] - a_cum2_t[h:h + 1, :]
                decay = jnp.exp2(seg + acausal_penalty)
                yp = yp + _mm(cb * decay, xp * _head_mask(j))
            y_ref[b, :, p * LANES:(p + 1) * LANES] = yp
            yield

    y = y_ref[b] + dexp_ref[...] * x
    gated = y * _silu(z_ref[b])
    ms = jnp.mean(gated * gated, axis=-1, keepdims=True)
    o_ref[b] = (gated * lax.rsqrt(ms + EPS) * nw_ref[...]).astype(BF16)


def _ret_chunk(b, q_ref, k_ref, v_ref, g_ref, cos_ref, sin_ref, intra_ref, qdec_ref, kte_ref,
               cg_ref, bd_ref, nw_ref, o_ref, r_ref):
    lane = _iota2((1, WIDTH), 1)
    first_half = (lane % HEAD_DIM) < (HEAD_DIM // 2)
    cos = jnp.concatenate([cos_ref[...]] * (WIDTH // LANES), axis=1)
    sin = jnp.concatenate([sin_ref[...]] * (WIDTH // LANES), axis=1)

    def rotary(v):
        other = jnp.where(first_half,
                          pltpu.roll(v, WIDTH - HEAD_DIM // 2, 1),
                          pltpu.roll(v, HEAD_DIM // 2, 1))
        return v * cos + other * sin

    q = rotary(q_ref[b])
    k = rotary(k_ref[b]) * (HEAD_DIM ** -0.5)
    v = v_ref[b]
    bd = bd_ref[...]
    pair = lambda t, p: t[:, p * LANES:(p + 1) * LANES]
    heads = [(p, j) for p in range(N_HEADS // 2) for j in range(2)]
    masks = [_head_mask(j) for j in range(2)]

    y_cross = _mm(q * qdec_ref[...], r_ref[b])
    r_ref[b] = r_ref[b] * cg_ref[...] + bd * _mm_tn(k * kte_ref[...], v)
    scores = [_mm_nt(pair(q, p) * masks[j], pair(k, p)) for p, j in heads]
    yield
    y_pairs = []
    for p in range(N_HEADS // 2):
        yp = pair(y_cross, p)
        for j in range(2):
            h = 2 * p + j
            yp = yp + _mm(scores[h] * intra_ref[h], pair(v, p) * masks[j])
        y_pairs.append(yp)
    yield
    y = jnp.concatenate(y_pairs, axis=1)
    ms = _mm(y * y, bd) * (1.0 / HEAD_DIM)
    yield
    o_ref[b] = (_silu(g_ref[b]) * (y * lax.rsqrt(ms + EPS) * nw_ref[...])).astype(BF16)


def _retention_tables(seq):
    half = HEAD_DIM // 2
    inv = 1.0 / (10000.0 ** (jnp.arange(half, dtype=F32) / half))
    ang = jnp.arange(seq, dtype=F32)[:, None] * inv[None, :]
    cos = jnp.tile(jnp.concatenate([jnp.cos(ang), jnp.cos(ang)], axis=-1), (1, LANES // HEAD_DIM))
    sin = jnp.tile(jnp.concatenate([-jnp.sin(ang), jnp.sin(ang)], axis=-1), (1, LANES // HEAD_DIM))
    log_gamma = jnp.log1p(-(2.0 ** (-5.0 - jnp.arange(N_HEADS, dtype=F32))))
    idx = jnp.arange(CHUNK, dtype=F32)
    rel = idx[:, None] - idx[None, :]
    intra = jnp.where(rel[None] >= 0,
                      jnp.exp(jnp.maximum(rel, 0.0)[None] * log_gamma[:, None, None]), 0.0)
    expand = lambda m: jnp.repeat(m, HEAD_DIM, axis=1)
    qdec = expand(jnp.exp((idx + 1.0)[:, None] * log_gamma[None, :]))
    kte = expand(jnp.exp((CHUNK - 1 - idx)[:, None] * log_gamma[None, :]))
    cgam = expand(jnp.exp(CHUNK * log_gamma)[None, :])
    head = jnp.arange(WIDTH) // HEAD_DIM
    bd = (head[:, None] == head[None, :]).astype(BF16)
    return cos, sin, intra, qdec, kte, cgam, bd


EVEN_SEGMENTS = {"z": (0, WIDTH), "x": (WIDTH, WIDTH), "bc": (2 * WIDTH, 2 * LANES),
                 "q": (2 * WIDTH + 2 * LANES, WIDTH), "k": (3 * WIDTH + 2 * LANES, WIDTH),
                 "v": (4 * WIDTH + 2 * LANES, WIDTH), "g": (5 * WIDTH + 2 * LANES, WIDTH),
                 "dt": (6 * WIDTH + 2 * LANES, LANES)}
EVEN_COLS = 6 * WIDTH + 3 * LANES
SSD_PARAMS = 9
RET_PARAMS = 8


def _even_layer_kernel(x_ref, nw_ref, w_ref, *rest):
    ssd_par = rest[:SSD_PARAMS]
    ret_par = rest[SSD_PARAMS:SSD_PARAMS + RET_PARAMS]
    (ssd_out, ret_out, buf_a, buf_b, halo_x, halo_bc,
     xe_ref, bce_ref, st_ref, y_ref, r_ref) = rest[SSD_PARAMS + RET_PARAMS:]
    c = pl.program_id(0)
    nb = x_ref.shape[0]

    @pl.when(c == 0)
    def _():
        buf_a[...] = jnp.zeros_like(buf_a)
        buf_b[...] = jnp.zeros_like(buf_b)
        halo_x[...] = jnp.zeros_like(halo_x)
        halo_bc[...] = jnp.zeros_like(halo_bc)

    @pl.when(c <= 1)
    def _():
        st_ref[...] = jnp.zeros_like(st_ref)
        r_ref[...] = jnp.zeros_like(r_ref)

    keep = (c > 1).astype(F32)

    def step(read_buf, write_buf):
        def project():
            x = x_ref[...].reshape(nb * CHUNK, D_MODEL)
            ms = jnp.mean(x * x, axis=-1, keepdims=True)
            xn = (x * lax.rsqrt(ms + EPS) * nw_ref[...]).astype(BF16)
            for off, width in EVEN_SEGMENTS.values():
                res = jnp.dot(xn, w_ref[:, off:off + width], preferred_element_type=F32)
                for b in range(nb):
                    write_buf[b, :, off:off + width] = res[b * CHUNK:(b + 1) * CHUNK]
                yield

        def view(name):
            off, width = EVEN_SEGMENTS[name]
            return read_buf.at[:, :, off:off + width]

        programs = [project()]
        for b in range(nb):
            programs.append(_ssd_chunk(b, keep, view("z"), view("x"), halo_x, view("bc"), halo_bc,
                                       view("dt"), *ssd_par, ssd_out, xe_ref, bce_ref, st_ref,
                                       y_ref))
            programs.append(_ret_chunk(b, view("q"), view("k"), view("v"), view("g"), *ret_par,
                                       ret_out, r_ref))
        _interleave(programs)
        halo_x[...] = view("x")[:, CHUNK - HALO:, :]
        halo_bc[...] = view("bc")[:, CHUNK - HALO:, :]

    @pl.when(c % 2 == 0)
    def _():
        step(buf_b, buf_a)

    @pl.when(c % 2 == 1)
    def _():
        step(buf_a, buf_b)


def _even_layer_mixers(x, norm_w, w, conv_w, conv_b, dt_bias, a_log, d_skip, ssd_norm_w,
                       ret_norm_w, batch, seq):
    nc = seq // CHUNK
    pad = LANES - N_HEADS
    e = jnp.repeat(jnp.eye(LANES, N_HEADS, dtype=BF16), HEAD_DIM, axis=1)
    ssd_args = (conv_w[:, :WIDTH], conv_b[:WIDTH].reshape(1, WIDTH),
                conv_w[:, WIDTH:], conv_b[WIDTH:].reshape(1, 2 * LANES),
                jnp.pad(dt_bias, (0, pad)).reshape(1, LANES),
                jnp.pad(a_log, (0, pad)).reshape(1, LANES),
                jnp.repeat(d_skip, HEAD_DIM).reshape(1, WIDTH),
                ssd_norm_w.reshape(1, WIDTH), e)
    ssd_specs = [_param_spec1((SSD_CONV, WIDTH)), _param_spec1((1, WIDTH)),
                 _param_spec1((SSD_CONV, 2 * LANES)), _param_spec1((1, 2 * LANES)),
                 _param_spec1((1, LANES)), _param_spec1((1, LANES)),
                 _param_spec1((1, WIDTH)), _param_spec1((1, WIDTH)),
                 _param_spec1((LANES, WIDTH))]
    cos, sin, intra, qdec, kte, cgam, bd = _retention_tables(seq)
    mixed_chunk = lambda c: jnp.maximum(c - 1, 0)
    table_spec = pl.BlockSpec((CHUNK, LANES), lambda c: (mixed_chunk(c), 0))
    ret_args = (cos, sin, intra, qdec, kte, cgam, bd, ret_norm_w.reshape(1, WIDTH))
    ret_specs = [table_spec, table_spec, _param_spec1((N_HEADS, CHUNK, CHUNK)),
                 _param_spec1((CHUNK, WIDTH)), _param_spec1((CHUNK, WIDTH)),
                 _param_spec1((1, WIDTH)), _param_spec1((WIDTH, WIDTH)), _param_spec1((1, WIDTH))]
    assert len(ssd_args) == SSD_PARAMS and len(ret_args) == RET_PARAMS
    out_spec = pl.BlockSpec((batch, CHUNK, WIDTH), lambda c: (0, mixed_chunk(c), 0))
    out_shape = jax.ShapeDtypeStruct((batch, seq, WIDTH), BF16)
    proj_buf = pltpu.VMEM((batch, CHUNK, EVEN_COLS), F32)
    y_ssd, y_ret = pl.pallas_call(
        _even_layer_kernel,
        grid=(nc + 1,),
        in_specs=[pl.BlockSpec((batch, CHUNK, D_MODEL), lambda c: (0, jnp.minimum(c, nc - 1), 0)),
                  _param_spec1((1, D_MODEL)), _param_spec1((D_MODEL, EVEN_COLS))]
                 + ssd_specs + ret_specs,
        out_specs=[out_spec, out_spec],
        out_shape=[out_shape, out_shape],
        scratch_shapes=[proj_buf, proj_buf,
                        pltpu.VMEM((batch, HALO, WIDTH), F32),
                        pltpu.VMEM((batch, HALO, 2 * LANES), F32),
                        pltpu.VMEM((batch, CHUNK + HALO, WIDTH), F32),
                        pltpu.VMEM((batch, CHUNK + HALO, 2 * LANES), F32),
                        pltpu.VMEM((batch, 2, LANES, 2 * LANES), F32),
                        pltpu.VMEM((batch, CHUNK, WIDTH), F32),
                        pltpu.VMEM((batch, WIDTH, WIDTH), F32)],
        compiler_params=_cparams(("arbitrary",)),
        name="even_layer_mixers",
    )(x.reshape(batch, seq, D_MODEL), norm_w.reshape(1, D_MODEL), w, *ssd_args, *ret_args)
    return y_ssd.reshape(batch * seq, WIDTH), y_ret.reshape(batch * seq, WIDTH)


def _rwkv_program(batches, keep, r_ref, rh_ref, k_ref, kh_ref, v_ref, vh_ref, lo_ref, loh_ref,
                  mur_ref, muk_ref, muv_ref, mulo_ref, w0_ref, wup_ref, a0_ref, aup_ref, gup_ref,
                  kkw_ref, kaw_ref, rkw_ref, lnw_ref, lnb_ref, bd_ref,
                  o_ref, s_ref, sr_ref, sk_ref, sv_ref, slo_ref):
    width = r_ref.shape[2]
    npair = width // LANES
    bd = bd_ref[...]
    row = _iota2((CHUNK, CHUNK), 0)
    col = _iota2((CHUNK, CHUNK), 1)
    strict = row > col
    causal = row >= col
    eye = (row == col).astype(F32)
    tri = causal.astype(BF16)
    same_head = (row // HEAD_DIM == col // HEAD_DIM).astype(F32)
    m1 = [_head_mask(j) for j in range(2)]
    m2 = [_head_mask(j, 2 * LANES) for j in range(2)]
    cat0 = lambda xs: jnp.concatenate(xs, axis=0)
    cat1 = lambda xs: jnp.concatenate(xs, axis=1)
    C = CHUNK

    def token_shift(b, main_ref, halo_ref, mu_ref, scr_ref):
        _with_prev_rows(scr_ref.at[b], main_ref.at[b], halo_ref.at[b], keep)
        cur = main_ref[b]
        prev = scr_ref[b, HALO - 1:HALO - 1 + CHUNK, :]
        return cur + (prev - cur) * mu_ref[...]

    pre = {}
    for b in batches:
        r = token_shift(b, r_ref, rh_ref, mur_ref, sr_ref)
        k = token_shift(b, k_ref, kh_ref, muk_ref, sk_ref)
        v = token_shift(b, v_ref, vh_ref, muv_ref, sv_ref)
        lo = token_shift(b, lo_ref, loh_ref, mulo_ref, slo_ref)
        lo_wa = lo[:, :LANES]
        log_w = -RWKV_DECAY_SCALE * _sigmoid(w0_ref[...] + _mm(jnp.tanh(lo_wa), wup_ref[...]))
        a = _sigmoid(a0_ref[...] + _mm(lo_wa, aup_ref[...]))
        gate = _mm(_sigmoid(lo[:, LANES:]), gup_ref[...])
        kk = k * kkw_ref[...]
        kk = kk * jnp.minimum(lax.rsqrt(_mm(kk * kk, bd)), 1e12)
        k = k * (1.0 + (a - 1.0) * kaw_ref[...])
        cum = _mm_split(tri, log_w, 2)
        mid = cum[C // 2 - 1:C // 2, :]
        last = cum[C - 1:C, :]
        inv_p = jnp.exp(mid - cum)
        to_end = jnp.exp(last - mid)
        pre[b] = dict(
            r=r, k=k, v=v, gate=gate,
            a_t=-kk * jnp.exp(cum - log_w - mid), r_t=r * jnp.exp(cum - mid),
            b_t=kk * a * inv_p, k_t=k * inv_p, e_mid=jnp.exp(mid),
            b_end=kk * a * inv_p * to_end, k_end=k * inv_p * to_end, decay=jnp.exp(last))

    yield
    units = [(b, q) for b in batches for q in range(npair)]
    slab = lambda name: [pre[b][name][:, q * LANES:(q + 1) * LANES] for b, q in units]
    a_t, r_t, b_t, k_t, v_u = slab("a_t"), slab("r_t"), slab("b_t"), slab("k_t"), slab("v")
    e_mid, b_end, k_end, decay = slab("e_mid"), slab("b_end"), slab("k_end"), slab("decay")

    mm = [_mm_nt(cat0([a * m1[0], r * m1[0], a * m1[1], r * m1[1]]), cat0([bt, kt]))
          for a, r, bt, kt in zip(a_t, r_t, b_t, k_t)]
    chains = [(u, j) for u in range(len(units)) for j in range(2)]
    m_ab = [jnp.where(strict, mm[u][2 * j * C:(2 * j + 1) * C, :C], 0.0) for u, j in chains]
    m_ak = [jnp.where(strict, mm[u][2 * j * C:(2 * j + 1) * C, C:], 0.0).astype(BF16)
            for u, j in chains]
    m_rb = [jnp.where(causal, mm[u][(2 * j + 1) * C:(2 * j + 2) * C, :C], 0.0).astype(BF16)
            for u, j in chains]
    m_rk = [jnp.where(causal, mm[u][(2 * j + 1) * C:(2 * j + 2) * C, C:], 0.0).astype(BF16)
            for u, j in chains]
    yield

    inv = [eye + m for m in m_ab]
    power = [m.astype(BF16) for m in m_ab]
    for _ in range(int(math.log2(C)) - 1):
        power = [_mm(p, p).astype(BF16) for p in power]
        inv = [i + _mm(i, p) for i, p in zip(inv, power)]
        yield

    n_units = range(len(units))
    per_head = lambda x2: x2[:C] * m2[0] + x2[C:] * m2[1]
    mv = [_mm(cat0([m_ak[2 * u], m_rk[2 * u], m_ak[2 * u + 1], m_rk[2 * u + 1]]), v_u[u])
          for u in n_units]
    mv_ak = [mv[u][:C] * m1[0] + mv[u][2 * C:3 * C] * m1[1] for u in n_units]
    mv_rk = [mv[u][C:2 * C] * m1[0] + mv[u][3 * C:] * m1[1] for u in n_units]
    z = [per_head(_mm(cat0([inv[2 * u], inv[2 * u + 1]]), cat1([a_t[u], mv_ak[u]])))
         for u in n_units]
    yield
    w = [per_head(_mm(cat0([m_rb[2 * u], m_rb[2 * u + 1]]), z[u])) for u in n_units]
    yield

    s = [s_ref[b, q] for b, q in units]
    uy = [_mm_nt(cat0([z[u][:, :LANES] * e_mid[u], (r_t[u] + w[u][:, :LANES]) * e_mid[u]]), s[u])
          for u in n_units]
    u_in = [uy[u][:C] + z[u][:, LANES:] for u in n_units]
    y = [uy[u][C:] + w[u][:, LANES:] + mv_rk[u] for u in n_units]
    s_inc = [_mm_tn(cat0([u_in[u], v_u[u]]), cat0([b_end[u], k_end[u]])) for u in n_units]
    for u, (b, q) in enumerate(units):
        s_ref[b, q] = s[u] * decay[u] + same_head * s_inc[u]

    for i, b in enumerate(batches):
        yb = cat1([y[i * npair + q] for q in range(npair)])
        mean = _mm(yb, bd) * (1.0 / HEAD_DIM)
        d = yb - mean
        var = _mm(d * d, bd) * (1.0 / HEAD_DIM)
        yn = d * lax.rsqrt(var + RWKV_LN_EPS) * lnw_ref[...] + lnb_ref[...]
        bonus = _mm(pre[b]["r"] * pre[b]["k"] * rkw_ref[...], bd) * pre[b]["v"]
        o_ref[b] = ((yn + bonus) * pre[b]["gate"]).astype(BF16)


FOX_TILE = 2048
FOX_Q_SPLIT = 8
FOX_GUARD = 64.0
LOG2E = math.log2(math.e)


def _fox_prep_chunk(b, q_ref, k_ref, v_ref, f_ref, qw_ref, kw_ref, fb_ref, bd_ref, spread_ref,
                    qo_ref, ko_ref, vo_ref, carry_ref):
    bd = bd_ref[...]
    q, k, v = q_ref[b], k_ref[b], v_ref[b]
    q_ms = _mm(q * q, bd) * (1.0 / HEAD_DIM)
    k_ms = _mm(k * k, bd) * (1.0 / HEAD_DIM)

    f = f_ref[b] + fb_ref[...]
    log_f = jnp.minimum(f, 0.0) - jnp.log1p(jnp.exp(-jnp.abs(f)))
    rows = CHUNK
    tri = (_iota2((rows, rows), 0) >= _iota2((rows, rows), 1)).astype(BF16)
    cum = _mm_split(tri, log_f, 3) + carry_ref[b, 0:1, :]
    carry_ref[b] = jnp.broadcast_to(cum[rows - 1:rows, :], carry_ref.shape[1:])
    yield
    q = q * lax.rsqrt(q_ms + EPS) * qw_ref[...] * (HEAD_DIM ** -0.5 * LOG2E)
    k = k * lax.rsqrt(k_ms + EPS) * kw_ref[...]
    c_all = _mm_split_rhs(cum * LOG2E, spread_ref[...], 3)
    yield

    lane = _iota2((1, LANES), 1)
    for h in range(N_HEADS):
        p, j = divmod(h, 2)
        ps = slice(p * LANES, (p + 1) * LANES)
        mj = _head_mask(j)
        qh, kh = q[:, ps] * mj, k[:, ps] * mj
        if j == 1:
            qh = pltpu.roll(qh, HEAD_DIM, 1)
            kh = pltpu.roll(kh, HEAD_DIM, 1)
        ch = c_all[:, h * LANES:(h + 1) * LANES]
        hi = ch.astype(BF16).astype(F32)
        mid = (ch - hi).astype(BF16).astype(F32)
        low = ch - hi - mid
        piece = lambda base: jnp.where(lane == base, hi,
                                       jnp.where(lane == base + 1, mid,
                                                 jnp.where(lane == base + 2, low, 0.0)))
        ones = lambda base: ((lane >= base) & (lane < base + 3)).astype(F32)
        qo_ref[b, h] = (qh + piece(HEAD_DIM) + ones(HEAD_DIM + 3)).astype(BF16)
        ko_ref[b, h] = (kh + ones(HEAD_DIM) - piece(HEAD_DIM + 3)).astype(BF16)
    row_head = _iota2((LANES, 1), 0) // HEAD_DIM
    row_in_head = _iota2((LANES, 1), 0) % HEAD_DIM
    for p in range(N_HEADS // 2):
        v_t = v[:, p * LANES:(p + 1) * LANES].T
        for j in range(2):
            ones_row = ((row_head != j) & (row_in_head == 0)).astype(F32)
            vo_ref[b, 2 * p + j] = jnp.where(row_head == j, v_t, ones_row).astype(BF16)


RWKV_INPUTS = 23
FOX_PREP_INPUTS = 9


def _rwkv_fox_prep_kernel(*refs):
    rwkv_in = refs[:RWKV_INPUTS]
    prep_in = refs[RWKV_INPUTS:RWKV_INPUTS + FOX_PREP_INPUTS]
    (rwkv_out, qo_ref, ko_ref, vo_ref,
     s_ref, sr_ref, sk_ref, sv_ref, slo_ref, carry_ref) = refs[RWKV_INPUTS + FOX_PREP_INPUTS:]
    c = pl.program_id(0)

    @pl.when(c == 0)
    def _():
        s_ref[...] = jnp.zeros_like(s_ref)
        carry_ref[...] = jnp.zeros_like(carry_ref)

    keep = (c > 0).astype(F32)
    def after(stages, program):
        for _ in range(stages):
            yield
        yield from program

    batches = list(range(carry_ref.shape[0]))
    programs = [_rwkv_program(batches, keep, *rwkv_in, rwkv_out, s_ref, sr_ref, sk_ref, sv_ref,
                              slo_ref)]
    for b in batches:
        prep = _fox_prep_chunk(b, *prep_in, qo_ref, ko_ref, vo_ref, carry_ref)
        programs.append(after(2 + 2 * b, prep))
    _interleave(programs)


def _rwkv_fox_prep(r, k, v, lo, mu, w0, w_up, a0, a_up, g_up, k_k, k_a, r_k, ln_w, ln_b,
                   fq, fk, fv, ff, q_norm_w, k_norm_w, f_bias, batch, seq):
    nc = seq // CHUNK
    lo_width = lo.shape[-1]
    rank = w_up.shape[0]
    row1 = lambda a: a.reshape(1, -1)
    seq3 = lambda a: a.reshape(batch, seq, a.shape[-1])

    wup_pad = jnp.concatenate([w_up, jnp.zeros((LANES - rank, WIDTH), F32)], axis=0).astype(BF16)
    aup_pad = jnp.concatenate([jnp.zeros((LANES - rank, WIDTH), F32), a_up], axis=0).astype(BF16)
    head = jnp.arange(WIDTH) // HEAD_DIM
    bd = (head[:, None] == head[None, :]).astype(BF16)
    r, k, v, lo = seq3(r), seq3(k), seq3(v), seq3(lo)
    rwkv_args = (r, r, k, k, v, v, lo, lo,
                 row1(mu[:WIDTH]), row1(mu[WIDTH:2 * WIDTH]), row1(mu[2 * WIDTH:3 * WIDTH]),
                 row1(mu[3 * WIDTH:]),
                 row1(w0), wup_pad, row1(a0), aup_pad, g_up.astype(BF16),
                 row1(k_k), row1(k_a), row1(r_k), row1(ln_w), row1(ln_b), bd)
    wide, wide_halo = _seq_spec(batch, WIDTH), _seq_halo_spec(batch, WIDTH)
    param = _param_spec1((1, WIDTH))
    weight = _param_spec1((LANES, WIDTH))
    rwkv_specs = [wide, wide_halo, wide, wide_halo, wide, wide_halo,
                  _seq_spec(batch, lo_width), _seq_halo_spec(batch, lo_width),
                  param, param, param, _param_spec1((1, lo_width)),
                  param, weight, param, weight, weight,
                  param, param, param, param, param, _param_spec1((WIDTH, WIDTH))]

    dst = jnp.arange(N_HEADS * LANES)
    spread = ((dst[None, :] // LANES == jnp.arange(LANES)[:, None])
              & (dst[None, :] % LANES >= HEAD_DIM) & (dst[None, :] % LANES < HEAD_DIM + 6)).astype(BF16)
    prep_args = (seq3(fq), seq3(fk), seq3(fv), seq3(ff),
                 row1(jnp.tile(q_norm_w, N_HEADS)), row1(jnp.tile(k_norm_w, N_HEADS)),
                 row1(jnp.pad(f_bias, (0, LANES - N_HEADS))), bd, spread)
    prep_specs = [wide, wide, wide, _seq_spec(batch, LANES), param, param,
                  _param_spec1((1, LANES)), _param_spec1((WIDTH, WIDTH)),
                  _param_spec1((LANES, N_HEADS * LANES))]
    assert len(rwkv_args) == RWKV_INPUTS and len(prep_args) == FOX_PREP_INPUTS

    head_out = pl.BlockSpec((batch, N_HEADS, CHUNK, LANES), lambda c: (0, 0, c, 0))
    head_shape = jax.ShapeDtypeStruct((batch, N_HEADS, seq, LANES), BF16)
    vt_out = pl.BlockSpec((batch, N_HEADS, LANES, CHUNK), lambda c: (0, 0, 0, c))
    vt_shape = jax.ShapeDtypeStruct((batch, N_HEADS, LANES, seq), BF16)
    shifted = lambda width: pltpu.VMEM((batch, CHUNK + HALO, width), F32)
    y_rwkv, qa, ka, va = pl.pallas_call(
        _rwkv_fox_prep_kernel,
        grid=(nc,),
        in_specs=rwkv_specs + prep_specs,
        out_specs=[wide, head_out, head_out, vt_out],
        out_shape=[jax.ShapeDtypeStruct((batch, seq, WIDTH), BF16), head_shape, head_shape,
                   vt_shape],
        scratch_shapes=[pltpu.VMEM((batch, N_HEADS // 2, LANES, LANES), F32),
                        shifted(WIDTH), shifted(WIDTH), shifted(WIDTH), shifted(lo_width),
                        pltpu.VMEM((batch, HALO, LANES), F32)],
        compiler_params=_cparams(("arbitrary",)),
        name="rwkv7_fox_prep",
    )(*rwkv_args, *prep_args)
    return y_rwkv.reshape(batch * seq, WIDTH), qa, ka, va


def _fox_kernel(qi_ref, ki_ref, q_ref, k_ref, vt_ref, o_ref, m_ref, acc_ref, redo_ref):
    t = pl.program_id(2)
    qi = qi_ref[t]
    ki = ki_ref[t]

    part = FOX_TILE // FOX_Q_SPLIT
    units = [(j, slice(h * part, (h + 1) * part)) for j in range(2) for h in range(FOX_Q_SPLIT)]

    def all_scores(on_diagonal):
        n_keys = lambda qs: qs.stop if on_diagonal else FOX_TILE
        raw = [lax.dot_general(k_ref[0, j, :n_keys(qs), :], q_ref[0, j, qs, :],
                               (((1,), (1,)), ((), ())), preferred_element_type=F32)
               for j, qs in units]
        if not on_diagonal:
            return raw
        masked = []
        for (_, qs), s in zip(units, raw):
            key = _iota2(s.shape, 0)
            query = _iota2(s.shape, 1) + qs.start
            masked.append(jnp.where(query >= key, s, NEG))
        return masked

    def pv(j, p):
        return jnp.dot(vt_ref[0, j, :, :p.shape[0]], p.astype(BF16), preferred_element_type=F32)

    def exact_step(on_diagonal):
        for (j, qs), s in zip(units, all_scores(on_diagonal)):
            m_prev = m_ref[j, :, qs]
            m_next = jnp.maximum(m_prev, jnp.max(s, axis=0, keepdims=True))
            alpha = jnp.exp2(m_prev - m_next)
            acc_ref[j, :, qs] = alpha * acc_ref[j, :, qs] + pv(j, jnp.exp2(s - m_next))
            m_ref[j, :, qs] = m_next

    def lagged_step(on_diagonal):
        done = []
        worst = None
        for (j, qs), s in zip(units, all_scores(on_diagonal)):
            m_used = m_ref[j, :, qs]
            block_max = jnp.max(s, axis=0, keepdims=True)
            contrib = pv(j, jnp.exp2(s - m_used))
            excess = jnp.max(block_max - m_used)
            worst = excess if worst is None else jnp.maximum(worst, excess)
            done.append((j, qs, m_used, block_max, contrib))
        ok = worst <= FOX_GUARD
        redo_ref[0] = jnp.where(ok, 0, 1)

        @pl.when(ok)
        def _():
            for j, qs, m_used, block_max, contrib in done:
                m_next = jnp.maximum(m_used, block_max)
                acc_ref[j, :, qs] = (acc_ref[j, :, qs] + contrib) * jnp.exp2(m_used - m_next)
                m_ref[j, :, qs] = m_next

    @pl.when(ki == qi)
    def _():
        m_ref[...] = jnp.full_like(m_ref, NEG)
        acc_ref[...] = jnp.zeros_like(acc_ref)
        exact_step(True)

    redo_ref[0] = 0

    @pl.when(ki < qi)
    def _():
        lagged_step(False)

    @pl.when(redo_ref[0] == 1)
    def _():
        exact_step(False)

    @pl.when(ki == 0)
    def _():
        head_rows = _iota2((LANES, 1), 0) // HEAD_DIM
        out_t = jnp.where(head_rows == 0,
                          acc_ref[0] / acc_ref[0, HEAD_DIM:HEAD_DIM + 1, :],
                          acc_ref[1] / acc_ref[1, 0:1, :])
        o_ref[0] = out_t.T.astype(BF16)


def _fox_attention(qa, ka, va, batch, seq):
    nt = seq // FOX_TILE
    pairs = [(i, j) for i in range(nt) for j in range(i, -1, -1)]
    qi = jnp.asarray([i for i, _ in pairs], jnp.int32)
    ki = jnp.asarray([j for _, j in pairs], jnp.int32)
    q_spec = pl.BlockSpec((1, 2, FOX_TILE, LANES), lambda b, p, t, qi, ki: (b, p, qi[t], 0))
    k_spec = pl.BlockSpec((1, 2, FOX_TILE, LANES), lambda b, p, t, qi, ki: (b, p, ki[t], 0))
    vt_spec = pl.BlockSpec((1, 2, LANES, FOX_TILE), lambda b, p, t, qi, ki: (b, p, 0, ki[t]))
    out = pl.pallas_call(
        _fox_kernel,
        grid_spec=pltpu.PrefetchScalarGridSpec(
            num_scalar_prefetch=2,
            grid=(batch, N_HEADS // 2, len(pairs)),
            in_specs=[q_spec, k_spec, vt_spec],
            out_specs=pl.BlockSpec((1, FOX_TILE, LANES), lambda b, p, t, qi, ki: (b, qi[t], p)),
            scratch_shapes=[pltpu.VMEM((2, 1, FOX_TILE), F32),
                            pltpu.VMEM((2, LANES, FOX_TILE), F32),
                            pltpu.SMEM((1,), jnp.int32)]),
        out_shape=jax.ShapeDtypeStruct((batch, seq, WIDTH), BF16),
        compiler_params=_cparams(("parallel", "parallel", "arbitrary")),
        name="fox_attention",
    )(qi, ki, qa, ka, va)
    return out.reshape(batch * seq, WIDTH)


def _pad_cols(w, width):
    return jnp.pad(w, ((0, 0), (0, width - w.shape[1])))


def _even_mixers(x, batch, seq, norm_w, w_in, conv_w, conv_b, dt_bias, a_log, d_skip,
                 ssd_norm_w, ret_norm_w):
    ssd_dt0 = WIDTH + (WIDTH + 4 * SSD_STATE)
    ret0 = ssd_dt0 + N_HEADS
    w = jnp.concatenate([w_in[:, :ssd_dt0], w_in[:, ret0:],
                         _pad_cols(w_in[:, ssd_dt0:ret0], LANES)], axis=1).astype(BF16)
    return _even_layer_mixers(x, norm_w, w, conv_w, conv_b, dt_bias, a_log, d_skip, ssd_norm_w,
                              ret_norm_w, batch, seq)


def _odd_mixers(x, batch, seq, norm_w, w_in, mu, w0, w_up, a0, a_up, g_up, k_k, k_a, r_k,
                ln_w, ln_b, q_norm_w, k_norm_w, f_bias):
    lo_width = w_up.shape[0] + a_up.shape[0] + g_up.shape[0]
    fox0 = 3 * WIDTH + lo_width
    f0 = fox0 + 3 * WIDTH
    w = jnp.concatenate([w_in[:, :f0], _pad_cols(w_in[:, f0:], LANES)], axis=1).astype(BF16)
    segments = ((0, WIDTH), (WIDTH, WIDTH), (2 * WIDTH, WIDTH), (3 * WIDTH, lo_width),
                (fox0, WIDTH), (fox0 + WIDTH, WIDTH), (fox0 + 2 * WIDTH, WIDTH), (f0, LANES))
    r, k, v, lo, fq, fk, fv, ff = _inproj(x, norm_w, w, segments)
    y_rwkv, qa, ka, va = _rwkv_fox_prep(r, k, v, lo, mu, w0, w_up, a0, a_up, g_up, k_k, k_a,
                                        r_k.reshape(-1), ln_w, ln_b, fq, fk, fv, ff,
                                        q_norm_w, k_norm_w, f_bias, batch, seq)
    return y_rwkv, _fox_attention(qa, ka, va, batch, seq)


def kernel(x, ev_norm_w, ev_w_in, ev_ssd_conv_w, ev_ssd_conv_b, ev_ssd_dt_bias, ev_ssd_a_log,
           ev_ssd_d, ev_ssd_norm_w, ev_ret_norm_w, ev_w_out,
           od_norm_w, od_w_in, od_rwkv_mu, od_rwkv_w0, od_rwkv_w_up, od_rwkv_a0, od_rwkv_a_up,
           od_rwkv_g_up, od_rwkv_k_k, od_rwkv_k_a, od_rwkv_r_k, od_rwkv_ln_w, od_rwkv_ln_b,
           od_fox_q_norm_w, od_fox_k_norm_w, od_fox_f_bias, od_w_out,
           ffn_norm_w, ffn_w_up, ffn_conv_w, ffn_conv_b, ffn_w_down):
    batch, seq, _ = x.shape
    depth = ffn_norm_w.shape[0]
    h = x.reshape(batch * seq, D_MODEL)
    w_up_all = ffn_w_up.astype(BF16)
    w_down_all = ffn_w_down.astype(BF16)
    for layer in range(depth):
        i = layer // 2
        if layer % 2 == 0:
            ya, yb = _even_mixers(h, batch, seq, ev_norm_w[i], ev_w_in[i], ev_ssd_conv_w[i],
                                  ev_ssd_conv_b[i], ev_ssd_dt_bias[i], ev_ssd_a_log[i],
                                  ev_ssd_d[i], ev_ssd_norm_w[i], ev_ret_norm_w[i])
            w_out = ev_w_out[i]
        else:
            ya, yb = _odd_mixers(h, batch, seq, od_norm_w[i], od_w_in[i], od_rwkv_mu[i],
                                 od_rwkv_w0[i], od_rwkv_w_up[i], od_rwkv_a0[i], od_rwkv_a_up[i],
                                 od_rwkv_g_up[i], od_rwkv_k_k[i], od_rwkv_k_a[i], od_rwkv_r_k[i],
                                 od_rwkv_ln_w[i], od_rwkv_ln_b[i], od_fox_q_norm_w[i],
                                 od_fox_k_norm_w[i], od_fox_f_bias[i])
            w_out = od_w_out[i]
        h = _proj_ffn(ya, yb, h, seq, w_out, ffn_norm_w[layer], layer, w_up_all,
                      ffn_conv_w[layer], ffn_conv_b[layer], w_down_all)
    return h.reshape(batch, seq, D_MODEL)
```
